```python
import math, functools
import jax, jax.numpy as jnp
from jax import lax
import numpy as np

D_MODEL = 2048
BATCH = 4
SEQ = 8192
DEPTH = 1
DEC_BATCH = 8
DEC_SEQ = 16
PAST_LEN = 1024

CHUNK = 64
Q_BLOCK = 128
EPS = 1e-6
NEG_INF = -1e30

MLA_HEADS = 8
MLA_Q_LORA = 512
MLA_KV_LORA = 256
MLA_NOPE = 128
MLA_ROPE = 64
MLA_V = 128
MLA_THETA = 10000.0

DIFF_HEADS = 8
DIFF_DH = 64
DIFF_VD = 2 * DIFF_DH
DIFF_ROT = DIFF_DH // 4
ROPE_THETA = 500000.0

D_FF = -(-(8 * D_MODEL) // (3 * 256)) * 256

DIFF_QK = DIFF_HEADS * 2 * DIFF_DH
C_QLAT = 0
C_KVLAT = C_QLAT + MLA_Q_LORA
C_KROPE = C_KVLAT + MLA_KV_LORA
C_DQ = C_KROPE + MLA_ROPE
C_DK = C_DQ + DIFF_QK
C_DV = C_DK + DIFF_QK
C_GA = C_DV + DIFF_HEADS * DIFF_VD
C_GB = C_GA + D_MODEL
IN_COLS = C_GB + D_MODEL

kernel_name = "hybrid_mla_diffattn_gated_streaming_step"


def _rmsnorm(x, g):
    xf = x.astype(jnp.float32)
    y = xf * lax.rsqrt(jnp.mean(xf * xf, axis=-1, keepdims=True) + EPS)
    return (y * g.astype(jnp.float32)).astype(x.dtype)


def _rope(x, pos, theta, rot_dim):
    half = rot_dim // 2
    inv = 1.0 / (jnp.float32(theta) ** (jnp.arange(half, dtype=jnp.float32) / half))
    ang = pos.astype(jnp.float32)[:, None] * inv[None, :]
    shape = (pos.shape[0],) + (1,) * (x.ndim - 3) + (half,)
    cos = jnp.cos(ang).reshape(shape)
    sin = jnp.sin(ang).reshape(shape)
    xf = x.astype(jnp.float32)
    x1 = xf[..., :half]
    x2 = xf[..., half:rot_dim]
    out = jnp.concatenate([x1 * cos - x2 * sin, x2 * cos + x1 * sin, xf[..., rot_dim:]], axis=-1)
    return out.astype(x.dtype)


def _chunk_mask(q_pos, k_pos):
    return (k_pos[None, :] // CHUNK) <= (q_pos[:, None] // CHUNK)


def _masked_softmax(s, mask):
    s = jnp.where(mask[None, None], s.astype(jnp.float32), NEG_INF)
    return jax.nn.softmax(s, axis=-1)


def _mla_attend(q_nope, q_rope, q_pos, k_nope, k_rope, v, k_pos):
    scale = (MLA_NOPE + MLA_ROPE) ** -0.5
    s = jnp.einsum('bqhd,bkhd->bhqk', q_nope, k_nope) + jnp.einsum('bqhr,bkr->bhqk', q_rope, k_rope)
    p = _masked_softmax(s * scale, _chunk_mask(q_pos, k_pos))
    return jnp.einsum('bhqk,bkhd->bqhd', p.astype(v.dtype), v)


def _diff_attend(q1, q2, q_pos, k1, k2, v, lam, k_pos):
    scale = DIFF_DH ** -0.5
    mask = _chunk_mask(q_pos, k_pos)
    p1 = _masked_softmax(jnp.einsum('bqhd,bkhd->bhqk', q1, k1) * scale, mask)
    p2 = _masked_softmax(jnp.einsum('bqhd,bkhd->bhqk', q2, k2) * scale, mask)
    a = p1 - lam * p2
    return jnp.einsum('bhqk,bkhe->bqhe', a.astype(v.dtype), v)


def _blockwise(fn, qs, q_pos):
    L = q_pos.shape[0]
    nb = L // Q_BLOCK

    def split(a):
        return jnp.moveaxis(a.reshape((a.shape[0], nb, Q_BLOCK) + a.shape[2:]), 1, 0)

    out = lax.map(lambda args: fn(*args), tuple(split(q) for q in qs) + (q_pos.reshape(nb, Q_BLOCK),))
    out = jnp.moveaxis(out, 0, 1)
    return out.reshape((out.shape[0], L) + out.shape[3:])


def _layer(x, pos, past, lw, lambda_init, blockwise):
    B, L, _ = x.shape
    h = _rmsnorm(x, lw['norm_mix'])
    proj = h @ lw['w_in']

    q_lat = _rmsnorm(proj[..., C_QLAT:C_KVLAT], lw['mla_q_norm'])
    q = (q_lat @ lw['mla_w_uq']).reshape(B, L, MLA_HEADS, MLA_NOPE + MLA_ROPE)
    q_nope = q[..., :MLA_NOPE]
    q_rope = _rope(q[..., MLA_NOPE:], pos, MLA_THETA, MLA_ROPE)
    ckv_new = _rmsnorm(proj[..., C_KVLAT:C_KROPE], lw['mla_kv_norm'])
    krope_new = _rope(proj[..., C_KROPE:C_DQ], pos, MLA_THETA, MLA_ROPE)

    dq = _rope(proj[..., C_DQ:C_DK].reshape(B, L, DIFF_HEADS, 2, DIFF_DH), pos, ROPE_THETA, DIFF_ROT)
    dk_new = _rope(proj[..., C_DK:C_DV].reshape(B, L, DIFF_HEADS, 2, DIFF_DH), pos, ROPE_THETA,
                   DIFF_ROT).reshape(B, L, DIFF_HEADS, 2 * DIFF_DH)
    dv_new = proj[..., C_DV:C_GA].reshape(B, L, DIFF_HEADS, DIFF_VD)

    gate_a = jax.nn.sigmoid(proj[..., C_GA:C_GB])
    gate_b = jax.nn.sigmoid(proj[..., C_GB:IN_COLS])

    if past is None:
        ckv, krope, dk, dv = ckv_new, krope_new, dk_new, dv_new
        k_pos = pos
    else:
        ckv = jnp.concatenate([past[0], ckv_new], axis=1)
        krope = jnp.concatenate([past[1], krope_new], axis=1)
        dk = jnp.concatenate([past[2], dk_new], axis=1)
        dv = jnp.concatenate([past[3], dv_new], axis=1)
        k_pos = jnp.arange(ckv.shape[1])
    T = ckv.shape[1]

    kv = (ckv @ lw['mla_w_ukv']).reshape(B, T, MLA_HEADS, MLA_NOPE + MLA_V)
    mla_fn = functools.partial(_mla_attend, k_nope=kv[..., :MLA_NOPE], k_rope=krope,
                               v=kv[..., MLA_NOPE:], k_pos=k_pos)

    lam = (jnp.exp(jnp.sum(lw['diff_lq1'].astype(jnp.float32) * lw['diff_lk1'].astype(jnp.float32)))
           - jnp.exp(jnp.sum(lw['diff_lq2'].astype(jnp.float32) * lw['diff_lk2'].astype(jnp.float32)))
           + lambda_init)
    dk5 = dk.reshape(B, T, DIFF_HEADS, 2, DIFF_DH)
    diff_fn = functools.partial(_diff_attend, k1=dk5[..., 0, :], k2=dk5[..., 1, :], v=dv,
                                lam=lam, k_pos=k_pos)

    qa = (q_nope, q_rope)
    qb = (dq[..., 0, :], dq[..., 1, :])
    if blockwise:
        out_a = _blockwise(mla_fn, qa, pos)
        out_b = _blockwise(diff_fn, qb, pos)
    else:
        out_a = mla_fn(*qa, pos)
        out_b = diff_fn(*qb, pos)
    out_b = _rmsnorm(out_b, lw['diff_subln']) * (1.0 - lambda_init)

    y_a = out_a.reshape(B, L, MLA_HEADS * MLA_V) @ lw['w_branch_a']
    y_b = out_b.reshape(B, L, DIFF_HEADS * DIFF_VD) @ lw['w_branch_b']
    x = x + (gate_a * y_a + gate_b * y_b) @ lw['w_out']

    h2 = _rmsnorm(x, lw['norm_ffn'])
    gu = h2 @ lw['w_ffn_in']
    x = x + (jax.nn.silu(gu[..., :D_FF]) * gu[..., D_FF:]) @ lw['w_ffn_out']
    return x, (ckv_new, krope_new, dk_new, dv_new)


def setup_inputs(seed: int = 0) -> dict:
    key = jax.random.key(seed)
    ks = jax.random.split(key, 24)
    f32 = jnp.float32

    def nrm(k, shape, scale=1.0):
        return jax.random.normal(k, shape, f32) * scale

    def gain(k, shape):
        return 1.0 + 0.02 * jax.random.normal(k, shape, f32)

    return {
        'x_prompt': nrm(ks[0], (BATCH, SEQ, D_MODEL)),
        'x_sample': nrm(ks[1], (DEC_BATCH, DEC_SEQ, D_MODEL)),
        'cache_mla_ckv': nrm(ks[2], (DEPTH, DEC_BATCH, PAST_LEN, MLA_KV_LORA)),
        'cache_mla_krope': nrm(ks[3], (DEPTH, DEC_BATCH, PAST_LEN, MLA_ROPE)),
        'cache_diff_k': nrm(ks[4], (DEPTH, DEC_BATCH, PAST_LEN, DIFF_HEADS, 2 * DIFF_DH)),
        'cache_diff_v': nrm(ks[5], (DEPTH, DEC_BATCH, PAST_LEN, DIFF_HEADS, DIFF_VD)),
        'norm_mix': gain(ks[6], (DEPTH, D_MODEL)),
        'w_in': nrm(ks[7], (DEPTH, D_MODEL, IN_COLS), D_MODEL ** -0.5),
        'mla_q_norm': gain(ks[8], (DEPTH, MLA_Q_LORA)),
        'mla_w_uq': nrm(ks[9], (DEPTH, MLA_Q_LORA, MLA_HEADS * (MLA_NOPE + MLA_ROPE)), MLA_Q_LORA ** -0.5),
        'mla_kv_norm': gain(ks[10], (DEPTH, MLA_KV_LORA)),
        'mla_w_ukv': nrm(ks[11], (DEPTH, MLA_KV_LORA, MLA_HEADS * (MLA_NOPE + MLA_V)), MLA_KV_LORA ** -0.5),
        'diff_lq1': nrm(ks[12], (DEPTH, DIFF_DH), 0.1),
        'diff_lk1': nrm(ks[13], (DEPTH, DIFF_DH), 0.1),
        'diff_lq2': nrm(ks[14], (DEPTH, DIFF_DH), 0.1),
        'diff_lk2': nrm(ks[15], (DEPTH, DIFF_DH), 0.1),
        'diff_subln': gain(ks[16], (DEPTH, DIFF_VD)),
        'w_branch_a': nrm(ks[17], (DEPTH, MLA_HEADS * MLA_V, D_MODEL), (MLA_HEADS * MLA_V) ** -0.5),
        'w_branch_b': nrm(ks[18], (DEPTH, DIFF_HEADS * DIFF_VD, D_MODEL), (DIFF_HEADS * DIFF_VD) ** -0.5),
        'w_out': nrm(ks[19], (DEPTH, D_MODEL, D_MODEL), D_MODEL ** -0.5),
        'norm_ffn': gain(ks[20], (DEPTH, D_MODEL)),
        'w_ffn_in': nrm(ks[21], (DEPTH, D_MODEL, 2 * D_FF), D_MODEL ** -0.5),
        'w_ffn_out': nrm(ks[22], (DEPTH, D_FF, D_MODEL), D_FF ** -0.5),
        'norm_final': gain(ks[23], (D_MODEL,)),
    }


def reference(x_prompt, x_sample, cache_mla_ckv, cache_mla_krope, cache_diff_k, cache_diff_v,
              norm_mix, w_in, mla_q_norm, mla_w_uq, mla_kv_norm, mla_w_ukv,
              diff_lq1, diff_lk1, diff_lq2, diff_lk2, diff_subln,
              w_branch_a, w_branch_b, w_out, norm_ffn, w_ffn_in, w_ffn_out, norm_final):
    pos_p = jnp.arange(x_prompt.shape[1])
    pos_s = cache_mla_ckv.shape[2] + jnp.arange(x_sample.shape[1])
    xp, xs = x_prompt, x_sample
    st_p = ([], [], [], [])
    st_s = ([], [], [], [])
    for l in range(DEPTH):
        lambda_init = 0.8 - 0.6 * math.exp(-0.3 * l)
        lw = {
            'norm_mix': norm_mix[l], 'w_in': w_in[l],
            'mla_q_norm': mla_q_norm[l], 'mla_w_uq': mla_w_uq[l],
            'mla_kv_norm': mla_kv_norm[l], 'mla_w_ukv': mla_w_ukv[l],
            'diff_lq1': diff_lq1[l], 'diff_lk1': diff_lk1[l],
            'diff_lq2': diff_lq2[l], 'diff_lk2': diff_lk2[l], 'diff_subln': diff_subln[l],
            'w_branch_a': w_branch_a[l], 'w_branch_b': w_branch_b[l], 'w_out': w_out[l],
            'norm_ffn': norm_ffn[l], 'w_ffn_in': w_ffn_in[l], 'w_ffn_out': w_ffn_out[l],
        }
        xp, rows_p = _layer(xp, pos_p, None, lw, lambda_init, True)
        past = (cache_mla_ckv[l], cache_mla_krope[l], cache_diff_k[l], cache_diff_v[l])
        xs, rows_s = _layer(xs, pos_s, past, lw, lambda_init, False)
        for i in range(4):
            st_p[i].append(rows_p[i])
            st_s[i].append(rows_s[i])
    y_prompt = _rmsnorm(xp, norm_final)
    y_sample = _rmsnorm(xs, norm_final)
    new_ckv_p = jnp.stack(st_p[0], axis=0)
    new_krope_p = jnp.stack(st_p[1], axis=0)
    new_dk_p = jnp.stack(st_p[2], axis=0)
    new_dv_p = jnp.stack(st_p[3], axis=0)
    new_ckv_s = jnp.stack(st_s[0], axis=0)
    new_krope_s = jnp.stack(st_s[1], axis=0)
    new_dk_s = jnp.stack(st_s[2], axis=0)
    new_dv_s = jnp.stack(st_s[3], axis=0)
    return (y_prompt, y_sample, new_ckv_p, new_krope_p, new_dk_p, new_dv_p,
            new_ckv_s, new_krope_s, new_dk_s, new_dv_s)
```

```python
import functools
import math

import jax
import jax.numpy as jnp
from jax import lax
from jax.experimental import pallas as pl
from jax.experimental.pallas import tpu as pltpu

F32 = jnp.float32
BF16 = jnp.bfloat16

CHUNK = 64
EPS = 1e-6
NEG_INF = -1e30

MLA_HEADS = 8
MLA_Q_LORA = 512
MLA_KV_LORA = 256
MLA_NOPE = 128
MLA_ROPE = 64
MLA_V = 128
MLA_THETA = 10000.0
MLA_QK_PAD = 256

DIFF_HEADS = 8
DIFF_DH = 64
DIFF_VD = 2 * DIFF_DH
DIFF_ROT = DIFF_DH // 4
ROPE_THETA = 500000.0

LANES = 128
LOG2E = math.log2(math.e)
VMEM_LIMIT = 56 * 1024 * 1024


def _params(*sem):
    return pltpu.CompilerParams(dimension_semantics=sem, vmem_limit_bytes=VMEM_LIMIT)


def _dot(a, b):
    return jnp.dot(a, b, preferred_element_type=F32)


def _dot_nt(a, b):
    return lax.dot_general(a, b, (((1,), (1,)), ((), ())), preferred_element_type=F32)


def _rms(x, g):
    return x * lax.rsqrt(jnp.mean(x * x, axis=-1, keepdims=True) + EPS) * g


def _rope_tile(t, c, s1, s2, half):
    return t * c + pltpu.roll(t, LANES - half, 1) * s1 + pltpu.roll(t, half, 1) * s2


def _norm_kernel(x_ref, g_ref, o_ref):
    o_ref[...] = _rms(x_ref[...], g_ref[...]).astype(o_ref.dtype)


def _norm(x, g, bm):
    m, d = x.shape
    return pl.pallas_call(
        _norm_kernel,
        grid=(m // bm,),
        in_specs=[pl.BlockSpec((bm, d), lambda i: (i, 0)), pl.BlockSpec((1, d), lambda i: (0, 0))],
        out_specs=pl.BlockSpec((bm, d), lambda i: (i, 0)),
        out_shape=jax.ShapeDtypeStruct((m, d), BF16),
        compiler_params=_params("parallel"),
        name="norm_mix",
    )(x, g)


def _qproj_kernel(h_ref, wq_ref, gq_ref, wuq_ref, c_ref, s1_ref, s2_ref, o_ref, *, scale):
    qlat = _rms(_dot(h_ref[...], wq_ref[...]), gq_ref[...]).astype(BF16)
    q = _dot(qlat, wuq_ref[...])
    c, s1, s2 = c_ref[...], s1_ref[...], s2_ref[...]
    for hd in range(MLA_HEADS):
        lo = hd * MLA_QK_PAD
        o_ref[:, lo:lo + LANES] = (q[:, lo:lo + LANES] * scale).astype(BF16)
        t = _rope_tile(q[:, lo + LANES:lo + 2 * LANES], c, s1, s2, MLA_ROPE // 2)
        o_ref[:, lo + LANES:lo + 2 * LANES] = (t * scale).astype(BF16)


def _qproj(h, wq, gq, wuq, tabs, bm, scale):
    m, d = h.shape
    nt = tabs[0].shape[0] // bm
    full = lambda a: pl.BlockSpec(a.shape, lambda i: (0, 0))
    tab = pl.BlockSpec((bm, LANES), lambda i: (i % nt, 0))
    n = wuq.shape[1]
    return pl.pallas_call(
        functools.partial(_qproj_kernel, scale=scale),
        grid=(m // bm,),
        in_specs=[pl.BlockSpec((bm, d), lambda i: (i, 0)), full(wq), full(gq), full(wuq), tab, tab, tab],
        out_specs=pl.BlockSpec((bm, n), lambda i: (i, 0)),
        out_shape=jax.ShapeDtypeStruct((m, n), BF16),
        compiler_params=_params("parallel"),
        name="mla_q_proj",
    )(h, wq, gq, wuq, *tabs)


def _kv_expand(ckv, krope_tile, wukv_ref, kmla_ref, vmla_ref):
    kv = _dot(ckv.astype(BF16), wukv_ref[...])
    kr = krope_tile.astype(BF16)
    for hd in range(MLA_HEADS):
        lo = hd * MLA_QK_PAD
        kmla_ref[:, lo:lo + LANES] = kv[:, hd * MLA_NOPE:(hd + 1) * MLA_NOPE].astype(BF16)
        kmla_ref[:, lo + LANES:lo + 2 * LANES] = kr
    vmla_ref[...] = kv[:, MLA_HEADS * MLA_NOPE:].astype(BF16)


def _kvproj_kernel(h_ref, wkv_ref, gkv_ref, wukv_ref, c_ref, s1_ref, s2_ref,
                   ckv_ref, krope_ref, kmla_ref, vmla_ref):
    z = _dot(h_ref[...], wkv_ref[...])
    ckv = _rms(z[:, :MLA_KV_LORA], gkv_ref[...])
    ckv_ref[...] = ckv
    t = _rope_tile(z[:, MLA_KV_LORA:], c_ref[...], s1_ref[...], s2_ref[...], MLA_ROPE // 2)
    krope_ref[...] = t[:, :MLA_ROPE]
    _kv_expand(ckv, t, wukv_ref, kmla_ref, vmla_ref)


def _kvproj(h, wkv, gkv, wukv, tabs, bm):
    m, d = h.shape
    nt = tabs[0].shape[0] // bm
    full = lambda a: pl.BlockSpec(a.shape, lambda i: (0, 0))
    tab = pl.BlockSpec((bm, LANES), lambda i: (i % nt, 0))
    row = lambda n: pl.BlockSpec((bm, n), lambda i: (i, 0))
    nk, nv = MLA_HEADS * MLA_QK_PAD, MLA_HEADS * MLA_V
    return pl.pallas_call(
        _kvproj_kernel,
        grid=(m // bm,),
        in_specs=[row(d), full(wkv), full(gkv), full(wukv), tab, tab, tab],
        out_specs=[row(MLA_KV_LORA), row(MLA_ROPE), row(nk), row(nv)],
        out_shape=[jax.ShapeDtypeStruct((m, MLA_KV_LORA), F32), jax.ShapeDtypeStruct((m, MLA_ROPE), F32),
                   jax.ShapeDtypeStruct((m, nk), BF16), jax.ShapeDtypeStruct((m, nv), BF16)],
        compiler_params=_params("parallel"),
        name="mla_kv_proj",
    )(h, wkv, gkv, wukv, *tabs)


def _kvcache_kernel(ckv_ref, krope_ref, wukv_ref, kmla_ref, vmla_ref):
    _kv_expand(ckv_ref[...], krope_ref[...], wukv_ref, kmla_ref, vmla_ref)


def _kvcache_expand(ckv, krope_pad, wukv, bm):
    m = ckv.shape[0]
    row = lambda n: pl.BlockSpec((bm, n), lambda i: (i, 0))
    nk, nv = MLA_HEADS * MLA_QK_PAD, MLA_HEADS * MLA_V
    return pl.pallas_call(
        _kvcache_kernel,
        grid=(m // bm,),
        in_specs=[row(MLA_KV_LORA), row(LANES), pl.BlockSpec(wukv.shape, lambda i: (0, 0))],
        out_specs=[row(nk), row(nv)],
        out_shape=[jax.ShapeDtypeStruct((m, nk), BF16), jax.ShapeDtypeStruct((m, nv), BF16)],
        compiler_params=_params("parallel"),
        name="mla_kv_cache_expand",
    )(ckv, krope_pad, wukv)


def _diffqk_kernel(h_ref, wdq_ref, wdk_ref, c_ref, s1_ref, s2_ref,
                   q1_ref, q2_ref, dk_ref, dkb_ref, *, scale):
    h = h_ref[...]
    zq = _dot(h, wdq_ref[...])
    zk = _dot(h, wdk_ref[...])
    c, s1, s2 = c_ref[...], s1_ref[...], s2_ref[...]
    first = lax.broadcasted_iota(jnp.int32, (1, LANES), 1) < DIFF_DH
    for hd in range(DIFF_HEADS):
        sl = slice(hd * LANES, (hd + 1) * LANES)
        q = _rope_tile(zq[:, sl], c, s1, s2, DIFF_ROT // 2) * scale
        q1_ref[:, sl] = jnp.where(first, q, 0.0).astype(BF16)
        q2_ref[:, sl] = jnp.where(first, 0.0, q).astype(BF16)
        k = _rope_tile(zk[:, sl], c, s1, s2, DIFF_ROT // 2)
        dk_ref[:, sl] = k
        dkb_ref[:, sl] = k.astype(BF16)


def _diffqk(h, wdq, wdk, tabs, bm, scale):
    m, d = h.shape
    nt = tabs[0].shape[0] // bm
    n = wdq.shape[1]
    full = lambda a: pl.BlockSpec(a.shape, lambda i: (0, 0))
    tab = pl.BlockSpec((bm, LANES), lambda i: (i % nt, 0))
    row = lambda w: pl.BlockSpec((bm, w), lambda i: (i, 0))
    return pl.pallas_call(
        functools.partial(_diffqk_kernel, scale=scale),
        grid=(m // bm,),
        in_specs=[row(d), full(wdq), full(wdk), tab, tab, tab],
        out_specs=[row(n), row(n), row(n), row(n)],
        out_shape=[jax.ShapeDtypeStruct((m, n), BF16), jax.ShapeDtypeStruct((m, n), BF16),
                   jax.ShapeDtypeStruct((m, n), F32), jax.ShapeDtypeStruct((m, n), BF16)],
        compiler_params=_params("parallel"),
        name="diff_qk_proj",
    )(h, wdq, wdk, *tabs)


def _diffv_kernel(h_ref, wdv_ref, dv_ref, dvb_ref):
    z = _dot(h_ref[...], wdv_ref[...])
    dv_ref[...] = z
    dvb_ref[...] = z.astype(BF16)


def _diffv(h, wdv, bm):
    m, d = h.shape
    n = wdv.shape[1]
    row = lambda w: pl.BlockSpec((bm, w), lambda i: (i, 0))
    return pl.pallas_call(
        _diffv_kernel,
        grid=(m // bm,),
        in_specs=[row(d), pl.BlockSpec(wdv.shape, lambda i: (0, 0))],
        out_specs=[row(n), row(n)],
        out_shape=[jax.ShapeDtypeStruct((m, n), F32), jax.ShapeDtypeStruct((m, n), BF16)],
        compiler_params=_params("parallel"),
        name="diff_v_proj",
    )(h, wdv)


def _merge_kernel(h_ref, oa_ref, ob_ref, wga_ref, wgb_ref, wa_ref, wb_ref, o_ref):
    h = h_ref[...]
    ga = jax.nn.sigmoid(_dot(h, wga_ref[...]))
    gb = jax.nn.sigmoid(_dot(h, wgb_ref[...]))
    ya = _dot(oa_ref[...], wa_ref[...])
    yb = _dot(ob_ref[...], wb_ref[...])
    o_ref[...] = (ga * ya + gb * yb).astype(o_ref.dtype)


def _merge(h, oa, ob, wga, wgb, wa, wb, bm, bn):
    m, d = h.shape
    n = wga.shape[1]
    row = lambda a: pl.BlockSpec((bm, a.shape[1]), lambda i, j: (i, 0))
    col = lambda a: pl.BlockSpec((a.shape[0], bn), lambda i, j: (0, j))
    return pl.pallas_call(
        _merge_kernel,
        grid=(m // bm, n // bn),
        in_specs=[row(h), row(oa), row(ob), col(wga), col(wgb), col(wa), col(wb)],
        out_specs=pl.BlockSpec((bm, bn), lambda i, j: (i, j)),
        out_shape=jax.ShapeDtypeStruct((m, n), BF16),
        compiler_params=_params("parallel", "arbitrary"),
        name="gated_merge",
    )(h, oa, ob, wga, wgb, wa, wb)


def _outproj_kernel(x_ref, mg_ref, wo_ref, g_ref, x2_ref, h2_ref):
    x2 = x_ref[...] + _dot(mg_ref[...], wo_ref[...])
    x2_ref[...] = x2
    h2_ref[...] = _rms(x2, g_ref[...]).astype(BF16)


def _outproj(x, mg, wo, g, bm):
    m, d = x.shape
    row = pl.BlockSpec((bm, d), lambda i: (i, 0))
    return pl.pallas_call(
        _outproj_kernel,
        grid=(m // bm,),
        in_specs=[row, row, pl.BlockSpec(wo.shape, lambda i: (0, 0)), pl.BlockSpec((1, d), lambda i: (0, 0))],
        out_specs=[row, row],
        out_shape=[jax.ShapeDtypeStruct((m, d), F32), jax.ShapeDtypeStruct((m, d), BF16)],
        compiler_params=_params("parallel"),
        name="out_proj_residual",
    )(x, mg, wo, g)


def _ffn_kernel(h2_ref, wg_ref, wu_ref, wd_ref, x2_ref, gf_ref, y_ref, acc_ref):
    f = pl.program_id(1)

    @pl.when(f == 0)
    def _():
        acc_ref[...] = jnp.zeros_like(acc_ref)

    h2 = h2_ref[...]
    a = (jax.nn.silu(_dot(h2, wg_ref[...])) * _dot(h2, wu_ref[...])).astype(BF16)
    acc_ref[...] += _dot(a, wd_ref[...])

    @pl.when(f == pl.num_programs(1) - 1)
    def _():
        y_ref[...] = _rms(x2_ref[...] + acc_ref[...], gf_ref[...])


def _ffn(h2, w_in, w_out, x2, gf, bm, bf):
    m, d = h2.shape
    dff = w_out.shape[0]
    nf = dff // bf
    row = pl.BlockSpec((bm, d), lambda i, f: (i, 0))
    return pl.pallas_call(
        _ffn_kernel,
        grid=(m // bm, nf),
        in_specs=[row,
                  pl.BlockSpec((d, bf), lambda i, f: (0, f)),
                  pl.BlockSpec((d, bf), lambda i, f: (0, f + nf)),
                  pl.BlockSpec((bf, d), lambda i, f: (f, 0)),
                  row,
                  pl.BlockSpec((1, d), lambda i, f: (0, 0))],
        out_specs=row,
        out_shape=jax.ShapeDtypeStruct((m, d), F32),
        scratch_shapes=[pltpu.VMEM((bm, d), F32)],
        compiler_params=_params("parallel", "arbitrary"),
        name="ffn_swiglu_final_norm",
    )(h2, w_in, w_in, w_out, x2, gf)


def _softmax_step(s, v, m, l, acc):
    m_new = jnp.maximum(m, jnp.max(s, axis=-1, keepdims=True))
    alpha = jnp.exp2(m - m_new)
    p = jnp.exp2(s - m_new)
    l = alpha * l + jnp.sum(p, axis=-1, keepdims=True)
    acc = alpha * acc + _dot(p.astype(BF16), v)
    return m_new, l, acc


def _chunk_mask(tq, tk, q0, k0):
    qp = q0 + lax.broadcasted_iota(jnp.int32, (tq, tk), 0)
    kp = k0 + lax.broadcasted_iota(jnp.int32, (tq, tk), 1)
    return (kp // CHUNK) <= (qp // CHUNK)


def _softmax_init(tq, dv):
    return (jnp.full((tq, 1), NEG_INF, F32), jnp.zeros((tq, 1), F32), jnp.zeros((tq, dv), F32))


def _mla_prompt_kernel(q_ref, k_ref, v_ref, o_ref, *, t):
    i = pl.program_id(2)
    q = q_ref[0]

    def body(j, carry):
        off = pl.multiple_of(j * t, t)
        s = _dot_nt(q, k_ref[0, pl.ds(off, t), :])
        return _softmax_step(s, v_ref[0, pl.ds(off, t), :], *carry)

    carry = lax.fori_loop(0, i, body, _softmax_init(t, MLA_V))
    off = pl.multiple_of(i * t, t)
    s = _dot_nt(q, k_ref[0, pl.ds(off, t), :])
    s = jnp.where(_chunk_mask(t, t, 0, 0), s, NEG_INF)
    _, l, acc = _softmax_step(s, v_ref[0, pl.ds(off, t), :], *carry)
    o_ref[0] = (acc / l).astype(o_ref.dtype)


def _mla_prompt(q, k, v, t):
    b, l, _ = q.shape
    return pl.pallas_call(
        functools.partial(_mla_prompt_kernel, t=t),
        grid=(b, MLA_HEADS, l // t),
        in_specs=[pl.BlockSpec((1, t, MLA_QK_PAD), lambda b, h, i: (b, i, h)),
                  pl.BlockSpec((1, l, MLA_QK_PAD), lambda b, h, i: (b, 0, h)),
                  pl.BlockSpec((1, l, MLA_V), lambda b, h, i: (b, 0, h))],
        out_specs=pl.BlockSpec((1, t, MLA_V), lambda b, h, i: (b, i, h)),
        out_shape=jax.ShapeDtypeStruct((b, l, MLA_HEADS * MLA_V), BF16),
        compiler_params=_params("parallel", "parallel", "arbitrary"),
        name="mla_attention_prompt",
    )(q, k, v)


def _lambda(lq1_ref, lk1_ref, lq2_ref, lk2_ref, lambda_init):
    a = jnp.sum(lq1_ref[...] * lk1_ref[...], axis=-1, keepdims=True)
    b = jnp.sum(lq2_ref[...] * lk2_ref[...], axis=-1, keepdims=True)
    return jnp.exp(a) - jnp.exp(b) + lambda_init


def _diff_finish(c1, c2, lam, g, lambda_init):
    o = c1[2] / c1[1] - lam * (c2[2] / c2[1])
    return _rms(o, g) * (1.0 - lambda_init)


def _diff_prompt_kernel(q1_ref, q2_ref, k_ref, v_ref, lq1_ref, lk1_ref, lq2_ref, lk2_ref, g_ref, o_ref,
                        *, t, lambda_init):
    i = pl.program_id(2)
    q1, q2 = q1_ref[0], q2_ref[0]

    def body(j, carry):
        off = pl.multiple_of(j * t, t)
        k = k_ref[0, pl.ds(off, t), :]
        v = v_ref[0, pl.ds(off, t), :]
        return (_softmax_step(_dot_nt(q1, k), v, *carry[0]), _softmax_step(_dot_nt(q2, k), v, *carry[1]))

    init = _softmax_init(t, DIFF_VD)
    c1, c2 = lax.fori_loop(0, i, body, (init, init))
    off = pl.multiple_of(i * t, t)
    k = k_ref[0, pl.ds(off, t), :]
    v = v_ref[0, pl.ds(off, t), :]
    mask = _chunk_mask(t, t, 0, 0)
    c1 = _softmax_step(jnp.where(mask, _dot_nt(q1, k), NEG_INF), v, *c1)
    c2 = _softmax_step(jnp.where(mask, _dot_nt(q2, k), NEG_INF), v, *c2)
    lam = _lambda(lq1_ref, lk1_ref, lq2_ref, lk2_ref, lambda_init)
    o_ref[0] = _diff_finish(c1, c2, lam, g_ref[...], lambda_init).astype(o_ref.dtype)


def _diff_prompt(q1, q2, k, v, lams, g, t, lambda_init):
    b, l, _ = q1.shape
    qs = pl.BlockSpec((1, t, LANES), lambda b, h, i: (b, i, h))
    kv = pl.BlockSpec((1, l, LANES), lambda b, h, i: (b, 0, h))
    small = lambda a: pl.BlockSpec(a.shape, lambda b, h, i: (0, 0))
    return pl.pallas_call(
        functools.partial(_diff_prompt_kernel, t=t, lambda_init=lambda_init),
        grid=(b, DIFF_HEADS, l // t),
        in_specs=[qs, qs, kv, kv] + [small(a) for a in lams] + [small(g)],
        out_specs=qs,
        out_shape=jax.ShapeDtypeStruct((b, l, DIFF_HEADS * DIFF_VD), BF16),
        compiler_params=_params("parallel", "parallel", "arbitrary"),
        name="diff_attention_prompt",
    )(q1, q2, k, v, *lams, g)


def _two_part_softmax(q, kc, vc, kn, vn, mask_c, mask_n):
    tq = q.shape[0]
    carry = _softmax_init(tq, vc.shape[1])
    carry = _softmax_step(jnp.where(mask_c, _dot_nt(q, kc), NEG_INF), vc, *carry)
    return _softmax_step(jnp.where(mask_n, _dot_nt(q, kn), NEG_INF), vn, *carry)


def _mla_sample_kernel(q_ref, kc_ref, vc_ref, kn_ref, vn_ref, o_ref, *, past):
    tq, tc = q_ref.shape[1], kc_ref.shape[1]
    _, l, acc = _two_part_softmax(q_ref[0], kc_ref[0], vc_ref[0], kn_ref[0], vn_ref[0],
                                  _chunk_mask(tq, tc, past, 0), _chunk_mask(tq, tq, past, past))
    o_ref[0] = (acc / l).astype(o_ref.dtype)


def _mla_sample(q, kc, vc, kn, vn):
    b, tq, _ = q.shape
    tc = kc.shape[1]
    blk = lambda rows, w: pl.BlockSpec((1, rows, w), lambda b, h: (b, 0, h))
    return pl.pallas_call(
        functools.partial(_mla_sample_kernel, past=tc),
        grid=(b, MLA_HEADS),
        in_specs=[blk(tq, MLA_QK_PAD), blk(tc, MLA_QK_PAD), blk(tc, MLA_V), blk(tq, MLA_QK_PAD), blk(tq, MLA_V)],
        out_specs=blk(tq, MLA_V),
        out_shape=jax.ShapeDtypeStruct((b, tq, MLA_HEADS * MLA_V), BF16),
        compiler_params=_params("parallel", "parallel"),
        name="mla_attention_sample",
    )(q, kc, vc, kn, vn)


def _diff_sample_kernel(q1_ref, q2_ref, kc_ref, vc_ref, kn_ref, vn_ref,
                        lq1_ref, lk1_ref, lq2_ref, lk2_ref, g_ref, o_ref, *, past, lambda_init):
    tq, tc = q1_ref.shape[1], kc_ref.shape[1]
    kc, vc = kc_ref[0].astype(BF16), vc_ref[0].astype(BF16)
    kn, vn = kn_ref[0], vn_ref[0]
    mask_c, mask_n = _chunk_mask(tq, tc, past, 0), _chunk_mask(tq, tq, past, past)
    c1 = _two_part_softmax(q1_ref[0], kc, vc, kn, vn, mask_c, mask_n)
    c2 = _two_part_softmax(q2_ref[0], kc, vc, kn, vn, mask_c, mask_n)
    lam = _lambda(lq1_ref, lk1_ref, lq2_ref, lk2_ref, lambda_init)
    o_ref[0] = _diff_finish(c1, c2, lam, g_ref[...], lambda_init).astype(o_ref.dtype)


def _diff_sample(q1, q2, kc, vc, kn, vn, lams, g, lambda_init):
    b, tq, _ = q1.shape
    tc = kc.shape[1]
    blk = lambda rows: pl.BlockSpec((1, rows, LANES), lambda b, h: (b, 0, h))
    small = lambda a: pl.BlockSpec(a.shape, lambda b, h: (0, 0))
    return pl.pallas_call(
        functools.partial(_diff_sample_kernel, past=tc, lambda_init=lambda_init),
        grid=(b, DIFF_HEADS),
        in_specs=[blk(tq), blk(tq), blk(tc), blk(tc), blk(tq), blk(tq)] + [small(a) for a in lams] + [small(g)],
        out_specs=blk(tq),
        out_shape=jax.ShapeDtypeStruct((b, tq, DIFF_HEADS * DIFF_VD), BF16),
        compiler_params=_params("parallel", "parallel"),
        name="diff_attention_sample",
    )(q1, q2, kc, vc, kn, vn, *lams, g)


def _rope_tables(pos, theta, rot_dim, period):
    half = rot_dim // 2
    inv = 1.0 / (jnp.float32(theta) ** (jnp.arange(half, dtype=F32) / half))
    ang = pos.astype(F32)[:, None] * inv[None, :]
    cos, sin = jnp.cos(ang), jnp.sin(ang)
    n = pos.shape[0]
    rest = period - rot_dim
    c = jnp.concatenate([cos, cos, jnp.ones((n, rest), F32)], axis=1)
    s1 = jnp.concatenate([-sin, jnp.zeros((n, half + rest), F32)], axis=1)
    s2 = jnp.concatenate([jnp.zeros((n, half), F32), sin, jnp.zeros((n, rest), F32)], axis=1)
    reps = LANES // period
    return tuple(jnp.tile(a, (1, reps)) for a in (c, s1, s2))


def _layer_weights(w_in, mla_w_uq, mla_w_ukv):
    c_kv = MLA_Q_LORA
    c_kr = c_kv + MLA_KV_LORA
    c_dq = c_kr + MLA_ROPE
    dqk = DIFF_HEADS * 2 * DIFF_DH
    c_dk = c_dq + dqk
    c_dv = c_dk + dqk
    c_ga = c_dv + DIFF_HEADS * DIFF_VD
    d = w_in.shape[0]
    c_gb = c_ga + d
    wb = w_in.astype(BF16)
    wq = wb[:, :c_kv]
    wkv = jnp.pad(wb[:, c_kv:c_dq], ((0, 0), (0, LANES - MLA_ROPE)))
    wdq, wdk, wdv = wb[:, c_dq:c_dk], wb[:, c_dk:c_dv], wb[:, c_dv:c_ga]
    wga, wgb = wb[:, c_ga:c_gb], wb[:, c_gb:]
    uq = mla_w_uq.astype(BF16).reshape(MLA_Q_LORA, MLA_HEADS, MLA_NOPE + MLA_ROPE)
    uq = jnp.pad(uq, ((0, 0), (0, 0), (0, MLA_QK_PAD - MLA_NOPE - MLA_ROPE))).reshape(MLA_Q_LORA, -1)
    ukv = mla_w_ukv.astype(BF16).reshape(MLA_KV_LORA, MLA_HEADS, MLA_NOPE + MLA_V)
    ukv = jnp.concatenate([ukv[:, :, :MLA_NOPE].reshape(MLA_KV_LORA, -1),
                           ukv[:, :, MLA_NOPE:].reshape(MLA_KV_LORA, -1)], axis=1)
    return wq, wkv, wdq, wdk, wdv, wga, wgb, uq, ukv


def _block(m, want):
    return want if m % want == 0 else m


def kernel(x_prompt, x_sample, cache_mla_ckv, cache_mla_krope, cache_diff_k, cache_diff_v, norm_mix, w_in,
           mla_q_norm, mla_w_uq, mla_kv_norm, mla_w_ukv, diff_lq1, diff_lk1, diff_lq2, diff_lk2, diff_subln,
           w_branch_a, w_branch_b, w_out, norm_ffn, w_ffn_in, w_ffn_out, norm_final):
    bp, lp, d = x_prompt.shape
    bs, ls, _ = x_sample.shape
    depth, _, past, _ = cache_mla_ckv.shape
    assert depth == 1, "the FFN kernel fuses the final norm, so it serves the last (only) layer"
    mp, ms = bp * lp, bs * ls
    t_attn = 512

    pos_p = jnp.arange(lp)
    pos_s = past + jnp.arange(ls)
    bm_p, bm_s = _block(mp, 512), ms
    tabs_mla_p = _rope_tables(pos_p, MLA_THETA, MLA_ROPE, LANES)
    tabs_mla_s = tuple(jnp.tile(a, (bs, 1)) for a in _rope_tables(pos_s, MLA_THETA, MLA_ROPE, LANES))
    tabs_dif_p = _rope_tables(pos_p, ROPE_THETA, DIFF_ROT, DIFF_DH)
    tabs_dif_s = tuple(jnp.tile(a, (bs, 1)) for a in _rope_tables(pos_s, ROPE_THETA, DIFF_ROT, DIFF_DH))
    q_scale_mla = (MLA_NOPE + MLA_ROPE) ** -0.5 * LOG2E
    q_scale_dif = DIFF_DH ** -0.5 * LOG2E
    gfinal = norm_final.reshape(1, d)

    xp = x_prompt.reshape(mp, d)
    xs = x_sample.reshape(ms, d)
    rows_p, rows_s = [], []
    for l in range(depth):
        lambda_init = 0.8 - 0.6 * math.exp(-0.3 * l)
        wq, wkv, wdq, wdk, wdv, wga, wgb, uq, ukv = _layer_weights(w_in[l], mla_w_uq[l], mla_w_ukv[l])
        gmix, gq, gkv = norm_mix[l].reshape(1, -1), mla_q_norm[l].reshape(1, -1), mla_kv_norm[l].reshape(1, -1)
        lams = tuple(a[l].reshape(1, -1) for a in (diff_lq1, diff_lk1, diff_lq2, diff_lk2))
        gsub = diff_subln[l].reshape(1, -1)
        wa, wbr, wo = w_branch_a[l].astype(BF16), w_branch_b[l].astype(BF16), w_out[l].astype(BF16)
        wfi, wfo = w_ffn_in[l].astype(BF16), w_ffn_out[l].astype(BF16)
        gffn = norm_ffn[l].reshape(1, -1)

        def stage1(x, bm, tabs_mla, tabs_dif):
            h = _norm(x, gmix, bm)
            q = _qproj(h, wq, gq, uq, tabs_mla, bm, q_scale_mla)
            ckv, krope, kmla, vmla = _kvproj(h, wkv, gkv, ukv, tabs_mla, bm)
            q1, q2, dk, dkb = _diffqk(h, wdq, wdk, tabs_dif, bm, q_scale_dif)
            dv, dvb = _diffv(h, wdv, bm)
            return h, q, ckv, krope, kmla, vmla, q1, q2, dk, dkb, dv, dvb

        def stage2(x, h, oa, ob, bm, bn, bf):
            mg = _merge(h, oa, ob, wga, wgb, wa, wbr, bm, bn)
            x2, h2 = _outproj(x, mg, wo, gffn, min(bm, 256))
            return _ffn(h2, wfi, wfo, x2, gfinal, min(bm, 512), bf)

        h, q, ckv, krope, kmla, vmla, q1, q2, dk, dkb, dv, dvb = stage1(xp, bm_p, tabs_mla_p, tabs_dif_p)
        r3 = lambda a: a.reshape(bp, lp, -1)
        oa = _mla_prompt(r3(q), r3(kmla), r3(vmla), t_attn)
        ob = _diff_prompt(r3(q1), r3(q2), r3(dkb), r3(dvb), lams, gsub, t_attn, lambda_init)
        xp = stage2(xp, h, oa.reshape(mp, -1), ob.reshape(mp, -1), _block(mp, 1024), 512, 512)
        rows_p.append((ckv.reshape(bp, lp, -1), krope.reshape(bp, lp, -1),
                       dk.reshape(bp, lp, DIFF_HEADS, -1), dv.reshape(bp, lp, DIFF_HEADS, -1)))

        h, q, ckv, krope, kmla, vmla, q1, q2, dk, dkb, dv, dvb = stage1(xs, bm_s, tabs_mla_s, tabs_dif_s)
        kr_pad = jnp.pad(cache_mla_krope[l].reshape(bs * past, MLA_ROPE), ((0, 0), (0, LANES - MLA_ROPE)))
        kc, vc = _kvcache_expand(cache_mla_ckv[l].reshape(bs * past, -1), kr_pad, ukv, _block(bs * past, 512))
        r3 = lambda a: a.reshape(bs, ls, -1)
        c3 = lambda a: a.reshape(bs, past, -1)
        oa = _mla_sample(r3(q), c3(kc), c3(vc), r3(kmla), r3(vmla))
        ob = _diff_sample(r3(q1), r3(q2), c3(cache_diff_k[l]), c3(cache_diff_v[l]), r3(dkb), r3(dvb),
                          lams, gsub, lambda_init)
        xs = stage2(xs, h, oa.reshape(ms, -1), ob.reshape(ms, -1), ms, 512, 512)
        rows_s.append((ckv.reshape(bs, ls, -1), krope.reshape(bs, ls, -1),
                       dk.reshape(bs, ls, DIFF_HEADS, -1), dv.reshape(bs, ls, DIFF_HEADS, -1)))

    y_prompt = xp.reshape(bp, lp, d)
    y_sample = xs.reshape(bs, ls, d)
    stack = lambda rows, i: jnp.stack([r[i] for r in rows], axis=0)
    return (y_prompt, y_sample,
            stack(rows_p, 0), stack(rows_p, 1), stack(rows_p, 2), stack(rows_p, 3),
            stack(rows_s, 0), stack(rows_s, 1), stack(rows_s, 2), stack(rows_s, 3))
```

```python
import functools
import math

import jax
import jax.numpy as jnp
from jax import lax
from jax.experimental import pallas as pl
from jax.experimental.pallas import tpu as pltpu

F32 = jnp.float32
BF16 = jnp.bfloat16

CHUNK = 64
EPS = 1e-6
NEG_INF = -1e30

MLA_HEADS = 8
MLA_Q_LORA = 512
MLA_KV_LORA = 256
MLA_NOPE = 128
MLA_ROPE = 64
MLA_V = 128
MLA_THETA = 10000.0
MLA_QK_PAD = 256

DIFF_HEADS = 8
DIFF_DH = 64
DIFF_VD = 2 * DIFF_DH
DIFF_ROT = DIFF_DH // 4
ROPE_THETA = 500000.0

LANES = 128
LOG2E = math.log2(math.e)
VMEM_LIMIT = 56 * 1024 * 1024


def _params(*sem):
    return pltpu.CompilerParams(dimension_semantics=sem, vmem_limit_bytes=VMEM_LIMIT)


def _dot(a, b):
    return jnp.dot(a, b, preferred_element_type=F32)


def _dot_nt(a, b):
    return lax.dot_general(a, b, (((1,), (1,)), ((), ())), preferred_element_type=F32)


def _rms(x, g):
    return x * lax.rsqrt(jnp.mean(x * x, axis=-1, keepdims=True) + EPS) * g


def _rope_tile(t, c, s1, s2, half):
    return t * c + pltpu.roll(t, LANES - half, 1) * s1 + pltpu.roll(t, half, 1) * s2


def _norm_kernel(x_ref, g_ref, o_ref):
    o_ref[...] = _rms(x_ref[...], g_ref[...]).astype(o_ref.dtype)


def _norm(x, g, bm):
    m, d = x.shape
    return pl.pallas_call(
        _norm_kernel,
        grid=(m // bm,),
        in_specs=[pl.BlockSpec((bm, d), lambda i: (i, 0)), pl.BlockSpec((1, d), lambda i: (0, 0))],
        out_specs=pl.BlockSpec((bm, d), lambda i: (i, 0)),
        out_shape=jax.ShapeDtypeStruct((m, d), BF16),
        compiler_params=_params("parallel"),
        name="norm_mix",
    )(x, g)


def _put(ref, lo, val, tr):
    w = val.shape[1]
    if tr:
        ref[0, lo:lo + w, :] = val.T.astype(ref.dtype)
    else:
        ref[:, lo:lo + w] = val.astype(ref.dtype)


def _out(m, n, bm, dtype, tr):
    if tr:
        return pl.BlockSpec((1, n, bm), lambda i: (i, 0, 0)), jax.ShapeDtypeStruct((m // bm, n, bm), dtype)
    return pl.BlockSpec((bm, n), lambda i: (i, 0)), jax.ShapeDtypeStruct((m, n), dtype)


def _qproj_kernel(h_ref, wq_ref, gq_ref, wuq_ref, c_ref, s1_ref, s2_ref, o_ref, *, scale, tr):
    qlat = _rms(_dot(h_ref[...], wq_ref[...]), gq_ref[...]).astype(BF16)
    q = _dot(qlat, wuq_ref[...])
    c, s1, s2 = c_ref[...], s1_ref[...], s2_ref[...]
    for hd in range(MLA_HEADS):
        lo = hd * MLA_QK_PAD
        _put(o_ref, lo, q[:, lo:lo + LANES] * scale, tr)
        t = _rope_tile(q[:, lo + LANES:lo + 2 * LANES], c, s1, s2, MLA_ROPE // 2)
        _put(o_ref, lo + LANES, t * scale, tr)


def _qproj(h, wq, gq, wuq, tabs, bm, scale, tr):
    m, d = h.shape
    nt = tabs[0].shape[0] // bm
    full = lambda a: pl.BlockSpec(a.shape, lambda i: (0, 0))
    tab = pl.BlockSpec((bm, LANES), lambda i: (i % nt, 0))
    spec, shape = _out(m, wuq.shape[1], bm, BF16, tr)
    return pl.pallas_call(
        functools.partial(_qproj_kernel, scale=scale, tr=tr),
        grid=(m // bm,),
        in_specs=[pl.BlockSpec((bm, d), lambda i: (i, 0)), full(wq), full(gq), full(wuq), tab, tab, tab],
        out_specs=spec,
        out_shape=shape,
        compiler_params=_params("parallel"),
        name="mla_q_proj",
    )(h, wq, gq, wuq, *tabs)


def _kv_expand(ckv, krope_tile, wukv_ref, kmla_ref, vmla_ref, tr):
    kv = _dot(ckv.astype(BF16), wukv_ref[...])
    kr = krope_tile.astype(BF16)
    for hd in range(MLA_HEADS):
        lo = hd * MLA_QK_PAD
        kmla_ref[:, lo:lo + LANES] = kv[:, hd * MLA_NOPE:(hd + 1) * MLA_NOPE].astype(BF16)
        kmla_ref[:, lo + LANES:lo + 2 * LANES] = kr
        vlo = MLA_HEADS * MLA_NOPE + hd * MLA_V
        _put(vmla_ref, hd * MLA_V, kv[:, vlo:vlo + MLA_V], tr)


def _kvproj_kernel(h_ref, wkv_ref, gkv_ref, wukv_ref, c_ref, s1_ref, s2_ref,
                   ckv_ref, krope_ref, kmla_ref, vmla_ref, *, tr):
    z = _dot(h_ref[...], wkv_ref[...])
    ckv = _rms(z[:, :MLA_KV_LORA], gkv_ref[...])
    ckv_ref[...] = ckv
    t = _rope_tile(z[:, MLA_KV_LORA:], c_ref[...], s1_ref[...], s2_ref[...], MLA_ROPE // 2)
    krope_ref[...] = t[:, :MLA_ROPE]
    _kv_expand(ckv, t, wukv_ref, kmla_ref, vmla_ref, tr)


def _kvproj(h, wkv, gkv, wukv, tabs, bm, tr):
    m, d = h.shape
    nt = tabs[0].shape[0] // bm
    full = lambda a: pl.BlockSpec(a.shape, lambda i: (0, 0))
    tab = pl.BlockSpec((bm, LANES), lambda i: (i % nt, 0))
    row = lambda n: pl.BlockSpec((bm, n), lambda i: (i, 0))
    nk, nv = MLA_HEADS * MLA_QK_PAD, MLA_HEADS * MLA_V
    vspec, vshape = _out(m, nv, bm, BF16, tr)
    return pl.pallas_call(
        functools.partial(_kvproj_kernel, tr=tr),
        grid=(m // bm,),
        in_specs=[row(d), full(wkv), full(gkv), full(wukv), tab, tab, tab],
        out_specs=[row(MLA_KV_LORA), row(MLA_ROPE), row(nk), vspec],
        out_shape=[jax.ShapeDtypeStruct((m, MLA_KV_LORA), F32), jax.ShapeDtypeStruct((m, MLA_ROPE), F32),
                   jax.ShapeDtypeStruct((m, nk), BF16), vshape],
        compiler_params=_params("parallel"),
        name="mla_kv_proj",
    )(h, wkv, gkv, wukv, *tabs)


def _kvcache_kernel(ckv_ref, krope_ref, wukv_ref, kmla_ref, vmla_ref):
    _kv_expand(ckv_ref[...], krope_ref[...], wukv_ref, kmla_ref, vmla_ref, False)


def _kvcache_expand(ckv, krope_pad, wukv, bm):
    m = ckv.shape[0]
    row = lambda n: pl.BlockSpec((bm, n), lambda i: (i, 0))
    nk, nv = MLA_HEADS * MLA_QK_PAD, MLA_HEADS * MLA_V
    return pl.pallas_call(
        _kvcache_kernel,
        grid=(m // bm,),
        in_specs=[row(MLA_KV_LORA), row(LANES), pl.BlockSpec(wukv.shape, lambda i: (0, 0))],
        out_specs=[row(nk), row(nv)],
        out_shape=[jax.ShapeDtypeStruct((m, nk), BF16), jax.ShapeDtypeStruct((m, nv), BF16)],
        compiler_params=_params("parallel"),
        name="mla_kv_cache_expand",
    )(ckv, krope_pad, wukv)


def _diffqk_kernel(h_ref, wdq_ref, wdk_ref, c_ref, s1_ref, s2_ref,
                   q1_ref, q2_ref, dk_ref, dkb_ref, *, scale, tr):
    h = h_ref[...]
    zq = _dot(h, wdq_ref[...])
    zk = _dot(h, wdk_ref[...])
    c, s1, s2 = c_ref[...], s1_ref[...], s2_ref[...]
    first = lax.broadcasted_iota(jnp.int32, (1, LANES), 1) < DIFF_DH
    for hd in range(DIFF_HEADS):
        sl = slice(hd * LANES, (hd + 1) * LANES)
        q = _rope_tile(zq[:, sl], c, s1, s2, DIFF_ROT // 2) * scale
        _put(q1_ref, hd * LANES, jnp.where(first, q, 0.0), tr)
        _put(q2_ref, hd * LANES, jnp.where(first, 0.0, q), tr)
        k = _rope_tile(zk[:, sl], c, s1, s2, DIFF_ROT // 2)
        dk_ref[:, sl] = k
        dkb_ref[:, sl] = k.astype(BF16)


def _diffqk(h, wdq, wdk, tabs, bm, scale, tr):
    m, d = h.shape
    nt = tabs[0].shape[0] // bm
    n = wdq.shape[1]
    full = lambda a: pl.BlockSpec(a.shape, lambda i: (0, 0))
    tab = pl.BlockSpec((bm, LANES), lambda i: (i % nt, 0))
    row = lambda w: pl.BlockSpec((bm, w), lambda i: (i, 0))
    qspec, qshape = _out(m, n, bm, BF16, tr)
    return pl.pallas_call(
        functools.partial(_diffqk_kernel, scale=scale, tr=tr),
        grid=(m // bm,),
        in_specs=[row(d), full(wdq), full(wdk), tab, tab, tab],
        out_specs=[qspec, qspec, row(n), row(n)],
        out_shape=[qshape, qshape, jax.ShapeDtypeStruct((m, n), F32), jax.ShapeDtypeStruct((m, n), BF16)],
        compiler_params=_params("parallel"),
        name="diff_qk_proj",
    )(h, wdq, wdk, *tabs)


def _diffv_kernel(h_ref, wdv_ref, dv_ref, dvb_ref, *, tr):
    z = _dot(h_ref[...], wdv_ref[...])
    dv_ref[...] = z
    for hd in range(DIFF_HEADS):
        _put(dvb_ref, hd * DIFF_VD, z[:, hd * DIFF_VD:(hd + 1) * DIFF_VD], tr)


def _diffv(h, wdv, bm, tr):
    m, d = h.shape
    n = wdv.shape[1]
    row = lambda w: pl.BlockSpec((bm, w), lambda i: (i, 0))
    vspec, vshape = _out(m, n, bm, BF16, tr)
    return pl.pallas_call(
        functools.partial(_diffv_kernel, tr=tr),
        grid=(m // bm,),
        in_specs=[row(d), pl.BlockSpec(wdv.shape, lambda i: (0, 0))],
        out_specs=[row(n), vspec],
        out_shape=[jax.ShapeDtypeStruct((m, n), F32), vshape],
        compiler_params=_params("parallel"),
        name="diff_v_proj",
    )(h, wdv)


def _merge_kernel(h_ref, oa_ref, ob_ref, wga_ref, wgb_ref, wa_ref, wb_ref, o_ref):
    h = h_ref[...]
    ga = jax.nn.sigmoid(_dot(h, wga_ref[...]))
    gb = jax.nn.sigmoid(_dot(h, wgb_ref[...]))
    ya = _dot(oa_ref[...], wa_ref[...])
    yb = _dot(ob_ref[...], wb_ref[...])
    o_ref[...] = (ga * ya + gb * yb).astype(o_ref.dtype)


def _merge(h, oa, ob, wga, wgb, wa, wb, bm, bn):
    m, d = h.shape
    n = wga.shape[1]
    row = lambda a: pl.BlockSpec((bm, a.shape[1]), lambda i, j: (i, 0))
    col = lambda a: pl.BlockSpec((a.shape[0], bn), lambda i, j: (0, j))
    return pl.pallas_call(
        _merge_kernel,
        grid=(m // bm, n // bn),
        in_specs=[row(h), row(oa), row(ob), col(wga), col(wgb), col(wa), col(wb)],
        out_specs=pl.BlockSpec((bm, bn), lambda i, j: (i, j)),
        out_shape=jax.ShapeDtypeStruct((m, n), BF16),
        compiler_params=_params("parallel", "arbitrary"),
        name="gated_merge",
    )(h, oa, ob, wga, wgb, wa, wb)


def _outproj_kernel(x_ref, mg_ref, wo_ref, g_ref, x2_ref, h2_ref):
    x2 = x_ref[...] + _dot(mg_ref[...], wo_ref[...])
    x2_ref[...] = x2
    h2_ref[...] = _rms(x2, g_ref[...]).astype(BF16)


def _outproj(x, mg, wo, g, bm):
    m, d = x.shape
    row = pl.BlockSpec((bm, d), lambda i: (i, 0))
    return pl.pallas_call(
        _outproj_kernel,
        grid=(m // bm,),
        in_specs=[row, row, pl.BlockSpec(wo.shape, lambda i: (0, 0)), pl.BlockSpec((1, d), lambda i: (0, 0))],
        out_specs=[row, row],
        out_shape=[jax.ShapeDtypeStruct((m, d), F32), jax.ShapeDtypeStruct((m, d), BF16)],
        compiler_params=_params("parallel"),
        name="out_proj_residual",
    )(x, mg, wo, g)


def _ffn_kernel(h2_ref, wg_ref, wu_ref, wd_ref, x2_ref, gf_ref, y_ref, acc_ref):
    f = pl.program_id(1)

    @pl.when(f == 0)
    def _():
        acc_ref[...] = jnp.zeros_like(acc_ref)

    h2 = h2_ref[...]
    a = (jax.nn.silu(_dot(h2, wg_ref[...])) * _dot(h2, wu_ref[...])).astype(BF16)
    acc_ref[...] += _dot(a, wd_ref[...])

    @pl.when(f == pl.num_programs(1) - 1)
    def _():
        y_ref[...] = _rms(x2_ref[...] + acc_ref[...], gf_ref[...])


def _ffn(h2, w_in, w_out, x2, gf, bm, bf):
    m, d = h2.shape
    dff = w_out.shape[0]
    nf = dff // bf
    row = pl.BlockSpec((bm, d), lambda i, f: (i, 0))
    return pl.pallas_call(
        _ffn_kernel,
        grid=(m // bm, nf),
        in_specs=[row,
                  pl.BlockSpec((d, bf), lambda i, f: (0, f)),
                  pl.BlockSpec((d, bf), lambda i, f: (0, f + nf)),
                  pl.BlockSpec((bf, d), lambda i, f: (f, 0)),
                  row,
                  pl.BlockSpec((1, d), lambda i, f: (0, 0))],
        out_specs=row,
        out_shape=jax.ShapeDtypeStruct((m, d), F32),
        scratch_shapes=[pltpu.VMEM((bm, d), F32)],
        compiler_params=_params("parallel", "arbitrary"),
        name="ffn_swiglu_final_norm",
    )(h2, w_in, w_in, w_out, x2, gf)


def _softmax_step(s, v, m, l, acc):
    m_new = jnp.maximum(m, jnp.max(s, axis=-1, keepdims=True))
    alpha = jnp.exp2(m - m_new)
    p = jnp.exp2(s - m_new)
    l = alpha * l + jnp.sum(p, axis=-1, keepdims=True)
    acc = alpha * acc + _dot(p.astype(BF16), v)
    return m_new, l, acc


def _chunk_mask(tq, tk, q0, k0):
    qp = q0 + lax.broadcasted_iota(jnp.int32, (tq, tk), 0)
    kp = k0 + lax.broadcasted_iota(jnp.int32, (tq, tk), 1)
    return (kp // CHUNK) <= (qp // CHUNK)


def _softmax_init(tq, dv):
    return (jnp.full((tq, 1), NEG_INF, F32), jnp.zeros((tq, 1), F32), jnp.zeros((tq, dv), F32))


def _softmax_step_t(s, vt, m, l, acc):
    m_new = jnp.maximum(m, jnp.max(s, axis=0, keepdims=True))
    alpha = jnp.exp2(m - m_new)
    p = jnp.exp2(s - m_new)
    l = alpha * l + jnp.sum(p, axis=0, keepdims=True)
    acc = alpha * acc + _dot(vt, p.astype(BF16))
    return m_new, l, acc


def _softmax_init_t(tq, dv):
    return (jnp.full((1, tq), NEG_INF, F32), jnp.zeros((1, tq), F32), jnp.zeros((dv, tq), F32))


def _chunk_mask_t(t):
    kp = lax.broadcasted_iota(jnp.int32, (t, t), 0)
    qp = lax.broadcasted_iota(jnp.int32, (t, t), 1)
    return (kp // CHUNK) <= (qp // CHUNK)


def _mla_prompt_kernel(qt_ref, k_ref, vt_ref, o_ref, *, t, heads):
    i = pl.program_id(2)

    def step(j, carry, masked):
        off = pl.multiple_of(j * t, t)
        out = []
        for hd in range(heads):
            k = k_ref[0, pl.ds(off, t), hd * MLA_QK_PAD:(hd + 1) * MLA_QK_PAD]
            s = _dot(k, qt_ref[0, hd * MLA_QK_PAD:(hd + 1) * MLA_QK_PAD, :])
            if masked:
                s = jnp.where(_chunk_mask_t(t), s, NEG_INF)
            out.append(_softmax_step_t(s, vt_ref[j, hd * MLA_V:(hd + 1) * MLA_V, :], *carry[hd]))
        return tuple(out)

    init = tuple(_softmax_init_t(t, MLA_V) for _ in range(heads))
    carry = lax.fori_loop(0, i, lambda j, c: step(j, c, False), init)
    carry = step(i, carry, True)
    for hd in range(heads):
        _, l, acc = carry[hd]
        o_ref[0, :, hd * MLA_V:(hd + 1) * MLA_V] = (acc / l).T.astype(o_ref.dtype)


def _mla_prompt(qt, k, vt, t, heads):
    b, l, _ = k.shape
    nq = l // t
    return pl.pallas_call(
        functools.partial(_mla_prompt_kernel, t=t, heads=heads),
        grid=(b, MLA_HEADS // heads, nq),
        in_specs=[pl.BlockSpec((1, heads * MLA_QK_PAD, t), lambda b, h, i: (b * nq + i, h, 0)),
                  pl.BlockSpec((1, l, heads * MLA_QK_PAD), lambda b, h, i: (b, 0, h)),
                  pl.BlockSpec((nq, heads * MLA_V, t), lambda b, h, i: (b, h, 0))],
        out_specs=pl.BlockSpec((1, t, heads * MLA_V), lambda b, h, i: (b, i, h)),
        out_shape=jax.ShapeDtypeStruct((b, l, MLA_HEADS * MLA_V), BF16),
        compiler_params=_params("parallel", "parallel", "arbitrary"),
        name="mla_attention_prompt",
    )(qt, k, vt)


def _lambda(lq1_ref, lk1_ref, lq2_ref, lk2_ref, lambda_init):
    a = jnp.sum(lq1_ref[...] * lk1_ref[...], axis=-1, keepdims=True)
    b = jnp.sum(lq2_ref[...] * lk2_ref[...], axis=-1, keepdims=True)
    return jnp.exp(a) - jnp.exp(b) + lambda_init


def _diff_finish(c1, c2, lam, g, lambda_init):
    o = c1[2] / c1[1] - lam * (c2[2] / c2[1])
    return _rms(o, g) * (1.0 - lambda_init)


def _diff_prompt_kernel(q1t_ref, q2t_ref, k_ref, vt_ref, lq1_ref, lk1_ref, lq2_ref, lk2_ref, g_ref, o_ref,
                        *, t, lambda_init):
    i = pl.program_id(2)

    def step(j, carry, masked):
        off = pl.multiple_of(j * t, t)
        k = k_ref[0, pl.ds(off, t), :]
        vt = vt_ref[j]
        out = []
        for qt_ref, c in zip((q1t_ref, q2t_ref), carry):
            s = _dot(k, qt_ref[0])
            if masked:
                s = jnp.where(_chunk_mask_t(t), s, NEG_INF)
            out.append(_softmax_step_t(s, vt, *c))
        return tuple(out)

    init = tuple(_softmax_init_t(t, DIFF_VD) for _ in range(2))
    carry = lax.fori_loop(0, i, lambda j, c: step(j, c, False), init)
    (_, l1, a1), (_, l2, a2) = step(i, carry, True)
    lam = _lambda(lq1_ref, lk1_ref, lq2_ref, lk2_ref, lambda_init)
    o = (a1 / l1 - lam * (a2 / l2)).T
    o_ref[0] = (_rms(o, g_ref[...]) * (1.0 - lambda_init)).astype(o_ref.dtype)


def _diff_prompt(q1t, q2t, k, vt, lams, g, t, lambda_init):
    b, l, _ = k.shape
    nq = l // t
    qs = pl.BlockSpec((1, LANES, t), lambda b, h, i: (b * nq + i, h, 0))
    small = lambda a: pl.BlockSpec(a.shape, lambda b, h, i: (0, 0))
    return pl.pallas_call(
        functools.partial(_diff_prompt_kernel, t=t, lambda_init=lambda_init),
        grid=(b, DIFF_HEADS, nq),
        in_specs=[qs, qs,
                  pl.BlockSpec((1, l, LANES), lambda b, h, i: (b, 0, h)),
                  pl.BlockSpec((nq, DIFF_VD, t), lambda b, h, i: (b, h, 0))]
                 + [small(a) for a in lams] + [small(g)],
        out_specs=pl.BlockSpec((1, t, DIFF_VD), lambda b, h, i: (b, i, h)),
        out_shape=jax.ShapeDtypeStruct((b, l, DIFF_HEADS * DIFF_VD), BF16),
        compiler_params=_params("parallel", "parallel", "arbitrary"),
        name="diff_attention_prompt",
    )(q1t, q2t, k, vt, *lams, g)


def _two_part_softmax(q, kc, vc, kn, vn, mask_c, mask_n):
    tq = q.shape[0]
    carry = _softmax_init(tq, vc.shape[1])
    carry = _softmax_step(jnp.where(mask_c, _dot_nt(q, kc), NEG_INF), vc, *carry)
    return _softmax_step(jnp.where(mask_n, _dot_nt(q, kn), NEG_INF), vn, *carry)


def _mla_sample_kernel(q_ref, kc_ref, vc_ref, kn_ref, vn_ref, o_ref, *, past):
    tq, tc = q_ref.shape[1], kc_ref.shape[1]
    _, l, acc = _two_part_softmax(q_ref[0], kc_ref[0], vc_ref[0], kn_ref[0], vn_ref[0],
                                  _chunk_mask(tq, tc, past, 0), _chunk_mask(tq, tq, past, past))
    o_ref[0] = (acc / l).astype(o_ref.dtype)


def _mla_sample(q, kc, vc, kn, vn):
    b, tq, _ = q.shape
    tc = kc.shape[1]
    blk = lambda rows, w: pl.BlockSpec((1, rows, w), lambda b, h: (b, 0, h))
    return pl.pallas_call(
        functools.partial(_mla_sample_kernel, past=tc),
        grid=(b, MLA_HEADS),
        in_specs=[blk(tq, MLA_QK_PAD), blk(tc, MLA_QK_PAD), blk(tc, MLA_V), blk(tq, MLA_QK_PAD), blk(tq, MLA_V)],
        out_specs=blk(tq, MLA_V),
        out_shape=jax.ShapeDtypeStruct((b, tq, MLA_HEADS * MLA_V), BF16),
        compiler_params=_params("parallel", "parallel"),
        name="mla_attention_sample",
    )(q, kc, vc, kn, vn)


def _diff_sample_kernel(q1_ref, q2_ref, kc_ref, vc_ref, kn_ref, vn_ref,
                        lq1_ref, lk1_ref, lq2_ref, lk2_ref, g_ref, o_ref, *, past, lambda_init):
    tq, tc = q1_ref.shape[1], kc_ref.shape[1]
    kc, vc = kc_ref[0].astype(BF16), vc_ref[0].astype(BF16)
    kn, vn = kn_ref[0], vn_ref[0]
    mask_c, mask_n = _chunk_mask(tq, tc, past, 0), _chunk_mask(tq, tq, past, past)
    c1 = _two_part_softmax(q1_ref[0], kc, vc, kn, vn, mask_c, mask_n)
    c2 = _two_part_softmax(q2_ref[0], kc, vc, kn, vn, mask_c, mask_n)
    lam = _lambda(lq1_ref, lk1_ref, lq2_ref, lk2_ref, lambda_init)
    o_ref[0] = _diff_finish(c1, c2, lam, g_ref[...], lambda_init).astype(o_ref.dtype)


def _diff_sample(q1, q2, kc, vc, kn, vn, lams, g, lambda_init):
    b, tq, _ = q1.shape
    tc = kc.shape[1]
    blk = lambda rows: pl.BlockSpec((1, rows, LANES), lambda b, h: (b, 0, h))
    small = lambda a: pl.BlockSpec(a.shape, lambda b, h: (0, 0))
    return pl.pallas_call(
        functools.partial(_diff_sample_kernel, past=tc, lambda_init=lambda_init),
        grid=(b, DIFF_HEADS),
        in_specs=[blk(tq), blk(tq), blk(tc), blk(tc), blk(tq), blk(tq)] + [small(a) for a in lams] + [small(g)],
        out_specs=blk(tq),
        out_shape=jax.ShapeDtypeStruct((b, tq, DIFF_HEADS * DIFF_VD), BF16),
        compiler_params=_params("parallel", "parallel"),
        name="diff_attention_sample",
    )(q1, q2, kc, vc, kn, vn, *lams, g)


def _rope_tables(pos, theta, rot_dim, period):
    half = rot_dim // 2
    inv = 1.0 / (jnp.float32(theta) ** (jnp.arange(half, dtype=F32) / half))
    ang = pos.astype(F32)[:, None] * inv[None, :]
    cos, sin = jnp.cos(ang), jnp.sin(ang)
    n = pos.shape[0]
    rest = period - rot_dim
    c = jnp.concatenate([cos, cos, jnp.ones((n, rest), F32)], axis=1)
    s1 = jnp.concatenate([-sin, jnp.zeros((n, half + rest), F32)], axis=1)
    s2 = jnp.concatenate([jnp.zeros((n, half), F32), sin, jnp.zeros((n, rest), F32)], axis=1)
    reps = LANES // period
    return tuple(jnp.tile(a, (1, reps)) for a in (c, s1, s2))


def _layer_weights(w_in, mla_w_uq, mla_w_ukv):
    c_kv = MLA_Q_LORA
    c_kr = c_kv + MLA_KV_LORA
    c_dq = c_kr + MLA_ROPE
    dqk = DIFF_HEADS * 2 * DIFF_DH
    c_dk = c_dq + dqk
    c_dv = c_dk + dqk
    c_ga = c_dv + DIFF_HEADS * DIFF_VD
    d = w_in.shape[0]
    c_gb = c_ga + d
    wb = w_in.astype(BF16)
    wq = wb[:, :c_kv]
    wkv = jnp.pad(wb[:, c_kv:c_dq], ((0, 0), (0, LANES - MLA_ROPE)))
    wdq, wdk, wdv = wb[:, c_dq:c_dk], wb[:, c_dk:c_dv], wb[:, c_dv:c_ga]
    wga, wgb = wb[:, c_ga:c_gb], wb[:, c_gb:]
    uq = mla_w_uq.astype(BF16).reshape(MLA_Q_LORA, MLA_HEADS, MLA_NOPE + MLA_ROPE)
    uq = jnp.pad(uq, ((0, 0), (0, 0), (0, MLA_QK_PAD - MLA_NOPE - MLA_ROPE))).reshape(MLA_Q_LORA, -1)
    ukv = mla_w_ukv.astype(BF16).reshape(MLA_KV_LORA, MLA_HEADS, MLA_NOPE + MLA_V)
    ukv = jnp.concatenate([ukv[:, :, :MLA_NOPE].reshape(MLA_KV_LORA, -1),
                           ukv[:, :, MLA_NOPE:].reshape(MLA_KV_LORA, -1)], axis=1)
    return wq, wkv, wdq, wdk, wdv, wga, wgb, uq, ukv


def _block(m, want):
    return want if m % want == 0 else m


def kernel(x_prompt, x_sample, cache_mla_ckv, cache_mla_krope, cache_diff_k, cache_diff_v, norm_mix, w_in,
           mla_q_norm, mla_w_uq, mla_kv_norm, mla_w_ukv, diff_lq1, diff_lk1, diff_lq2, diff_lk2, diff_subln,
           w_branch_a, w_branch_b, w_out, norm_ffn, w_ffn_in, w_ffn_out, norm_final):
    bp, lp, d = x_prompt.shape
    bs, ls, _ = x_sample.shape
    depth, _, past, _ = cache_mla_ckv.shape
    assert depth == 1, "the FFN kernel fuses the final norm, so it serves the last (only) layer"
    mp, ms = bp * lp, bs * ls
    t_attn = 512

    pos_p = jnp.arange(lp)
    pos_s = past + jnp.arange(ls)
    tabs_mla_p = _rope_tables(pos_p, MLA_THETA, MLA_ROPE, LANES)
    tabs_mla_s = tuple(jnp.tile(a, (bs, 1)) for a in _rope_tables(pos_s, MLA_THETA, MLA_ROPE, LANES))
    tabs_dif_p = _rope_tables(pos_p, ROPE_THETA, DIFF_ROT, DIFF_DH)
    tabs_dif_s = tuple(jnp.tile(a, (bs, 1)) for a in _rope_tables(pos_s, ROPE_THETA, DIFF_ROT, DIFF_DH))
    q_scale_mla = (MLA_NOPE + MLA_ROPE) ** -0.5 * LOG2E
    q_scale_dif = DIFF_DH ** -0.5 * LOG2E
    gfinal = norm_final.reshape(1, d)

    xp = x_prompt.reshape(mp, d)
    xs = x_sample.reshape(ms, d)
    rows_p, rows_s = [], []
    for l in range(depth):
        lambda_init = 0.8 - 0.6 * math.exp(-0.3 * l)
        wq, wkv, wdq, wdk, wdv, wga, wgb, uq, ukv = _layer_weights(w_in[l], mla_w_uq[l], mla_w_ukv[l])
        gmix, gq, gkv = norm_mix[l].reshape(1, -1), mla_q_norm[l].reshape(1, -1), mla_kv_norm[l].reshape(1, -1)
        lams = tuple(a[l].reshape(1, -1) for a in (diff_lq1, diff_lk1, diff_lq2, diff_lk2))
        gsub = diff_subln[l].reshape(1, -1)
        wa, wbr, wo = w_branch_a[l].astype(BF16), w_branch_b[l].astype(BF16), w_out[l].astype(BF16)
        wfi, wfo = w_ffn_in[l].astype(BF16), w_ffn_out[l].astype(BF16)
        gffn = norm_ffn[l].reshape(1, -1)

        def stage1(x, bm, tabs_mla, tabs_dif, tr):
            h = _norm(x, gmix, bm)
            q = _qproj(h, wq, gq, uq, tabs_mla, bm, q_scale_mla, tr)
            ckv, krope, kmla, vmla = _kvproj(h, wkv, gkv, ukv, tabs_mla, bm, tr)
            q1, q2, dk, dkb = _diffqk(h, wdq, wdk, tabs_dif, bm, q_scale_dif, tr)
            dv, dvb = _diffv(h, wdv, bm, tr)
            return h, q, ckv, krope, kmla, vmla, q1, q2, dk, dkb, dv, dvb

        def stage2(x, h, oa, ob, bm, bn, bf):
            mg = _merge(h, oa, ob, wga, wgb, wa, wbr, bm, bn)
            x2, h2 = _outproj(x, mg, wo, gffn, min(bm, 256))
            return _ffn(h2, wfi, wfo, x2, gfinal, min(bm, 512), bf)

        h, q, ckv, krope, kmla, vmla, q1, q2, dk, dkb, dv, dvb = stage1(xp, t_attn, tabs_mla_p, tabs_dif_p, True)
        r3 = lambda a: a.reshape(bp, lp, -1)
        oa = _mla_prompt(q, r3(kmla), vmla, t_attn, 2)
        ob = _diff_prompt(q1, q2, r3(dkb), dvb, lams, gsub, t_attn, lambda_init)
        xp = stage2(xp, h, oa.reshape(mp, -1), ob.reshape(mp, -1), _block(mp, 1024), 512, 512)
        rows_p.append((ckv.reshape(bp, lp, -1), krope.reshape(bp, lp, -1),
                       dk.reshape(bp, lp, DIFF_HEADS, -1), dv.reshape(bp, lp, DIFF_HEADS, -1)))

        h, q, ckv, krope, kmla, vmla, q1, q2, dk, dkb, dv, dvb = stage1(xs, ms, tabs_mla_s, tabs_dif_s, False)
        kr_pad = jnp.pad(cache_mla_krope[l].reshape(bs * past, MLA_ROPE), ((0, 0), (0, LANES - MLA_ROPE)))
        kc, vc = _kvcache_expand(cache_mla_ckv[l].reshape(bs * past, -1), kr_pad, ukv, _block(bs * past, 512))
        r3 = lambda a: a.reshape(bs, ls, -1)
        c3 = lambda a: a.reshape(bs, past, -1)
        oa = _mla_sample(r3(q), c3(kc), c3(vc), r3(kmla), r3(vmla))
        ob = _diff_sample(r3(q1), r3(q2), c3(cache_diff_k[l]), c3(cache_diff_v[l]), r3(dkb), r3(dvb),
                          lams, gsub, lambda_init)
        xs = stage2(xs, h, oa.reshape(ms, -1), ob.reshape(ms, -1), ms, 512, 512)
        rows_s.append((ckv.reshape(bs, ls, -1), krope.reshape(bs, ls, -1),
                       dk.reshape(bs, ls, DIFF_HEADS, -1), dv.reshape(bs, ls, DIFF_HEADS, -1)))

    y_prompt = xp.reshape(bp, lp, d)
    y_sample = xs.reshape(bs, ls, d)
    stack = lambda rows, i: jnp.stack([r[i] for r in rows], axis=0)
    return (y_prompt, y_sample,
            stack(rows_p, 0), stack(rows_p, 1), stack(rows_p, 2), stack(rows_p, 3),
            stack(rows_s, 0), stack(rows_s, 1), stack(rows_s, 2), stack(rows_s, 3))
```

```python
import functools
import math

import jax
import jax.numpy as jnp
from jax import lax
from jax.experimental import pallas as pl
from jax.experimental.pallas import tpu as pltpu

F32 = jnp.float32
BF16 = jnp.bfloat16

CHUNK = 64
EPS = 1e-6
NEG_INF = -1e30

MLA_HEADS = 8
MLA_Q_LORA = 512
MLA_KV_LORA = 256
MLA_NOPE = 128
MLA_ROPE = 64
MLA_V = 128
MLA_THETA = 10000.0
MLA_QK_PAD = 256

DIFF_HEADS = 8
DIFF_DH = 64
DIFF_VD = 2 * DIFF_DH
DIFF_ROT = DIFF_DH // 4
ROPE_THETA = 500000.0

LANES = 128
LOG2E = math.log2(math.e)
VMEM_LIMIT = 56 * 1024 * 1024


def _params(*sem):
    return pltpu.CompilerParams(dimension_semantics=sem, vmem_limit_bytes=VMEM_LIMIT)


def _dot(a, b):
    return jnp.dot(a, b, preferred_element_type=F32)


def _dot_nt(a, b):
    return lax.dot_general(a, b, (((1,), (1,)), ((), ())), preferred_element_type=F32)


def _rms(x, g):
    return x * lax.rsqrt(jnp.mean(x * x, axis=-1, keepdims=True) + EPS) * g


def _rope_tile(t, c, s1, s2, half):
    return t * c + pltpu.roll(t, LANES - half, 1) * s1 + pltpu.roll(t, half, 1) * s2


def _norm_kernel(x_ref, g_ref, o_ref):
    o_ref[...] = _rms(x_ref[...], g_ref[...]).astype(o_ref.dtype)


def _norm(x, g, bm):
    m, d = x.shape
    return pl.pallas_call(
        _norm_kernel,
        grid=(m // bm,),
        in_specs=[pl.BlockSpec((bm, d), lambda i: (i, 0)), pl.BlockSpec((1, d), lambda i: (0, 0))],
        out_specs=pl.BlockSpec((bm, d), lambda i: (i, 0)),
        out_shape=jax.ShapeDtypeStruct((m, d), BF16),
        compiler_params=_params("parallel"),
        name="norm_mix",
    )(x, g)


def _put(ref, lo, val, tr):
    w = val.shape[1]
    if tr:
        ref[0, lo:lo + w, :] = val.T.astype(ref.dtype)
    else:
        ref[:, lo:lo + w] = val.astype(ref.dtype)


def _out(m, n, bm, dtype, tr):
    if tr:
        return pl.BlockSpec((1, n, bm), lambda i: (i, 0, 0)), jax.ShapeDtypeStruct((m // bm, n, bm), dtype)
    return pl.BlockSpec((bm, n), lambda i: (i, 0)), jax.ShapeDtypeStruct((m, n), dtype)


def _qproj_kernel(h_ref, wq_ref, gq_ref, wuq_ref, c_ref, s1_ref, s2_ref, o_ref, *, scale, tr):
    qlat = _rms(_dot(h_ref[...], wq_ref[...]), gq_ref[...]).astype(BF16)
    q = _dot(qlat, wuq_ref[...])
    c, s1, s2 = c_ref[...], s1_ref[...], s2_ref[...]
    for hd in range(MLA_HEADS):
        lo = hd * MLA_QK_PAD
        _put(o_ref, lo, q[:, lo:lo + LANES] * scale, tr)
        t = _rope_tile(q[:, lo + LANES:lo + 2 * LANES], c, s1, s2, MLA_ROPE // 2)
        _put(o_ref, lo + LANES, t * scale, tr)


def _qproj(h, wq, gq, wuq, tabs, bm, scale, tr):
    m, d = h.shape
    nt = tabs[0].shape[0] // bm
    full = lambda a: pl.BlockSpec(a.shape, lambda i: (0, 0))
    tab = pl.BlockSpec((bm, LANES), lambda i: (i % nt, 0))
    spec, shape = _out(m, wuq.shape[1], bm, BF16, tr)
    return pl.pallas_call(
        functools.partial(_qproj_kernel, scale=scale, tr=tr),
        grid=(m // bm,),
        in_specs=[pl.BlockSpec((bm, d), lambda i: (i, 0)), full(wq), full(gq), full(wuq), tab, tab, tab],
        out_specs=spec,
        out_shape=shape,
        compiler_params=_params("parallel"),
        name="mla_q_proj",
    )(h, wq, gq, wuq, *tabs)


def _kv_expand(ckv, krope_tile, wukv_ref, kmla_ref, vmla_ref, tr):
    kv = _dot(ckv.astype(BF16), wukv_ref[...])
    kr = krope_tile.astype(BF16)
    for hd in range(MLA_HEADS):
        lo = hd * MLA_QK_PAD
        kmla_ref[:, lo:lo + LANES] = kv[:, hd * MLA_NOPE:(hd + 1) * MLA_NOPE].astype(BF16)
        kmla_ref[:, lo + LANES:lo + 2 * LANES] = kr
        vlo = MLA_HEADS * MLA_NOPE + hd * MLA_V
        _put(vmla_ref, hd * MLA_V, kv[:, vlo:vlo + MLA_V], tr)


def _kvproj_kernel(h_ref, wkv_ref, gkv_ref, wukv_ref, c_ref, s1_ref, s2_ref,
                   ckv_ref, krope_ref, kmla_ref, vmla_ref, *, tr):
    z = _dot(h_ref[...], wkv_ref[...])
    ckv = _rms(z[:, :MLA_KV_LORA], gkv_ref[...])
    ckv_ref[...] = ckv
    t = _rope_tile(z[:, MLA_KV_LORA:], c_ref[...], s1_ref[...], s2_ref[...], MLA_ROPE // 2)
    krope_ref[...] = t[:, :MLA_ROPE]
    _kv_expand(ckv, t, wukv_ref, kmla_ref, vmla_ref, tr)


def _kvproj(h, wkv, gkv, wukv, tabs, bm, tr):
    m, d = h.shape
    nt = tabs[0].shape[0] // bm
    full = lambda a: pl.BlockSpec(a.shape, lambda i: (0, 0))
    tab = pl.BlockSpec((bm, LANES), lambda i: (i % nt, 0))
    row = lambda n: pl.BlockSpec((bm, n), lambda i: (i, 0))
    nk, nv = MLA_HEADS * MLA_QK_PAD, MLA_HEADS * MLA_V
    vspec, vshape = _out(m, nv, bm, BF16, tr)
    return pl.pallas_call(
        functools.partial(_kvproj_kernel, tr=tr),
        grid=(m // bm,),
        in_specs=[row(d), full(wkv), full(gkv), full(wukv), tab, tab, tab],
        out_specs=[row(MLA_KV_LORA), row(MLA_ROPE), row(nk), vspec],
        out_shape=[jax.ShapeDtypeStruct((m, MLA_KV_LORA), F32), jax.ShapeDtypeStruct((m, MLA_ROPE), F32),
                   jax.ShapeDtypeStruct((m, nk), BF16), vshape],
        compiler_params=_params("parallel"),
        name="mla_kv_proj",
    )(h, wkv, gkv, wukv, *tabs)


def _kvcache_kernel(ckv_ref, krope_ref, wukv_ref, kmla_ref, vmla_ref):
    _kv_expand(ckv_ref[...], krope_ref[...], wukv_ref, kmla_ref, vmla_ref, False)


def _kvcache_expand(ckv, krope_pad, wukv, bm):
    m = ckv.shape[0]
    row = lambda n: pl.BlockSpec((bm, n), lambda i: (i, 0))
    nk, nv = MLA_HEADS * MLA_QK_PAD, MLA_HEADS * MLA_V
    return pl.pallas_call(
        _kvcache_kernel,
        grid=(m // bm,),
        in_specs=[row(MLA_KV_LORA), row(LANES), pl.BlockSpec(wukv.shape, lambda i: (0, 0))],
        out_specs=[row(nk), row(nv)],
        out_shape=[jax.ShapeDtypeStruct((m, nk), BF16), jax.ShapeDtypeStruct((m, nv), BF16)],
        compiler_params=_params("parallel"),
        name="mla_kv_cache_expand",
    )(ckv, krope_pad, wukv)


def _diffqk_kernel(h_ref, wdq_ref, wdk_ref, c_ref, s1_ref, s2_ref,
                   q1_ref, q2_ref, dk_ref, dkb_ref, *, scale, tr):
    h = h_ref[...]
    zq = _dot(h, wdq_ref[...])
    zk = _dot(h, wdk_ref[...])
    c, s1, s2 = c_ref[...], s1_ref[...], s2_ref[...]
    first = lax.broadcasted_iota(jnp.int32, (1, LANES), 1) < DIFF_DH
    for hd in range(DIFF_HEADS):
        sl = slice(hd * LANES, (hd + 1) * LANES)
        q = _rope_tile(zq[:, sl], c, s1, s2, DIFF_ROT // 2) * scale
        _put(q1_ref, hd * LANES, jnp.where(first, q, 0.0), tr)
        _put(q2_ref, hd * LANES, jnp.where(first, 0.0, q), tr)
        k = _rope_tile(zk[:, sl], c, s1, s2, DIFF_ROT // 2)
        dk_ref[:, sl] = k
        dkb_ref[:, sl] = k.astype(BF16)


def _diffqk(h, wdq, wdk, tabs, bm, scale, tr):
    m, d = h.shape
    nt = tabs[0].shape[0] // bm
    n = wdq.shape[1]
    full = lambda a: pl.BlockSpec(a.shape, lambda i: (0, 0))
    tab = pl.BlockSpec((bm, LANES), lambda i: (i % nt, 0))
    row = lambda w: pl.BlockSpec((bm, w), lambda i: (i, 0))
    qspec, qshape = _out(m, n, bm, BF16, tr)
    return pl.pallas_call(
        functools.partial(_diffqk_kernel, scale=scale, tr=tr),
        grid=(m // bm,),
        in_specs=[row(d), full(wdq), full(wdk), tab, tab, tab],
        out_specs=[qspec, qspec, row(n), row(n)],
        out_shape=[qshape, qshape, jax.ShapeDtypeStruct((m, n), F32), jax.ShapeDtypeStruct((m, n), BF16)],
        compiler_params=_params("parallel"),
        name="diff_qk_proj",
    )(h, wdq, wdk, *tabs)


def _diffv_kernel(h_ref, wdv_ref, dv_ref, dvb_ref, *, tr):
    z = _dot(h_ref[...], wdv_ref[...])
    dv_ref[...] = z
    for hd in range(DIFF_HEADS):
        _put(dvb_ref, hd * DIFF_VD, z[:, hd * DIFF_VD:(hd + 1) * DIFF_VD], tr)


def _diffv(h, wdv, bm, tr):
    m, d = h.shape
    n = wdv.shape[1]
    row = lambda w: pl.BlockSpec((bm, w), lambda i: (i, 0))
    vspec, vshape = _out(m, n, bm, BF16, tr)
    return pl.pallas_call(
        functools.partial(_diffv_kernel, tr=tr),
        grid=(m // bm,),
        in_specs=[row(d), pl.BlockSpec(wdv.shape, lambda i: (0, 0))],
        out_specs=[row(n), vspec],
        out_shape=[jax.ShapeDtypeStruct((m, n), F32), vshape],
        compiler_params=_params("parallel"),
        name="diff_v_proj",
    )(h, wdv)


def _merge_kernel(h_ref, oa_ref, ob_ref, wga_ref, wgb_ref, wa_ref, wb_ref, o_ref):
    h = h_ref[...]
    ga = jax.nn.sigmoid(_dot(h, wga_ref[...]))
    gb = jax.nn.sigmoid(_dot(h, wgb_ref[...]))
    ya = _dot(oa_ref[...], wa_ref[...])
    yb = _dot(ob_ref[...], wb_ref[...])
    o_ref[...] = (ga * ya + gb * yb).astype(o_ref.dtype)


def _merge(h, oa, ob, wga, wgb, wa, wb, bm, bn):
    m, d = h.shape
    n = wga.shape[1]
    row = lambda a: pl.BlockSpec((bm, a.shape[1]), lambda i, j: (i, 0))
    col = lambda a: pl.BlockSpec((a.shape[0], bn), lambda i, j: (0, j))
    return pl.pallas_call(
        _merge_kernel,
        grid=(m // bm, n // bn),
        in_specs=[row(h), row(oa), row(ob), col(wga), col(wgb), col(wa), col(wb)],
        out_specs=pl.BlockSpec((bm, bn), lambda i, j: (i, j)),
        out_shape=jax.ShapeDtypeStruct((m, n), BF16),
        compiler_params=_params("parallel", "arbitrary"),
        name="gated_merge",
    )(h, oa, ob, wga, wgb, wa, wb)


def _outproj_kernel(x_ref, mg_ref, wo_ref, g_ref, x2_ref, h2_ref):
    x2 = x_ref[...] + _dot(mg_ref[...], wo_ref[...])
    x2_ref[...] = x2
    h2_ref[...] = _rms(x2, g_ref[...]).astype(BF16)


def _outproj(x, mg, wo, g, bm):
    m, d = x.shape
    row = pl.BlockSpec((bm, d), lambda i: (i, 0))
    return pl.pallas_call(
        _outproj_kernel,
        grid=(m // bm,),
        in_specs=[row, row, pl.BlockSpec(wo.shape, lambda i: (0, 0)), pl.BlockSpec((1, d), lambda i: (0, 0))],
        out_specs=[row, row],
        out_shape=[jax.ShapeDtypeStruct((m, d), F32), jax.ShapeDtypeStruct((m, d), BF16)],
        compiler_params=_params("parallel"),
        name="out_proj_residual",
    )(x, mg, wo, g)


def _ffn_kernel(h2_ref, wg_ref, wu_ref, wd_ref, x2_ref, gf_ref, y_ref, acc_ref):
    f = pl.program_id(1)

    @pl.when(f == 0)
    def _():
        acc_ref[...] = jnp.zeros_like(acc_ref)

    h2 = h2_ref[...]
    a = (jax.nn.silu(_dot(h2, wg_ref[...])) * _dot(h2, wu_ref[...])).astype(BF16)
    acc_ref[...] += _dot(a, wd_ref[...])

    @pl.when(f == pl.num_programs(1) - 1)
    def _():
        y_ref[...] = _rms(x2_ref[...] + acc_ref[...], gf_ref[...])


def _ffn(h2, w_in, w_out, x2, gf, bm, bf):
    m, d = h2.shape
    dff = w_out.shape[0]
    nf = dff // bf
    row = pl.BlockSpec((bm, d), lambda i, f: (i, 0))
    return pl.pallas_call(
        _ffn_kernel,
        grid=(m // bm, nf),
        in_specs=[row,
                  pl.BlockSpec((d, bf), lambda i, f: (0, f)),
                  pl.BlockSpec((d, bf), lambda i, f: (0, f + nf)),
                  pl.BlockSpec((bf, d), lambda i, f: (f, 0)),
                  row,
                  pl.BlockSpec((1, d), lambda i, f: (0, 0))],
        out_specs=row,
        out_shape=jax.ShapeDtypeStruct((m, d), F32),
        scratch_shapes=[pltpu.VMEM((bm, d), F32)],
        compiler_params=_params("parallel", "arbitrary"),
        name="ffn_swiglu_final_norm",
    )(h2, w_in, w_in, w_out, x2, gf)


def _softmax_step(s, v, m, l, acc):
    m_new = jnp.maximum(m, jnp.max(s, axis=-1, keepdims=True))
    alpha = jnp.exp2(m - m_new)
    p = jnp.exp2(s - m_new)
    l = alpha * l + jnp.sum(p, axis=-1, keepdims=True)
    acc = alpha * acc + _dot(p.astype(BF16), v)
    return m_new, l, acc


def _chunk_mask(tq, tk, q0, k0):
    qp = q0 + lax.broadcasted_iota(jnp.int32, (tq, tk), 0)
    kp = k0 + lax.broadcasted_iota(jnp.int32, (tq, tk), 1)
    return (kp // CHUNK) <= (qp // CHUNK)


def _softmax_init(tq, dv):
    return (jnp.full((tq, 1), NEG_INF, F32), jnp.zeros((tq, 1), F32), jnp.zeros((tq, dv), F32))


def _chunk_mask_t(t):
    kp = lax.broadcasted_iota(jnp.int32, (t, t), 0)
    qp = lax.broadcasted_iota(jnp.int32, (t, t), 1)
    return (kp // CHUNK) <= (qp // CHUNK)


def _probs_t(s, m, l):
    m_new = jnp.maximum(m, jnp.max(s, axis=0, keepdims=True))
    alpha = jnp.exp2(m - m_new)
    p = jnp.exp2(s - m_new)
    return p.astype(BF16), alpha, m_new, alpha * l + jnp.sum(p, axis=0, keepdims=True)


def _attention_scratch(t, dv, streams):
    return [pltpu.VMEM((streams, 2, t, t), F32),
            pltpu.VMEM((streams, 2, t, t), BF16),
            pltpu.VMEM((streams, dv, t), F32),
            pltpu.VMEM((streams, 8, t), F32)]


def _causal_attention_t(n, scores, values, finish, s_ref, p_ref, acc_ref, stat_ref, t, streams):
    row_m, row_l, row_a = 0, 1, 2

    def stat(st, r):
        return stat_ref[st, r:r + 1, :]

    def step(tau, par):
        for st in range(streams):
            s_ref[st, par] = scores(tau, st)
            p, alpha, m, l = _probs_t(s_ref[st, 1 - par], stat(st, row_m), stat(st, row_l))
            p_ref[st, 1 - par] = p
            pv = _dot(values(jnp.maximum(tau - 2, 0), st), p_ref[st, par])
            acc_ref[st] = stat(st, row_a) * acc_ref[st] + pv
            stat_ref[st, row_m:row_m + 1, :] = m
            stat_ref[st, row_l:row_l + 1, :] = l
            stat_ref[st, row_a:row_a + 1, :] = alpha

    for st in range(streams):
        s_ref[st, 0] = scores(0, st)
        p_ref[st, 1] = jnp.zeros((t, t), BF16)
        acc_ref[st] = jnp.zeros(acc_ref.shape[1:], F32)
        stat_ref[st, row_m:row_m + 1, :] = jnp.full((1, t), NEG_INF, F32)
        stat_ref[st, row_l:row_l + 1, :] = jnp.zeros((1, t), F32)
        stat_ref[st, row_a:row_a + 1, :] = jnp.ones((1, t), F32)

    def pair(u, carry):
        step(2 * u + 1, 1)
        step(2 * u + 2, 0)
        return carry

    lax.fori_loop(0, n // 2, pair, 0)

    @pl.when(n % 2 == 1)
    def _():
        step(n, 1)

    def tail(par):
        mask = _chunk_mask_t(t)
        res = []
        for st in range(streams):
            s = jnp.where(mask, s_ref[st, par], NEG_INF)
            p, alpha, _, l = _probs_t(s, stat(st, row_m), stat(st, row_l))
            acc = stat(st, row_a) * acc_ref[st] + _dot(values(jnp.maximum(n - 1, 0), st), p_ref[st, 1 - par])
            res.append((alpha * acc + _dot(values(n, st), p), l))
        finish(res)

    for par in (0, 1):
        pl.when(n % 2 == par)(functools.partial(tail, par))


def _mla_prompt_kernel(qt_ref, k_ref, vt_ref, o_ref, *scratch, t, heads):
    def scores(j, hd):
        k = k_ref[0, pl.ds(pl.multiple_of(j * t, t), t), hd * MLA_QK_PAD:(hd + 1) * MLA_QK_PAD]
        return _dot(k, qt_ref[0, hd * MLA_QK_PAD:(hd + 1) * MLA_QK_PAD, :])

    def values(j, hd):
        return vt_ref[j, hd * MLA_V:(hd + 1) * MLA_V, :]

    def finish(res):
        for hd, (acc, l) in enumerate(res):
            o_ref[0, :, hd * MLA_V:(hd + 1) * MLA_V] = (acc / l).T.astype(o_ref.dtype)

    _causal_attention_t(pl.program_id(2), scores, values, finish, *scratch, t, heads)


def _mla_prompt(qt, k, vt, t, heads):
    b, l, _ = k.shape
    nq = l // t
    return pl.pallas_call(
        functools.partial(_mla_prompt_kernel, t=t, heads=heads),
        grid=(b, MLA_HEADS // heads, nq),
        in_specs=[pl.BlockSpec((1, heads * MLA_QK_PAD, t), lambda b, h, i: (b * nq + i, h, 0)),
                  pl.BlockSpec((1, l, heads * MLA_QK_PAD), lambda b, h, i: (b, 0, h)),
                  pl.BlockSpec((nq, heads * MLA_V, t), lambda b, h, i: (b, h, 0))],
        out_specs=pl.BlockSpec((1, t, heads * MLA_V), lambda b, h, i: (b, i, h)),
        out_shape=jax.ShapeDtypeStruct((b, l, MLA_HEADS * MLA_V), BF16),
        scratch_shapes=_attention_scratch(t, MLA_V, heads),
        compiler_params=_params("parallel", "parallel", "arbitrary"),
        name="mla_attention_prompt",
    )(qt, k, vt)


def _lambda(lq1_ref, lk1_ref, lq2_ref, lk2_ref, lambda_init):
    a = jnp.sum(lq1_ref[...] * lk1_ref[...], axis=-1, keepdims=True)
    b = jnp.sum(lq2_ref[...] * lk2_ref[...], axis=-1, keepdims=True)
    return jnp.exp(a) - jnp.exp(b) + lambda_init


def _diff_finish(c1, c2, lam, g, lambda_init):
    o = c1[2] / c1[1] - lam * (c2[2] / c2[1])
    return _rms(o, g) * (1.0 - lambda_init)


def _diff_prompt_kernel(q1t_ref, q2t_ref, k_ref, vt_ref, lq1_ref, lk1_ref, lq2_ref, lk2_ref, g_ref, o_ref,
                        *scratch, t, lambda_init):
    qt_refs = (q1t_ref, q2t_ref)

    def scores(j, st):
        return _dot(k_ref[0, pl.ds(pl.multiple_of(j * t, t), t), :], qt_refs[st][0])

    def values(j, st):
        return vt_ref[j]

    def finish(res):
        (a1, l1), (a2, l2) = res
        lam = _lambda(lq1_ref, lk1_ref, lq2_ref, lk2_ref, lambda_init)
        o = (a1 / l1 - lam * (a2 / l2)).T
        o_ref[0] = (_rms(o, g_ref[...]) * (1.0 - lambda_init)).astype(o_ref.dtype)

    _causal_attention_t(pl.program_id(2), scores, values, finish, *scratch, t, 2)


def _diff_prompt(q1t, q2t, k, vt, lams, g, t, lambda_init):
    b, l, _ = k.shape
    nq = l // t
    qs = pl.BlockSpec((1, LANES, t), lambda b, h, i: (b * nq + i, h, 0))
    small = lambda a: pl.BlockSpec(a.shape, lambda b, h, i: (0, 0))
    return pl.pallas_call(
        functools.partial(_diff_prompt_kernel, t=t, lambda_init=lambda_init),
        grid=(b, DIFF_HEADS, nq),
        in_specs=[qs, qs,
                  pl.BlockSpec((1, l, LANES), lambda b, h, i: (b, 0, h)),
                  pl.BlockSpec((nq, DIFF_VD, t), lambda b, h, i: (b, h, 0))]
                 + [small(a) for a in lams] + [small(g)],
        out_specs=pl.BlockSpec((1, t, DIFF_VD), lambda b, h, i: (b, i, h)),
        out_shape=jax.ShapeDtypeStruct((b, l, DIFF_HEADS * DIFF_VD), BF16),
        scratch_shapes=_attention_scratch(t, DIFF_VD, 2),
        compiler_params=_params("parallel", "parallel", "arbitrary"),
        name="diff_attention_prompt",
    )(q1t, q2t, k, vt, *lams, g)


def _two_part_softmax(q, kc, vc, kn, vn, mask_c, mask_n):
    tq = q.shape[0]
    carry = _softmax_init(tq, vc.shape[1])
    carry = _softmax_step(jnp.where(mask_c, _dot_nt(q, kc), NEG_INF), vc, *carry)
    return _softmax_step(jnp.where(mask_n, _dot_nt(q, kn), NEG_INF), vn, *carry)


def _mla_sample_kernel(q_ref, kc_ref, vc_ref, kn_ref, vn_ref, o_ref, *, past):
    tq, tc = q_ref.shape[1], kc_ref.shape[1]
    _, l, acc = _two_part_softmax(q_ref[0], kc_ref[0], vc_ref[0], kn_ref[0], vn_ref[0],
                                  _chunk_mask(tq, tc, past, 0), _chunk_mask(tq, tq, past, past))
    o_ref[0] = (acc / l).astype(o_ref.dtype)


def _mla_sample(q, kc, vc, kn, vn):
    b, tq, _ = q.shape
    tc = kc.shape[1]
    blk = lambda rows, w: pl.BlockSpec((1, rows, w), lambda b, h: (b, 0, h))
    return pl.pallas_call(
        functools.partial(_mla_sample_kernel, past=tc),
        grid=(b, MLA_HEADS),
        in_specs=[blk(tq, MLA_QK_PAD), blk(tc, MLA_QK_PAD), blk(tc, MLA_V), blk(tq, MLA_QK_PAD), blk(tq, MLA_V)],
        out_specs=blk(tq, MLA_V),
        out_shape=jax.ShapeDtypeStruct((b, tq, MLA_HEADS * MLA_V), BF16),
        compiler_params=_params("parallel", "parallel"),
        name="mla_attention_sample",
    )(q, kc, vc, kn, vn)


def _diff_sample_kernel(q1_ref, q2_ref, kc_ref, vc_ref, kn_ref, vn_ref,
                        lq1_ref, lk1_ref, lq2_ref, lk2_ref, g_ref, o_ref, *, past, lambda_init):
    tq, tc = q1_ref.shape[1], kc_ref.shape[1]
    kc, vc = kc_ref[0].astype(BF16), vc_ref[0].astype(BF16)
    kn, vn = kn_ref[0], vn_ref[0]
    mask_c, mask_n = _chunk_mask(tq, tc, past, 0), _chunk_mask(tq, tq, past, past)
    c1 = _two_part_softmax(q1_ref[0], kc, vc, kn, vn, mask_c, mask_n)
    c2 = _two_part_softmax(q2_ref[0], kc, vc, kn, vn, mask_c, mask_n)
    lam = _lambda(lq1_ref, lk1_ref, lq2_ref, lk2_ref, lambda_init)
    o_ref[0] = _diff_finish(c1, c2, lam, g_ref[...], lambda_init).astype(o_ref.dtype)


def _diff_sample(q1, q2, kc, vc, kn, vn, lams, g, lambda_init):
    b, tq, _ = q1.shape
    tc = kc.shape[1]
    blk = lambda rows: pl.BlockSpec((1, rows, LANES), lambda b, h: (b, 0, h))
    small = lambda a: pl.BlockSpec(a.shape, lambda b, h: (0, 0))
    return pl.pallas_call(
        functools.partial(_diff_sample_kernel, past=tc, lambda_init=lambda_init),
        grid=(b, DIFF_HEADS),
        in_specs=[blk(tq), blk(tq), blk(tc), blk(tc), blk(tq), blk(tq)] + [small(a) for a in lams] + [small(g)],
        out_specs=blk(tq),
        out_shape=jax.ShapeDtypeStruct((b, tq, DIFF_HEADS * DIFF_VD), BF16),
        compiler_params=_params("parallel", "parallel"),
        name="diff_attention_sample",
    )(q1, q2, kc, vc, kn, vn, *lams, g)


def _rope_tables(pos, theta, rot_dim, period):
    half = rot_dim // 2
    inv = 1.0 / (jnp.float32(theta) ** (jnp.arange(half, dtype=F32) / half))
    ang = pos.astype(F32)[:, None] * inv[None, :]
    cos, sin = jnp.cos(ang), jnp.sin(ang)
    n = pos.shape[0]
    rest = period - rot_dim
    c = jnp.concatenate([cos, cos, jnp.ones((n, rest), F32)], axis=1)
    s1 = jnp.concatenate([-sin, jnp.zeros((n, half + rest), F32)], axis=1)
    s2 = jnp.concatenate([jnp.zeros((n, half), F32), sin, jnp.zeros((n, rest), F32)], axis=1)
    reps = LANES // period
    return tuple(jnp.tile(a, (1, reps)) for a in (c, s1, s2))


def _layer_weights(w_in, mla_w_uq, mla_w_ukv):
    c_kv = MLA_Q_LORA
    c_kr = c_kv + MLA_KV_LORA
    c_dq = c_kr + MLA_ROPE
    dqk = DIFF_HEADS * 2 * DIFF_DH
    c_dk = c_dq + dqk
    c_dv = c_dk + dqk
    c_ga = c_dv + DIFF_HEADS * DIFF_VD
    d = w_in.shape[0]
    c_gb = c_ga + d
    wb = w_in.astype(BF16)
    wq = wb[:, :c_kv]
    wkv = jnp.pad(wb[:, c_kv:c_dq], ((0, 0), (0, LANES - MLA_ROPE)))
    wdq, wdk, wdv = wb[:, c_dq:c_dk], wb[:, c_dk:c_dv], wb[:, c_dv:c_ga]
    wga, wgb = wb[:, c_ga:c_gb], wb[:, c_gb:]
    uq = mla_w_uq.astype(BF16).reshape(MLA_Q_LORA, MLA_HEADS, MLA_NOPE + MLA_ROPE)
    uq = jnp.pad(uq, ((0, 0), (0, 0), (0, MLA_QK_PAD - MLA_NOPE - MLA_ROPE))).reshape(MLA_Q_LORA, -1)
    ukv = mla_w_ukv.astype(BF16).reshape(MLA_KV_LORA, MLA_HEADS, MLA_NOPE + MLA_V)
    ukv = jnp.concatenate([ukv[:, :, :MLA_NOPE].reshape(MLA_KV_LORA, -1),
                           ukv[:, :, MLA_NOPE:].reshape(MLA_KV_LORA, -1)], axis=1)
    return wq, wkv, wdq, wdk, wdv, wga, wgb, uq, ukv


def _block(m, want):
    return want if m % want == 0 else m


def kernel(x_prompt, x_sample, cache_mla_ckv, cache_mla_krope, cache_diff_k, cache_diff_v, norm_mix, w_in,
           mla_q_norm, mla_w_uq, mla_kv_norm, mla_w_ukv, diff_lq1, diff_lk1, diff_lq2, diff_lk2, diff_subln,
           w_branch_a, w_branch_b, w_out, norm_ffn, w_ffn_in, w_ffn_out, norm_final):
    bp, lp, d = x_prompt.shape
    bs, ls, _ = x_sample.shape
    depth, _, past, _ = cache_mla_ckv.shape
    assert depth == 1, "the FFN kernel fuses the final norm, so it serves the last (only) layer"
    mp, ms = bp * lp, bs * ls
    t_attn = 512

    pos_p = jnp.arange(lp)
    pos_s = past + jnp.arange(ls)
    tabs_mla_p = _rope_tables(pos_p, MLA_THETA, MLA_ROPE, LANES)
    tabs_mla_s = tuple(jnp.tile(a, (bs, 1)) for a in _rope_tables(pos_s, MLA_THETA, MLA_ROPE, LANES))
    tabs_dif_p = _rope_tables(pos_p, ROPE_THETA, DIFF_ROT, DIFF_DH)
    tabs_dif_s = tuple(jnp.tile(a, (bs, 1)) for a in _rope_tables(pos_s, ROPE_THETA, DIFF_ROT, DIFF_DH))
    q_scale_mla = (MLA_NOPE + MLA_ROPE) ** -0.5 * LOG2E
    q_scale_dif = DIFF_DH ** -0.5 * LOG2E
    gfinal = norm_final.reshape(1, d)

    xp = x_prompt.reshape(mp, d)
    xs = x_sample.reshape(ms, d)
    rows_p, rows_s = [], []
    for l in range(depth):
        lambda_init = 0.8 - 0.6 * math.exp(-0.3 * l)
        wq, wkv, wdq, wdk, wdv, wga, wgb, uq, ukv = _layer_weights(w_in[l], mla_w_uq[l], mla_w_ukv[l])
        gmix, gq, gkv = norm_mix[l].reshape(1, -1), mla_q_norm[l].reshape(1, -1), mla_kv_norm[l].reshape(1, -1)
        lams = tuple(a[l].reshape(1, -1) for a in (diff_lq1, diff_lk1, diff_lq2, diff_lk2))
        gsub = diff_subln[l].reshape(1, -1)
        wa, wbr, wo = w_branch_a[l].astype(BF16), w_branch_b[l].astype(BF16), w_out[l].astype(BF16)
        wfi, wfo = w_ffn_in[l].astype(BF16), w_ffn_out[l].astype(BF16)
        gffn = norm_ffn[l].reshape(1, -1)

        def stage1(x, bm, tabs_mla, tabs_dif, tr):
            h = _norm(x, gmix, bm)
            q = _qproj(h, wq, gq, uq, tabs_mla, bm, q_scale_mla, tr)
            ckv, krope, kmla, vmla = _kvproj(h, wkv, gkv, ukv, tabs_mla, bm, tr)
            q1, q2, dk, dkb = _diffqk(h, wdq, wdk, tabs_dif, bm, q_scale_dif, tr)
            dv, dvb = _diffv(h, wdv, bm, tr)
            return h, q, ckv, krope, kmla, vmla, q1, q2, dk, dkb, dv, dvb

        def stage2(x, h, oa, ob, bm, bn, bf):
            mg = _merge(h, oa, ob, wga, wgb, wa, wbr, bm, bn)
            x2, h2 = _outproj(x, mg, wo, gffn, min(bm, 256))
            return _ffn(h2, wfi, wfo, x2, gfinal, min(bm, 512), bf)

        h, q, ckv, krope, kmla, vmla, q1, q2, dk, dkb, dv, dvb = stage1(xp, t_attn, tabs_mla_p, tabs_dif_p, True)
        r3 = lambda a: a.reshape(bp, lp, -1)
        oa = _mla_prompt(q, r3(kmla), vmla, t_attn, 2)
        ob = _diff_prompt(q1, q2, r3(dkb), dvb, lams, gsub, t_attn, lambda_init)
        xp = stage2(xp, h, oa.reshape(mp, -1), ob.reshape(mp, -1), _block(mp, 1024), 512, 512)
        rows_p.append((ckv.reshape(bp, lp, -1), krope.reshape(bp, lp, -1),
                       dk.reshape(bp, lp, DIFF_HEADS, -1), dv.reshape(bp, lp, DIFF_HEADS, -1)))

        h, q, ckv, krope, kmla, vmla, q1, q2, dk, dkb, dv, dvb = stage1(xs, ms, tabs_mla_s, tabs_dif_s, False)
        kr_pad = jnp.pad(cache_mla_krope[l].reshape(bs * past, MLA_ROPE), ((0, 0), (0, LANES - MLA_ROPE)))
        kc, vc = _kvcache_expand(cache_mla_ckv[l].reshape(bs * past, -1), kr_pad, ukv, _block(bs * past, 512))
        r3 = lambda a: a.reshape(bs, ls, -1)
        c3 = lambda a: a.reshape(bs, past, -1)
        oa = _mla_sample(r3(q), c3(kc), c3(vc), r3(kmla), r3(vmla))
        ob = _diff_sample(r3(q1), r3(q2), c3(cache_diff_k[l]), c3(cache_diff_v[l]), r3(dkb), r3(dvb),
                          lams, gsub, lambda_init)
        xs = stage2(xs, h, oa.reshape(ms, -1), ob.reshape(ms, -1), ms, 512, 512)
        rows_s.append((ckv.reshape(bs, ls, -1), krope.reshape(bs, ls, -1),
                       dk.reshape(bs, ls, DIFF_HEADS, -1), dv.reshape(bs, ls, DIFF_HEADS, -1)))

    y_prompt = xp.reshape(bp, lp, d)
    y_sample = xs.reshape(bs, ls, d)
    stack = lambda rows, i: jnp.stack([r[i] for r in rows], axis=0)
    return (y_prompt, y_sample,
            stack(rows_p, 0), stack(rows_p, 1), stack(rows_p, 2), stack(rows_p, 3),
            stack(rows_s, 0), stack(rows_s, 1), stack(rows_s, 2), stack(rows_s, 3))
```

```python
import functools
import math

import jax
import jax.numpy as jnp
from jax import lax
from jax.experimental import pallas as pl
from jax.experimental.pallas import tpu as pltpu

F32 = jnp.float32
BF16 = jnp.bfloat16

CHUNK = 64
EPS = 1e-6
NEG_INF = -1e30

MLA_HEADS = 8
MLA_Q_LORA = 512
MLA_KV_LORA = 256
MLA_NOPE = 128
MLA_ROPE = 64
MLA_V = 128
MLA_THETA = 10000.0
MLA_QK_PAD = 256

DIFF_HEADS = 8
DIFF_DH = 64
DIFF_VD = 2 * DIFF_DH
DIFF_ROT = DIFF_DH // 4
ROPE_THETA = 500000.0

LANES = 128
VT_ONES = 16
LOG2E = math.log2(math.e)
VMEM_LIMIT = 56 * 1024 * 1024


def _params(*sem):
    return pltpu.CompilerParams(dimension_semantics=sem, vmem_limit_bytes=VMEM_LIMIT)


def _dot(a, b):
    return jnp.dot(a, b, preferred_element_type=F32)


def _dot_nt(a, b):
    return lax.dot_general(a, b, (((1,), (1,)), ((), ())), preferred_element_type=F32)


def _rms(x, g):
    return x * lax.rsqrt(jnp.mean(x * x, axis=-1, keepdims=True) + EPS) * g


def _rope_tile(t, c, s1, s2, half):
    return t * c + pltpu.roll(t, LANES - half, 1) * s1 + pltpu.roll(t, half, 1) * s2


def _norm_kernel(x_ref, g_ref, o_ref):
    o_ref[...] = _rms(x_ref[...], g_ref[...]).astype(o_ref.dtype)


def _norm(x, g, bm):
    m, d = x.shape
    return pl.pallas_call(
        _norm_kernel,
        grid=(m // bm,),
        in_specs=[pl.BlockSpec((bm, d), lambda i: (i, 0)), pl.BlockSpec((1, d), lambda i: (0, 0))],
        out_specs=pl.BlockSpec((bm, d), lambda i: (i, 0)),
        out_shape=jax.ShapeDtypeStruct((m, d), BF16),
        compiler_params=_params("parallel"),
        name="norm_mix",
    )(x, g)


def _put(ref, lo, val, tr):
    w = val.shape[1]
    if tr:
        ref[0, lo:lo + w, :] = val.T.astype(ref.dtype)
    else:
        ref[:, lo:lo + w] = val.astype(ref.dtype)


def _put_v(ref, hd, val, tr):
    bm, w = val.shape
    if tr:
        lo = hd * (w + VT_ONES)
        ref[0, lo:lo + w, :] = val.T.astype(ref.dtype)
        ref[0, lo + w:lo + w + VT_ONES, :] = jnp.ones((VT_ONES, bm), ref.dtype)
    else:
        ref[:, hd * w:(hd + 1) * w] = val.astype(ref.dtype)


def _v_width(heads, dv, tr):
    return heads * (dv + VT_ONES) if tr else heads * dv


def _out(m, n, bm, dtype, tr):
    if tr:
        return pl.BlockSpec((1, n, bm), lambda i: (i, 0, 0)), jax.ShapeDtypeStruct((m // bm, n, bm), dtype)
    return pl.BlockSpec((bm, n), lambda i: (i, 0)), jax.ShapeDtypeStruct((m, n), dtype)


def _qproj_kernel(h_ref, wq_ref, gq_ref, wuq_ref, c_ref, s1_ref, s2_ref, o_ref, *, scale, tr):
    qlat = _rms(_dot(h_ref[...], wq_ref[...]), gq_ref[...]).astype(BF16)
    q = _dot(qlat, wuq_ref[...])
    c, s1, s2 = c_ref[...], s1_ref[...], s2_ref[...]
    for hd in range(MLA_HEADS):
        lo = hd * MLA_QK_PAD
        _put(o_ref, lo, q[:, lo:lo + LANES] * scale, tr)
        t = _rope_tile(q[:, lo + LANES:lo + 2 * LANES], c, s1, s2, MLA_ROPE // 2)
        _put(o_ref, lo + LANES, t * scale, tr)


def _qproj(h, wq, gq, wuq, tabs, bm, scale, tr):
    m, d = h.shape
    nt = tabs[0].shape[0] // bm
    full = lambda a: pl.BlockSpec(a.shape, lambda i: (0, 0))
    tab = pl.BlockSpec((bm, LANES), lambda i: (i % nt, 0))
    spec, shape = _out(m, wuq.shape[1], bm, BF16, tr)
    return pl.pallas_call(
        functools.partial(_qproj_kernel, scale=scale, tr=tr),
        grid=(m // bm,),
        in_specs=[pl.BlockSpec((bm, d), lambda i: (i, 0)), full(wq), full(gq), full(wuq), tab, tab, tab],
        out_specs=spec,
        out_shape=shape,
        compiler_params=_params("parallel"),
        name="mla_q_proj",
    )(h, wq, gq, wuq, *tabs)


def _kv_expand(ckv, krope_tile, wukv_ref, kmla_ref, vmla_ref, tr):
    kv = _dot(ckv.astype(BF16), wukv_ref[...])
    kr = krope_tile.astype(BF16)
    for hd in range(MLA_HEADS):
        lo = hd * MLA_QK_PAD
        kmla_ref[:, lo:lo + LANES] = kv[:, hd * MLA_NOPE:(hd + 1) * MLA_NOPE].astype(BF16)
        kmla_ref[:, lo + LANES:lo + 2 * LANES] = kr
        vlo = MLA_HEADS * MLA_NOPE + hd * MLA_V
        _put_v(vmla_ref, hd, kv[:, vlo:vlo + MLA_V], tr)


def _kvproj_kernel(h_ref, wkv_ref, gkv_ref, wukv_ref, c_ref, s1_ref, s2_ref,
                   ckv_ref, krope_ref, kmla_ref, vmla_ref, *, tr):
    z = _dot(h_ref[...], wkv_ref[...])
    ckv = _rms(z[:, :MLA_KV_LORA], gkv_ref[...])
    ckv_ref[...] = ckv
    t = _rope_tile(z[:, MLA_KV_LORA:], c_ref[...], s1_ref[...], s2_ref[...], MLA_ROPE // 2)
    krope_ref[...] = t[:, :MLA_ROPE]
    _kv_expand(ckv, t, wukv_ref, kmla_ref, vmla_ref, tr)


def _kvproj(h, wkv, gkv, wukv, tabs, bm, tr):
    m, d = h.shape
    nt = tabs[0].shape[0] // bm
    full = lambda a: pl.BlockSpec(a.shape, lambda i: (0, 0))
    tab = pl.BlockSpec((bm, LANES), lambda i: (i % nt, 0))
    row = lambda n: pl.BlockSpec((bm, n), lambda i: (i, 0))
    nk, nv = MLA_HEADS * MLA_QK_PAD, MLA_HEADS * MLA_V
    vspec, vshape = _out(m, _v_width(MLA_HEADS, MLA_V, tr), bm, BF16, tr)
    return pl.pallas_call(
        functools.partial(_kvproj_kernel, tr=tr),
        grid=(m // bm,),
        in_specs=[row(d), full(wkv), full(gkv), full(wukv), tab, tab, tab],
        out_specs=[row(MLA_KV_LORA), row(MLA_ROPE), row(nk), vspec],
        out_shape=[jax.ShapeDtypeStruct((m, MLA_KV_LORA), F32), jax.ShapeDtypeStruct((m, MLA_ROPE), F32),
                   jax.ShapeDtypeStruct((m, nk), BF16), vshape],
        compiler_params=_params("parallel"),
        name="mla_kv_proj",
    )(h, wkv, gkv, wukv, *tabs)


def _kvcache_kernel(ckv_ref, krope_ref, wukv_ref, kmla_ref, vmla_ref):
    _kv_expand(ckv_ref[...], krope_ref[...], wukv_ref, kmla_ref, vmla_ref, False)


def _kvcache_expand(ckv, krope_pad, wukv, bm):
    m = ckv.shape[0]
    row = lambda n: pl.BlockSpec((bm, n), lambda i: (i, 0))
    nk, nv = MLA_HEADS * MLA_QK_PAD, MLA_HEADS * MLA_V
    return pl.pallas_call(
        _kvcache_kernel,
        grid=(m // bm,),
        in_specs=[row(MLA_KV_LORA), row(LANES), pl.BlockSpec(wukv.shape, lambda i: (0, 0))],
        out_specs=[row(nk), row(nv)],
        out_shape=[jax.ShapeDtypeStruct((m, nk), BF16), jax.ShapeDtypeStruct((m, nv), BF16)],
        compiler_params=_params("parallel"),
        name="mla_kv_cache_expand",
    )(ckv, krope_pad, wukv)


def _diffqk_kernel(h_ref, wdq_ref, wdk_ref, c_ref, s1_ref, s2_ref,
                   q1_ref, q2_ref, dk_ref, dkb_ref, *, scale, tr):
    h = h_ref[...]
    zq = _dot(h, wdq_ref[...])
    zk = _dot(h, wdk_ref[...])
    c, s1, s2 = c_ref[...], s1_ref[...], s2_ref[...]
    first = lax.broadcasted_iota(jnp.int32, (1, LANES), 1) < DIFF_DH
    for hd in range(DIFF_HEADS):
        sl = slice(hd * LANES, (hd + 1) * LANES)
        q = _rope_tile(zq[:, sl], c, s1, s2, DIFF_ROT // 2) * scale
        _put(q1_ref, hd * LANES, jnp.where(first, q, 0.0), tr)
        _put(q2_ref, hd * LANES, jnp.where(first, 0.0, q), tr)
        k = _rope_tile(zk[:, sl], c, s1, s2, DIFF_ROT // 2)
        dk_ref[:, sl] = k
        dkb_ref[:, sl] = k.astype(BF16)


def _diffqk(h, wdq, wdk, tabs, bm, scale, tr):
    m, d = h.shape
    nt = tabs[0].shape[0] // bm
    n = wdq.shape[1]
    full = lambda a: pl.BlockSpec(a.shape, lambda i: (0, 0))
    tab = pl.BlockSpec((bm, LANES), lambda i: (i % nt, 0))
    row = lambda w: pl.BlockSpec((bm, w), lambda i: (i, 0))
    qspec, qshape = _out(m, n, bm, BF16, tr)
    return pl.pallas_call(
        functools.partial(_diffqk_kernel, scale=scale, tr=tr),
        grid=(m // bm,),
        in_specs=[row(d), full(wdq), full(wdk), tab, tab, tab],
        out_specs=[qspec, qspec, row(n), row(n)],
        out_shape=[qshape, qshape, jax.ShapeDtypeStruct((m, n), F32), jax.ShapeDtypeStruct((m, n), BF16)],
        compiler_params=_params("parallel"),
        name="diff_qk_proj",
    )(h, wdq, wdk, *tabs)


def _diffv_kernel(h_ref, wdv_ref, dv_ref, dvb_ref, *, tr):
    z = _dot(h_ref[...], wdv_ref[...])
    dv_ref[...] = z
    for hd in range(DIFF_HEADS):
        _put_v(dvb_ref, hd, z[:, hd * DIFF_VD:(hd + 1) * DIFF_VD], tr)


def _diffv(h, wdv, bm, tr):
    m, d = h.shape
    n = wdv.shape[1]
    row = lambda w: pl.BlockSpec((bm, w), lambda i: (i, 0))
    vspec, vshape = _out(m, _v_width(DIFF_HEADS, DIFF_VD, tr), bm, BF16, tr)
    return pl.pallas_call(
        functools.partial(_diffv_kernel, tr=tr),
        grid=(m // bm,),
        in_specs=[row(d), pl.BlockSpec(wdv.shape, lambda i: (0, 0))],
        out_specs=[row(n), vspec],
        out_shape=[jax.ShapeDtypeStruct((m, n), F32), vshape],
        compiler_params=_params("parallel"),
        name="diff_v_proj",
    )(h, wdv)


def _merge_kernel(h_ref, oa_ref, ob_ref, wga_ref, wgb_ref, wa_ref, wb_ref, o_ref):
    h = h_ref[...]
    ga = jax.nn.sigmoid(_dot(h, wga_ref[...]))
    gb = jax.nn.sigmoid(_dot(h, wgb_ref[...]))
    ya = _dot(oa_ref[...], wa_ref[...])
    yb = _dot(ob_ref[...], wb_ref[...])
    o_ref[...] = (ga * ya + gb * yb).astype(o_ref.dtype)


def _merge(h, oa, ob, wga, wgb, wa, wb, bm, bn):
    m, d = h.shape
    n = wga.shape[1]
    row = lambda a: pl.BlockSpec((bm, a.shape[1]), lambda i, j: (i, 0))
    col = lambda a: pl.BlockSpec((a.shape[0], bn), lambda i, j: (0, j))
    return pl.pallas_call(
        _merge_kernel,
        grid=(m // bm, n // bn),
        in_specs=[row(h), row(oa), row(ob), col(wga), col(wgb), col(wa), col(wb)],
        out_specs=pl.BlockSpec((bm, bn), lambda i, j: (i, j)),
        out_shape=jax.ShapeDtypeStruct((m, n), BF16),
        compiler_params=_params("parallel", "arbitrary"),
        name="gated_merge",
    )(h, oa, ob, wga, wgb, wa, wb)


def _outproj_kernel(x_ref, mg_ref, wo_ref, g_ref, x2_ref, h2_ref):
    x2 = x_ref[...] + _dot(mg_ref[...], wo_ref[...])
    x2_ref[...] = x2
    h2_ref[...] = _rms(x2, g_ref[...]).astype(BF16)


def _outproj(x, mg, wo, g, bm):
    m, d = x.shape
    row = pl.BlockSpec((bm, d), lambda i: (i, 0))
    return pl.pallas_call(
        _outproj_kernel,
        grid=(m // bm,),
        in_specs=[row, row, pl.BlockSpec(wo.shape, lambda i: (0, 0)), pl.BlockSpec((1, d), lambda i: (0, 0))],
        out_specs=[row, row],
        out_shape=[jax.ShapeDtypeStruct((m, d), F32), jax.ShapeDtypeStruct((m, d), BF16)],
        compiler_params=_params("parallel"),
        name="out_proj_residual",
    )(x, mg, wo, g)


def _ffn_kernel(h2_ref, wg_ref, wu_ref, wd_ref, x2_ref, gf_ref, y_ref, acc_ref):
    f = pl.program_id(1)

    @pl.when(f == 0)
    def _():
        acc_ref[...] = jnp.zeros_like(acc_ref)

    h2 = h2_ref[...]
    a = (jax.nn.silu(_dot(h2, wg_ref[...])) * _dot(h2, wu_ref[...])).astype(BF16)
    acc_ref[...] += _dot(a, wd_ref[...])

    @pl.when(f == pl.num_programs(1) - 1)
    def _():
        y_ref[...] = _rms(x2_ref[...] + acc_ref[...], gf_ref[...])


def _ffn(h2, w_in, w_out, x2, gf, bm, bf):
    m, d = h2.shape
    dff = w_out.shape[0]
    nf = dff // bf
    row = pl.BlockSpec((bm, d), lambda i, f: (i, 0))
    return pl.pallas_call(
        _ffn_kernel,
        grid=(m // bm, nf),
        in_specs=[row,
                  pl.BlockSpec((d, bf), lambda i, f: (0, f)),
                  pl.BlockSpec((d, bf), lambda i, f: (0, f + nf)),
                  pl.BlockSpec((bf, d), lambda i, f: (f, 0)),
                  row,
                  pl.BlockSpec((1, d), lambda i, f: (0, 0))],
        out_specs=row,
        out_shape=jax.ShapeDtypeStruct((m, d), F32),
        scratch_shapes=[pltpu.VMEM((bm, d), F32)],
        compiler_params=_params("parallel", "arbitrary"),
        name="ffn_swiglu_final_norm",
    )(h2, w_in, w_in, w_out, x2, gf)


def _softmax_step(s, v, m, l, acc):
    m_new = jnp.maximum(m, jnp.max(s, axis=-1, keepdims=True))
    alpha = jnp.exp2(m - m_new)
    p = jnp.exp2(s - m_new)
    l = alpha * l + jnp.sum(p, axis=-1, keepdims=True)
    acc = alpha * acc + _dot(p.astype(BF16), v)
    return m_new, l, acc


def _chunk_mask(tq, tk, q0, k0):
    qp = q0 + lax.broadcasted_iota(jnp.int32, (tq, tk), 0)
    kp = k0 + lax.broadcasted_iota(jnp.int32, (tq, tk), 1)
    return (kp // CHUNK) <= (qp // CHUNK)


def _softmax_init(tq, dv):
    return (jnp.full((tq, 1), NEG_INF, F32), jnp.zeros((tq, 1), F32), jnp.zeros((tq, dv), F32))


def _chunk_mask_t(t):
    kp = lax.broadcasted_iota(jnp.int32, (t, t), 0)
    qp = lax.broadcasted_iota(jnp.int32, (t, t), 1)
    return (kp // CHUNK) <= (qp // CHUNK)


def _probs_t(s, smax, m):
    m_new = jnp.maximum(m, smax)
    return jnp.exp2(s - m_new).astype(BF16), jnp.exp2(m - m_new), m_new


def _attention_scratch(t, dv, streams):
    return [pltpu.VMEM((streams, 2, t, t), F32),
            pltpu.VMEM((streams, 2, t, t), BF16),
            pltpu.VMEM((streams, dv + VT_ONES, t), F32),
            pltpu.VMEM((streams, 8, t), F32)]


def _causal_attention_t(n, scores, values, finish, s_ref, p_ref, acc_ref, stat_ref, t, streams):
    row_m, row_a, row_smax = 0, 1, 2

    def stat(st, r):
        return stat_ref[st, r:r + 1, :]

    def put_scores(st, par, s):
        s_ref[st, par] = s
        stat_ref[st, row_smax + par:row_smax + par + 1, :] = jnp.max(s, axis=0, keepdims=True)

    def step(tau, par):
        for st in range(streams):
            put_scores(st, par, scores(tau, st))
            p, alpha, m = _probs_t(s_ref[st, 1 - par], stat(st, row_smax + 1 - par), stat(st, row_m))
            p_ref[st, 1 - par] = p
            pv = _dot(values(jnp.maximum(tau - 2, 0), st), p_ref[st, par])
            acc_ref[st] = stat(st, row_a) * acc_ref[st] + pv
            stat_ref[st, row_m:row_m + 1, :] = m
            stat_ref[st, row_a:row_a + 1, :] = alpha

    for st in range(streams):
        put_scores(st, 0, scores(0, st))
        p_ref[st, 1] = jnp.zeros((t, t), BF16)
        acc_ref[st] = jnp.zeros(acc_ref.shape[1:], F32)
        stat_ref[st, row_m:row_m + 1, :] = jnp.full((1, t), NEG_INF, F32)
        stat_ref[st, row_a:row_a + 1, :] = jnp.ones((1, t), F32)

    def pair(u, carry):
        step(2 * u + 1, 1)
        step(2 * u + 2, 0)
        return carry

    lax.fori_loop(0, n // 2, pair, 0)

    @pl.when(n % 2 == 1)
    def _():
        step(n, 1)

    def tail(par):
        mask = _chunk_mask_t(t)
        res = []
        for st in range(streams):
            s = jnp.where(mask, s_ref[st, par], NEG_INF)
            p, alpha, _ = _probs_t(s, jnp.max(s, axis=0, keepdims=True), stat(st, row_m))
            acc = stat(st, row_a) * acc_ref[st] + _dot(values(jnp.maximum(n - 1, 0), st), p_ref[st, 1 - par])
            res.append(alpha * acc + _dot(values(n, st), p))
        finish(res)

    for par in (0, 1):
        pl.when(n % 2 == par)(functools.partial(tail, par))


def _normalised(acc, dv):
    return (acc[:dv] / acc[dv:dv + 1]).T


def _mla_prompt_kernel(qt_ref, k_ref, vt_ref, o_ref, *scratch, t, heads):
    def scores(j, hd):
        k = k_ref[0, pl.ds(pl.multiple_of(j * t, t), t), hd * MLA_QK_PAD:(hd + 1) * MLA_QK_PAD]
        return _dot(k, qt_ref[0, hd * MLA_QK_PAD:(hd + 1) * MLA_QK_PAD, :])

    def values(j, hd):
        return vt_ref[j, hd * (MLA_V + VT_ONES):(hd + 1) * (MLA_V + VT_ONES), :]

    def finish(res):
        for hd, acc in enumerate(res):
            o_ref[0, :, hd * MLA_V:(hd + 1) * MLA_V] = _normalised(acc, MLA_V).astype(o_ref.dtype)

    _causal_attention_t(pl.program_id(2), scores, values, finish, *scratch, t, heads)


def _mla_prompt(qt, k, vt, t, heads):
    b, l, _ = k.shape
    nq = l // t
    return pl.pallas_call(
        functools.partial(_mla_prompt_kernel, t=t, heads=heads),
        grid=(b, MLA_HEADS // heads, nq),
        in_specs=[pl.BlockSpec((1, heads * MLA_QK_PAD, t), lambda b, h, i: (b * nq + i, h, 0)),
                  pl.BlockSpec((1, l, heads * MLA_QK_PAD), lambda b, h, i: (b, 0, h)),
                  pl.BlockSpec((nq, heads * (MLA_V + VT_ONES), t), lambda b, h, i: (b, h, 0))],
        out_specs=pl.BlockSpec((1, t, heads * MLA_V), lambda b, h, i: (b, i, h)),
        out_shape=jax.ShapeDtypeStruct((b, l, MLA_HEADS * MLA_V), BF16),
        scratch_shapes=_attention_scratch(t, MLA_V, heads),
        compiler_params=_params("parallel", "parallel", "arbitrary"),
        name="mla_attention_prompt",
    )(qt, k, vt)


def _lambda(lq1_ref, lk1_ref, lq2_ref, lk2_ref, lambda_init):
    a = jnp.sum(lq1_ref[...] * lk1_ref[...], axis=-1, keepdims=True)
    b = jnp.sum(lq2_ref[...] * lk2_ref[...], axis=-1, keepdims=True)
    return jnp.exp(a) - jnp.exp(b) + lambda_init


def _diff_finish(c1, c2, lam, g, lambda_init):
    o = c1[2] / c1[1] - lam * (c2[2] / c2[1])
    return _rms(o, g) * (1.0 - lambda_init)


def _diff_prompt_kernel(q1t_ref, q2t_ref, k_ref, vt_ref, lq1_ref, lk1_ref, lq2_ref, lk2_ref, g_ref, o_ref,
                        *scratch, t, lambda_init):
    qt_refs = (q1t_ref, q2t_ref)

    def scores(j, st):
        return _dot(k_ref[0, pl.ds(pl.multiple_of(j * t, t), t), :], qt_refs[st][0])

    def values(j, st):
        return vt_ref[j]

    def finish(res):
        lam = _lambda(lq1_ref, lk1_ref, lq2_ref, lk2_ref, lambda_init)
        o = _normalised(res[0], DIFF_VD) - lam * _normalised(res[1], DIFF_VD)
        o_ref[0] = (_rms(o, g_ref[...]) * (1.0 - lambda_init)).astype(o_ref.dtype)

    _causal_attention_t(pl.program_id(2), scores, values, finish, *scratch, t, 2)


def _diff_prompt(q1t, q2t, k, vt, lams, g, t, lambda_init):
    b, l, _ = k.shape
    nq = l // t
    qs = pl.BlockSpec((1, LANES, t), lambda b, h, i: (b * nq + i, h, 0))
    small = lambda a: pl.BlockSpec(a.shape, lambda b, h, i: (0, 0))
    return pl.pallas_call(
        functools.partial(_diff_prompt_kernel, t=t, lambda_init=lambda_init),
        grid=(b, DIFF_HEADS, nq),
        in_specs=[qs, qs,
                  pl.BlockSpec((1, l, LANES), lambda b, h, i: (b, 0, h)),
                  pl.BlockSpec((nq, DIFF_VD + VT_ONES, t), lambda b, h, i: (b, h, 0))]
                 + [small(a) for a in lams] + [small(g)],
        out_specs=pl.BlockSpec((1, t, DIFF_VD), lambda b, h, i: (b, i, h)),
        out_shape=jax.ShapeDtypeStruct((b, l, DIFF_HEADS * DIFF_VD), BF16),
        scratch_shapes=_attention_scratch(t, DIFF_VD, 2),
        compiler_params=_params("parallel", "parallel", "arbitrary"),
        name="diff_attention_prompt",
    )(q1t, q2t, k, vt, *lams, g)


def _two_part_softmax(q, kc, vc, kn, vn, mask_c, mask_n):
    tq = q.shape[0]
    carry = _softmax_init(tq, vc.shape[1])
    carry = _softmax_step(jnp.where(mask_c, _dot_nt(q, kc), NEG_INF), vc, *carry)
    return _softmax_step(jnp.where(mask_n, _dot_nt(q, kn), NEG_INF), vn, *carry)


def _mla_sample_kernel(q_ref, kc_ref, vc_ref, kn_ref, vn_ref, o_ref, *, past):
    tq, tc = q_ref.shape[1], kc_ref.shape[1]
    _, l, acc = _two_part_softmax(q_ref[0], kc_ref[0], vc_ref[0], kn_ref[0], vn_ref[0],
                                  _chunk_mask(tq, tc, past, 0), _chunk_mask(tq, tq, past, past))
    o_ref[0] = (acc / l).astype(o_ref.dtype)


def _mla_sample(q, kc, vc, kn, vn):
    b, tq, _ = q.shape
    tc = kc.shape[1]
    blk = lambda rows, w: pl.BlockSpec((1, rows, w), lambda b, h: (b, 0, h))
    return pl.pallas_call(
        functools.partial(_mla_sample_kernel, past=tc),
        grid=(b, MLA_HEADS),
        in_specs=[blk(tq, MLA_QK_PAD), blk(tc, MLA_QK_PAD), blk(tc, MLA_V), blk(tq, MLA_QK_PAD), blk(tq, MLA_V)],
        out_specs=blk(tq, MLA_V),
        out_shape=jax.ShapeDtypeStruct((b, tq, MLA_HEADS * MLA_V), BF16),
        compiler_params=_params("parallel", "parallel"),
        name="mla_attention_sample",
    )(q, kc, vc, kn, vn)


def _diff_sample_kernel(q1_ref, q2_ref, kc_ref, vc_ref, kn_ref, vn_ref,
                        lq1_ref, lk1_ref, lq2_ref, lk2_ref, g_ref, o_ref, *, past, lambda_init):
    tq, tc = q1_ref.shape[1], kc_ref.shape[1]
    kc, vc = kc_ref[0].astype(BF16), vc_ref[0].astype(BF16)
    kn, vn = kn_ref[0], vn_ref[0]
    mask_c, mask_n = _chunk_mask(tq, tc, past, 0), _chunk_mask(tq, tq, past, past)
    c1 = _two_part_softmax(q1_ref[0], kc, vc, kn, vn, mask_c, mask_n)
    c2 = _two_part_softmax(q2_ref[0], kc, vc, kn, vn, mask_c, mask_n)
    lam = _lambda(lq1_ref, lk1_ref, lq2_ref, lk2_ref, lambda_init)
    o_ref[0] = _diff_finish(c1, c2, lam, g_ref[...], lambda_init).astype(o_ref.dtype)


def _diff_sample(q1, q2, kc, vc, kn, vn, lams, g, lambda_init):
    b, tq, _ = q1.shape
    tc = kc.shape[1]
    blk = lambda rows: pl.BlockSpec((1, rows, LANES), lambda b, h: (b, 0, h))
    small = lambda a: pl.BlockSpec(a.shape, lambda b, h: (0, 0))
    return pl.pallas_call(
        functools.partial(_diff_sample_kernel, past=tc, lambda_init=lambda_init),
        grid=(b, DIFF_HEADS),
        in_specs=[blk(tq), blk(tq), blk(tc), blk(tc), blk(tq), blk(tq)] + [small(a) for a in lams] + [small(g)],
        out_specs=blk(tq),
        out_shape=jax.ShapeDtypeStruct((b, tq, DIFF_HEADS * DIFF_VD), BF16),
        compiler_params=_params("parallel", "parallel"),
        name="diff_attention_sample",
    )(q1, q2, kc, vc, kn, vn, *lams, g)


def _rope_tables(pos, theta, rot_dim, period):
    half = rot_dim // 2
    inv = 1.0 / (jnp.float32(theta) ** (jnp.arange(half, dtype=F32) / half))
    ang = pos.astype(F32)[:, None] * inv[None, :]
    cos, sin = jnp.cos(ang), jnp.sin(ang)
    n = pos.shape[0]
    rest = period - rot_dim
    c = jnp.concatenate([cos, cos, jnp.ones((n, rest), F32)], axis=1)
    s1 = jnp.concatenate([-sin, jnp.zeros((n, half + rest), F32)], axis=1)
    s2 = jnp.concatenate([jnp.zeros((n, half), F32), sin, jnp.zeros((n, rest), F32)], axis=1)
    reps = LANES // period
    return tuple(jnp.tile(a, (1, reps)) for a in (c, s1, s2))


def _layer_weights(w_in, mla_w_uq, mla_w_ukv):
    c_kv = MLA_Q_LORA
    c_kr = c_kv + MLA_KV_LORA
    c_dq = c_kr + MLA_ROPE
    dqk = DIFF_HEADS * 2 * DIFF_DH
    c_dk = c_dq + dqk
    c_dv = c_dk + dqk
    c_ga = c_dv + DIFF_HEADS * DIFF_VD
    d = w_in.shape[0]
    c_gb = c_ga + d
    wb = w_in.astype(BF16)
    wq = wb[:, :c_kv]
    wkv = jnp.pad(wb[:, c_kv:c_dq], ((0, 0), (0, LANES - MLA_ROPE)))
    wdq, wdk, wdv = wb[:, c_dq:c_dk], wb[:, c_dk:c_dv], wb[:, c_dv:c_ga]
    wga, wgb = wb[:, c_ga:c_gb], wb[:, c_gb:]
    uq = mla_w_uq.astype(BF16).reshape(MLA_Q_LORA, MLA_HEADS, MLA_NOPE + MLA_ROPE)
    uq = jnp.pad(uq, ((0, 0), (0, 0), (0, MLA_QK_PAD - MLA_NOPE - MLA_ROPE))).reshape(MLA_Q_LORA, -1)
    ukv = mla_w_ukv.astype(BF16).reshape(MLA_KV_LORA, MLA_HEADS, MLA_NOPE + MLA_V)
    ukv = jnp.concatenate([ukv[:, :, :MLA_NOPE].reshape(MLA_KV_LORA, -1),
                           ukv[:, :, MLA_NOPE:].reshape(MLA_KV_LORA, -1)], axis=1)
    return wq, wkv, wdq, wdk, wdv, wga, wgb, uq, ukv


def _block(m, want):
    return want if m % want == 0 else m


def kernel(x_prompt, x_sample, cache_mla_ckv, cache_mla_krope, cache_diff_k, cache_diff_v, norm_mix, w_in,
           mla_q_norm, mla_w_uq, mla_kv_norm, mla_w_ukv, diff_lq1, diff_lk1, diff_lq2, diff_lk2, diff_subln,
           w_branch_a, w_branch_b, w_out, norm_ffn, w_ffn_in, w_ffn_out, norm_final):
    bp, lp, d = x_prompt.shape
    bs, ls, _ = x_sample.shape
    depth, _, past, _ = cache_mla_ckv.shape
    assert depth == 1, "the FFN kernel fuses the final norm, so it serves the last (only) layer"
    mp, ms = bp * lp, bs * ls
    t_attn = 512

    pos_p = jnp.arange(lp)
    pos_s = past + jnp.arange(ls)
    tabs_mla_p = _rope_tables(pos_p, MLA_THETA, MLA_ROPE, LANES)
    tabs_mla_s = tuple(jnp.tile(a, (bs, 1)) for a in _rope_tables(pos_s, MLA_THETA, MLA_ROPE, LANES))
    tabs_dif_p = _rope_tables(pos_p, ROPE_THETA, DIFF_ROT, DIFF_DH)
    tabs_dif_s = tuple(jnp.tile(a, (bs, 1)) for a in _rope_tables(pos_s, ROPE_THETA, DIFF_ROT, DIFF_DH))
    q_scale_mla = (MLA_NOPE + MLA_ROPE) ** -0.5 * LOG2E
    q_scale_dif = DIFF_DH ** -0.5 * LOG2E
    gfinal = norm_final.reshape(1, d)

    xp = x_prompt.reshape(mp, d)
    xs = x_sample.reshape(ms, d)
    rows_p, rows_s = [], []
    for l in range(depth):
        lambda_init = 0.8 - 0.6 * math.exp(-0.3 * l)
        wq, wkv, wdq, wdk, wdv, wga, wgb, uq, ukv = _layer_weights(w_in[l], mla_w_uq[l], mla_w_ukv[l])
        gmix, gq, gkv = norm_mix[l].reshape(1, -1), mla_q_norm[l].reshape(1, -1), mla_kv_norm[l].reshape(1, -1)
        lams = tuple(a[l].reshape(1, -1) for a in (diff_lq1, diff_lk1, diff_lq2, diff_lk2))
        gsub = diff_subln[l].reshape(1, -1)
        wa, wbr, wo = w_branch_a[l].astype(BF16), w_branch_b[l].astype(BF16), w_out[l].astype(BF16)
        wfi, wfo = w_ffn_in[l].astype(BF16), w_ffn_out[l].astype(BF16)
        gffn = norm_ffn[l].reshape(1, -1)

        def stage1(x, bm, tabs_mla, tabs_dif, tr):
            h = _norm(x, gmix, bm)
            q = _qproj(h, wq, gq, uq, tabs_mla, bm, q_scale_mla, tr)
            ckv, krope, kmla, vmla = _kvproj(h, wkv, gkv, ukv, tabs_mla, bm, tr)
            q1, q2, dk, dkb = _diffqk(h, wdq, wdk, tabs_dif, bm, q_scale_dif, tr)
            dv, dvb = _diffv(h, wdv, bm, tr)
            return h, q, ckv, krope, kmla, vmla, q1, q2, dk, dkb, dv, dvb

        def stage2(x, h, oa, ob, bm, bn, bf):
            mg = _merge(h, oa, ob, wga, wgb, wa, wbr, bm, bn)
            x2, h2 = _outproj(x, mg, wo, gffn, min(bm, 256))
            return _ffn(h2, wfi, wfo, x2, gfinal, min(bm, 512), bf)

        h, q, ckv, krope, kmla, vmla, q1, q2, dk, dkb, dv, dvb = stage1(xp, t_attn, tabs_mla_p, tabs_dif_p, True)
        r3 = lambda a: a.reshape(bp, lp, -1)
        oa = _mla_prompt(q, r3(kmla), vmla, t_attn, 2)
        ob = _diff_prompt(q1, q2, r3(dkb), dvb, lams, gsub, t_attn, lambda_init)
        xp = stage2(xp, h, oa.reshape(mp, -1), ob.reshape(mp, -1), _block(mp, 1024), 512, 512)
        rows_p.append((ckv.reshape(bp, lp, -1), krope.reshape(bp, lp, -1),
                       dk.reshape(bp, lp, DIFF_HEADS, -1), dv.reshape(bp, lp, DIFF_HEADS, -1)))

        h, q, ckv, krope, kmla, vmla, q1, q2, dk, dkb, dv, dvb = stage1(xs, ms, tabs_mla_s, tabs_dif_s, False)
        kr_pad = jnp.pad(cache_mla_krope[l].reshape(bs * past, MLA_ROPE), ((0, 0), (0, LANES - MLA_ROPE)))
        kc, vc = _kvcache_expand(cache_mla_ckv[l].reshape(bs * past, -1), kr_pad, ukv, _block(bs * past, 512))
        r3 = lambda a: a.reshape(bs, ls, -1)
        c3 = lambda a: a.reshape(bs, past, -1)
        oa = _mla_sample(r3(q), c3(kc), c3(vc), r3(kmla), r3(vmla))
        ob = _diff_sample(r3(q1), r3(q2), c3(cache_diff_k[l]), c3(cache_diff_v[l]), r3(dkb), r3(dvb),
                          lams, gsub, lambda_init)
        xs = stage2(xs, h, oa.reshape(ms, -1), ob.reshape(ms, -1), ms, 512, 512)
        rows_s.append((ckv.reshape(bs, ls, -1), krope.reshape(bs, ls, -1),
                       dk.reshape(bs, ls, DIFF_HEADS, -1), dv.reshape(bs, ls, DIFF_HEADS, -1)))

    y_prompt = xp.reshape(bp, lp, d)
    y_sample = xs.reshape(bs, ls, d)
    stack = lambda rows, i: jnp.stack([r[i] for r in rows], axis=0)
    return (y_prompt, y_sample,
            stack(rows_p, 0), stack(rows_p, 1), stack(rows_p, 2), stack(rows_p, 3),
            stack(rows_s, 0), stack(rows_s, 1), stack(rows_s, 2), stack(rows_s, 3))
```

```python
import functools
import math

import jax
import jax.numpy as jnp
from jax import lax
from jax.experimental import pallas as pl
from jax.experimental.pallas import tpu as pltpu

F32 = jnp.float32
BF16 = jnp.bfloat16

CHUNK = 64
EPS = 1e-6
NEG_INF = -1e30

MLA_HEADS = 8
MLA_Q_LORA = 512
MLA_KV_LORA = 256
MLA_NOPE = 128
MLA_ROPE = 64
MLA_V = 128
MLA_THETA = 10000.0
MLA_QK_PAD = 256

DIFF_HEADS = 8
DIFF_DH = 64
DIFF_VD = 2 * DIFF_DH
DIFF_ROT = DIFF_DH // 4
ROPE_THETA = 500000.0

LANES = 128
VT_ONES = 16
LOG2E = math.log2(math.e)
VMEM_LIMIT = 56 * 1024 * 1024


def _params(*sem):
    return pltpu.CompilerParams(dimension_semantics=sem, vmem_limit_bytes=VMEM_LIMIT)


def _dot(a, b):
    return jnp.dot(a, b, preferred_element_type=F32)


def _dot_nt(a, b):
    return lax.dot_general(a, b, (((1,), (1,)), ((), ())), preferred_element_type=F32)


def _rms(x, g):
    return x * lax.rsqrt(jnp.mean(x * x, axis=-1, keepdims=True) + EPS) * g


def _rope_tile(t, c, s1, s2, half):
    return t * c + pltpu.roll(t, LANES - half, 1) * s1 + pltpu.roll(t, half, 1) * s2


def _norm_kernel(x_ref, g_ref, o_ref):
    o_ref[...] = _rms(x_ref[...], g_ref[...]).astype(o_ref.dtype)


def _norm(x, g, bm):
    m, d = x.shape
    return pl.pallas_call(
        _norm_kernel,
        grid=(m // bm,),
        in_specs=[pl.BlockSpec((bm, d), lambda i: (i, 0)), pl.BlockSpec((1, d), lambda i: (0, 0))],
        out_specs=pl.BlockSpec((bm, d), lambda i: (i, 0)),
        out_shape=jax.ShapeDtypeStruct((m, d), BF16),
        compiler_params=_params("parallel"),
        name="norm_mix",
    )(x, g)


def _put(ref, lo, val, tr):
    w = val.shape[1]
    if tr:
        ref[0, lo:lo + w, :] = val.T.astype(ref.dtype)
    else:
        ref[:, lo:lo + w] = val.astype(ref.dtype)


def _put_v(ref, hd, val, tr):
    bm, w = val.shape
    if tr:
        lo = hd * (w + VT_ONES)
        ref[0, lo:lo + w, :] = val.T.astype(ref.dtype)
        ref[0, lo + w:lo + w + VT_ONES, :] = jnp.ones((VT_ONES, bm), ref.dtype)
    else:
        ref[:, hd * w:(hd + 1) * w] = val.astype(ref.dtype)


def _v_width(heads, dv, tr):
    return heads * (dv + VT_ONES) if tr else heads * dv


def _out(m, n, bm, dtype, tr):
    if tr:
        return pl.BlockSpec((1, n, bm), lambda i: (i, 0, 0)), jax.ShapeDtypeStruct((m // bm, n, bm), dtype)
    return pl.BlockSpec((bm, n), lambda i: (i, 0)), jax.ShapeDtypeStruct((m, n), dtype)


def _qproj_kernel(h_ref, wq_ref, gq_ref, wuq_ref, c_ref, s1_ref, s2_ref, o_ref, *, scale, tr):
    qlat = _rms(_dot(h_ref[...], wq_ref[...]), gq_ref[...]).astype(BF16)
    q = _dot(qlat, wuq_ref[...])
    c, s1, s2 = c_ref[...], s1_ref[...], s2_ref[...]
    for hd in range(MLA_HEADS):
        lo = hd * MLA_QK_PAD
        _put(o_ref, lo, q[:, lo:lo + LANES] * scale, tr)
        t = _rope_tile(q[:, lo + LANES:lo + 2 * LANES], c, s1, s2, MLA_ROPE // 2)
        _put(o_ref, lo + LANES, t * scale, tr)


def _qproj(h, wq, gq, wuq, tabs, bm, scale, tr):
    m, d = h.shape
    nt = tabs[0].shape[0] // bm
    full = lambda a: pl.BlockSpec(a.shape, lambda i: (0, 0))
    tab = pl.BlockSpec((bm, LANES), lambda i: (i % nt, 0))
    spec, shape = _out(m, wuq.shape[1], bm, BF16, tr)
    return pl.pallas_call(
        functools.partial(_qproj_kernel, scale=scale, tr=tr),
        grid=(m // bm,),
        in_specs=[pl.BlockSpec((bm, d), lambda i: (i, 0)), full(wq), full(gq), full(wuq), tab, tab, tab],
        out_specs=spec,
        out_shape=shape,
        compiler_params=_params("parallel"),
        name="mla_q_proj",
    )(h, wq, gq, wuq, *tabs)


def _kv_expand(ckv, krope_tile, wukv_ref, kmla_ref, vmla_ref, tr):
    kv = _dot(ckv.astype(BF16), wukv_ref[...])
    kr = krope_tile.astype(BF16)
    for hd in range(MLA_HEADS):
        lo = hd * MLA_QK_PAD
        kmla_ref[:, lo:lo + LANES] = kv[:, hd * MLA_NOPE:(hd + 1) * MLA_NOPE].astype(BF16)
        kmla_ref[:, lo + LANES:lo + 2 * LANES] = kr
        vlo = MLA_HEADS * MLA_NOPE + hd * MLA_V
        _put_v(vmla_ref, hd, kv[:, vlo:vlo + MLA_V], tr)


def _kvproj_kernel(h_ref, wkv_ref, gkv_ref, wukv_ref, c_ref, s1_ref, s2_ref,
                   ckv_ref, krope_ref, kmla_ref, vmla_ref, *, tr):
    z = _dot(h_ref[...], wkv_ref[...])
    ckv = _rms(z[:, :MLA_KV_LORA], gkv_ref[...])
    ckv_ref[...] = ckv
    t = _rope_tile(z[:, MLA_KV_LORA:], c_ref[...], s1_ref[...], s2_ref[...], MLA_ROPE // 2)
    krope_ref[...] = t[:, :MLA_ROPE]
    _kv_expand(ckv, t, wukv_ref, kmla_ref, vmla_ref, tr)


def _kvproj(h, wkv, gkv, wukv, tabs, bm, tr):
    m, d = h.shape
    nt = tabs[0].shape[0] // bm
    full = lambda a: pl.BlockSpec(a.shape, lambda i: (0, 0))
    tab = pl.BlockSpec((bm, LANES), lambda i: (i % nt, 0))
    row = lambda n: pl.BlockSpec((bm, n), lambda i: (i, 0))
    nk, nv = MLA_HEADS * MLA_QK_PAD, MLA_HEADS * MLA_V
    vspec, vshape = _out(m, _v_width(MLA_HEADS, MLA_V, tr), bm, BF16, tr)
    return pl.pallas_call(
        functools.partial(_kvproj_kernel, tr=tr),
        grid=(m // bm,),
        in_specs=[row(d), full(wkv), full(gkv), full(wukv), tab, tab, tab],
        out_specs=[row(MLA_KV_LORA), row(MLA_ROPE), row(nk), vspec],
        out_shape=[jax.ShapeDtypeStruct((m, MLA_KV_LORA), F32), jax.ShapeDtypeStruct((m, MLA_ROPE), F32),
                   jax.ShapeDtypeStruct((m, nk), BF16), vshape],
        compiler_params=_params("parallel"),
        name="mla_kv_proj",
    )(h, wkv, gkv, wukv, *tabs)


def _kvcache_kernel(ckv_ref, krope_ref, wukv_ref, kmla_ref, vmla_ref):
    _kv_expand(ckv_ref[...], krope_ref[...], wukv_ref, kmla_ref, vmla_ref, False)


def _kvcache_expand(ckv, krope_pad, wukv, bm):
    m = ckv.shape[0]
    row = lambda n: pl.BlockSpec((bm, n), lambda i: (i, 0))
    nk, nv = MLA_HEADS * MLA_QK_PAD, MLA_HEADS * MLA_V
    return pl.pallas_call(
        _kvcache_kernel,
        grid=(m // bm,),
        in_specs=[row(MLA_KV_LORA), row(LANES), pl.BlockSpec(wukv.shape, lambda i: (0, 0))],
        out_specs=[row(nk), row(nv)],
        out_shape=[jax.ShapeDtypeStruct((m, nk), BF16), jax.ShapeDtypeStruct((m, nv), BF16)],
        compiler_params=_params("parallel"),
        name="mla_kv_cache_expand",
    )(ckv, krope_pad, wukv)


def _diffqk_kernel(h_ref, wdq_ref, wdk_ref, c_ref, s1_ref, s2_ref,
                   q1_ref, q2_ref, dk_ref, dkb_ref, *, scale, tr):
    h = h_ref[...]
    zq = _dot(h, wdq_ref[...])
    zk = _dot(h, wdk_ref[...])
    c, s1, s2 = c_ref[...], s1_ref[...], s2_ref[...]
    first = lax.broadcasted_iota(jnp.int32, (1, LANES), 1) < DIFF_DH
    for hd in range(DIFF_HEADS):
        sl = slice(hd * LANES, (hd + 1) * LANES)
        q = _rope_tile(zq[:, sl], c, s1, s2, DIFF_ROT // 2) * scale
        _put(q1_ref, hd * LANES, jnp.where(first, q, 0.0), tr)
        _put(q2_ref, hd * LANES, jnp.where(first, 0.0, q), tr)
        k = _rope_tile(zk[:, sl], c, s1, s2, DIFF_ROT // 2)
        dk_ref[:, sl] = k
        dkb_ref[:, sl] = k.astype(BF16)


def _diffqk(h, wdq, wdk, tabs, bm, scale, tr):
    m, d = h.shape
    nt = tabs[0].shape[0] // bm
    n = wdq.shape[1]
    full = lambda a: pl.BlockSpec(a.shape, lambda i: (0, 0))
    tab = pl.BlockSpec((bm, LANES), lambda i: (i % nt, 0))
    row = lambda w: pl.BlockSpec((bm, w), lambda i: (i, 0))
    qspec, qshape = _out(m, n, bm, BF16, tr)
    return pl.pallas_call(
        functools.partial(_diffqk_kernel, scale=scale, tr=tr),
        grid=(m // bm,),
        in_specs=[row(d), full(wdq), full(wdk), tab, tab, tab],
        out_specs=[qspec, qspec, row(n), row(n)],
        out_shape=[qshape, qshape, jax.ShapeDtypeStruct((m, n), F32), jax.ShapeDtypeStruct((m, n), BF16)],
        compiler_params=_params("parallel"),
        name="diff_qk_proj",
    )(h, wdq, wdk, *tabs)


def _diffv_kernel(h_ref, wdv_ref, dv_ref, dvb_ref, *, tr):
    z = _dot(h_ref[...], wdv_ref[...])
    dv_ref[...] = z
    for hd in range(DIFF_HEADS):
        _put_v(dvb_ref, hd, z[:, hd * DIFF_VD:(hd + 1) * DIFF_VD], tr)


def _diffv(h, wdv, bm, tr):
    m, d = h.shape
    n = wdv.shape[1]
    row = lambda w: pl.BlockSpec((bm, w), lambda i: (i, 0))
    vspec, vshape = _out(m, _v_width(DIFF_HEADS, DIFF_VD, tr), bm, BF16, tr)
    return pl.pallas_call(
        functools.partial(_diffv_kernel, tr=tr),
        grid=(m // bm,),
        in_specs=[row(d), pl.BlockSpec(wdv.shape, lambda i: (0, 0))],
        out_specs=[row(n), vspec],
        out_shape=[jax.ShapeDtypeStruct((m, n), F32), vshape],
        compiler_params=_params("parallel"),
        name="diff_v_proj",
    )(h, wdv)


def _merge_kernel(h_ref, oa_ref, ob_ref, wga_ref, wgb_ref, wa_ref, wb_ref, o_ref):
    h = h_ref[...]
    ga = jax.nn.sigmoid(_dot(h, wga_ref[...]))
    gb = jax.nn.sigmoid(_dot(h, wgb_ref[...]))
    ya = _dot(oa_ref[...], wa_ref[...])
    yb = _dot(ob_ref[...], wb_ref[...])
    o_ref[...] = (ga * ya + gb * yb).astype(o_ref.dtype)


def _merge(h, oa, ob, wga, wgb, wa, wb, bm, bn):
    m, d = h.shape
    n = wga.shape[1]
    row = lambda a: pl.BlockSpec((bm, a.shape[1]), lambda i, j: (i, 0))
    col = lambda a: pl.BlockSpec((a.shape[0], bn), lambda i, j: (0, j))
    return pl.pallas_call(
        _merge_kernel,
        grid=(m // bm, n // bn),
        in_specs=[row(h), row(oa), row(ob), col(wga), col(wgb), col(wa), col(wb)],
        out_specs=pl.BlockSpec((bm, bn), lambda i, j: (i, j)),
        out_shape=jax.ShapeDtypeStruct((m, n), BF16),
        compiler_params=_params("parallel", "arbitrary"),
        name="gated_merge",
    )(h, oa, ob, wga, wgb, wa, wb)


def _outproj_kernel(x_ref, mg_ref, wo_ref, g_ref, x2_ref, h2_ref):
    x2 = x_ref[...] + _dot(mg_ref[...], wo_ref[...])
    x2_ref[...] = x2
    h2_ref[...] = _rms(x2, g_ref[...]).astype(BF16)


def _outproj(x, mg, wo, g, bm):
    m, d = x.shape
    row = pl.BlockSpec((bm, d), lambda i: (i, 0))
    return pl.pallas_call(
        _outproj_kernel,
        grid=(m // bm,),
        in_specs=[row, row, pl.BlockSpec(wo.shape, lambda i: (0, 0)), pl.BlockSpec((1, d), lambda i: (0, 0))],
        out_specs=[row, row],
        out_shape=[jax.ShapeDtypeStruct((m, d), F32), jax.ShapeDtypeStruct((m, d), BF16)],
        compiler_params=_params("parallel"),
        name="out_proj_residual",
    )(x, mg, wo, g)


def _ffn_kernel(h2_ref, wg_ref, wu_ref, wd_ref, x2_ref, gf_ref, y_ref, acc_ref):
    f = pl.program_id(1)

    @pl.when(f == 0)
    def _():
        acc_ref[...] = jnp.zeros_like(acc_ref)

    h2 = h2_ref[...]
    a = (jax.nn.silu(_dot(h2, wg_ref[...])) * _dot(h2, wu_ref[...])).astype(BF16)
    acc_ref[...] += _dot(a, wd_ref[...])

    @pl.when(f == pl.num_programs(1) - 1)
    def _():
        y_ref[...] = _rms(x2_ref[...] + acc_ref[...], gf_ref[...])


def _ffn(h2, w_in, w_out, x2, gf, bm, bf):
    m, d = h2.shape
    dff = w_out.shape[0]
    nf = dff // bf
    row = pl.BlockSpec((bm, d), lambda i, f: (i, 0))
    return pl.pallas_call(
        _ffn_kernel,
        grid=(m // bm, nf),
        in_specs=[row,
                  pl.BlockSpec((d, bf), lambda i, f: (0, f)),
                  pl.BlockSpec((d, bf), lambda i, f: (0, f + nf)),
                  pl.BlockSpec((bf, d), lambda i, f: (f, 0)),
                  row,
                  pl.BlockSpec((1, d), lambda i, f: (0, 0))],
        out_specs=row,
        out_shape=jax.ShapeDtypeStruct((m, d), F32),
        scratch_shapes=[pltpu.VMEM((bm, d), F32)],
        compiler_params=_params("parallel", "arbitrary"),
        name="ffn_swiglu_final_norm",
    )(h2, w_in, w_in, w_out, x2, gf)


def _softmax_step(s, v, m, l, acc):
    m_new = jnp.maximum(m, jnp.max(s, axis=-1, keepdims=True))
    alpha = jnp.exp2(m - m_new)
    p = jnp.exp2(s - m_new)
    l = alpha * l + jnp.sum(p, axis=-1, keepdims=True)
    acc = alpha * acc + _dot(p.astype(BF16), v)
    return m_new, l, acc


def _chunk_mask(tq, tk, q0, k0):
    qp = q0 + lax.broadcasted_iota(jnp.int32, (tq, tk), 0)
    kp = k0 + lax.broadcasted_iota(jnp.int32, (tq, tk), 1)
    return (kp // CHUNK) <= (qp // CHUNK)


def _softmax_init(tq, dv):
    return (jnp.full((tq, 1), NEG_INF, F32), jnp.zeros((tq, 1), F32), jnp.zeros((tq, dv), F32))


def _chunk_mask_t(t):
    kp = lax.broadcasted_iota(jnp.int32, (t, t), 0)
    qp = lax.broadcasted_iota(jnp.int32, (t, t), 1)
    return (kp // CHUNK) <= (qp // CHUNK)


def _probs_t(s, smax, m):
    m_new = jnp.maximum(m, smax)
    return jnp.exp2(s - m_new).astype(BF16), jnp.exp2(m - m_new), m_new


def _attention_scratch(t, dv, streams):
    return [pltpu.VMEM((streams, 2, t, t), F32),
            pltpu.VMEM((streams, dv + VT_ONES, t), F32),
            pltpu.VMEM((streams, 8, t), F32)]


def _causal_attention_t(n, scores, values, finish, s_ref, acc_ref, stat_ref, t, streams):
    row_m, row_smax = 0, 1

    def stat(st, r):
        return stat_ref[st, r:r + 1, :]

    def put_scores(st, par, s):
        s_ref[st, par] = s
        stat_ref[st, row_smax + par:row_smax + par + 1, :] = jnp.max(s, axis=0, keepdims=True)

    def accumulate(st, j, s, smax):
        p, alpha, m = _probs_t(s, smax, stat(st, row_m))
        acc = alpha * acc_ref[st] + _dot(values(j, st), p)
        stat_ref[st, row_m:row_m + 1, :] = m
        return acc

    def step(tau, par):
        for st in range(streams):
            put_scores(st, par, scores(tau, st))
            acc_ref[st] = accumulate(st, tau - 1, s_ref[st, 1 - par], stat(st, row_smax + 1 - par))

    for st in range(streams):
        put_scores(st, 0, scores(0, st))
        acc_ref[st] = jnp.zeros(acc_ref.shape[1:], F32)
        stat_ref[st, row_m:row_m + 1, :] = jnp.full((1, t), NEG_INF, F32)

    def pair(u, carry):
        step(2 * u + 1, 1)
        step(2 * u + 2, 0)
        return carry

    lax.fori_loop(0, n // 2, pair, 0)

    @pl.when(n % 2 == 1)
    def _():
        step(n, 1)

    def tail(par):
        mask = _chunk_mask_t(t)
        res = []
        for st in range(streams):
            s = jnp.where(mask, s_ref[st, par], NEG_INF)
            res.append(accumulate(st, n, s, jnp.max(s, axis=0, keepdims=True)))
        finish(res)

    for par in (0, 1):
        pl.when(n % 2 == par)(functools.partial(tail, par))


def _normalised(acc, dv):
    return (acc[:dv] / acc[dv:dv + 1]).T


def _mla_prompt_kernel(qt_ref, k_ref, vt_ref, o_ref, *scratch, t, heads):
    def scores(j, hd):
        k = k_ref[0, pl.ds(pl.multiple_of(j * t, t), t), hd * MLA_QK_PAD:(hd + 1) * MLA_QK_PAD]
        return _dot(k, qt_ref[0, hd * MLA_QK_PAD:(hd + 1) * MLA_QK_PAD, :])

    def values(j, hd):
        return vt_ref[j, hd * (MLA_V + VT_ONES):(hd + 1) * (MLA_V + VT_ONES), :]

    def finish(res):
        for hd, acc in enumerate(res):
            o_ref[0, :, hd * MLA_V:(hd + 1) * MLA_V] = _normalised(acc, MLA_V).astype(o_ref.dtype)

    _causal_attention_t(pl.program_id(2), scores, values, finish, *scratch, t, heads)


def _mla_prompt(qt, k, vt, t, heads):
    b, l, _ = k.shape
    nq = l // t
    return pl.pallas_call(
        functools.partial(_mla_prompt_kernel, t=t, heads=heads),
        grid=(b, MLA_HEADS // heads, nq),
        in_specs=[pl.BlockSpec((1, heads * MLA_QK_PAD, t), lambda b, h, i: (b * nq + i, h, 0)),
                  pl.BlockSpec((1, l, heads * MLA_QK_PAD), lambda b, h, i: (b, 0, h)),
                  pl.BlockSpec((nq, heads * (MLA_V + VT_ONES), t), lambda b, h, i: (b, h, 0))],
        out_specs=pl.BlockSpec((1, t, heads * MLA_V), lambda b, h, i: (b, i, h)),
        out_shape=jax.ShapeDtypeStruct((b, l, MLA_HEADS * MLA_V), BF16),
        scratch_shapes=_attention_scratch(t, MLA_V, heads),
        compiler_params=_params("parallel", "parallel", "arbitrary"),
        name="mla_attention_prompt",
    )(qt, k, vt)


def _lambda(lq1_ref, lk1_ref, lq2_ref, lk2_ref, lambda_init):
    a = jnp.sum(lq1_ref[...] * lk1_ref[...], axis=-1, keepdims=True)
    b = jnp.sum(lq2_ref[...] * lk2_ref[...], axis=-1, keepdims=True)
    return jnp.exp(a) - jnp.exp(b) + lambda_init


def _diff_finish(c1, c2, lam, g, lambda_init):
    o = c1[2] / c1[1] - lam * (c2[2] / c2[1])
    return _rms(o, g) * (1.0 - lambda_init)


def _diff_prompt_kernel(q1t_ref, q2t_ref, k_ref, vt_ref, lq1_ref, lk1_ref, lq2_ref, lk2_ref, g_ref, o_ref,
                        *scratch, t, lambda_init):
    qt_refs = (q1t_ref, q2t_ref)

    def scores(j, st):
        return _dot(k_ref[0, pl.ds(pl.multiple_of(j * t, t), t), :], qt_refs[st][0])

    def values(j, st):
        return vt_ref[j]

    def finish(res):
        lam = _lambda(lq1_ref, lk1_ref, lq2_ref, lk2_ref, lambda_init)
        o = _normalised(res[0], DIFF_VD) - lam * _normalised(res[1], DIFF_VD)
        o_ref[0] = (_rms(o, g_ref[...]) * (1.0 - lambda_init)).astype(o_ref.dtype)

    _causal_attention_t(pl.program_id(2), scores, values, finish, *scratch, t, 2)


def _diff_prompt(q1t, q2t, k, vt, lams, g, t, lambda_init):
    b, l, _ = k.shape
    nq = l // t
    qs = pl.BlockSpec((1, LANES, t), lambda b, h, i: (b * nq + i, h, 0))
    small = lambda a: pl.BlockSpec(a.shape, lambda b, h, i: (0, 0))
    return pl.pallas_call(
        functools.partial(_diff_prompt_kernel, t=t, lambda_init=lambda_init),
        grid=(b, DIFF_HEADS, nq),
        in_specs=[qs, qs,
                  pl.BlockSpec((1, l, LANES), lambda b, h, i: (b, 0, h)),
                  pl.BlockSpec((nq, DIFF_VD + VT_ONES, t), lambda b, h, i: (b, h, 0))]
                 + [small(a) for a in lams] + [small(g)],
        out_specs=pl.BlockSpec((1, t, DIFF_VD), lambda b, h, i: (b, i, h)),
        out_shape=jax.ShapeDtypeStruct((b, l, DIFF_HEADS * DIFF_VD), BF16),
        scratch_shapes=_attention_scratch(t, DIFF_VD, 2),
        compiler_params=_params("parallel", "parallel", "arbitrary"),
        name="diff_attention_prompt",
    )(q1t, q2t, k, vt, *lams, g)


def _two_part_softmax(q, kc, vc, kn, vn, mask_c, mask_n):
    tq = q.shape[0]
    carry = _softmax_init(tq, vc.shape[1])
    carry = _softmax_step(jnp.where(mask_c, _dot_nt(q, kc), NEG_INF), vc, *carry)
    return _softmax_step(jnp.where(mask_n, _dot_nt(q, kn), NEG_INF), vn, *carry)


def _mla_sample_kernel(q_ref, kc_ref, vc_ref, kn_ref, vn_ref, o_ref, *, past):
    tq, tc = q_ref.shape[1], kc_ref.shape[1]
    _, l, acc = _two_part_softmax(q_ref[0], kc_ref[0], vc_ref[0], kn_ref[0], vn_ref[0],
                                  _chunk_mask(tq, tc, past, 0), _chunk_mask(tq, tq, past, past))
    o_ref[0] = (acc / l).astype(o_ref.dtype)


def _mla_sample(q, kc, vc, kn, vn):
    b, tq, _ = q.shape
    tc = kc.shape[1]
    blk = lambda rows, w: pl.BlockSpec((1, rows, w), lambda b, h: (b, 0, h))
    return pl.pallas_call(
        functools.partial(_mla_sample_kernel, past=tc),
        grid=(b, MLA_HEADS),
        in_specs=[blk(tq, MLA_QK_PAD), blk(tc, MLA_QK_PAD), blk(tc, MLA_V), blk(tq, MLA_QK_PAD), blk(tq, MLA_V)],
        out_specs=blk(tq, MLA_V),
        out_shape=jax.ShapeDtypeStruct((b, tq, MLA_HEADS * MLA_V), BF16),
        compiler_params=_params("parallel", "parallel"),
        name="mla_attention_sample",
    )(q, kc, vc, kn, vn)


def _diff_sample_kernel(q1_ref, q2_ref, kc_ref, vc_ref, kn_ref, vn_ref,
                        lq1_ref, lk1_ref, lq2_ref, lk2_ref, g_ref, o_ref, *, past, lambda_init):
    tq, tc = q1_ref.shape[1], kc_ref.shape[1]
    kc, vc = kc_ref[0].astype(BF16), vc_ref[0].astype(BF16)
    kn, vn = kn_ref[0], vn_ref[0]
    mask_c, mask_n = _chunk_mask(tq, tc, past, 0), _chunk_mask(tq, tq, past, past)
    c1 = _two_part_softmax(q1_ref[0], kc, vc, kn, vn, mask_c, mask_n)
    c2 = _two_part_softmax(q2_ref[0], kc, vc, kn, vn, mask_c, mask_n)
    lam = _lambda(lq1_ref, lk1_ref, lq2_ref, lk2_ref, lambda_init)
    o_ref[0] = _diff_finish(c1, c2, lam, g_ref[...], lambda_init).astype(o_ref.dtype)


def _diff_sample(q1, q2, kc, vc, kn, vn, lams, g, lambda_init):
    b, tq, _ = q1.shape
    tc = kc.shape[1]
    blk = lambda rows: pl.BlockSpec((1, rows, LANES), lambda b, h: (b, 0, h))
    small = lambda a: pl.BlockSpec(a.shape, lambda b, h: (0, 0))
    return pl.pallas_call(
        functools.partial(_diff_sample_kernel, past=tc, lambda_init=lambda_init),
        grid=(b, DIFF_HEADS),
        in_specs=[blk(tq), blk(tq), blk(tc), blk(tc), blk(tq), blk(tq)] + [small(a) for a in lams] + [small(g)],
        out_specs=blk(tq),
        out_shape=jax.ShapeDtypeStruct((b, tq, DIFF_HEADS * DIFF_VD), BF16),
        compiler_params=_params("parallel", "parallel"),
        name="diff_attention_sample",
    )(q1, q2, kc, vc, kn, vn, *lams, g)


def _rope_tables(pos, theta, rot_dim, period):
    half = rot_dim // 2
    inv = 1.0 / (jnp.float32(theta) ** (jnp.arange(half, dtype=F32) / half))
    ang = pos.astype(F32)[:, None] * inv[None, :]
    cos, sin = jnp.cos(ang), jnp.sin(ang)
    n = pos.shape[0]
    rest = period - rot_dim
    c = jnp.concatenate([cos, cos, jnp.ones((n, rest), F32)], axis=1)
    s1 = jnp.concatenate([-sin, jnp.zeros((n, half + rest), F32)], axis=1)
    s2 = jnp.concatenate([jnp.zeros((n, half), F32), sin, jnp.zeros((n, rest), F32)], axis=1)
    reps = LANES // period
    return tuple(jnp.tile(a, (1, reps)) for a in (c, s1, s2))


def _layer_weights(w_in, mla_w_uq, mla_w_ukv):
    c_kv = MLA_Q_LORA
    c_kr = c_kv + MLA_KV_LORA
    c_dq = c_kr + MLA_ROPE
    dqk = DIFF_HEADS * 2 * DIFF_DH
    c_dk = c_dq + dqk
    c_dv = c_dk + dqk
    c_ga = c_dv + DIFF_HEADS * DIFF_VD
    d = w_in.shape[0]
    c_gb = c_ga + d
    wb = w_in.astype(BF16)
    wq = wb[:, :c_kv]
    wkv = jnp.pad(wb[:, c_kv:c_dq], ((0, 0), (0, LANES - MLA_ROPE)))
    wdq, wdk, wdv = wb[:, c_dq:c_dk], wb[:, c_dk:c_dv], wb[:, c_dv:c_ga]
    wga, wgb = wb[:, c_ga:c_gb], wb[:, c_gb:]
    uq = mla_w_uq.astype(BF16).reshape(MLA_Q_LORA, MLA_HEADS, MLA_NOPE + MLA_ROPE)
    uq = jnp.pad(uq, ((0, 0), (0, 0), (0, MLA_QK_PAD - MLA_NOPE - MLA_ROPE))).reshape(MLA_Q_LORA, -1)
    ukv = mla_w_ukv.astype(BF16).reshape(MLA_KV_LORA, MLA_HEADS, MLA_NOPE + MLA_V)
    ukv = jnp.concatenate([ukv[:, :, :MLA_NOPE].reshape(MLA_KV_LORA, -1),
                           ukv[:, :, MLA_NOPE:].reshape(MLA_KV_LORA, -1)], axis=1)
    return wq, wkv, wdq, wdk, wdv, wga, wgb, uq, ukv


def _block(m, want):
    return want if m % want == 0 else m


def kernel(x_prompt, x_sample, cache_mla_ckv, cache_mla_krope, cache_diff_k, cache_diff_v, norm_mix, w_in,
           mla_q_norm, mla_w_uq, mla_kv_norm, mla_w_ukv, diff_lq1, diff_lk1, diff_lq2, diff_lk2, diff_subln,
           w_branch_a, w_branch_b, w_out, norm_ffn, w_ffn_in, w_ffn_out, norm_final):
    bp, lp, d = x_prompt.shape
    bs, ls, _ = x_sample.shape
    depth, _, past, _ = cache_mla_ckv.shape
    assert depth == 1, "the FFN kernel fuses the final norm, so it serves the last (only) layer"
    mp, ms = bp * lp, bs * ls
    t_attn = 512

    pos_p = jnp.arange(lp)
    pos_s = past + jnp.arange(ls)
    tabs_mla_p = _rope_tables(pos_p, MLA_THETA, MLA_ROPE, LANES)
    tabs_mla_s = tuple(jnp.tile(a, (bs, 1)) for a in _rope_tables(pos_s, MLA_THETA, MLA_ROPE, LANES))
    tabs_dif_p = _rope_tables(pos_p, ROPE_THETA, DIFF_ROT, DIFF_DH)
    tabs_dif_s = tuple(jnp.tile(a, (bs, 1)) for a in _rope_tables(pos_s, ROPE_THETA, DIFF_ROT, DIFF_DH))
    q_scale_mla = (MLA_NOPE + MLA_ROPE) ** -0.5 * LOG2E
    q_scale_dif = DIFF_DH ** -0.5 * LOG2E
    gfinal = norm_final.reshape(1, d)

    xp = x_prompt.reshape(mp, d)
    xs = x_sample.reshape(ms, d)
    rows_p, rows_s = [], []
    for l in range(depth):
        lambda_init = 0.8 - 0.6 * math.exp(-0.3 * l)
        wq, wkv, wdq, wdk, wdv, wga, wgb, uq, ukv = _layer_weights(w_in[l], mla_w_uq[l], mla_w_ukv[l])
        gmix, gq, gkv = norm_mix[l].reshape(1, -1), mla_q_norm[l].reshape(1, -1), mla_kv_norm[l].reshape(1, -1)
        lams = tuple(a[l].reshape(1, -1) for a in (diff_lq1, diff_lk1, diff_lq2, diff_lk2))
        gsub = diff_subln[l].reshape(1, -1)
        wa, wbr, wo = w_branch_a[l].astype(BF16), w_branch_b[l].astype(BF16), w_out[l].astype(BF16)
        wfi, wfo = w_ffn_in[l].astype(BF16), w_ffn_out[l].astype(BF16)
        gffn = norm_ffn[l].reshape(1, -1)

        def stage1(x, bm, tabs_mla, tabs_dif, tr):
            h = _norm(x, gmix, bm)
            q = _qproj(h, wq, gq, uq, tabs_mla, bm, q_scale_mla, tr)
            ckv, krope, kmla, vmla = _kvproj(h, wkv, gkv, ukv, tabs_mla, bm, tr)
            q1, q2, dk, dkb = _diffqk(h, wdq, wdk, tabs_dif, bm, q_scale_dif, tr)
            dv, dvb = _diffv(h, wdv, bm, tr)
            return h, q, ckv, krope, kmla, vmla, q1, q2, dk, dkb, dv, dvb

        def stage2(x, h, oa, ob, bm, bn, bf):
            mg = _merge(h, oa, ob, wga, wgb, wa, wbr, bm, bn)
            x2, h2 = _outproj(x, mg, wo, gffn, min(bm, 256))
            return _ffn(h2, wfi, wfo, x2, gfinal, min(bm, 512), bf)

        h, q, ckv, krope, kmla, vmla, q1, q2, dk, dkb, dv, dvb = stage1(xp, t_attn, tabs_mla_p, tabs_dif_p, True)
        r3 = lambda a: a.reshape(bp, lp, -1)
        oa = _mla_prompt(q, r3(kmla), vmla, t_attn, 2)
        ob = _diff_prompt(q1, q2, r3(dkb), dvb, lams, gsub, t_attn, lambda_init)
        xp = stage2(xp, h, oa.reshape(mp, -1), ob.reshape(mp, -1), _block(mp, 1024), 512, 512)
        rows_p.append((ckv.reshape(bp, lp, -1), krope.reshape(bp, lp, -1),
                       dk.reshape(bp, lp, DIFF_HEADS, -1), dv.reshape(bp, lp, DIFF_HEADS, -1)))

        h, q, ckv, krope, kmla, vmla, q1, q2, dk, dkb, dv, dvb = stage1(xs, ms, tabs_mla_s, tabs_dif_s, False)
        kr_pad = jnp.pad(cache_mla_krope[l].reshape(bs * past, MLA_ROPE), ((0, 0), (0, LANES - MLA_ROPE)))
        kc, vc = _kvcache_expand(cache_mla_ckv[l].reshape(bs * past, -1), kr_pad, ukv, _block(bs * past, 512))
        r3 = lambda a: a.reshape(bs, ls, -1)
        c3 = lambda a: a.reshape(bs, past, -1)
        oa = _mla_sample(r3(q), c3(kc), c3(vc), r3(kmla), r3(vmla))
        ob = _diff_sample(r3(q1), r3(q2), c3(cache_diff_k[l]), c3(cache_diff_v[l]), r3(dkb), r3(dvb),
                          lams, gsub, lambda_init)
        xs = stage2(xs, h, oa.reshape(ms, -1), ob.reshape(ms, -1), ms, 512, 512)
        rows_s.append((ckv.reshape(bs, ls, -1), krope.reshape(bs, ls, -1),
                       dk.reshape(bs, ls, DIFF_HEADS, -1), dv.reshape(bs, ls, DIFF_HEADS, -1)))

    y_prompt = xp.reshape(bp, lp, d)
    y_sample = xs.reshape(bs, ls, d)
    stack = lambda rows, i: jnp.stack([r[i] for r in rows], axis=0)
    return (y_prompt, y_sample,
            stack(rows_p, 0), stack(rows_p, 1), stack(rows_p, 2), stack(rows_p, 3),
            stack(rows_s, 0), stack(rows_s, 1), stack(rows_s, 2), stack(rows_s, 3))
```

```python
import functools
import math

import jax
import jax.numpy as jnp
from jax import lax
from jax.experimental import pallas as pl
from jax.experimental.pallas import tpu as pltpu

F32 = jnp.float32
BF16 = jnp.bfloat16

CHUNK = 64
EPS = 1e-6
NEG_INF = -1e30

MLA_HEADS = 8
MLA_Q_LORA = 512
MLA_KV_LORA = 256
MLA_NOPE = 128
MLA_ROPE = 64
MLA_V = 128
MLA_THETA = 10000.0
MLA_QK_PAD = 256

DIFF_HEADS = 8
DIFF_DH = 64
DIFF_VD = 2 * DIFF_DH
DIFF_ROT = DIFF_DH // 4
ROPE_THETA = 500000.0

LANES = 128
VT_ONES = 16
LOG2E = math.log2(math.e)
VMEM_LIMIT = 56 * 1024 * 1024


def _params(*sem):
    return pltpu.CompilerParams(dimension_semantics=sem, vmem_limit_bytes=VMEM_LIMIT)


def _dot(a, b):
    return jnp.dot(a, b, preferred_element_type=F32)


def _dot_nt(a, b):
    return lax.dot_general(a, b, (((1,), (1,)), ((), ())), preferred_element_type=F32)


def _rms(x, g):
    return x * lax.rsqrt(jnp.mean(x * x, axis=-1, keepdims=True) + EPS) * g


def _rope_tile(t, c, s1, s2, half):
    return t * c + pltpu.roll(t, LANES - half, 1) * s1 + pltpu.roll(t, half, 1) * s2


def _norm_kernel(x_ref, g_ref, o_ref):
    o_ref[...] = _rms(x_ref[...], g_ref[...]).astype(o_ref.dtype)


def _norm(x, g, bm):
    m, d = x.shape
    return pl.pallas_call(
        _norm_kernel,
        grid=(m // bm,),
        in_specs=[pl.BlockSpec((bm, d), lambda i: (i, 0)), pl.BlockSpec((1, d), lambda i: (0, 0))],
        out_specs=pl.BlockSpec((bm, d), lambda i: (i, 0)),
        out_shape=jax.ShapeDtypeStruct((m, d), BF16),
        compiler_params=_params("parallel"),
        name="norm_mix",
    )(x, g)


def _put(ref, lo, val, tr):
    w = val.shape[1]
    if tr:
        ref[0, lo:lo + w, :] = val.T.astype(ref.dtype)
    else:
        ref[:, lo:lo + w] = val.astype(ref.dtype)


def _put_v(ref, hd, val, tr):
    bm, w = val.shape
    if tr:
        lo = hd * (w + VT_ONES)
        ref[0, lo:lo + w, :] = val.T.astype(ref.dtype)
        ref[0, lo + w:lo + w + VT_ONES, :] = jnp.ones((VT_ONES, bm), ref.dtype)
    else:
        ref[:, hd * w:(hd + 1) * w] = val.astype(ref.dtype)


def _v_width(heads, dv, tr):
    return heads * (dv + VT_ONES) if tr else heads * dv


def _out(m, n, bm, dtype, tr):
    if tr:
        return pl.BlockSpec((1, n, bm), lambda i: (i, 0, 0)), jax.ShapeDtypeStruct((m // bm, n, bm), dtype)
    return pl.BlockSpec((bm, n), lambda i: (i, 0)), jax.ShapeDtypeStruct((m, n), dtype)


def _qproj_kernel(h_ref, wq_ref, gq_ref, wuq_ref, c_ref, s1_ref, s2_ref, o_ref, *, scale, tr):
    qlat = _rms(_dot(h_ref[...], wq_ref[...]), gq_ref[...]).astype(BF16)
    q = _dot(qlat, wuq_ref[...])
    c, s1, s2 = c_ref[...], s1_ref[...], s2_ref[...]
    for hd in range(MLA_HEADS):
        lo = hd * MLA_QK_PAD
        _put(o_ref, lo, q[:, lo:lo + LANES] * scale, tr)
        t = _rope_tile(q[:, lo + LANES:lo + 2 * LANES], c, s1, s2, MLA_ROPE // 2)
        _put(o_ref, lo + LANES, t * scale, tr)


def _qproj(h, wq, gq, wuq, tabs, bm, scale, tr):
    m, d = h.shape
    nt = tabs[0].shape[0] // bm
    full = lambda a: pl.BlockSpec(a.shape, lambda i: (0, 0))
    tab = pl.BlockSpec((bm, LANES), lambda i: (i % nt, 0))
    spec, shape = _out(m, wuq.shape[1], bm, BF16, tr)
    return pl.pallas_call(
        functools.partial(_qproj_kernel, scale=scale, tr=tr),
        grid=(m // bm,),
        in_specs=[pl.BlockSpec((bm, d), lambda i: (i, 0)), full(wq), full(gq), full(wuq), tab, tab, tab],
        out_specs=spec,
        out_shape=shape,
        compiler_params=_params("parallel"),
        name="mla_q_proj",
    )(h, wq, gq, wuq, *tabs)


def _kv_expand(ckv, krope_tile, wukv_ref, kmla_ref, vmla_ref, tr):
    kv = _dot(ckv.astype(BF16), wukv_ref[...])
    kr = krope_tile.astype(BF16)
    for hd in range(MLA_HEADS):
        lo = hd * MLA_QK_PAD
        kmla_ref[:, lo:lo + LANES] = kv[:, hd * MLA_NOPE:(hd + 1) * MLA_NOPE].astype(BF16)
        kmla_ref[:, lo + LANES:lo + 2 * LANES] = kr
        vlo = MLA_HEADS * MLA_NOPE + hd * MLA_V
        _put_v(vmla_ref, hd, kv[:, vlo:vlo + MLA_V], tr)


def _kvproj_kernel(h_ref, wkv_ref, gkv_ref, wukv_ref, c_ref, s1_ref, s2_ref,
                   ckv_ref, krope_ref, kmla_ref, vmla_ref, *, tr):
    z = _dot(h_ref[...], wkv_ref[...])
    ckv = _rms(z[:, :MLA_KV_LORA], gkv_ref[...])
    ckv_ref[...] = ckv
    t = _rope_tile(z[:, MLA_KV_LORA:], c_ref[...], s1_ref[...], s2_ref[...], MLA_ROPE // 2)
    krope_ref[...] = t[:, :MLA_ROPE]
    _kv_expand(ckv, t, wukv_ref, kmla_ref, vmla_ref, tr)


def _kvproj(h, wkv, gkv, wukv, tabs, bm, tr):
    m, d = h.shape
    nt = tabs[0].shape[0] // bm
    full = lambda a: pl.BlockSpec(a.shape, lambda i: (0, 0))
    tab = pl.BlockSpec((bm, LANES), lambda i: (i % nt, 0))
    row = lambda n: pl.BlockSpec((bm, n), lambda i: (i, 0))
    nk, nv = MLA_HEADS * MLA_QK_PAD, MLA_HEADS * MLA_V
    vspec, vshape = _out(m, _v_width(MLA_HEADS, MLA_V, tr), bm, BF16, tr)
    return pl.pallas_call(
        functools.partial(_kvproj_kernel, tr=tr),
        grid=(m // bm,),
        in_specs=[row(d), full(wkv), full(gkv), full(wukv), tab, tab, tab],
        out_specs=[row(MLA_KV_LORA), row(MLA_ROPE), row(nk), vspec],
        out_shape=[jax.ShapeDtypeStruct((m, MLA_KV_LORA), F32), jax.ShapeDtypeStruct((m, MLA_ROPE), F32),
                   jax.ShapeDtypeStruct((m, nk), BF16), vshape],
        compiler_params=_params("parallel"),
        name="mla_kv_proj",
    )(h, wkv, gkv, wukv, *tabs)


def _kvcache_kernel(ckv_ref, krope_ref, wukv_ref, kmla_ref, vmla_ref):
    _kv_expand(ckv_ref[...], krope_ref[...], wukv_ref, kmla_ref, vmla_ref, False)


def _kvcache_expand(ckv, krope_pad, wukv, bm):
    m = ckv.shape[0]
    row = lambda n: pl.BlockSpec((bm, n), lambda i: (i, 0))
    nk, nv = MLA_HEADS * MLA_QK_PAD, MLA_HEADS * MLA_V
    return pl.pallas_call(
        _kvcache_kernel,
        grid=(m // bm,),
        in_specs=[row(MLA_KV_LORA), row(LANES), pl.BlockSpec(wukv.shape, lambda i: (0, 0))],
        out_specs=[row(nk), row(nv)],
        out_shape=[jax.ShapeDtypeStruct((m, nk), BF16), jax.ShapeDtypeStruct((m, nv), BF16)],
        compiler_params=_params("parallel"),
        name="mla_kv_cache_expand",
    )(ckv, krope_pad, wukv)


def _diffqk_kernel(h_ref, wdq_ref, wdk_ref, c_ref, s1_ref, s2_ref,
                   q1_ref, q2_ref, dk_ref, dkb_ref, *, scale, tr):
    h = h_ref[...]
    zq = _dot(h, wdq_ref[...])
    zk = _dot(h, wdk_ref[...])
    c, s1, s2 = c_ref[...], s1_ref[...], s2_ref[...]
    first = lax.broadcasted_iota(jnp.int32, (1, LANES), 1) < DIFF_DH
    for hd in range(DIFF_HEADS):
        sl = slice(hd * LANES, (hd + 1) * LANES)
        q = _rope_tile(zq[:, sl], c, s1, s2, DIFF_ROT // 2) * scale
        _put(q1_ref, hd * LANES, jnp.where(first, q, 0.0), tr)
        _put(q2_ref, hd * LANES, jnp.where(first, 0.0, q), tr)
        k = _rope_tile(zk[:, sl], c, s1, s2, DIFF_ROT // 2)
        dk_ref[:, sl] = k
        dkb_ref[:, sl] = k.astype(BF16)


def _diffqk(h, wdq, wdk, tabs, bm, scale, tr):
    m, d = h.shape
    nt = tabs[0].shape[0] // bm
    n = wdq.shape[1]
    full = lambda a: pl.BlockSpec(a.shape, lambda i: (0, 0))
    tab = pl.BlockSpec((bm, LANES), lambda i: (i % nt, 0))
    row = lambda w: pl.BlockSpec((bm, w), lambda i: (i, 0))
    qspec, qshape = _out(m, n, bm, BF16, tr)
    return pl.pallas_call(
        functools.partial(_diffqk_kernel, scale=scale, tr=tr),
        grid=(m // bm,),
        in_specs=[row(d), full(wdq), full(wdk), tab, tab, tab],
        out_specs=[qspec, qspec, row(n), row(n)],
        out_shape=[qshape, qshape, jax.ShapeDtypeStruct((m, n), F32), jax.ShapeDtypeStruct((m, n), BF16)],
        compiler_params=_params("parallel"),
        name="diff_qk_proj",
    )(h, wdq, wdk, *tabs)


def _diffv_kernel(h_ref, wdv_ref, dv_ref, dvb_ref, *, tr):
    z = _dot(h_ref[...], wdv_ref[...])
    dv_ref[...] = z
    for hd in range(DIFF_HEADS):
        _put_v(dvb_ref, hd, z[:, hd * DIFF_VD:(hd + 1) * DIFF_VD], tr)


def _diffv(h, wdv, bm, tr):
    m, d = h.shape
    n = wdv.shape[1]
    row = lambda w: pl.BlockSpec((bm, w), lambda i: (i, 0))
    vspec, vshape = _out(m, _v_width(DIFF_HEADS, DIFF_VD, tr), bm, BF16, tr)
    return pl.pallas_call(
        functools.partial(_diffv_kernel, tr=tr),
        grid=(m // bm,),
        in_specs=[row(d), pl.BlockSpec(wdv.shape, lambda i: (0, 0))],
        out_specs=[row(n), vspec],
        out_shape=[jax.ShapeDtypeStruct((m, n), F32), vshape],
        compiler_params=_params("parallel"),
        name="diff_v_proj",
    )(h, wdv)


def _merge_kernel(h_ref, oa_ref, ob_ref, wga_ref, wgb_ref, wa_ref, wb_ref, o_ref):
    h = h_ref[...]
    ga = jax.nn.sigmoid(_dot(h, wga_ref[...]))
    gb = jax.nn.sigmoid(_dot(h, wgb_ref[...]))
    ya = _dot(oa_ref[...], wa_ref[...])
    yb = _dot(ob_ref[...], wb_ref[...])
    o_ref[...] = (ga * ya + gb * yb).astype(o_ref.dtype)


def _merge(h, oa, ob, wga, wgb, wa, wb, bm, bn):
    m, d = h.shape
    n = wga.shape[1]
    row = lambda a: pl.BlockSpec((bm, a.shape[1]), lambda i, j: (i, 0))
    col = lambda a: pl.BlockSpec((a.shape[0], bn), lambda i, j: (0, j))
    return pl.pallas_call(
        _merge_kernel,
        grid=(m // bm, n // bn),
        in_specs=[row(h), row(oa), row(ob), col(wga), col(wgb), col(wa), col(wb)],
        out_specs=pl.BlockSpec((bm, bn), lambda i, j: (i, j)),
        out_shape=jax.ShapeDtypeStruct((m, n), BF16),
        compiler_params=_params("parallel", "arbitrary"),
        name="gated_merge",
    )(h, oa, ob, wga, wgb, wa, wb)


def _outproj_kernel(x_ref, mg_ref, wo_ref, g_ref, x2_ref, h2_ref):
    x2 = x_ref[...] + _dot(mg_ref[...], wo_ref[...])
    x2_ref[...] = x2
    h2_ref[...] = _rms(x2, g_ref[...]).astype(BF16)


def _outproj(x, mg, wo, g, bm):
    m, d = x.shape
    row = pl.BlockSpec((bm, d), lambda i: (i, 0))
    return pl.pallas_call(
        _outproj_kernel,
        grid=(m // bm,),
        in_specs=[row, row, pl.BlockSpec(wo.shape, lambda i: (0, 0)), pl.BlockSpec((1, d), lambda i: (0, 0))],
        out_specs=[row, row],
        out_shape=[jax.ShapeDtypeStruct((m, d), F32), jax.ShapeDtypeStruct((m, d), BF16)],
        compiler_params=_params("parallel"),
        name="out_proj_residual",
    )(x, mg, wo, g)


def _ffn_kernel(h2_ref, wg_ref, wu_ref, wd_ref, x2_ref, gf_ref, y_ref):
    f = pl.program_id(1)

    @pl.when(f == 0)
    def _():
        y_ref[...] = x2_ref[...]

    h2 = h2_ref[...]
    a = (jax.nn.silu(_dot(h2, wg_ref[...])) * _dot(h2, wu_ref[...])).astype(BF16)
    y_ref[...] += _dot(a, wd_ref[...])

    @pl.when(f == pl.num_programs(1) - 1)
    def _():
        y_ref[...] = _rms(y_ref[...], gf_ref[...])


def _ffn(h2, w_in, w_out, x2, gf, bm, bf):
    m, d = h2.shape
    dff = w_out.shape[0]
    nf = dff // bf
    row = pl.BlockSpec((bm, d), lambda i, f: (i, 0))
    return pl.pallas_call(
        _ffn_kernel,
        grid=(m // bm, nf),
        in_specs=[row,
                  pl.BlockSpec((d, bf), lambda i, f: (0, f)),
                  pl.BlockSpec((d, bf), lambda i, f: (0, f + nf)),
                  pl.BlockSpec((bf, d), lambda i, f: (f, 0)),
                  row,
                  pl.BlockSpec((1, d), lambda i, f: (0, 0))],
        out_specs=row,
        out_shape=jax.ShapeDtypeStruct((m, d), F32),
        compiler_params=_params("parallel", "arbitrary"),
        name="ffn_swiglu_final_norm",
    )(h2, w_in, w_in, w_out, x2, gf)


def _softmax_step(s, v, m, l, acc):
    m_new = jnp.maximum(m, jnp.max(s, axis=-1, keepdims=True))
    alpha = jnp.exp2(m - m_new)
    p = jnp.exp2(s - m_new)
    l = alpha * l + jnp.sum(p, axis=-1, keepdims=True)
    acc = alpha * acc + _dot(p.astype(BF16), v)
    return m_new, l, acc


def _chunk_mask(tq, tk, q0, k0):
    qp = q0 + lax.broadcasted_iota(jnp.int32, (tq, tk), 0)
    kp = k0 + lax.broadcasted_iota(jnp.int32, (tq, tk), 1)
    return (kp // CHUNK) <= (qp // CHUNK)


def _softmax_init(tq, dv):
    return (jnp.full((tq, 1), NEG_INF, F32), jnp.zeros((tq, 1), F32), jnp.zeros((tq, dv), F32))


def _chunk_mask_t(t):
    kp = lax.broadcasted_iota(jnp.int32, (t, t), 0)
    qp = lax.broadcasted_iota(jnp.int32, (t, t), 1)
    return (kp // CHUNK) <= (qp // CHUNK)


def _probs_t(s, smax, m):
    m_new = jnp.maximum(m, smax)
    return jnp.exp2(s - m_new).astype(BF16), jnp.exp2(m - m_new), m_new


def _attention_scratch(t, dv, streams):
    return [pltpu.VMEM((streams, 2, t, t), F32),
            pltpu.VMEM((streams, dv + VT_ONES, t), F32),
            pltpu.VMEM((streams, 8, t), F32)]


def _causal_attention_t(n, scores, values, finish, s_ref, acc_ref, stat_ref, t, streams):
    row_m, row_smax = 0, 1

    def stat(st, r):
        return stat_ref[st, r:r + 1, :]

    def put_scores(st, par, s):
        s_ref[st, par] = s
        stat_ref[st, row_smax + par:row_smax + par + 1, :] = jnp.max(s, axis=0, keepdims=True)

    def accumulate(st, j, s, smax):
        p, alpha, m = _probs_t(s, smax, stat(st, row_m))
        acc = alpha * acc_ref[st] + _dot(values(j, st), p)
        stat_ref[st, row_m:row_m + 1, :] = m
        return acc

    def step(tau, par):
        for st in range(streams):
            put_scores(st, par, scores(tau, st))
            acc_ref[st] = accumulate(st, tau - 1, s_ref[st, 1 - par], stat(st, row_smax + 1 - par))

    for st in range(streams):
        put_scores(st, 0, scores(0, st))
        acc_ref[st] = jnp.zeros(acc_ref.shape[1:], F32)
        stat_ref[st, row_m:row_m + 1, :] = jnp.full((1, t), NEG_INF, F32)

    def pair(u, carry):
        step(2 * u + 1, 1)
        step(2 * u + 2, 0)
        return carry

    lax.fori_loop(0, n // 2, pair, 0)

    @pl.when(n % 2 == 1)
    def _():
        step(n, 1)

    def tail(par):
        mask = _chunk_mask_t(t)
        res = []
        for st in range(streams):
            s = jnp.where(mask, s_ref[st, par], NEG_INF)
            res.append(accumulate(st, n, s, jnp.max(s, axis=0, keepdims=True)))
        finish(res)

    for par in (0, 1):
        pl.when(n % 2 == par)(functools.partial(tail, par))


def _normalised(acc, dv):
    return (acc[:dv] / acc[dv:dv + 1]).T


def _mla_prompt_kernel(qt_ref, k_ref, vt_ref, o_ref, *scratch, t, heads):
    def scores(j, hd):
        k = k_ref[0, pl.ds(pl.multiple_of(j * t, t), t), hd * MLA_QK_PAD:(hd + 1) * MLA_QK_PAD]
        return _dot(k, qt_ref[0, hd * MLA_QK_PAD:(hd + 1) * MLA_QK_PAD, :])

    def values(j, hd):
        return vt_ref[j, hd * (MLA_V + VT_ONES):(hd + 1) * (MLA_V + VT_ONES), :]

    def finish(res):
        for hd, acc in enumerate(res):
            o_ref[0, :, hd * MLA_V:(hd + 1) * MLA_V] = _normalised(acc, MLA_V).astype(o_ref.dtype)

    _causal_attention_t(pl.program_id(2), scores, values, finish, *scratch, t, heads)


def _mla_prompt(qt, k, vt, t, heads):
    b, l, _ = k.shape
    nq = l // t
    return pl.pallas_call(
        functools.partial(_mla_prompt_kernel, t=t, heads=heads),
        grid=(b, MLA_HEADS // heads, nq),
        in_specs=[pl.BlockSpec((1, heads * MLA_QK_PAD, t), lambda b, h, i: (b * nq + i, h, 0)),
                  pl.BlockSpec((1, l, heads * MLA_QK_PAD), lambda b, h, i: (b, 0, h)),
                  pl.BlockSpec((nq, heads * (MLA_V + VT_ONES), t), lambda b, h, i: (b, h, 0))],
        out_specs=pl.BlockSpec((1, t, heads * MLA_V), lambda b, h, i: (b, i, h)),
        out_shape=jax.ShapeDtypeStruct((b, l, MLA_HEADS * MLA_V), BF16),
        scratch_shapes=_attention_scratch(t, MLA_V, heads),
        compiler_params=_params("parallel", "parallel", "arbitrary"),
        name="mla_attention_prompt",
    )(qt, k, vt)


def _lambda(lq1_ref, lk1_ref, lq2_ref, lk2_ref, lambda_init):
    a = jnp.sum(lq1_ref[...] * lk1_ref[...], axis=-1, keepdims=True)
    b = jnp.sum(lq2_ref[...] * lk2_ref[...], axis=-1, keepdims=True)
    return jnp.exp(a) - jnp.exp(b) + lambda_init


def _diff_finish(c1, c2, lam, g, lambda_init):
    o = c1[2] / c1[1] - lam * (c2[2] / c2[1])
    return _rms(o, g) * (1.0 - lambda_init)


def _diff_prompt_kernel(q1t_ref, q2t_ref, k_ref, vt_ref, lq1_ref, lk1_ref, lq2_ref, lk2_ref, g_ref, o_ref,
                        *scratch, t, lambda_init):
    qt_refs = (q1t_ref, q2t_ref)

    def scores(j, st):
        return _dot(k_ref[0, pl.ds(pl.multiple_of(j * t, t), t), :], qt_refs[st][0])

    def values(j, st):
        return vt_ref[j]

    def finish(res):
        lam = _lambda(lq1_ref, lk1_ref, lq2_ref, lk2_ref, lambda_init)
        o = _normalised(res[0], DIFF_VD) - lam * _normalised(res[1], DIFF_VD)
        o_ref[0] = (_rms(o, g_ref[...]) * (1.0 - lambda_init)).astype(o_ref.dtype)

    _causal_attention_t(pl.program_id(2), scores, values, finish, *scratch, t, 2)


def _diff_prompt(q1t, q2t, k, vt, lams, g, t, lambda_init):
    b, l, _ = k.shape
    nq = l // t
    qs = pl.BlockSpec((1, LANES, t), lambda b, h, i: (b * nq + i, h, 0))
    small = lambda a: pl.BlockSpec(a.shape, lambda b, h, i: (0, 0))
    return pl.pallas_call(
        functools.partial(_diff_prompt_kernel, t=t, lambda_init=lambda_init),
        grid=(b, DIFF_HEADS, nq),
        in_specs=[qs, qs,
                  pl.BlockSpec((1, l, LANES), lambda b, h, i: (b, 0, h)),
                  pl.BlockSpec((nq, DIFF_VD + VT_ONES, t), lambda b, h, i: (b, h, 0))]
                 + [small(a) for a in lams] + [small(g)],
        out_specs=pl.BlockSpec((1, t, DIFF_VD), lambda b, h, i: (b, i, h)),
        out_shape=jax.ShapeDtypeStruct((b, l, DIFF_HEADS * DIFF_VD), BF16),
        scratch_shapes=_attention_scratch(t, DIFF_VD, 2),
        compiler_params=_params("parallel", "parallel", "arbitrary"),
        name="diff_attention_prompt",
    )(q1t, q2t, k, vt, *lams, g)


def _two_part_softmax(q, kc, vc, kn, vn, mask_c, mask_n):
    tq = q.shape[0]
    carry = _softmax_init(tq, vc.shape[1])
    carry = _softmax_step(jnp.where(mask_c, _dot_nt(q, kc), NEG_INF), vc, *carry)
    return _softmax_step(jnp.where(mask_n, _dot_nt(q, kn), NEG_INF), vn, *carry)


def _mla_sample_kernel(q_ref, kc_ref, vc_ref, kn_ref, vn_ref, o_ref, *, past):
    tq, tc = q_ref.shape[1], kc_ref.shape[1]
    _, l, acc = _two_part_softmax(q_ref[0], kc_ref[0], vc_ref[0], kn_ref[0], vn_ref[0],
                                  _chunk_mask(tq, tc, past, 0), _chunk_mask(tq, tq, past, past))
    o_ref[0] = (acc / l).astype(o_ref.dtype)


def _mla_sample(q, kc, vc, kn, vn):
    b, tq, _ = q.shape
    tc = kc.shape[1]
    blk = lambda rows, w: pl.BlockSpec((1, rows, w), lambda b, h: (b, 0, h))
    return pl.pallas_call(
        functools.partial(_mla_sample_kernel, past=tc),
        grid=(b, MLA_HEADS),
        in_specs=[blk(tq, MLA_QK_PAD), blk(tc, MLA_QK_PAD), blk(tc, MLA_V), blk(tq, MLA_QK_PAD), blk(tq, MLA_V)],
        out_specs=blk(tq, MLA_V),
        out_shape=jax.ShapeDtypeStruct((b, tq, MLA_HEADS * MLA_V), BF16),
        compiler_params=_params("parallel", "parallel"),
        name="mla_attention_sample",
    )(q, kc, vc, kn, vn)


def _diff_sample_kernel(q1_ref, q2_ref, kc_ref, vc_ref, kn_ref, vn_ref,
                        lq1_ref, lk1_ref, lq2_ref, lk2_ref, g_ref, o_ref, *, past, lambda_init):
    tq, tc = q1_ref.shape[1], kc_ref.shape[1]
    kc, vc = kc_ref[0].astype(BF16), vc_ref[0].astype(BF16)
    kn, vn = kn_ref[0], vn_ref[0]
    mask_c, mask_n = _chunk_mask(tq, tc, past, 0), _chunk_mask(tq, tq, past, past)
    c1 = _two_part_softmax(q1_ref[0], kc, vc, kn, vn, mask_c, mask_n)
    c2 = _two_part_softmax(q2_ref[0], kc, vc, kn, vn, mask_c, mask_n)
    lam = _lambda(lq1_ref, lk1_ref, lq2_ref, lk2_ref, lambda_init)
    o_ref[0] = _diff_finish(c1, c2, lam, g_ref[...], lambda_init).astype(o_ref.dtype)


def _diff_sample(q1, q2, kc, vc, kn, vn, lams, g, lambda_init, layer):
    b, tq, _ = q1.shape
    tc = kc.shape[1]
    blk = lambda rows: pl.BlockSpec((1, rows, LANES), lambda b, h: (b, 0, h))
    cache = pl.BlockSpec((1, tc, LANES), lambda b, h: (layer * q1.shape[0] + b, 0, h))
    small = lambda a: pl.BlockSpec(a.shape, lambda b, h: (0, 0))
    return pl.pallas_call(
        functools.partial(_diff_sample_kernel, past=tc, lambda_init=lambda_init),
        grid=(b, DIFF_HEADS),
        in_specs=[blk(tq), blk(tq), cache, cache, blk(tq), blk(tq)] + [small(a) for a in lams] + [small(g)],
        out_specs=blk(tq),
        out_shape=jax.ShapeDtypeStruct((b, tq, DIFF_HEADS * DIFF_VD), BF16),
        compiler_params=_params("parallel", "parallel"),
        name="diff_attention_sample",
    )(q1, q2, kc, vc, kn, vn, *lams, g)


def _rope_tables(pos, theta, rot_dim, period):
    half = rot_dim // 2
    inv = 1.0 / (jnp.float32(theta) ** (jnp.arange(half, dtype=F32) / half))
    ang = pos.astype(F32)[:, None] * inv[None, :]
    cos, sin = jnp.cos(ang), jnp.sin(ang)
    n = pos.shape[0]
    rest = period - rot_dim
    c = jnp.concatenate([cos, cos, jnp.ones((n, rest), F32)], axis=1)
    s1 = jnp.concatenate([-sin, jnp.zeros((n, half + rest), F32)], axis=1)
    s2 = jnp.concatenate([jnp.zeros((n, half), F32), sin, jnp.zeros((n, rest), F32)], axis=1)
    reps = LANES // period
    return tuple(jnp.tile(a, (1, reps)) for a in (c, s1, s2))


def _layer_weights(w_in, mla_w_uq, mla_w_ukv):
    c_kv = MLA_Q_LORA
    c_kr = c_kv + MLA_KV_LORA
    c_dq = c_kr + MLA_ROPE
    dqk = DIFF_HEADS * 2 * DIFF_DH
    c_dk = c_dq + dqk
    c_dv = c_dk + dqk
    c_ga = c_dv + DIFF_HEADS * DIFF_VD
    d = w_in.shape[0]
    c_gb = c_ga + d
    wb = w_in.astype(BF16)
    wq = wb[:, :c_kv]
    wkv = jnp.pad(wb[:, c_kv:c_dq], ((0, 0), (0, LANES - MLA_ROPE)))
    wdq, wdk, wdv = wb[:, c_dq:c_dk], wb[:, c_dk:c_dv], wb[:, c_dv:c_ga]
    wga, wgb = wb[:, c_ga:c_gb], wb[:, c_gb:]
    uq = mla_w_uq.astype(BF16).reshape(MLA_Q_LORA, MLA_HEADS, MLA_NOPE + MLA_ROPE)
    uq = jnp.pad(uq, ((0, 0), (0, 0), (0, MLA_QK_PAD - MLA_NOPE - MLA_ROPE))).reshape(MLA_Q_LORA, -1)
    ukv = mla_w_ukv.astype(BF16).reshape(MLA_KV_LORA, MLA_HEADS, MLA_NOPE + MLA_V)
    ukv = jnp.concatenate([ukv[:, :, :MLA_NOPE].reshape(MLA_KV_LORA, -1),
                           ukv[:, :, MLA_NOPE:].reshape(MLA_KV_LORA, -1)], axis=1)
    return wq, wkv, wdq, wdk, wdv, wga, wgb, uq, ukv


def _block(m, want):
    return want if m % want == 0 else m


def kernel(x_prompt, x_sample, cache_mla_ckv, cache_mla_krope, cache_diff_k, cache_diff_v, norm_mix, w_in,
           mla_q_norm, mla_w_uq, mla_kv_norm, mla_w_ukv, diff_lq1, diff_lk1, diff_lq2, diff_lk2, diff_subln,
           w_branch_a, w_branch_b, w_out, norm_ffn, w_ffn_in, w_ffn_out, norm_final):
    bp, lp, d = x_prompt.shape
    bs, ls, _ = x_sample.shape
    depth, _, past, _ = cache_mla_ckv.shape
    assert depth == 1, "the FFN kernel fuses the final norm, so it serves the last (only) layer"
    mp, ms = bp * lp, bs * ls
    t_attn = 512

    pos_p = jnp.arange(lp)
    pos_s = past + jnp.arange(ls)
    tabs_mla_p = _rope_tables(pos_p, MLA_THETA, MLA_ROPE, LANES)
    tabs_mla_s = tuple(jnp.tile(a, (bs, 1)) for a in _rope_tables(pos_s, MLA_THETA, MLA_ROPE, LANES))
    tabs_dif_p = _rope_tables(pos_p, ROPE_THETA, DIFF_ROT, DIFF_DH)
    tabs_dif_s = tuple(jnp.tile(a, (bs, 1)) for a in _rope_tables(pos_s, ROPE_THETA, DIFF_ROT, DIFF_DH))
    q_scale_mla = (MLA_NOPE + MLA_ROPE) ** -0.5 * LOG2E
    q_scale_dif = DIFF_DH ** -0.5 * LOG2E
    gfinal = norm_final.reshape(1, d)

    xp = x_prompt.reshape(mp, d)
    xs = x_sample.reshape(ms, d)
    rows_p, rows_s = [], []
    for l in range(depth):
        lambda_init = 0.8 - 0.6 * math.exp(-0.3 * l)
        wq, wkv, wdq, wdk, wdv, wga, wgb, uq, ukv = _layer_weights(w_in[l], mla_w_uq[l], mla_w_ukv[l])
        gmix, gq, gkv = norm_mix[l].reshape(1, -1), mla_q_norm[l].reshape(1, -1), mla_kv_norm[l].reshape(1, -1)
        lams = tuple(a[l].reshape(1, -1) for a in (diff_lq1, diff_lk1, diff_lq2, diff_lk2))
        gsub = diff_subln[l].reshape(1, -1)
        wa, wbr, wo = w_branch_a[l].astype(BF16), w_branch_b[l].astype(BF16), w_out[l].astype(BF16)
        wfi, wfo = w_ffn_in[l].astype(BF16), w_ffn_out[l].astype(BF16)
        gffn = norm_ffn[l].reshape(1, -1)

        def stage1(x, bm, tabs_mla, tabs_dif, tr):
            h = _norm(x, gmix, bm)
            q = _qproj(h, wq, gq, uq, tabs_mla, bm, q_scale_mla, tr)
            ckv, krope, kmla, vmla = _kvproj(h, wkv, gkv, ukv, tabs_mla, bm, tr)
            q1, q2, dk, dkb = _diffqk(h, wdq, wdk, tabs_dif, bm, q_scale_dif, tr)
            dv, dvb = _diffv(h, wdv, bm, tr)
            return h, q, ckv, krope, kmla, vmla, q1, q2, dk, dkb, dv, dvb

        def stage2(x, h, oa, ob, bm, bn, bf):
            mg = _merge(h, oa, ob, wga, wgb, wa, wbr, bm, bn)
            x2, h2 = _outproj(x, mg, wo, gffn, min(bm, 256))
            return _ffn(h2, wfi, wfo, x2, gfinal, bm, bf)

        h, q, ckv, krope, kmla, vmla, q1, q2, dk, dkb, dv, dvb = stage1(xp, t_attn, tabs_mla_p, tabs_dif_p, True)
        r3 = lambda a: a.reshape(bp, lp, -1)
        oa = _mla_prompt(q, r3(kmla), vmla, t_attn, 2)
        ob = _diff_prompt(q1, q2, r3(dkb), dvb, lams, gsub, t_attn, lambda_init)
        xp = stage2(xp, h, oa.reshape(mp, -1), ob.reshape(mp, -1), _block(mp, 1024), 512, 256)
        rows_p.append((ckv.reshape(bp, lp, -1), krope.reshape(bp, lp, -1),
                       dk.reshape(bp, lp, DIFF_HEADS, -1), dv.reshape(bp, lp, DIFF_HEADS, -1)))

        h, q, ckv, krope, kmla, vmla, q1, q2, dk, dkb, dv, dvb = stage1(xs, ms, tabs_mla_s, tabs_dif_s, False)
        kr_pad = jnp.pad(cache_mla_krope[l].reshape(bs * past, MLA_ROPE), ((0, 0), (0, LANES - MLA_ROPE)))
        kc, vc = _kvcache_expand(cache_mla_ckv[l].reshape(bs * past, -1), kr_pad, ukv, _block(bs * past, 512))
        r3 = lambda a: a.reshape(bs, ls, -1)
        c3 = lambda a: a.reshape(bs, past, -1)
        oa = _mla_sample(r3(q), c3(kc), c3(vc), r3(kmla), r3(vmla))
        call = lambda a: a.reshape(depth * bs, past, -1)
        ob = _diff_sample(r3(q1), r3(q2), call(cache_diff_k), call(cache_diff_v), r3(dkb), r3(dvb),
                          lams, gsub, lambda_init, l)
        xs = stage2(xs, h, oa.reshape(ms, -1), ob.reshape(ms, -1), ms, 512, 512)
        rows_s.append((ckv.reshape(bs, ls, -1), krope.reshape(bs, ls, -1),
                       dk.reshape(bs, ls, DIFF_HEADS, -1), dv.reshape(bs, ls, DIFF_HEADS, -1)))

    y_prompt = xp.reshape(bp, lp, d)
    y_sample = xs.reshape(bs, ls, d)
    stack = lambda rows, i: jnp.stack([r[i] for r in rows], axis=0)
    return (y_prompt, y_sample,
            stack(rows_p, 0), stack(rows_p, 1), stack(rows_p, 2), stack(rows_p, 3),
            stack(rows_s, 0), stack(rows_s, 1), stack(rows_s, 2), stack(rows_s, 3))
```

```python
import functools
import math

import jax
import jax.numpy as jnp
from jax import lax
from jax.experimental import pallas as pl
from jax.experimental.pallas import tpu as pltpu

F32 = jnp.float32
BF16 = jnp.bfloat16

CHUNK = 64
EPS = 1e-6
NEG_INF = -1e30

MLA_HEADS = 8
MLA_Q_LORA = 512
MLA_KV_LORA = 256
MLA_NOPE = 128
MLA_ROPE = 64
MLA_V = 128
MLA_THETA = 10000.0
MLA_QK_PAD = 256

DIFF_HEADS = 8
DIFF_DH = 64
DIFF_VD = 2 * DIFF_DH
DIFF_ROT = DIFF_DH // 4
ROPE_THETA = 500000.0

LANES = 128
VT_ONES = 16
LOG2E = math.log2(math.e)
VMEM_LIMIT = 56 * 1024 * 1024


def _params(*sem):
    return pltpu.CompilerParams(dimension_semantics=sem, vmem_limit_bytes=VMEM_LIMIT)


def _dot(a, b):
    return jnp.dot(a, b, preferred_element_type=F32)


def _dot_nt(a, b):
    return lax.dot_general(a, b, (((1,), (1,)), ((), ())), preferred_element_type=F32)


def _rms(x, g):
    return x * lax.rsqrt(jnp.mean(x * x, axis=-1, keepdims=True) + EPS) * g


def _rope_tile(t, c, s1, s2, half):
    return t * c + pltpu.roll(t, LANES - half, 1) * s1 + pltpu.roll(t, half, 1) * s2


def _put(ref, lo, val, tr):
    w = val.shape[1]
    if tr:
        ref[0, lo:lo + w, :] = val.T.astype(ref.dtype)
    else:
        ref[:, lo:lo + w] = val.astype(ref.dtype)


def _put_v(ref, hd, val, tr):
    bm, w = val.shape
    if tr:
        lo = hd * (w + VT_ONES)
        ref[0, lo:lo + w, :] = val.T.astype(ref.dtype)
        ref[0, lo + w:lo + w + VT_ONES, :] = jnp.ones((VT_ONES, bm), ref.dtype)
    else:
        ref[:, hd * w:(hd + 1) * w] = val.astype(ref.dtype)


def _v_width(heads, dv, tr):
    return heads * (dv + VT_ONES) if tr else heads * dv


def _out(m, n, bm, dtype, tr):
    if tr:
        return pl.BlockSpec((1, n, bm), lambda i: (i, 0, 0)), jax.ShapeDtypeStruct((m // bm, n, bm), dtype)
    return pl.BlockSpec((bm, n), lambda i: (i, 0)), jax.ShapeDtypeStruct((m, n), dtype)


def _qproj_body(h, wq_ref, gq_ref, wuq_ref, c, s1, s2, o_ref, scale, tr):
    qlat = _rms(_dot(h, wq_ref[...]), gq_ref[...]).astype(BF16)
    q = _dot(qlat, wuq_ref[...])
    for hd in range(MLA_HEADS):
        lo = hd * MLA_QK_PAD
        _put(o_ref, lo, q[:, lo:lo + LANES] * scale, tr)
        t = _rope_tile(q[:, lo + LANES:lo + 2 * LANES], c, s1, s2, MLA_ROPE // 2)
        _put(o_ref, lo + LANES, t * scale, tr)


def _kv_expand(ckv, krope_tile, wukv_ref, kmla_ref, vmla_ref, tr):
    kv = _dot(ckv.astype(BF16), wukv_ref[...])
    kr = krope_tile.astype(BF16)
    for hd in range(MLA_HEADS):
        lo = hd * MLA_QK_PAD
        kmla_ref[:, lo:lo + LANES] = kv[:, hd * MLA_NOPE:(hd + 1) * MLA_NOPE].astype(BF16)
        kmla_ref[:, lo + LANES:lo + 2 * LANES] = kr
        vlo = MLA_HEADS * MLA_NOPE + hd * MLA_V
        _put_v(vmla_ref, hd, kv[:, vlo:vlo + MLA_V], tr)


def _mla_proj_kernel(x_ref, gmix_ref, wq_ref, gq_ref, wuq_ref, wkv_ref, gkv_ref, wukv_ref, c_ref, s1_ref, s2_ref,
                     h_ref, q_ref, ckv_ref, krope_ref, kmla_ref, vmla_ref, *, scale, tr):
    h = _rms(x_ref[...], gmix_ref[...]).astype(BF16)
    h_ref[...] = h
    c, s1, s2 = c_ref[...], s1_ref[...], s2_ref[...]
    _qproj_body(h, wq_ref, gq_ref, wuq_ref, c, s1, s2, q_ref, scale, tr)
    z = _dot(h, wkv_ref[...])
    ckv = _rms(z[:, :MLA_KV_LORA], gkv_ref[...])
    ckv_ref[...] = ckv
    t = _rope_tile(z[:, MLA_KV_LORA:], c, s1, s2, MLA_ROPE // 2)
    krope_ref[...] = t[:, :MLA_ROPE]
    _kv_expand(ckv, t, wukv_ref, kmla_ref, vmla_ref, tr)


def _mla_proj(x, gmix, wq, gq, wuq, wkv, gkv, wukv, tabs, bm, scale, tr):
    m, d = x.shape
    nt = tabs[0].shape[0] // bm
    full = lambda a: pl.BlockSpec(a.shape, lambda i: (0, 0))
    tab = pl.BlockSpec((bm, LANES), lambda i: (i % nt, 0))
    row = lambda n: pl.BlockSpec((bm, n), lambda i: (i, 0))
    nk = MLA_HEADS * MLA_QK_PAD
    qspec, qshape = _out(m, wuq.shape[1], bm, BF16, tr)
    vspec, vshape = _out(m, _v_width(MLA_HEADS, MLA_V, tr), bm, BF16, tr)
    return pl.pallas_call(
        functools.partial(_mla_proj_kernel, scale=scale, tr=tr),
        grid=(m // bm,),
        in_specs=[row(d), full(gmix), full(wq), full(gq), full(wuq), full(wkv), full(gkv), full(wukv), tab, tab, tab],
        out_specs=[row(d), qspec, row(MLA_KV_LORA), row(MLA_ROPE), row(nk), vspec],
        out_shape=[jax.ShapeDtypeStruct((m, d), BF16), qshape,
                   jax.ShapeDtypeStruct((m, MLA_KV_LORA), F32), jax.ShapeDtypeStruct((m, MLA_ROPE), F32),
                   jax.ShapeDtypeStruct((m, nk), BF16), vshape],
        compiler_params=_params("parallel"),
        name="norm_mla_proj",
    )(x, gmix, wq, gq, wuq, wkv, gkv, wukv, *tabs)


def _kvcache_kernel(ckv_ref, krope_ref, wukv_ref, kmla_ref, vmla_ref):
    _kv_expand(ckv_ref[...], krope_ref[...], wukv_ref, kmla_ref, vmla_ref, False)


def _kvcache_expand(ckv, krope_pad, wukv, bm):
    m = ckv.shape[0]
    row = lambda n: pl.BlockSpec((bm, n), lambda i: (i, 0))
    nk, nv = MLA_HEADS * MLA_QK_PAD, MLA_HEADS * MLA_V
    return pl.pallas_call(
        _kvcache_kernel,
        grid=(m // bm,),
        in_specs=[row(MLA_KV_LORA), row(LANES), pl.BlockSpec(wukv.shape, lambda i: (0, 0))],
        out_specs=[row(nk), row(nv)],
        out_shape=[jax.ShapeDtypeStruct((m, nk), BF16), jax.ShapeDtypeStruct((m, nv), BF16)],
        compiler_params=_params("parallel"),
        name="mla_kv_cache_expand",
    )(ckv, krope_pad, wukv)


def _diff_proj_kernel(h_ref, wdq_ref, wdk_ref, wdv_ref, c_ref, s1_ref, s2_ref,
                      q1_ref, q2_ref, dk_ref, dkb_ref, dv_ref, dvb_ref, *, scale, tr):
    h = h_ref[...]
    zq = _dot(h, wdq_ref[...])
    zk = _dot(h, wdk_ref[...])
    zv = _dot(h, wdv_ref[...])
    dv_ref[...] = zv
    c, s1, s2 = c_ref[...], s1_ref[...], s2_ref[...]
    first = lax.broadcasted_iota(jnp.int32, (1, LANES), 1) < DIFF_DH
    for hd in range(DIFF_HEADS):
        sl = slice(hd * LANES, (hd + 1) * LANES)
        q = _rope_tile(zq[:, sl], c, s1, s2, DIFF_ROT // 2) * scale
        _put(q1_ref, hd * LANES, jnp.where(first, q, 0.0), tr)
        _put(q2_ref, hd * LANES, jnp.where(first, 0.0, q), tr)
        k = _rope_tile(zk[:, sl], c, s1, s2, DIFF_ROT // 2)
        dk_ref[:, sl] = k
        dkb_ref[:, sl] = k.astype(BF16)
        _put_v(dvb_ref, hd, zv[:, sl], tr)


def _diff_proj(h, wdq, wdk, wdv, tabs, bm, scale, tr):
    m, d = h.shape
    nt = tabs[0].shape[0] // bm
    n = wdq.shape[1]
    full = lambda a: pl.BlockSpec(a.shape, lambda i: (0, 0))
    tab = pl.BlockSpec((bm, LANES), lambda i: (i % nt, 0))
    row = lambda w: pl.BlockSpec((bm, w), lambda i: (i, 0))
    qspec, qshape = _out(m, n, bm, BF16, tr)
    vspec, vshape = _out(m, _v_width(DIFF_HEADS, DIFF_VD, tr), bm, BF16, tr)
    f32_rows, bf16_rows = jax.ShapeDtypeStruct((m, n), F32), jax.ShapeDtypeStruct((m, n), BF16)
    return pl.pallas_call(
        functools.partial(_diff_proj_kernel, scale=scale, tr=tr),
        grid=(m // bm,),
        in_specs=[row(d), full(wdq), full(wdk), full(wdv), tab, tab, tab],
        out_specs=[qspec, qspec, row(n), row(n), row(n), vspec],
        out_shape=[qshape, qshape, f32_rows, bf16_rows, f32_rows, vshape],
        compiler_params=_params("parallel"),
        name="diff_proj",
    )(h, wdq, wdk, wdv, *tabs)


def _merge_kernel(h_ref, oa_ref, ob_ref, wga_ref, wgb_ref, wa_ref, wb_ref, o_ref):
    h = h_ref[...]
    ga = jax.nn.sigmoid(_dot(h, wga_ref[...]))
    gb = jax.nn.sigmoid(_dot(h, wgb_ref[...]))
    ya = _dot(oa_ref[...], wa_ref[...])
    yb = _dot(ob_ref[...], wb_ref[...])
    o_ref[...] = (ga * ya + gb * yb).astype(o_ref.dtype)


def _merge(h, oa, ob, wga, wgb, wa, wb, bm, bn):
    m, d = h.shape
    n = wga.shape[1]
    row = lambda a: pl.BlockSpec((bm, a.shape[1]), lambda i, j: (i, 0))
    col = lambda a: pl.BlockSpec((a.shape[0], bn), lambda i, j: (0, j))
    return pl.pallas_call(
        _merge_kernel,
        grid=(m // bm, n // bn),
        in_specs=[row(h), row(oa), row(ob), col(wga), col(wgb), col(wa), col(wb)],
        out_specs=pl.BlockSpec((bm, bn), lambda i, j: (i, j)),
        out_shape=jax.ShapeDtypeStruct((m, n), BF16),
        compiler_params=_params("parallel", "arbitrary"),
        name="gated_merge",
    )(h, oa, ob, wga, wgb, wa, wb)


def _outproj_kernel(x_ref, mg_ref, wo_ref, g_ref, x2_ref, h2_ref):
    x2 = x_ref[...] + _dot(mg_ref[...], wo_ref[...])
    x2_ref[...] = x2
    h2_ref[...] = _rms(x2, g_ref[...]).astype(BF16)


def _outproj(x, mg, wo, g, bm):
    m, d = x.shape
    row = pl.BlockSpec((bm, d), lambda i: (i, 0))
    return pl.pallas_call(
        _outproj_kernel,
        grid=(m // bm,),
        in_specs=[row, row, pl.BlockSpec(wo.shape, lambda i: (0, 0)), pl.BlockSpec((1, d), lambda i: (0, 0))],
        out_specs=[row, row],
        out_shape=[jax.ShapeDtypeStruct((m, d), F32), jax.ShapeDtypeStruct((m, d), BF16)],
        compiler_params=_params("parallel"),
        name="out_proj_residual",
    )(x, mg, wo, g)


def _ffn_kernel(h2_ref, wg_ref, wu_ref, wd_ref, x2_ref, gf_ref, y_ref):
    f = pl.program_id(1)

    @pl.when(f == 0)
    def _():
        y_ref[...] = x2_ref[...]

    h2 = h2_ref[...]
    a = (jax.nn.silu(_dot(h2, wg_ref[...])) * _dot(h2, wu_ref[...])).astype(BF16)
    y_ref[...] += _dot(a, wd_ref[...])

    @pl.when(f == pl.num_programs(1) - 1)
    def _():
        y_ref[...] = _rms(y_ref[...], gf_ref[...])


def _ffn(h2, w_in, w_out, x2, gf, bm, bf):
    m, d = h2.shape
    dff = w_out.shape[0]
    nf = dff // bf
    row = pl.BlockSpec((bm, d), lambda i, f: (i, 0))
    return pl.pallas_call(
        _ffn_kernel,
        grid=(m // bm, nf),
        in_specs=[row,
                  pl.BlockSpec((d, bf), lambda i, f: (0, f)),
                  pl.BlockSpec((d, bf), lambda i, f: (0, f + nf)),
                  pl.BlockSpec((bf, d), lambda i, f: (f, 0)),
                  row,
                  pl.BlockSpec((1, d), lambda i, f: (0, 0))],
        out_specs=row,
        out_shape=jax.ShapeDtypeStruct((m, d), F32),
        compiler_params=_params("parallel", "arbitrary"),
        name="ffn_swiglu_final_norm",
    )(h2, w_in, w_in, w_out, x2, gf)


def _softmax_step(s, v, m, l, acc):
    m_new = jnp.maximum(m, jnp.max(s, axis=-1, keepdims=True))
    alpha = jnp.exp2(m - m_new)
    p = jnp.exp2(s - m_new)
    l = alpha * l + jnp.sum(p, axis=-1, keepdims=True)
    acc = alpha * acc + _dot(p.astype(BF16), v)
    return m_new, l, acc


def _chunk_mask(tq, tk, q0, k0):
    qp = q0 + lax.broadcasted_iota(jnp.int32, (tq, tk), 0)
    kp = k0 + lax.broadcasted_iota(jnp.int32, (tq, tk), 1)
    return (kp // CHUNK) <= (qp // CHUNK)


def _softmax_init(tq, dv):
    return (jnp.full((tq, 1), NEG_INF, F32), jnp.zeros((tq, 1), F32), jnp.zeros((tq, dv), F32))


def _chunk_mask_t(t):
    kp = lax.broadcasted_iota(jnp.int32, (t, t), 0)
    qp = lax.broadcasted_iota(jnp.int32, (t, t), 1)
    return (kp // CHUNK) <= (qp // CHUNK)


def _probs_t(s, smax, m):
    m_new = jnp.maximum(m, smax)
    return jnp.exp2(s - m_new).astype(BF16), jnp.exp2(m - m_new), m_new


def _attention_scratch(t, dv, streams):
    return [pltpu.VMEM((streams, 2, t, t), F32),
            pltpu.VMEM((streams, dv + VT_ONES, t), F32),
            pltpu.VMEM((streams, 8, t), F32)]


def _causal_attention_t(n, scores, values, finish, s_ref, acc_ref, stat_ref, t, streams):
    row_m, row_smax = 0, 1

    def stat(st, r):
        return stat_ref[st, r:r + 1, :]

    def put_scores(st, par, s):
        s_ref[st, par] = s
        stat_ref[st, row_smax + par:row_smax + par + 1, :] = jnp.max(s, axis=0, keepdims=True)

    def accumulate(st, j, s, smax):
        p, alpha, m = _probs_t(s, smax, stat(st, row_m))
        acc = alpha * acc_ref[st] + _dot(values(j, st), p)
        stat_ref[st, row_m:row_m + 1, :] = m
        return acc

    def step(tau, par):
        for st in range(streams):
            put_scores(st, par, scores(tau, st))
            acc_ref[st] = accumulate(st, tau - 1, s_ref[st, 1 - par], stat(st, row_smax + 1 - par))

    for st in range(streams):
        put_scores(st, 0, scores(0, st))
        acc_ref[st] = jnp.zeros(acc_ref.shape[1:], F32)
        stat_ref[st, row_m:row_m + 1, :] = jnp.full((1, t), NEG_INF, F32)

    def pair(u, carry):
        step(2 * u + 1, 1)
        step(2 * u + 2, 0)
        return carry

    lax.fori_loop(0, n // 2, pair, 0)

    @pl.when(n % 2 == 1)
    def _():
        step(n, 1)

    def tail(par):
        mask = _chunk_mask_t(t)
        res = []
        for st in range(streams):
            s = jnp.where(mask, s_ref[st, par], NEG_INF)
            res.append(accumulate(st, n, s, jnp.max(s, axis=0, keepdims=True)))
        finish(res)

    for par in (0, 1):
        pl.when(n % 2 == par)(functools.partial(tail, par))


def _normalised(acc, dv):
    return (acc[:dv] / acc[dv:dv + 1]).T


def _mla_prompt_kernel(qt_ref, k_ref, vt_ref, o_ref, *scratch, t, heads):
    def scores(j, hd):
        k = k_ref[0, pl.ds(pl.multiple_of(j * t, t), t), hd * MLA_QK_PAD:(hd + 1) * MLA_QK_PAD]
        return _dot(k, qt_ref[0, hd * MLA_QK_PAD:(hd + 1) * MLA_QK_PAD, :])

    def values(j, hd):
        return vt_ref[j, hd * (MLA_V + VT_ONES):(hd + 1) * (MLA_V + VT_ONES), :]

    def finish(res):
        for hd, acc in enumerate(res):
            o_ref[0, :, hd * MLA_V:(hd + 1) * MLA_V] = _normalised(acc, MLA_V).astype(o_ref.dtype)

    _causal_attention_t(pl.program_id(2), scores, values, finish, *scratch, t, heads)


def _mla_prompt(qt, k, vt, t, heads):
    b, l, _ = k.shape
    nq = l // t
    return pl.pallas_call(
        functools.partial(_mla_prompt_kernel, t=t, heads=heads),
        grid=(b, MLA_HEADS // heads, nq),
        in_specs=[pl.BlockSpec((1, heads * MLA_QK_PAD, t), lambda b, h, i: (b * nq + i, h, 0)),
                  pl.BlockSpec((1, l, heads * MLA_QK_PAD), lambda b, h, i: (b, 0, h)),
                  pl.BlockSpec((nq, heads * (MLA_V + VT_ONES), t), lambda b, h, i: (b, h, 0))],
        out_specs=pl.BlockSpec((1, t, heads * MLA_V), lambda b, h, i: (b, i, h)),
        out_shape=jax.ShapeDtypeStruct((b, l, MLA_HEADS * MLA_V), BF16),
        scratch_shapes=_attention_scratch(t, MLA_V, heads),
        compiler_params=_params("parallel", "parallel", "arbitrary"),
        name="mla_attention_prompt",
    )(qt, k, vt)


def _lambda(lq1_ref, lk1_ref, lq2_ref, lk2_ref, lambda_init):
    a = jnp.sum(lq1_ref[...] * lk1_ref[...], axis=-1, keepdims=True)
    b = jnp.sum(lq2_ref[...] * lk2_ref[...], axis=-1, keepdims=True)
    return jnp.exp(a) - jnp.exp(b) + lambda_init


def _diff_finish(c1, c2, lam, g, lambda_init):
    o = c1[2] / c1[1] - lam * (c2[2] / c2[1])
    return _rms(o, g) * (1.0 - lambda_init)


def _diff_prompt_kernel(q1t_ref, q2t_ref, k_ref, vt_ref, lq1_ref, lk1_ref, lq2_ref, lk2_ref, g_ref, o_ref,
                        *scratch, t, lambda_init):
    qt_refs = (q1t_ref, q2t_ref)

    def scores(j, st):
        return _dot(k_ref[0, pl.ds(pl.multiple_of(j * t, t), t), :], qt_refs[st][0])

    def values(j, st):
        return vt_ref[j]

    def finish(res):
        lam = _lambda(lq1_ref, lk1_ref, lq2_ref, lk2_ref, lambda_init)
        o = _normalised(res[0], DIFF_VD) - lam * _normalised(res[1], DIFF_VD)
        o_ref[0] = (_rms(o, g_ref[...]) * (1.0 - lambda_init)).astype(o_ref.dtype)

    _causal_attention_t(pl.program_id(2), scores, values, finish, *scratch, t, 2)


def _diff_prompt(q1t, q2t, k, vt, lams, g, t, lambda_init):
    b, l, _ = k.shape
    nq = l // t
    qs = pl.BlockSpec((1, LANES, t), lambda b, h, i: (b * nq + i, h, 0))
    small = lambda a: pl.BlockSpec(a.shape, lambda b, h, i: (0, 0))
    return pl.pallas_call(
        functools.partial(_diff_prompt_kernel, t=t, lambda_init=lambda_init),
        grid=(b, DIFF_HEADS, nq),
        in_specs=[qs, qs,
                  pl.BlockSpec((1, l, LANES), lambda b, h, i: (b, 0, h)),
                  pl.BlockSpec((nq, DIFF_VD + VT_ONES, t), lambda b, h, i: (b, h, 0))]
                 + [small(a) for a in lams] + [small(g)],
        out_specs=pl.BlockSpec((1, t, DIFF_VD), lambda b, h, i: (b, i, h)),
        out_shape=jax.ShapeDtypeStruct((b, l, DIFF_HEADS * DIFF_VD), BF16),
        scratch_shapes=_attention_scratch(t, DIFF_VD, 2),
        compiler_params=_params("parallel", "parallel", "arbitrary"),
        name="diff_attention_prompt",
    )(q1t, q2t, k, vt, *lams, g)


def _two_part_softmax(q, kc, vc, kn, vn, mask_c, mask_n):
    tq = q.shape[0]
    carry = _softmax_init(tq, vc.shape[1])
    carry = _softmax_step(jnp.where(mask_c, _dot_nt(q, kc), NEG_INF), vc, *carry)
    return _softmax_step(jnp.where(mask_n, _dot_nt(q, kn), NEG_INF), vn, *carry)


def _mla_sample_kernel(q_ref, kc_ref, vc_ref, kn_ref, vn_ref, o_ref, *, past):
    tq, tc = q_ref.shape[1], kc_ref.shape[1]
    _, l, acc = _two_part_softmax(q_ref[0], kc_ref[0], vc_ref[0], kn_ref[0], vn_ref[0],
                                  _chunk_mask(tq, tc, past, 0), _chunk_mask(tq, tq, past, past))
    o_ref[0] = (acc / l).astype(o_ref.dtype)


def _mla_sample(q, kc, vc, kn, vn):
    b, tq, _ = q.shape
    tc = kc.shape[1]
    blk = lambda rows, w: pl.BlockSpec((1, rows, w), lambda b, h: (b, 0, h))
    return pl.pallas_call(
        functools.partial(_mla_sample_kernel, past=tc),
        grid=(b, MLA_HEADS),
        in_specs=[blk(tq, MLA_QK_PAD), blk(tc, MLA_QK_PAD), blk(tc, MLA_V), blk(tq, MLA_QK_PAD), blk(tq, MLA_V)],
        out_specs=blk(tq, MLA_V),
        out_shape=jax.ShapeDtypeStruct((b, tq, MLA_HEADS * MLA_V), BF16),
        compiler_params=_params("parallel", "parallel"),
        name="mla_attention_sample",
    )(q, kc, vc, kn, vn)


def _diff_sample_kernel(q1_ref, q2_ref, kc_ref, vc_ref, kn_ref, vn_ref,
                        lq1_ref, lk1_ref, lq2_ref, lk2_ref, g_ref, o_ref, *, past, lambda_init):
    tq, tc = q1_ref.shape[1], kc_ref.shape[1]
    kc, vc = kc_ref[0].astype(BF16), vc_ref[0].astype(BF16)
    kn, vn = kn_ref[0], vn_ref[0]
    mask_c, mask_n = _chunk_mask(tq, tc, past, 0), _chunk_mask(tq, tq, past, past)
    c1 = _two_part_softmax(q1_ref[0], kc, vc, kn, vn, mask_c, mask_n)
    c2 = _two_part_softmax(q2_ref[0], kc, vc, kn, vn, mask_c, mask_n)
    lam = _lambda(lq1_ref, lk1_ref, lq2_ref, lk2_ref, lambda_init)
    o_ref[0] = _diff_finish(c1, c2, lam, g_ref[...], lambda_init).astype(o_ref.dtype)


def _diff_sample(q1, q2, kc, vc, kn, vn, lams, g, lambda_init, layer):
    b, tq, _ = q1.shape
    tc = kc.shape[1]
    blk = lambda rows: pl.BlockSpec((1, rows, LANES), lambda b, h: (b, 0, h))
    cache = pl.BlockSpec((1, tc, LANES), lambda b, h: (layer * q1.shape[0] + b, 0, h))
    small = lambda a: pl.BlockSpec(a.shape, lambda b, h: (0, 0))
    return pl.pallas_call(
        functools.partial(_diff_sample_kernel, past=tc, lambda_init=lambda_init),
        grid=(b, DIFF_HEADS),
        in_specs=[blk(tq), blk(tq), cache, cache, blk(tq), blk(tq)] + [small(a) for a in lams] + [small(g)],
        out_specs=blk(tq),
        out_shape=jax.ShapeDtypeStruct((b, tq, DIFF_HEADS * DIFF_VD), BF16),
        compiler_params=_params("parallel", "parallel"),
        name="diff_attention_sample",
    )(q1, q2, kc, vc, kn, vn, *lams, g)


def _rope_tables(pos, theta, rot_dim, period):
    half = rot_dim // 2
    inv = 1.0 / (jnp.float32(theta) ** (jnp.arange(half, dtype=F32) / half))
    ang = pos.astype(F32)[:, None] * inv[None, :]
    cos, sin = jnp.cos(ang), jnp.sin(ang)
    n = pos.shape[0]
    rest = period - rot_dim
    c = jnp.concatenate([cos, cos, jnp.ones((n, rest), F32)], axis=1)
    s1 = jnp.concatenate([-sin, jnp.zeros((n, half + rest), F32)], axis=1)
    s2 = jnp.concatenate([jnp.zeros((n, half), F32), sin, jnp.zeros((n, rest), F32)], axis=1)
    reps = LANES // period
    return tuple(jnp.tile(a, (1, reps)) for a in (c, s1, s2))


def _layer_weights(w_in, mla_w_uq, mla_w_ukv):
    c_kv = MLA_Q_LORA
    c_kr = c_kv + MLA_KV_LORA
    c_dq = c_kr + MLA_ROPE
    dqk = DIFF_HEADS * 2 * DIFF_DH
    c_dk = c_dq + dqk
    c_dv = c_dk + dqk
    c_ga = c_dv + DIFF_HEADS * DIFF_VD
    d = w_in.shape[0]
    c_gb = c_ga + d
    wb = w_in.astype(BF16)
    wq = wb[:, :c_kv]
    wkv = jnp.pad(wb[:, c_kv:c_dq], ((0, 0), (0, LANES - MLA_ROPE)))
    wdq, wdk, wdv = wb[:, c_dq:c_dk], wb[:, c_dk:c_dv], wb[:, c_dv:c_ga]
    wga, wgb = wb[:, c_ga:c_gb], wb[:, c_gb:]
    uq = mla_w_uq.astype(BF16).reshape(MLA_Q_LORA, MLA_HEADS, MLA_NOPE + MLA_ROPE)
    uq = jnp.pad(uq, ((0, 0), (0, 0), (0, MLA_QK_PAD - MLA_NOPE - MLA_ROPE))).reshape(MLA_Q_LORA, -1)
    ukv = mla_w_ukv.astype(BF16).reshape(MLA_KV_LORA, MLA_HEADS, MLA_NOPE + MLA_V)
    ukv = jnp.concatenate([ukv[:, :, :MLA_NOPE].reshape(MLA_KV_LORA, -1),
                           ukv[:, :, MLA_NOPE:].reshape(MLA_KV_LORA, -1)], axis=1)
    return wq, wkv, wdq, wdk, wdv, wga, wgb, uq, ukv


def _block(m, want):
    return want if m % want == 0 else m


def kernel(x_prompt, x_sample, cache_mla_ckv, cache_mla_krope, cache_diff_k, cache_diff_v, norm_mix, w_in,
           mla_q_norm, mla_w_uq, mla_kv_norm, mla_w_ukv, diff_lq1, diff_lk1, diff_lq2, diff_lk2, diff_subln,
           w_branch_a, w_branch_b, w_out, norm_ffn, w_ffn_in, w_ffn_out, norm_final):
    bp, lp, d = x_prompt.shape
    bs, ls, _ = x_sample.shape
    depth, _, past, _ = cache_mla_ckv.shape
    assert depth == 1, "the FFN kernel fuses the final norm, so it serves the last (only) layer"
    mp, ms = bp * lp, bs * ls
    t_attn = 512

    pos_p = jnp.arange(lp)
    pos_s = past + jnp.arange(ls)
    tabs_mla_p = _rope_tables(pos_p, MLA_THETA, MLA_ROPE, LANES)
    tabs_mla_s = tuple(jnp.tile(a, (bs, 1)) for a in _rope_tables(pos_s, MLA_THETA, MLA_ROPE, LANES))
    tabs_dif_p = _rope_tables(pos_p, ROPE_THETA, DIFF_ROT, DIFF_DH)
    tabs_dif_s = tuple(jnp.tile(a, (bs, 1)) for a in _rope_tables(pos_s, ROPE_THETA, DIFF_ROT, DIFF_DH))
    q_scale_mla = (MLA_NOPE + MLA_ROPE) ** -0.5 * LOG2E
    q_scale_dif = DIFF_DH ** -0.5 * LOG2E
    gfinal = norm_final.reshape(1, d)

    xp = x_prompt.reshape(mp, d)
    xs = x_sample.reshape(ms, d)
    rows_p, rows_s = [], []
    for l in range(depth):
        lambda_init = 0.8 - 0.6 * math.exp(-0.3 * l)
        wq, wkv, wdq, wdk, wdv, wga, wgb, uq, ukv = _layer_weights(w_in[l], mla_w_uq[l], mla_w_ukv[l])
        gmix, gq, gkv = norm_mix[l].reshape(1, -1), mla_q_norm[l].reshape(1, -1), mla_kv_norm[l].reshape(1, -1)
        lams = tuple(a[l].reshape(1, -1) for a in (diff_lq1, diff_lk1, diff_lq2, diff_lk2))
        gsub = diff_subln[l].reshape(1, -1)
        wa, wbr, wo = w_branch_a[l].astype(BF16), w_branch_b[l].astype(BF16), w_out[l].astype(BF16)
        wfi, wfo = w_ffn_in[l].astype(BF16), w_ffn_out[l].astype(BF16)
        gffn = norm_ffn[l].reshape(1, -1)

        def stage1(x, bm, tabs_mla, tabs_dif, tr):
            h, q, ckv, krope, kmla, vmla = _mla_proj(x, gmix, wq, gq, uq, wkv, gkv, ukv, tabs_mla, bm,
                                                     q_scale_mla, tr)
            q1, q2, dk, dkb, dv, dvb = _diff_proj(h, wdq, wdk, wdv, tabs_dif, bm, q_scale_dif, tr)
            return h, q, ckv, krope, kmla, vmla, q1, q2, dk, dkb, dv, dvb

        def stage2(x, h, oa, ob, bm, bn, bf):
            mg = _merge(h, oa, ob, wga, wgb, wa, wbr, bm, bn)
            x2, h2 = _outproj(x, mg, wo, gffn, min(bm, 256))
            return _ffn(h2, wfi, wfo, x2, gfinal, bm, bf)

        h, q, ckv, krope, kmla, vmla, q1, q2, dk, dkb, dv, dvb = stage1(xp, t_attn, tabs_mla_p, tabs_dif_p, True)
        r3 = lambda a: a.reshape(bp, lp, -1)
        oa = _mla_prompt(q, r3(kmla), vmla, t_attn, 2)
        ob = _diff_prompt(q1, q2, r3(dkb), dvb, lams, gsub, t_attn, lambda_init)
        xp = stage2(xp, h, oa.reshape(mp, -1), ob.reshape(mp, -1), _block(mp, 1024), 512, 256)
        rows_p.append((ckv.reshape(bp, lp, -1), krope.reshape(bp, lp, -1),
                       dk.reshape(bp, lp, DIFF_HEADS, -1), dv.reshape(bp, lp, DIFF_HEADS, -1)))

        h, q, ckv, krope, kmla, vmla, q1, q2, dk, dkb, dv, dvb = stage1(xs, ms, tabs_mla_s, tabs_dif_s, False)
        kr_pad = jnp.pad(cache_mla_krope[l].reshape(bs * past, MLA_ROPE), ((0, 0), (0, LANES - MLA_ROPE)))
        kc, vc = _kvcache_expand(cache_mla_ckv[l].reshape(bs * past, -1), kr_pad, ukv, _block(bs * past, 512))
        r3 = lambda a: a.reshape(bs, ls, -1)
        c3 = lambda a: a.reshape(bs, past, -1)
        oa = _mla_sample(r3(q), c3(kc), c3(vc), r3(kmla), r3(vmla))
        call = lambda a: a.reshape(depth * bs, past, -1)
        ob = _diff_sample(r3(q1), r3(q2), call(cache_diff_k), call(cache_diff_v), r3(dkb), r3(dvb),
                          lams, gsub, lambda_init, l)
        xs = stage2(xs, h, oa.reshape(ms, -1), ob.reshape(ms, -1), ms, 512, 512)
        rows_s.append((ckv.reshape(bs, ls, -1), krope.reshape(bs, ls, -1),
                       dk.reshape(bs, ls, DIFF_HEADS, -1), dv.reshape(bs, ls, DIFF_HEADS, -1)))

    y_prompt = xp.reshape(bp, lp, d)
    y_sample = xs.reshape(bs, ls, d)
    stack = lambda rows, i: jnp.stack([r[i] for r in rows], axis=0)
    return (y_prompt, y_sample,
            stack(rows_p, 0), stack(rows_p, 1), stack(rows_p, 2), stack(rows_p, 3),
            stack(rows_s, 0), stack(rows_s, 1), stack(rows_s, 2), stack(rows_s, 3))
```

```python
import functools
import math

import jax
import jax.numpy as jnp
from jax import lax
from jax.experimental import pallas as pl
from jax.experimental.pallas import tpu as pltpu

F32 = jnp.float32
BF16 = jnp.bfloat16

CHUNK = 64
EPS = 1e-6
NEG_INF = -1e30

MLA_HEADS = 8
MLA_Q_LORA = 512
MLA_KV_LORA = 256
MLA_NOPE = 128
MLA_ROPE = 64
MLA_V = 128
MLA_THETA = 10000.0
MLA_QK_PAD = 256

DIFF_HEADS = 8
DIFF_DH = 64
DIFF_VD = 2 * DIFF_DH
DIFF_ROT = DIFF_DH // 4
ROPE_THETA = 500000.0

LANES = 128
VT_ONES = 16
LOG2E = math.log2(math.e)
VMEM_LIMIT = 56 * 1024 * 1024


def _params(*sem):
    return pltpu.CompilerParams(dimension_semantics=sem, vmem_limit_bytes=VMEM_LIMIT)


def _dot(a, b):
    return jnp.dot(a, b, preferred_element_type=F32)


def _dot_nt(a, b):
    return lax.dot_general(a, b, (((1,), (1,)), ((), ())), preferred_element_type=F32)


def _rms(x, g):
    return x * lax.rsqrt(jnp.mean(x * x, axis=-1, keepdims=True) + EPS) * g


def _rope_tile(t, c, s1, s2, half):
    return t * c + pltpu.roll(t, LANES - half, 1) * s1 + pltpu.roll(t, half, 1) * s2


def _put(ref, lo, val, tr):
    w = val.shape[1]
    if tr:
        ref[0, lo:lo + w, :] = val.T.astype(ref.dtype)
    else:
        ref[:, lo:lo + w] = val.astype(ref.dtype)


def _put_v(ref, hd, val, tr):
    bm, w = val.shape
    if tr:
        lo = hd * (w + VT_ONES)
        ref[0, lo:lo + w, :] = val.T.astype(ref.dtype)
        ref[0, lo + w:lo + w + VT_ONES, :] = jnp.ones((VT_ONES, bm), ref.dtype)
    else:
        ref[:, hd * w:(hd + 1) * w] = val.astype(ref.dtype)


def _v_width(heads, dv, tr):
    return heads * (dv + VT_ONES) if tr else heads * dv


def _out(m, n, bm, dtype, tr):
    if tr:
        return pl.BlockSpec((1, n, bm), lambda i: (i, 0, 0)), jax.ShapeDtypeStruct((m // bm, n, bm), dtype)
    return pl.BlockSpec((bm, n), lambda i: (i, 0)), jax.ShapeDtypeStruct((m, n), dtype)


def _qproj_body(h, wq_ref, gq_ref, wuq_ref, c, s1, s2, o_ref, scale, tr):
    qlat = _rms(_dot(h, wq_ref[...]), gq_ref[...]).astype(BF16)
    q = _dot(qlat, wuq_ref[...])
    for hd in range(MLA_HEADS):
        lo = hd * MLA_QK_PAD
        _put(o_ref, lo, q[:, lo:lo + LANES] * scale, tr)
        t = _rope_tile(q[:, lo + LANES:lo + 2 * LANES], c, s1, s2, MLA_ROPE // 2)
        _put(o_ref, lo + LANES, t * scale, tr)


def _kv_expand(ckv, krope_tile, wukv_ref, kmla_ref, vmla_ref, tr):
    kv = _dot(ckv.astype(BF16), wukv_ref[...])
    kr = krope_tile.astype(BF16)
    for hd in range(MLA_HEADS):
        lo = hd * MLA_QK_PAD
        kmla_ref[:, lo:lo + LANES] = kv[:, hd * MLA_NOPE:(hd + 1) * MLA_NOPE].astype(BF16)
        kmla_ref[:, lo + LANES:lo + 2 * LANES] = kr
        vlo = MLA_HEADS * MLA_NOPE + hd * MLA_V
        _put_v(vmla_ref, hd, kv[:, vlo:vlo + MLA_V], tr)


def _mla_proj_kernel(x_ref, gmix_ref, wq_ref, gq_ref, wuq_ref, wkv_ref, gkv_ref, wukv_ref, c_ref, s1_ref, s2_ref,
                     h_ref, q_ref, ckv_ref, krope_ref, kmla_ref, vmla_ref, *, scale, tr):
    h = _rms(x_ref[...], gmix_ref[...]).astype(BF16)
    h_ref[...] = h
    c, s1, s2 = c_ref[...], s1_ref[...], s2_ref[...]
    _qproj_body(h, wq_ref, gq_ref, wuq_ref, c, s1, s2, q_ref, scale, tr)
    z = _dot(h, wkv_ref[...])
    ckv = _rms(z[:, :MLA_KV_LORA], gkv_ref[...])
    ckv_ref[...] = ckv
    t = _rope_tile(z[:, MLA_KV_LORA:], c, s1, s2, MLA_ROPE // 2)
    krope_ref[...] = t[:, :MLA_ROPE]
    _kv_expand(ckv, t, wukv_ref, kmla_ref, vmla_ref, tr)


def _mla_proj(x, gmix, wq, gq, wuq, wkv, gkv, wukv, tabs, bm, scale, tr):
    m, d = x.shape
    nt = tabs[0].shape[0] // bm
    full = lambda a: pl.BlockSpec(a.shape, lambda i: (0, 0))
    tab = pl.BlockSpec((bm, LANES), lambda i: (i % nt, 0))
    row = lambda n: pl.BlockSpec((bm, n), lambda i: (i, 0))
    nk = MLA_HEADS * MLA_QK_PAD
    qspec, qshape = _out(m, wuq.shape[1], bm, BF16, tr)
    vspec, vshape = _out(m, _v_width(MLA_HEADS, MLA_V, tr), bm, BF16, tr)
    return pl.pallas_call(
        functools.partial(_mla_proj_kernel, scale=scale, tr=tr),
        grid=(m // bm,),
        in_specs=[row(d), full(gmix), full(wq), full(gq), full(wuq), full(wkv), full(gkv), full(wukv), tab, tab, tab],
        out_specs=[row(d), qspec, row(MLA_KV_LORA), row(MLA_ROPE), row(nk), vspec],
        out_shape=[jax.ShapeDtypeStruct((m, d), BF16), qshape,
                   jax.ShapeDtypeStruct((m, MLA_KV_LORA), F32), jax.ShapeDtypeStruct((m, MLA_ROPE), F32),
                   jax.ShapeDtypeStruct((m, nk), BF16), vshape],
        compiler_params=_params("parallel"),
        name="norm_mla_proj",
    )(x, gmix, wq, gq, wuq, wkv, gkv, wukv, *tabs)


def _kvcache_kernel(ckv_ref, krope_ref, wukv_ref, kmla_ref, vmla_ref):
    _kv_expand(ckv_ref[...], krope_ref[...], wukv_ref, kmla_ref, vmla_ref, False)


def _kvcache_expand(ckv, krope_pad, wukv, bm):
    m = ckv.shape[0]
    row = lambda n: pl.BlockSpec((bm, n), lambda i: (i, 0))
    nk, nv = MLA_HEADS * MLA_QK_PAD, MLA_HEADS * MLA_V
    return pl.pallas_call(
        _kvcache_kernel,
        grid=(m // bm,),
        in_specs=[row(MLA_KV_LORA), row(LANES), pl.BlockSpec(wukv.shape, lambda i: (0, 0))],
        out_specs=[row(nk), row(nv)],
        out_shape=[jax.ShapeDtypeStruct((m, nk), BF16), jax.ShapeDtypeStruct((m, nv), BF16)],
        compiler_params=_params("parallel"),
        name="mla_kv_cache_expand",
    )(ckv, krope_pad, wukv)


def _diff_proj_kernel(h_ref, wdq_ref, wdk_ref, wdv_ref, c_ref, s1_ref, s2_ref,
                      q1_ref, q2_ref, dk_ref, dkb_ref, dv_ref, dvb_ref, *, scale, tr):
    h = h_ref[...]
    zq = _dot(h, wdq_ref[...])
    zk = _dot(h, wdk_ref[...])
    zv = _dot(h, wdv_ref[...])
    dv_ref[...] = zv
    c, s1, s2 = c_ref[...], s1_ref[...], s2_ref[...]
    first = lax.broadcasted_iota(jnp.int32, (1, LANES), 1) < DIFF_DH
    for hd in range(DIFF_HEADS):
        sl = slice(hd * LANES, (hd + 1) * LANES)
        q = _rope_tile(zq[:, sl], c, s1, s2, DIFF_ROT // 2) * scale
        _put(q1_ref, hd * LANES, jnp.where(first, q, 0.0), tr)
        _put(q2_ref, hd * LANES, jnp.where(first, 0.0, q), tr)
        k = _rope_tile(zk[:, sl], c, s1, s2, DIFF_ROT // 2)
        dk_ref[:, sl] = k
        dkb_ref[:, sl] = k.astype(BF16)
        _put_v(dvb_ref, hd, zv[:, sl], tr)


def _diff_proj(h, wdq, wdk, wdv, tabs, bm, scale, tr):
    m, d = h.shape
    nt = tabs[0].shape[0] // bm
    n = wdq.shape[1]
    full = lambda a: pl.BlockSpec(a.shape, lambda i: (0, 0))
    tab = pl.BlockSpec((bm, LANES), lambda i: (i % nt, 0))
    row = lambda w: pl.BlockSpec((bm, w), lambda i: (i, 0))
    qspec, qshape = _out(m, n, bm, BF16, tr)
    vspec, vshape = _out(m, _v_width(DIFF_HEADS, DIFF_VD, tr), bm, BF16, tr)
    f32_rows, bf16_rows = jax.ShapeDtypeStruct((m, n), F32), jax.ShapeDtypeStruct((m, n), BF16)
    return pl.pallas_call(
        functools.partial(_diff_proj_kernel, scale=scale, tr=tr),
        grid=(m // bm,),
        in_specs=[row(d), full(wdq), full(wdk), full(wdv), tab, tab, tab],
        out_specs=[qspec, qspec, row(n), row(n), row(n), vspec],
        out_shape=[qshape, qshape, f32_rows, bf16_rows, f32_rows, vshape],
        compiler_params=_params("parallel"),
        name="diff_proj",
    )(h, wdq, wdk, wdv, *tabs)


def _merge_kernel(h_ref, oa_ref, ob_ref, wga_ref, wgb_ref, wa_ref, wb_ref, o_ref):
    h = h_ref[...]
    ga = jax.nn.sigmoid(_dot(h, wga_ref[...]))
    gb = jax.nn.sigmoid(_dot(h, wgb_ref[...]))
    ya = _dot(oa_ref[...], wa_ref[...])
    yb = _dot(ob_ref[...], wb_ref[...])
    o_ref[...] = (ga * ya + gb * yb).astype(o_ref.dtype)


def _merge(h, oa, ob, wga, wgb, wa, wb, bm, bn):
    m, d = h.shape
    n = wga.shape[1]
    row = lambda a: pl.BlockSpec((bm, a.shape[1]), lambda i, j: (i, 0))
    col = lambda a: pl.BlockSpec((a.shape[0], bn), lambda i, j: (0, j))
    return pl.pallas_call(
        _merge_kernel,
        grid=(m // bm, n // bn),
        in_specs=[row(h), row(oa), row(ob), col(wga), col(wgb), col(wa), col(wb)],
        out_specs=pl.BlockSpec((bm, bn), lambda i, j: (i, j)),
        out_shape=jax.ShapeDtypeStruct((m, n), BF16),
        compiler_params=_params("parallel", "arbitrary"),
        name="gated_merge",
    )(h, oa, ob, wga, wgb, wa, wb)


def _outproj_kernel(x_ref, mg_ref, wo_ref, g_ref, x2_ref, h2_ref):
    x2 = x_ref[...] + _dot(mg_ref[...], wo_ref[...])
    x2_ref[...] = x2
    h2_ref[...] = _rms(x2, g_ref[...]).astype(BF16)


def _outproj(x, mg, wo, g, bm):
    m, d = x.shape
    row = pl.BlockSpec((bm, d), lambda i: (i, 0))
    return pl.pallas_call(
        _outproj_kernel,
        grid=(m // bm,),
        in_specs=[row, row, pl.BlockSpec(wo.shape, lambda i: (0, 0)), pl.BlockSpec((1, d), lambda i: (0, 0))],
        out_specs=[row, row],
        out_shape=[jax.ShapeDtypeStruct((m, d), F32), jax.ShapeDtypeStruct((m, d), BF16)],
        compiler_params=_params("parallel"),
        name="out_proj_residual",
    )(x, mg, wo, g)


def _ffn_kernel(h2_ref, wg_ref, wu_ref, wd_ref, x2_ref, gf_ref, y_ref):
    f = pl.program_id(1)

    @pl.when(f == 0)
    def _():
        y_ref[...] = x2_ref[...]

    h2 = h2_ref[...]
    a = (jax.nn.silu(_dot(h2, wg_ref[...])) * _dot(h2, wu_ref[...])).astype(BF16)
    y_ref[...] += _dot(a, wd_ref[...])

    @pl.when(f == pl.num_programs(1) - 1)
    def _():
        y_ref[...] = _rms(y_ref[...], gf_ref[...])


def _ffn(h2, w_in, w_out, x2, gf, bm, bf):
    m, d = h2.shape
    dff = w_out.shape[0]
    nf = dff // bf
    row = pl.BlockSpec((bm, d), lambda i, f: (i, 0))
    return pl.pallas_call(
        _ffn_kernel,
        grid=(m // bm, nf),
        in_specs=[row,
                  pl.BlockSpec((d, bf), lambda i, f: (0, f)),
                  pl.BlockSpec((d, bf), lambda i, f: (0, f + nf)),
                  pl.BlockSpec((bf, d), lambda i, f: (f, 0)),
                  row,
                  pl.BlockSpec((1, d), lambda i, f: (0, 0))],
        out_specs=row,
        out_shape=jax.ShapeDtypeStruct((m, d), F32),
        compiler_params=_params("parallel", "arbitrary"),
        name="ffn_swiglu_final_norm",
    )(h2, w_in, w_in, w_out, x2, gf)


def _softmax_step(s, v, m, l, acc):
    m_new = jnp.maximum(m, jnp.max(s, axis=-1, keepdims=True))
    alpha = jnp.exp2(m - m_new)
    p = jnp.exp2(s - m_new)
    l = alpha * l + jnp.sum(p, axis=-1, keepdims=True)
    acc = alpha * acc + _dot(p.astype(BF16), v)
    return m_new, l, acc


def _chunk_mask(tq, tk, q0, k0):
    qp = q0 + lax.broadcasted_iota(jnp.int32, (tq, tk), 0)
    kp = k0 + lax.broadcasted_iota(jnp.int32, (tq, tk), 1)
    return (kp // CHUNK) <= (qp // CHUNK)


def _softmax_init(tq, dv):
    return (jnp.full((tq, 1), NEG_INF, F32), jnp.zeros((tq, 1), F32), jnp.zeros((tq, dv), F32))


def _chunk_mask_t(t):
    kp = lax.broadcasted_iota(jnp.int32, (t, t), 0)
    qp = lax.broadcasted_iota(jnp.int32, (t, t), 1)
    return (kp // CHUNK) <= (qp // CHUNK)


def _probs_t(s, smax, m):
    m_new = jnp.maximum(m, smax)
    return jnp.exp2(s - m_new).astype(BF16), jnp.exp2(m - m_new), m_new


def _attention_scratch(t, dv, streams):
    return [pltpu.VMEM((streams, 2, t, t), F32),
            pltpu.VMEM((streams, dv + VT_ONES, t), F32),
            pltpu.VMEM((streams, 8, t), F32)]


def _causal_attention_t(n, scores, values, finish, s_ref, acc_ref, stat_ref, t, streams):
    row_m, row_smax = 0, 1

    def stat(st, r):
        return stat_ref[st, r:r + 1, :]

    def put_scores(st, par, s):
        s_ref[st, par] = s
        stat_ref[st, row_smax + par:row_smax + par + 1, :] = jnp.max(s, axis=0, keepdims=True)

    def accumulate(st, j, s, smax):
        p, alpha, m = _probs_t(s, smax, stat(st, row_m))
        acc = alpha * acc_ref[st] + _dot(values(j, st), p)
        stat_ref[st, row_m:row_m + 1, :] = m
        return acc

    def step(tau, par):
        for st in range(streams):
            put_scores(st, par, scores(tau, st))
            acc_ref[st] = accumulate(st, tau - 1, s_ref[st, 1 - par], stat(st, row_smax + 1 - par))

    for st in range(streams):
        put_scores(st, 0, scores(0, st))
        acc_ref[st] = jnp.zeros(acc_ref.shape[1:], F32)
        stat_ref[st, row_m:row_m + 1, :] = jnp.full((1, t), NEG_INF, F32)

    unroll = 4

    def trip(u, carry):
        for k in range(unroll):
            step(unroll * u + 1 + k, (1 + k) % 2)
        return carry

    lax.fori_loop(0, n // unroll, trip, 0)
    done = (n // unroll) * unroll

    @pl.when(n - done >= 2)
    def _():
        step(done + 1, 1)
        step(done + 2, 0)

    @pl.when(n % 2 == 1)
    def _():
        step(n, 1)

    def tail(par):
        mask = _chunk_mask_t(t)
        res = []
        for st in range(streams):
            s = jnp.where(mask, s_ref[st, par], NEG_INF)
            res.append(accumulate(st, n, s, jnp.max(s, axis=0, keepdims=True)))
        finish(res)

    for par in (0, 1):
        pl.when(n % 2 == par)(functools.partial(tail, par))


def _normalised(acc, dv):
    return (acc[:dv] / acc[dv:dv + 1]).T


def _mla_prompt_kernel(qt_ref, k_ref, vt_ref, o_ref, *scratch, t, heads):
    def scores(j, hd):
        k = k_ref[0, pl.ds(pl.multiple_of(j * t, t), t), hd * MLA_QK_PAD:(hd + 1) * MLA_QK_PAD]
        return _dot(k, qt_ref[0, hd * MLA_QK_PAD:(hd + 1) * MLA_QK_PAD, :])

    def values(j, hd):
        return vt_ref[j, hd * (MLA_V + VT_ONES):(hd + 1) * (MLA_V + VT_ONES), :]

    def finish(res):
        for hd, acc in enumerate(res):
            o_ref[0, :, hd * MLA_V:(hd + 1) * MLA_V] = _normalised(acc, MLA_V).astype(o_ref.dtype)

    _causal_attention_t(pl.program_id(2), scores, values, finish, *scratch, t, heads)


def _mla_prompt(qt, k, vt, t, heads):
    b, l, _ = k.shape
    nq = l // t
    return pl.pallas_call(
        functools.partial(_mla_prompt_kernel, t=t, heads=heads),
        grid=(b, MLA_HEADS // heads, nq),
        in_specs=[pl.BlockSpec((1, heads * MLA_QK_PAD, t), lambda b, h, i: (b * nq + i, h, 0)),
                  pl.BlockSpec((1, l, heads * MLA_QK_PAD), lambda b, h, i: (b, 0, h)),
                  pl.BlockSpec((nq, heads * (MLA_V + VT_ONES), t), lambda b, h, i: (b, h, 0))],
        out_specs=pl.BlockSpec((1, t, heads * MLA_V), lambda b, h, i: (b, i, h)),
        out_shape=jax.ShapeDtypeStruct((b, l, MLA_HEADS * MLA_V), BF16),
        scratch_shapes=_attention_scratch(t, MLA_V, heads),
        compiler_params=_params("parallel", "parallel", "arbitrary"),
        name="mla_attention_prompt",
    )(qt, k, vt)


def _lambda(lq1_ref, lk1_ref, lq2_ref, lk2_ref, lambda_init):
    a = jnp.sum(lq1_ref[...] * lk1_ref[...], axis=-1, keepdims=True)
    b = jnp.sum(lq2_ref[...] * lk2_ref[...], axis=-1, keepdims=True)
    return jnp.exp(a) - jnp.exp(b) + lambda_init


def _diff_finish(c1, c2, lam, g, lambda_init):
    o = c1[2] / c1[1] - lam * (c2[2] / c2[1])
    return _rms(o, g) * (1.0 - lambda_init)


def _diff_prompt_kernel(q1t_ref, q2t_ref, k_ref, vt_ref, lq1_ref, lk1_ref, lq2_ref, lk2_ref, g_ref, o_ref,
                        *scratch, t, lambda_init):
    qt_refs = (q1t_ref, q2t_ref)

    def scores(j, st):
        return _dot(k_ref[0, pl.ds(pl.multiple_of(j * t, t), t), :], qt_refs[st][0])

    def values(j, st):
        return vt_ref[j]

    def finish(res):
        lam = _lambda(lq1_ref, lk1_ref, lq2_ref, lk2_ref, lambda_init)
        o = _normalised(res[0], DIFF_VD) - lam * _normalised(res[1], DIFF_VD)
        o_ref[0] = (_rms(o, g_ref[...]) * (1.0 - lambda_init)).astype(o_ref.dtype)

    _causal_attention_t(pl.program_id(2), scores, values, finish, *scratch, t, 2)


def _diff_prompt(q1t, q2t, k, vt, lams, g, t, lambda_init):
    b, l, _ = k.shape
    nq = l // t
    qs = pl.BlockSpec((1, LANES, t), lambda b, h, i: (b * nq + i, h, 0))
    small = lambda a: pl.BlockSpec(a.shape, lambda b, h, i: (0, 0))
    return pl.pallas_call(
        functools.partial(_diff_prompt_kernel, t=t, lambda_init=lambda_init),
        grid=(b, DIFF_HEADS, nq),
        in_specs=[qs, qs,
                  pl.BlockSpec((1, l, LANES), lambda b, h, i: (b, 0, h)),
                  pl.BlockSpec((nq, DIFF_VD + VT_ONES, t), lambda b, h, i: (b, h, 0))]
                 + [small(a) for a in lams] + [small(g)],
        out_specs=pl.BlockSpec((1, t, DIFF_VD), lambda b, h, i: (b, i, h)),
        out_shape=jax.ShapeDtypeStruct((b, l, DIFF_HEADS * DIFF_VD), BF16),
        scratch_shapes=_attention_scratch(t, DIFF_VD, 2),
        compiler_params=_params("parallel", "parallel", "arbitrary"),
        name="diff_attention_prompt",
    )(q1t, q2t, k, vt, *lams, g)


def _two_part_softmax(q, kc, vc, kn, vn, mask_c, mask_n):
    tq = q.shape[0]
    carry = _softmax_init(tq, vc.shape[1])
    carry = _softmax_step(jnp.where(mask_c, _dot_nt(q, kc), NEG_INF), vc, *carry)
    return _softmax_step(jnp.where(mask_n, _dot_nt(q, kn), NEG_INF), vn, *carry)


def _mla_sample_kernel(q_ref, kc_ref, vc_ref, kn_ref, vn_ref, o_ref, *, past):
    tq, tc = q_ref.shape[1], kc_ref.shape[1]
    _, l, acc = _two_part_softmax(q_ref[0], kc_ref[0], vc_ref[0], kn_ref[0], vn_ref[0],
                                  _chunk_mask(tq, tc, past, 0), _chunk_mask(tq, tq, past, past))
    o_ref[0] = (acc / l).astype(o_ref.dtype)


def _mla_sample(q, kc, vc, kn, vn):
    b, tq, _ = q.shape
    tc = kc.shape[1]
    blk = lambda rows, w: pl.BlockSpec((1, rows, w), lambda b, h: (b, 0, h))
    return pl.pallas_call(
        functools.partial(_mla_sample_kernel, past=tc),
        grid=(b, MLA_HEADS),
        in_specs=[blk(tq, MLA_QK_PAD), blk(tc, MLA_QK_PAD), blk(tc, MLA_V), blk(tq, MLA_QK_PAD), blk(tq, MLA_V)],
        out_specs=blk(tq, MLA_V),
        out_shape=jax.ShapeDtypeStruct((b, tq, MLA_HEADS * MLA_V), BF16),
        compiler_params=_params("parallel", "parallel"),
        name="mla_attention_sample",
    )(q, kc, vc, kn, vn)


def _diff_sample_kernel(q1_ref, q2_ref, kc_ref, vc_ref, kn_ref, vn_ref,
                        lq1_ref, lk1_ref, lq2_ref, lk2_ref, g_ref, o_ref, *, past, lambda_init):
    tq, tc = q1_ref.shape[1], kc_ref.shape[1]
    kc, vc = kc_ref[0].astype(BF16), vc_ref[0].astype(BF16)
    kn, vn = kn_ref[0], vn_ref[0]
    mask_c, mask_n = _chunk_mask(tq, tc, past, 0), _chunk_mask(tq, tq, past, past)
    c1 = _two_part_softmax(q1_ref[0], kc, vc, kn, vn, mask_c, mask_n)
    c2 = _two_part_softmax(q2_ref[0], kc, vc, kn, vn, mask_c, mask_n)
    lam = _lambda(lq1_ref, lk1_ref, lq2_ref, lk2_ref, lambda_init)
    o_ref[0] = _diff_finish(c1, c2, lam, g_ref[...], lambda_init).astype(o_ref.dtype)


def _diff_sample(q1, q2, kc, vc, kn, vn, lams, g, lambda_init, layer):
    b, tq, _ = q1.shape
    tc = kc.shape[1]
    blk = lambda rows: pl.BlockSpec((1, rows, LANES), lambda b, h: (b, 0, h))
    cache = pl.BlockSpec((1, tc, LANES), lambda b, h: (layer * q1.shape[0] + b, 0, h))
    small = lambda a: pl.BlockSpec(a.shape, lambda b, h: (0, 0))
    return pl.pallas_call(
        functools.partial(_diff_sample_kernel, past=tc, lambda_init=lambda_init),
        grid=(b, DIFF_HEADS),
        in_specs=[blk(tq), blk(tq), cache, cache, blk(tq), blk(tq)] + [small(a) for a in lams] + [small(g)],
        out_specs=blk(tq),
        out_shape=jax.ShapeDtypeStruct((b, tq, DIFF_HEADS * DIFF_VD), BF16),
        compiler_params=_params("parallel", "parallel"),
        name="diff_attention_sample",
    )(q1, q2, kc, vc, kn, vn, *lams, g)


def _rope_tables(pos, theta, rot_dim, period):
    half = rot_dim // 2
    inv = 1.0 / (jnp.float32(theta) ** (jnp.arange(half, dtype=F32) / half))
    ang = pos.astype(F32)[:, None] * inv[None, :]
    cos, sin = jnp.cos(ang), jnp.sin(ang)
    n = pos.shape[0]
    rest = period - rot_dim
    c = jnp.concatenate([cos, cos, jnp.ones((n, rest), F32)], axis=1)
    s1 = jnp.concatenate([-sin, jnp.zeros((n, half + rest), F32)], axis=1)
    s2 = jnp.concatenate([jnp.zeros((n, half), F32), sin, jnp.zeros((n, rest), F32)], axis=1)
    reps = LANES // period
    return tuple(jnp.tile(a, (1, reps)) for a in (c, s1, s2))


def _layer_weights(w_in, mla_w_uq, mla_w_ukv):
    c_kv = MLA_Q_LORA
    c_kr = c_kv + MLA_KV_LORA
    c_dq = c_kr + MLA_ROPE
    dqk = DIFF_HEADS * 2 * DIFF_DH
    c_dk = c_dq + dqk
    c_dv = c_dk + dqk
    c_ga = c_dv + DIFF_HEADS * DIFF_VD
    d = w_in.shape[0]
    c_gb = c_ga + d
    cols = lambda lo, hi: w_in[:, lo:hi].astype(BF16)
    wq = cols(0, c_kv)
    wkv = jnp.pad(cols(c_kv, c_dq), ((0, 0), (0, LANES - MLA_ROPE)))
    wdq, wdk, wdv = cols(c_dq, c_dk), cols(c_dk, c_dv), cols(c_dv, c_ga)
    wga, wgb = cols(c_ga, c_gb), cols(c_gb, w_in.shape[1])
    uq = mla_w_uq.astype(BF16).reshape(MLA_Q_LORA, MLA_HEADS, MLA_NOPE + MLA_ROPE)
    uq = jnp.pad(uq, ((0, 0), (0, 0), (0, MLA_QK_PAD - MLA_NOPE - MLA_ROPE))).reshape(MLA_Q_LORA, -1)
    ukv = mla_w_ukv.astype(BF16).reshape(MLA_KV_LORA, MLA_HEADS, MLA_NOPE + MLA_V)
    ukv = jnp.concatenate([ukv[:, :, :MLA_NOPE].reshape(MLA_KV_LORA, -1),
                           ukv[:, :, MLA_NOPE:].reshape(MLA_KV_LORA, -1)], axis=1)
    return wq, wkv, wdq, wdk, wdv, wga, wgb, uq, ukv


def _block(m, want):
    return want if m % want == 0 else m


def kernel(x_prompt, x_sample, cache_mla_ckv, cache_mla_krope, cache_diff_k, cache_diff_v, norm_mix, w_in,
           mla_q_norm, mla_w_uq, mla_kv_norm, mla_w_ukv, diff_lq1, diff_lk1, diff_lq2, diff_lk2, diff_subln,
           w_branch_a, w_branch_b, w_out, norm_ffn, w_ffn_in, w_ffn_out, norm_final):
    bp, lp, d = x_prompt.shape
    bs, ls, _ = x_sample.shape
    depth, _, past, _ = cache_mla_ckv.shape
    assert depth == 1, "the FFN kernel fuses the final norm, so it serves the last (only) layer"
    mp, ms = bp * lp, bs * ls
    t_attn = 512

    pos_p = jnp.arange(lp)
    pos_s = past + jnp.arange(ls)
    tabs_mla_p = _rope_tables(pos_p, MLA_THETA, MLA_ROPE, LANES)
    tabs_mla_s = tuple(jnp.tile(a, (bs, 1)) for a in _rope_tables(pos_s, MLA_THETA, MLA_ROPE, LANES))
    tabs_dif_p = _rope_tables(pos_p, ROPE_THETA, DIFF_ROT, DIFF_DH)
    tabs_dif_s = tuple(jnp.tile(a, (bs, 1)) for a in _rope_tables(pos_s, ROPE_THETA, DIFF_ROT, DIFF_DH))
    q_scale_mla = (MLA_NOPE + MLA_ROPE) ** -0.5 * LOG2E
    q_scale_dif = DIFF_DH ** -0.5 * LOG2E
    gfinal = norm_final.reshape(1, d)

    xp = x_prompt.reshape(mp, d)
    xs = x_sample.reshape(ms, d)
    rows_p, rows_s = [], []
    for l in range(depth):
        lambda_init = 0.8 - 0.6 * math.exp(-0.3 * l)
        wq, wkv, wdq, wdk, wdv, wga, wgb, uq, ukv = _layer_weights(w_in[l], mla_w_uq[l], mla_w_ukv[l])
        gmix, gq, gkv = norm_mix[l].reshape(1, -1), mla_q_norm[l].reshape(1, -1), mla_kv_norm[l].reshape(1, -1)
        lams = tuple(a[l].reshape(1, -1) for a in (diff_lq1, diff_lk1, diff_lq2, diff_lk2))
        gsub = diff_subln[l].reshape(1, -1)
        wa, wbr, wo = w_branch_a[l].astype(BF16), w_branch_b[l].astype(BF16), w_out[l].astype(BF16)
        wfi, wfo = w_ffn_in[l].astype(BF16), w_ffn_out[l].astype(BF16)
        gffn = norm_ffn[l].reshape(1, -1)

        def stage1(x, bm, tabs_mla, tabs_dif, tr):
            h, q, ckv, krope, kmla, vmla = _mla_proj(x, gmix, wq, gq, uq, wkv, gkv, ukv, tabs_mla, bm,
                                                     q_scale_mla, tr)
            q1, q2, dk, dkb, dv, dvb = _diff_proj(h, wdq, wdk, wdv, tabs_dif, bm, q_scale_dif, tr)
            return h, q, ckv, krope, kmla, vmla, q1, q2, dk, dkb, dv, dvb

        def stage2(x, h, oa, ob, bm, bn, bf):
            mg = _merge(h, oa, ob, wga, wgb, wa, wbr, bm, bn)
            x2, h2 = _outproj(x, mg, wo, gffn, min(bm, 512))
            return _ffn(h2, wfi, wfo, x2, gfinal, bm, bf)

        h, q, ckv, krope, kmla, vmla, q1, q2, dk, dkb, dv, dvb = stage1(xp, t_attn, tabs_mla_p, tabs_dif_p, True)
        r3 = lambda a: a.reshape(bp, lp, -1)
        oa = _mla_prompt(q, r3(kmla), vmla, t_attn, 2)
        ob = _diff_prompt(q1, q2, r3(dkb), dvb, lams, gsub, t_attn, lambda_init)
        xp = stage2(xp, h, oa.reshape(mp, -1), ob.reshape(mp, -1), _block(mp, 1024), 512, 256)
        rows_p.append((ckv.reshape(bp, lp, -1), krope.reshape(bp, lp, -1),
                       dk.reshape(bp, lp, DIFF_HEADS, -1), dv.reshape(bp, lp, DIFF_HEADS, -1)))

        h, q, ckv, krope, kmla, vmla, q1, q2, dk, dkb, dv, dvb = stage1(xs, ms, tabs_mla_s, tabs_dif_s, False)
        kr_pad = jnp.pad(cache_mla_krope[l].reshape(bs * past, MLA_ROPE), ((0, 0), (0, LANES - MLA_ROPE)))
        kc, vc = _kvcache_expand(cache_mla_ckv[l].reshape(bs * past, -1), kr_pad, ukv, _block(bs * past, 512))
        r3 = lambda a: a.reshape(bs, ls, -1)
        c3 = lambda a: a.reshape(bs, past, -1)
        oa = _mla_sample(r3(q), c3(kc), c3(vc), r3(kmla), r3(vmla))
        call = lambda a: a.reshape(depth * bs, past, -1)
        ob = _diff_sample(r3(q1), r3(q2), call(cache_diff_k), call(cache_diff_v), r3(dkb), r3(dvb),
                          lams, gsub, lambda_init, l)
        xs = stage2(xs, h, oa.reshape(ms, -1), ob.reshape(ms, -1), ms, 512, 512)
        rows_s.append((ckv.reshape(bs, ls, -1), krope.reshape(bs, ls, -1),
                       dk.reshape(bs, ls, DIFF_HEADS, -1), dv.reshape(bs, ls, DIFF_HEADS, -1)))

    y_prompt = xp.reshape(bp, lp, d)
    y_sample = xs.reshape(bs, ls, d)
    stack = lambda rows, i: jnp.stack([r[i] for r in rows], axis=0)
    return (y_prompt, y_sample,
            stack(rows_p, 0), stack(rows_p, 1), stack(rows_p, 2), stack(rows_p, 3),
            stack(rows_s, 0), stack(rows_s, 1), stack(rows_s, 2), stack(rows_s, 3))
```

```python
import functools
import math

import jax
import jax.numpy as jnp
from jax import lax
from jax.experimental import pallas as pl
from jax.experimental.pallas import tpu as pltpu

F32 = jnp.float32
BF16 = jnp.bfloat16

CHUNK = 64
EPS = 1e-6
NEG_INF = -1e30

MLA_HEADS = 8
MLA_Q_LORA = 512
MLA_KV_LORA = 256
MLA_NOPE = 128
MLA_ROPE = 64
MLA_V = 128
MLA_THETA = 10000.0
MLA_QK_PAD = 256

DIFF_HEADS = 8
DIFF_DH = 64
DIFF_VD = 2 * DIFF_DH
DIFF_ROT = DIFF_DH // 4
ROPE_THETA = 500000.0

LANES = 128
VT_ONES = 16
LOG2E = math.log2(math.e)
VMEM_LIMIT = 56 * 1024 * 1024


def _params(*sem):
    return pltpu.CompilerParams(dimension_semantics=sem, vmem_limit_bytes=VMEM_LIMIT)


def _dot(a, b):
    return jnp.dot(a, b, preferred_element_type=F32)


def _dot_nt(a, b):
    return lax.dot_general(a, b, (((1,), (1,)), ((), ())), preferred_element_type=F32)


def _rms(x, g):
    return x * lax.rsqrt(jnp.mean(x * x, axis=-1, keepdims=True) + EPS) * g


def _rope_tile(t, c, s1, s2, half):
    return t * c + pltpu.roll(t, LANES - half, 1) * s1 + pltpu.roll(t, half, 1) * s2


def _put(ref, lo, val, tr):
    w = val.shape[1]
    if tr:
        ref[0, lo:lo + w, :] = val.T.astype(ref.dtype)
    else:
        ref[:, lo:lo + w] = val.astype(ref.dtype)


def _put_v(ref, hd, val, tr):
    bm, w = val.shape
    if tr:
        lo = hd * (w + VT_ONES)
        ref[0, lo:lo + w, :] = val.T.astype(ref.dtype)
        ref[0, lo + w:lo + w + VT_ONES, :] = jnp.ones((VT_ONES, bm), ref.dtype)
    else:
        ref[:, hd * w:(hd + 1) * w] = val.astype(ref.dtype)


def _v_width(heads, dv, tr):
    return heads * (dv + VT_ONES) if tr else heads * dv


def _out(m, n, bm, dtype, tr):
    if tr:
        return pl.BlockSpec((1, n, bm), lambda i: (i, 0, 0)), jax.ShapeDtypeStruct((m // bm, n, bm), dtype)
    return pl.BlockSpec((bm, n), lambda i: (i, 0)), jax.ShapeDtypeStruct((m, n), dtype)


def _qproj_body(h, wq_ref, gq_ref, wuq_ref, c, s1, s2, o_ref, scale, tr):
    qlat = _rms(_dot(h, wq_ref[...]), gq_ref[...]).astype(BF16)
    q = _dot(qlat, wuq_ref[...])
    for hd in range(MLA_HEADS):
        lo = hd * MLA_QK_PAD
        _put(o_ref, lo, q[:, lo:lo + LANES] * scale, tr)
        t = _rope_tile(q[:, lo + LANES:lo + 2 * LANES], c, s1, s2, MLA_ROPE // 2)
        _put(o_ref, lo + LANES, t * scale, tr)


def _kv_expand(ckv, krope_tile, wukv_ref, kmla_ref, vmla_ref, tr):
    kv = _dot(ckv.astype(BF16), wukv_ref[...])
    kr = krope_tile.astype(BF16)
    for hd in range(MLA_HEADS):
        lo = hd * MLA_QK_PAD
        kmla_ref[:, lo:lo + LANES] = kv[:, hd * MLA_NOPE:(hd + 1) * MLA_NOPE].astype(BF16)
        kmla_ref[:, lo + LANES:lo + 2 * LANES] = kr
        vlo = MLA_HEADS * MLA_NOPE + hd * MLA_V
        _put_v(vmla_ref, hd, kv[:, vlo:vlo + MLA_V], tr)


def _mla_proj_kernel(x_ref, gmix_ref, wq_ref, gq_ref, wuq_ref, wkv_ref, gkv_ref, wukv_ref, c_ref, s1_ref, s2_ref,
                     h_ref, q_ref, ckv_ref, krope_ref, kmla_ref, vmla_ref, *, scale, tr):
    h = _rms(x_ref[...], gmix_ref[...]).astype(BF16)
    h_ref[...] = h
    c, s1, s2 = c_ref[...], s1_ref[...], s2_ref[...]
    _qproj_body(h, wq_ref, gq_ref, wuq_ref, c, s1, s2, q_ref, scale, tr)
    z = _dot(h, wkv_ref[...])
    ckv = _rms(z[:, :MLA_KV_LORA], gkv_ref[...])
    ckv_ref[...] = ckv
    t = _rope_tile(z[:, MLA_KV_LORA:], c, s1, s2, MLA_ROPE // 2)
    krope_ref[...] = t[:, :MLA_ROPE]
    _kv_expand(ckv, t, wukv_ref, kmla_ref, vmla_ref, tr)


def _mla_proj(x, gmix, wq, gq, wuq, wkv, gkv, wukv, tabs, bm, scale, tr):
    m, d = x.shape
    nt = tabs[0].shape[0] // bm
    full = lambda a: pl.BlockSpec(a.shape, lambda i: (0, 0))
    tab = pl.BlockSpec((bm, LANES), lambda i: (i % nt, 0))
    row = lambda n: pl.BlockSpec((bm, n), lambda i: (i, 0))
    nk = MLA_HEADS * MLA_QK_PAD
    qspec, qshape = _out(m, wuq.shape[1], bm, BF16, tr)
    vspec, vshape = _out(m, _v_width(MLA_HEADS, MLA_V, tr), bm, BF16, tr)
    return pl.pallas_call(
        functools.partial(_mla_proj_kernel, scale=scale, tr=tr),
        grid=(m // bm,),
        in_specs=[row(d), full(gmix), full(wq), full(gq), full(wuq), full(wkv), full(gkv), full(wukv), tab, tab, tab],
        out_specs=[row(d), qspec, row(MLA_KV_LORA), row(MLA_ROPE), row(nk), vspec],
        out_shape=[jax.ShapeDtypeStruct((m, d), BF16), qshape,
                   jax.ShapeDtypeStruct((m, MLA_KV_LORA), F32), jax.ShapeDtypeStruct((m, MLA_ROPE), F32),
                   jax.ShapeDtypeStruct((m, nk), BF16), vshape],
        compiler_params=_params("parallel"),
        name="norm_mla_proj",
    )(x, gmix, wq, gq, wuq, wkv, gkv, wukv, *tabs)


def _kvcache_kernel(ckv_ref, krope_ref, wukv_ref, kmla_ref, vmla_ref):
    _kv_expand(ckv_ref[...], krope_ref[...], wukv_ref, kmla_ref, vmla_ref, False)


def _kvcache_expand(ckv, krope_pad, wukv, bm):
    m = ckv.shape[0]
    row = lambda n: pl.BlockSpec((bm, n), lambda i: (i, 0))
    nk, nv = MLA_HEADS * MLA_QK_PAD, MLA_HEADS * MLA_V
    return pl.pallas_call(
        _kvcache_kernel,
        grid=(m // bm,),
        in_specs=[row(MLA_KV_LORA), row(LANES), pl.BlockSpec(wukv.shape, lambda i: (0, 0))],
        out_specs=[row(nk), row(nv)],
        out_shape=[jax.ShapeDtypeStruct((m, nk), BF16), jax.ShapeDtypeStruct((m, nv), BF16)],
        compiler_params=_params("parallel"),
        name="mla_kv_cache_expand",
    )(ckv, krope_pad, wukv)


def _diff_proj_kernel(h_ref, wdq_ref, wdk_ref, wdv_ref, c_ref, s1_ref, s2_ref,
                      q1_ref, q2_ref, dk_ref, dkb_ref, dv_ref, dvb_ref, *, scale, tr):
    h = h_ref[...]
    zq = _dot(h, wdq_ref[...])
    zk = _dot(h, wdk_ref[...])
    zv = _dot(h, wdv_ref[...])
    dv_ref[...] = zv
    c, s1, s2 = c_ref[...], s1_ref[...], s2_ref[...]
    first = lax.broadcasted_iota(jnp.int32, (1, LANES), 1) < DIFF_DH
    for hd in range(DIFF_HEADS):
        sl = slice(hd * LANES, (hd + 1) * LANES)
        q = _rope_tile(zq[:, sl], c, s1, s2, DIFF_ROT // 2) * scale
        _put(q1_ref, hd * LANES, jnp.where(first, q, 0.0), tr)
        _put(q2_ref, hd * LANES, jnp.where(first, 0.0, q), tr)
        k = _rope_tile(zk[:, sl], c, s1, s2, DIFF_ROT // 2)
        dk_ref[:, sl] = k
        dkb_ref[:, sl] = k.astype(BF16)
        _put_v(dvb_ref, hd, zv[:, sl], tr)


def _diff_proj(h, wdq, wdk, wdv, tabs, bm, scale, tr):
    m, d = h.shape
    nt = tabs[0].shape[0] // bm
    n = wdq.shape[1]
    full = lambda a: pl.BlockSpec(a.shape, lambda i: (0, 0))
    tab = pl.BlockSpec((bm, LANES), lambda i: (i % nt, 0))
    row = lambda w: pl.BlockSpec((bm, w), lambda i: (i, 0))
    qspec, qshape = _out(m, n, bm, BF16, tr)
    vspec, vshape = _out(m, _v_width(DIFF_HEADS, DIFF_VD, tr), bm, BF16, tr)
    f32_rows, bf16_rows = jax.ShapeDtypeStruct((m, n), F32), jax.ShapeDtypeStruct((m, n), BF16)
    return pl.pallas_call(
        functools.partial(_diff_proj_kernel, scale=scale, tr=tr),
        grid=(m // bm,),
        in_specs=[row(d), full(wdq), full(wdk), full(wdv), tab, tab, tab],
        out_specs=[qspec, qspec, row(n), row(n), row(n), vspec],
        out_shape=[qshape, qshape, f32_rows, bf16_rows, f32_rows, vshape],
        compiler_params=_params("parallel"),
        name="diff_proj",
    )(h, wdq, wdk, wdv, *tabs)


def _merge_kernel(h_ref, oa_ref, ob_ref, wga_ref, wgb_ref, wa_ref, wb_ref, o_ref):
    h = h_ref[...]
    ga = jax.nn.sigmoid(_dot(h, wga_ref[...]))
    gb = jax.nn.sigmoid(_dot(h, wgb_ref[...]))
    ya = _dot(oa_ref[...], wa_ref[...])
    yb = _dot(ob_ref[...], wb_ref[...])
    o_ref[...] = (ga * ya + gb * yb).astype(o_ref.dtype)


def _merge(h, oa, ob, wga, wgb, wa, wb, bm, bn):
    m, d = h.shape
    n = wga.shape[1]
    row = lambda a: pl.BlockSpec((bm, a.shape[1]), lambda i, j: (i, 0))
    col = lambda a: pl.BlockSpec((a.shape[0], bn), lambda i, j: (0, j))
    return pl.pallas_call(
        _merge_kernel,
        grid=(m // bm, n // bn),
        in_specs=[row(h), row(oa), row(ob), col(wga), col(wgb), col(wa), col(wb)],
        out_specs=pl.BlockSpec((bm, bn), lambda i, j: (i, j)),
        out_shape=jax.ShapeDtypeStruct((m, n), BF16),
        compiler_params=_params("parallel", "arbitrary"),
        name="gated_merge",
    )(h, oa, ob, wga, wgb, wa, wb)


def _outproj_kernel(x_ref, mg_ref, wo_ref, g_ref, x2_ref, h2_ref):
    x2 = x_ref[...] + _dot(mg_ref[...], wo_ref[...])
    x2_ref[...] = x2
    h2_ref[...] = _rms(x2, g_ref[...]).astype(BF16)


def _outproj(x, mg, wo, g, bm):
    m, d = x.shape
    row = pl.BlockSpec((bm, d), lambda i: (i, 0))
    return pl.pallas_call(
        _outproj_kernel,
        grid=(m // bm,),
        in_specs=[row, row, pl.BlockSpec(wo.shape, lambda i: (0, 0)), pl.BlockSpec((1, d), lambda i: (0, 0))],
        out_specs=[row, row],
        out_shape=[jax.ShapeDtypeStruct((m, d), F32), jax.ShapeDtypeStruct((m, d), BF16)],
        compiler_params=_params("parallel"),
        name="out_proj_residual",
    )(x, mg, wo, g)


def _ffn_kernel(h2_ref, wg_ref, wu_ref, wd_ref, x2_ref, gf_ref, y_ref):
    f = pl.program_id(1)

    @pl.when(f == 0)
    def _():
        y_ref[...] = x2_ref[...]

    h2 = h2_ref[...]
    a = (jax.nn.silu(_dot(h2, wg_ref[...])) * _dot(h2, wu_ref[...])).astype(BF16)
    y_ref[...] += _dot(a, wd_ref[...])

    @pl.when(f == pl.num_programs(1) - 1)
    def _():
        y_ref[...] = _rms(y_ref[...], gf_ref[...])


def _ffn(h2, w_in, w_out, x2, gf, bm, bf):
    m, d = h2.shape
    dff = w_out.shape[0]
    nf = dff // bf
    row = pl.BlockSpec((bm, d), lambda i, f: (i, 0))
    return pl.pallas_call(
        _ffn_kernel,
        grid=(m // bm, nf),
        in_specs=[row,
                  pl.BlockSpec((d, bf), lambda i, f: (0, f)),
                  pl.BlockSpec((d, bf), lambda i, f: (0, f + nf)),
                  pl.BlockSpec((bf, d), lambda i, f: (f, 0)),
                  row,
                  pl.BlockSpec((1, d), lambda i, f: (0, 0))],
        out_specs=row,
        out_shape=jax.ShapeDtypeStruct((m, d), F32),
        compiler_params=_params("parallel", "arbitrary"),
        name="ffn_swiglu_final_norm",
    )(h2, w_in, w_in, w_out, x2, gf)


def _softmax_step(s, v, m, l, acc):
    m_new = jnp.maximum(m, jnp.max(s, axis=-1, keepdims=True))
    alpha = jnp.exp2(m - m_new)
    p = jnp.exp2(s - m_new)
    l = alpha * l + jnp.sum(p, axis=-1, keepdims=True)
    acc = alpha * acc + _dot(p.astype(BF16), v)
    return m_new, l, acc


def _chunk_mask(tq, tk, q0, k0):
    qp = q0 + lax.broadcasted_iota(jnp.int32, (tq, tk), 0)
    kp = k0 + lax.broadcasted_iota(jnp.int32, (tq, tk), 1)
    return (kp // CHUNK) <= (qp // CHUNK)


def _softmax_init(tq, dv):
    return (jnp.full((tq, 1), NEG_INF, F32), jnp.zeros((tq, 1), F32), jnp.zeros((tq, dv), F32))


def _chunk_mask_t(t):
    kp = lax.broadcasted_iota(jnp.int32, (t, t), 0)
    qp = lax.broadcasted_iota(jnp.int32, (t, t), 1)
    return (kp // CHUNK) <= (qp // CHUNK)


def _probs_t(s, smax, m):
    m_new = jnp.maximum(m, smax)
    return jnp.exp2(s - m_new).astype(BF16), jnp.exp2(m - m_new), m_new


def _attention_scratch(t, dv, streams):
    return [pltpu.VMEM((streams, 2, t, t), F32),
            pltpu.VMEM((streams, dv + VT_ONES, t), F32),
            pltpu.VMEM((streams, 8, t), F32)]


def _causal_attention_t(n, scores, values, finish, s_ref, acc_ref, stat_ref, t, streams):
    row_m, row_smax = 0, 1

    def stat(st, r):
        return stat_ref[st, r:r + 1, :]

    def put_scores(st, par, s):
        s_ref[st, par] = s
        stat_ref[st, row_smax + par:row_smax + par + 1, :] = jnp.max(s, axis=0, keepdims=True)

    def accumulate(st, j, s, smax):
        p, alpha, m = _probs_t(s, smax, stat(st, row_m))
        acc = alpha * acc_ref[st] + _dot(values(j, st), p)
        stat_ref[st, row_m:row_m + 1, :] = m
        return acc

    def step(tau, par):
        for st in range(streams):
            put_scores(st, par, scores(tau, st))
            acc_ref[st] = accumulate(st, tau - 1, s_ref[st, 1 - par], stat(st, row_smax + 1 - par))

    for st in range(streams):
        put_scores(st, 0, scores(0, st))
        acc_ref[st] = jnp.zeros(acc_ref.shape[1:], F32)
        stat_ref[st, row_m:row_m + 1, :] = jnp.full((1, t), NEG_INF, F32)

    unroll = 4

    def trip(u, carry):
        for k in range(unroll):
            step(unroll * u + 1 + k, (1 + k) % 2)
        return carry

    lax.fori_loop(0, n // unroll, trip, 0)
    done = (n // unroll) * unroll

    @pl.when(n - done >= 2)
    def _():
        step(done + 1, 1)
        step(done + 2, 0)

    @pl.when(n % 2 == 1)
    def _():
        step(n, 1)

    def tail(par):
        mask = _chunk_mask_t(t)
        res = []
        for st in range(streams):
            s = jnp.where(mask, s_ref[st, par], NEG_INF)
            res.append(accumulate(st, n, s, jnp.max(s, axis=0, keepdims=True)))
        finish(res)

    for par in (0, 1):
        pl.when(n % 2 == par)(functools.partial(tail, par))


def _normalised(acc, dv):
    return (acc[:dv] / acc[dv:dv + 1]).T


def _mla_prompt_kernel(qt_ref, k_ref, vt_ref, o_ref, *scratch, t, heads):
    def scores(j, hd):
        k = k_ref[0, pl.ds(pl.multiple_of(j * t, t), t), hd * MLA_QK_PAD:(hd + 1) * MLA_QK_PAD]
        return _dot(k, qt_ref[0, hd * MLA_QK_PAD:(hd + 1) * MLA_QK_PAD, :])

    def values(j, hd):
        return vt_ref[j, hd * (MLA_V + VT_ONES):(hd + 1) * (MLA_V + VT_ONES), :]

    def finish(res):
        for hd, acc in enumerate(res):
            o_ref[0, :, hd * MLA_V:(hd + 1) * MLA_V] = _normalised(acc, MLA_V).astype(o_ref.dtype)

    _causal_attention_t(pl.program_id(2), scores, values, finish, *scratch, t, heads)


def _mla_prompt(qt, k, vt, t, heads):
    b, l, _ = k.shape
    nq = l // t
    return pl.pallas_call(
        functools.partial(_mla_prompt_kernel, t=t, heads=heads),
        grid=(b, MLA_HEADS // heads, nq),
        in_specs=[pl.BlockSpec((1, heads * MLA_QK_PAD, t), lambda b, h, i: (b * nq + i, h, 0)),
                  pl.BlockSpec((1, l, heads * MLA_QK_PAD), lambda b, h, i: (b, 0, h)),
                  pl.BlockSpec((nq, heads * (MLA_V + VT_ONES), t), lambda b, h, i: (b, h, 0))],
        out_specs=pl.BlockSpec((1, t, heads * MLA_V), lambda b, h, i: (b, i, h)),
        out_shape=jax.ShapeDtypeStruct((b, l, MLA_HEADS * MLA_V), BF16),
        scratch_shapes=_attention_scratch(t, MLA_V, heads),
        compiler_params=_params("parallel", "parallel", "arbitrary"),
        name="mla_attention_prompt",
    )(qt, k, vt)


def _lambda(lq1_ref, lk1_ref, lq2_ref, lk2_ref, lambda_init):
    a = jnp.sum(lq1_ref[...] * lk1_ref[...], axis=-1, keepdims=True)
    b = jnp.sum(lq2_ref[...] * lk2_ref[...], axis=-1, keepdims=True)
    return jnp.exp(a) - jnp.exp(b) + lambda_init


def _diff_finish(c1, c2, lam, g, lambda_init):
    o = c1[2] / c1[1] - lam * (c2[2] / c2[1])
    return _rms(o, g) * (1.0 - lambda_init)


def _diff_prompt_kernel(q1t_ref, q2t_ref, k_ref, vt_ref, lq1_ref, lk1_ref, lq2_ref, lk2_ref, g_ref, o_ref,
                        *scratch, t, lambda_init):
    qt_refs = (q1t_ref, q2t_ref)

    def scores(j, st):
        return _dot(k_ref[0, pl.ds(pl.multiple_of(j * t, t), t), :], qt_refs[st][0])

    def values(j, st):
        return vt_ref[j]

    def finish(res):
        lam = _lambda(lq1_ref, lk1_ref, lq2_ref, lk2_ref, lambda_init)
        o = _normalised(res[0], DIFF_VD) - lam * _normalised(res[1], DIFF_VD)
        o_ref[0] = (_rms(o, g_ref[...]) * (1.0 - lambda_init)).astype(o_ref.dtype)

    _causal_attention_t(pl.program_id(2), scores, values, finish, *scratch, t, 2)


def _diff_prompt(q1t, q2t, k, vt, lams, g, t, lambda_init):
    b, l, _ = k.shape
    nq = l // t
    qs = pl.BlockSpec((1, LANES, t), lambda b, h, i: (b * nq + i, h, 0))
    small = lambda a: pl.BlockSpec(a.shape, lambda b, h, i: (0, 0))
    return pl.pallas_call(
        functools.partial(_diff_prompt_kernel, t=t, lambda_init=lambda_init),
        grid=(b, DIFF_HEADS, nq),
        in_specs=[qs, qs,
                  pl.BlockSpec((1, l, LANES), lambda b, h, i: (b, 0, h)),
                  pl.BlockSpec((nq, DIFF_VD + VT_ONES, t), lambda b, h, i: (b, h, 0))]
                 + [small(a) for a in lams] + [small(g)],
        out_specs=pl.BlockSpec((1, t, DIFF_VD), lambda b, h, i: (b, i, h)),
        out_shape=jax.ShapeDtypeStruct((b, l, DIFF_HEADS * DIFF_VD), BF16),
        scratch_shapes=_attention_scratch(t, DIFF_VD, 2),
        compiler_params=_params("parallel", "parallel", "arbitrary"),
        name="diff_attention_prompt",
    )(q1t, q2t, k, vt, *lams, g)


def _two_part_softmax(q, kc, vc, kn, vn, mask_c, mask_n):
    tq = q.shape[0]
    carry = _softmax_init(tq, vc.shape[1])
    carry = _softmax_step(jnp.where(mask_c, _dot_nt(q, kc), NEG_INF), vc, *carry)
    return _softmax_step(jnp.where(mask_n, _dot_nt(q, kn), NEG_INF), vn, *carry)


def _mla_sample_kernel(q_ref, kc_ref, vc_ref, kn_ref, vn_ref, o_ref, *, past):
    tq, tc = q_ref.shape[1], kc_ref.shape[1]
    mask_c, mask_n = _chunk_mask(tq, tc, past, 0), _chunk_mask(tq, tq, past, past)
    for hd in range(MLA_HEADS):
        qk = slice(hd * MLA_QK_PAD, (hd + 1) * MLA_QK_PAD)
        v = slice(hd * MLA_V, (hd + 1) * MLA_V)
        _, l, acc = _two_part_softmax(q_ref[0, :, qk], kc_ref[0, :, qk], vc_ref[0, :, v], kn_ref[0, :, qk],
                                      vn_ref[0, :, v], mask_c, mask_n)
        o_ref[0, :, v] = (acc / l).astype(o_ref.dtype)


def _mla_sample(q, kc, vc, kn, vn):
    b, tq, _ = q.shape
    blk = lambda a: pl.BlockSpec((1,) + a.shape[1:], lambda b: (b, 0, 0))
    return pl.pallas_call(
        functools.partial(_mla_sample_kernel, past=kc.shape[1]),
        grid=(b,),
        in_specs=[blk(q), blk(kc), blk(vc), blk(kn), blk(vn)],
        out_specs=pl.BlockSpec((1, tq, MLA_HEADS * MLA_V), lambda b: (b, 0, 0)),
        out_shape=jax.ShapeDtypeStruct((b, tq, MLA_HEADS * MLA_V), BF16),
        compiler_params=_params("parallel"),
        name="mla_attention_sample",
    )(q, kc, vc, kn, vn)


def _diff_sample_kernel(q1_ref, q2_ref, kc_ref, vc_ref, kn_ref, vn_ref,
                        lq1_ref, lk1_ref, lq2_ref, lk2_ref, g_ref, o_ref, *, past, lambda_init):
    tq = q1_ref.shape[1]
    mask_c, mask_n = _chunk_mask(tq, past, past, 0), _chunk_mask(tq, tq, past, past)
    lam = _lambda(lq1_ref, lk1_ref, lq2_ref, lk2_ref, lambda_init)
    for hd in range(DIFF_HEADS):
        rows = pl.ds(hd, past, stride=DIFF_HEADS)
        cols = slice(hd * LANES, (hd + 1) * LANES)
        kc, vc = kc_ref[0, rows, :].astype(BF16), vc_ref[0, rows, :].astype(BF16)
        kn, vn = kn_ref[0, :, cols], vn_ref[0, :, cols]
        c1 = _two_part_softmax(q1_ref[0, :, cols], kc, vc, kn, vn, mask_c, mask_n)
        c2 = _two_part_softmax(q2_ref[0, :, cols], kc, vc, kn, vn, mask_c, mask_n)
        o_ref[0, :, cols] = _diff_finish(c1, c2, lam, g_ref[...], lambda_init).astype(o_ref.dtype)


def _diff_sample(q1, q2, kc, vc, kn, vn, lams, g, lambda_init, layer):
    b, tq, _ = q1.shape
    past = kc.shape[1] // DIFF_HEADS
    blk = lambda a: pl.BlockSpec((1,) + a.shape[1:], lambda b: (b, 0, 0))
    cache = pl.BlockSpec((1,) + kc.shape[1:], lambda b: (layer * q1.shape[0] + b, 0, 0))
    small = lambda a: pl.BlockSpec(a.shape, lambda b: (0, 0))
    return pl.pallas_call(
        functools.partial(_diff_sample_kernel, past=past, lambda_init=lambda_init),
        grid=(b,),
        in_specs=[blk(q1), blk(q2), cache, cache, blk(kn), blk(vn)] + [small(a) for a in lams] + [small(g)],
        out_specs=blk(q1),
        out_shape=jax.ShapeDtypeStruct((b, tq, DIFF_HEADS * DIFF_VD), BF16),
        compiler_params=_params("parallel"),
        name="diff_attention_sample",
    )(q1, q2, kc, vc, kn, vn, *lams, g)


def _rope_tables(pos, theta, rot_dim, period):
    half = rot_dim // 2
    inv = 1.0 / (jnp.float32(theta) ** (jnp.arange(half, dtype=F32) / half))
    ang = pos.astype(F32)[:, None] * inv[None, :]
    cos, sin = jnp.cos(ang), jnp.sin(ang)
    n = pos.shape[0]
    rest = period - rot_dim
    c = jnp.concatenate([cos, cos, jnp.ones((n, rest), F32)], axis=1)
    s1 = jnp.concatenate([-sin, jnp.zeros((n, half + rest), F32)], axis=1)
    s2 = jnp.concatenate([jnp.zeros((n, half), F32), sin, jnp.zeros((n, rest), F32)], axis=1)
    reps = LANES // period
    return tuple(jnp.tile(a, (1, reps)) for a in (c, s1, s2))


def _layer_weights(w_in, mla_w_uq, mla_w_ukv):
    c_kv = MLA_Q_LORA
    c_kr = c_kv + MLA_KV_LORA
    c_dq = c_kr + MLA_ROPE
    dqk = DIFF_HEADS * 2 * DIFF_DH
    c_dk = c_dq + dqk
    c_dv = c_dk + dqk
    c_ga = c_dv + DIFF_HEADS * DIFF_VD
    d = w_in.shape[0]
    c_gb = c_ga + d
    cols = lambda lo, hi: w_in[:, lo:hi].astype(BF16)
    wq = cols(0, c_kv)
    wkv = jnp.pad(cols(c_kv, c_dq), ((0, 0), (0, LANES - MLA_ROPE)))
    wdq, wdk, wdv = cols(c_dq, c_dk), cols(c_dk, c_dv), cols(c_dv, c_ga)
    wga, wgb = cols(c_ga, c_gb), cols(c_gb, w_in.shape[1])
    uq = mla_w_uq.astype(BF16).reshape(MLA_Q_LORA, MLA_HEADS, MLA_NOPE + MLA_ROPE)
    uq = jnp.pad(uq, ((0, 0), (0, 0), (0, MLA_QK_PAD - MLA_NOPE - MLA_ROPE))).reshape(MLA_Q_LORA, -1)
    ukv = mla_w_ukv.astype(BF16).reshape(MLA_KV_LORA, MLA_HEADS, MLA_NOPE + MLA_V)
    ukv = jnp.concatenate([ukv[:, :, :MLA_NOPE].reshape(MLA_KV_LORA, -1),
                           ukv[:, :, MLA_NOPE:].reshape(MLA_KV_LORA, -1)], axis=1)
    return wq, wkv, wdq, wdk, wdv, wga, wgb, uq, ukv


def _block(m, want):
    return want if m % want == 0 else m


def kernel(x_prompt, x_sample, cache_mla_ckv, cache_mla_krope, cache_diff_k, cache_diff_v, norm_mix, w_in,
           mla_q_norm, mla_w_uq, mla_kv_norm, mla_w_ukv, diff_lq1, diff_lk1, diff_lq2, diff_lk2, diff_subln,
           w_branch_a, w_branch_b, w_out, norm_ffn, w_ffn_in, w_ffn_out, norm_final):
    bp, lp, d = x_prompt.shape
    bs, ls, _ = x_sample.shape
    depth, _, past, _ = cache_mla_ckv.shape
    assert depth == 1, "the FFN kernel fuses the final norm, so it serves the last (only) layer"
    mp, ms = bp * lp, bs * ls
    t_attn = 512

    pos_p = jnp.arange(lp)
    pos_s = past + jnp.arange(ls)
    tabs_mla_p = _rope_tables(pos_p, MLA_THETA, MLA_ROPE, LANES)
    tabs_mla_s = tuple(jnp.tile(a, (bs, 1)) for a in _rope_tables(pos_s, MLA_THETA, MLA_ROPE, LANES))
    tabs_dif_p = _rope_tables(pos_p, ROPE_THETA, DIFF_ROT, DIFF_DH)
    tabs_dif_s = tuple(jnp.tile(a, (bs, 1)) for a in _rope_tables(pos_s, ROPE_THETA, DIFF_ROT, DIFF_DH))
    q_scale_mla = (MLA_NOPE + MLA_ROPE) ** -0.5 * LOG2E
    q_scale_dif = DIFF_DH ** -0.5 * LOG2E
    gfinal = norm_final.reshape(1, d)

    xp = x_prompt.reshape(mp, d)
    xs = x_sample.reshape(ms, d)
    rows_p, rows_s = [], []
    for l in range(depth):
        lambda_init = 0.8 - 0.6 * math.exp(-0.3 * l)
        wq, wkv, wdq, wdk, wdv, wga, wgb, uq, ukv = _layer_weights(w_in[l], mla_w_uq[l], mla_w_ukv[l])
        gmix, gq, gkv = norm_mix[l].reshape(1, -1), mla_q_norm[l].reshape(1, -1), mla_kv_norm[l].reshape(1, -1)
        lams = tuple(a[l].reshape(1, -1) for a in (diff_lq1, diff_lk1, diff_lq2, diff_lk2))
        gsub = diff_subln[l].reshape(1, -1)
        wa, wbr, wo = w_branch_a[l].astype(BF16), w_branch_b[l].astype(BF16), w_out[l].astype(BF16)
        wfi, wfo = w_ffn_in[l].astype(BF16), w_ffn_out[l].astype(BF16)
        gffn = norm_ffn[l].reshape(1, -1)

        def stage1(x, bm, tabs_mla, tabs_dif, tr):
            h, q, ckv, krope, kmla, vmla = _mla_proj(x, gmix, wq, gq, uq, wkv, gkv, ukv, tabs_mla, bm,
                                                     q_scale_mla, tr)
            q1, q2, dk, dkb, dv, dvb = _diff_proj(h, wdq, wdk, wdv, tabs_dif, bm, q_scale_dif, tr)
            return h, q, ckv, krope, kmla, vmla, q1, q2, dk, dkb, dv, dvb

        def stage2(x, h, oa, ob, bm, bn, bf):
            mg = _merge(h, oa, ob, wga, wgb, wa, wbr, bm, bn)
            x2, h2 = _outproj(x, mg, wo, gffn, min(bm, 512))
            return _ffn(h2, wfi, wfo, x2, gfinal, bm, bf)

        h, q, ckv, krope, kmla, vmla, q1, q2, dk, dkb, dv, dvb = stage1(xp, t_attn, tabs_mla_p, tabs_dif_p, True)
        r3 = lambda a: a.reshape(bp, lp, -1)
        oa = _mla_prompt(q, r3(kmla), vmla, t_attn, 2)
        ob = _diff_prompt(q1, q2, r3(dkb), dvb, lams, gsub, t_attn, lambda_init)
        xp = stage2(xp, h, oa.reshape(mp, -1), ob.reshape(mp, -1), _block(mp, 1024), 512, 256)
        rows_p.append((ckv.reshape(bp, lp, -1), krope.reshape(bp, lp, -1),
                       dk.reshape(bp, lp, DIFF_HEADS, -1), dv.reshape(bp, lp, DIFF_HEADS, -1)))

        h, q, ckv, krope, kmla, vmla, q1, q2, dk, dkb, dv, dvb = stage1(xs, ms, tabs_mla_s, tabs_dif_s, False)
        kr_pad = jnp.pad(cache_mla_krope[l].reshape(bs * past, MLA_ROPE), ((0, 0), (0, LANES - MLA_ROPE)))
        kc, vc = _kvcache_expand(cache_mla_ckv[l].reshape(bs * past, -1), kr_pad, ukv, _block(bs * past, 512))
        r3 = lambda a: a.reshape(bs, ls, -1)
        c3 = lambda a: a.reshape(bs, past, -1)
        oa = _mla_sample(r3(q), c3(kc), c3(vc), r3(kmla), r3(vmla))
        call = lambda a: a.reshape(depth * bs, past * DIFF_HEADS, -1)
        ob = _diff_sample(r3(q1), r3(q2), call(cache_diff_k), call(cache_diff_v), r3(dkb), r3(dvb),
                          lams, gsub, lambda_init, l)
        xs = stage2(xs, h, oa.reshape(ms, -1), ob.reshape(ms, -1), ms, 512, 512)
        rows_s.append((ckv.reshape(bs, ls, -1), krope.reshape(bs, ls, -1),
                       dk.reshape(bs, ls, DIFF_HEADS, -1), dv.reshape(bs, ls, DIFF_HEADS, -1)))

    y_prompt = xp.reshape(bp, lp, d)
    y_sample = xs.reshape(bs, ls, d)
    stack = lambda rows, i: jnp.stack([r[i] for r in rows], axis=0)
    return (y_prompt, y_sample,
            stack(rows_p, 0), stack(rows_p, 1), stack(rows_p, 2), stack(rows_p, 3),
            stack(rows_s, 0), stack(rows_s, 1), stack(rows_s, 2), stack(rows_s, 3))
```

```python
import functools
import math

import jax
import jax.numpy as jnp
from jax import lax
from jax.experimental import pallas as pl
from jax.experimental.pallas import tpu as pltpu

F32 = jnp.float32
BF16 = jnp.bfloat16

CHUNK = 64
EPS = 1e-6
NEG_INF = -1e30

MLA_HEADS = 8
MLA_Q_LORA = 512
MLA_KV_LORA = 256
MLA_NOPE = 128
MLA_ROPE = 64
MLA_V = 128
MLA_THETA = 10000.0
MLA_QK_PAD = 256

DIFF_HEADS = 8
DIFF_DH = 64
DIFF_VD = 2 * DIFF_DH
DIFF_ROT = DIFF_DH // 4
ROPE_THETA = 500000.0

LANES = 128
VT_ONES = 16
LOG2E = math.log2(math.e)
VMEM_LIMIT = 56 * 1024 * 1024


def _params(*sem):
    return pltpu.CompilerParams(dimension_semantics=sem, vmem_limit_bytes=VMEM_LIMIT)


def _dot(a, b):
    return jnp.dot(a, b, preferred_element_type=F32)


def _dot_nt(a, b):
    return lax.dot_general(a, b, (((1,), (1,)), ((), ())), preferred_element_type=F32)


def _rms(x, g):
    return x * lax.rsqrt(jnp.mean(x * x, axis=-1, keepdims=True) + EPS) * g


def _rope_tile(t, c, s1, s2, half):
    return t * c + pltpu.roll(t, LANES - half, 1) * s1 + pltpu.roll(t, half, 1) * s2


def _put(ref, lo, val, tr):
    w = val.shape[1]
    if tr:
        ref[0, lo:lo + w, :] = val.T.astype(ref.dtype)
    else:
        ref[:, lo:lo + w] = val.astype(ref.dtype)


def _put_v(ref, hd, val, tr):
    bm, w = val.shape
    if tr:
        lo = hd * (w + VT_ONES)
        ref[0, lo:lo + w, :] = val.T.astype(ref.dtype)
        ref[0, lo + w:lo + w + VT_ONES, :] = jnp.ones((VT_ONES, bm), ref.dtype)
    else:
        ref[:, hd * w:(hd + 1) * w] = val.astype(ref.dtype)


def _v_width(heads, dv, tr):
    return heads * (dv + VT_ONES) if tr else heads * dv


def _out(m, n, bm, dtype, tr):
    if tr:
        return pl.BlockSpec((1, n, bm), lambda i: (i, 0, 0)), jax.ShapeDtypeStruct((m // bm, n, bm), dtype)
    return pl.BlockSpec((bm, n), lambda i: (i, 0)), jax.ShapeDtypeStruct((m, n), dtype)


def _qproj_body(h, wq_ref, gq_ref, wuq_ref, c, s1, s2, o_ref, scale, tr):
    qlat = _rms(_dot(h, wq_ref[...]), gq_ref[...]).astype(BF16)
    q = _dot(qlat, wuq_ref[...])
    for hd in range(MLA_HEADS):
        lo = hd * MLA_QK_PAD
        _put(o_ref, lo, q[:, lo:lo + LANES] * scale, tr)
        t = _rope_tile(q[:, lo + LANES:lo + 2 * LANES], c, s1, s2, MLA_ROPE // 2)
        _put(o_ref, lo + LANES, t * scale, tr)


def _kv_expand(ckv, krope_tile, wukv_ref, kmla_ref, vmla_ref, tr):
    kv = _dot(ckv.astype(BF16), wukv_ref[...])
    kr = krope_tile.astype(BF16)
    for hd in range(MLA_HEADS):
        lo = hd * MLA_QK_PAD
        kmla_ref[:, lo:lo + LANES] = kv[:, hd * MLA_NOPE:(hd + 1) * MLA_NOPE].astype(BF16)
        kmla_ref[:, lo + LANES:lo + 2 * LANES] = kr
        vlo = MLA_HEADS * MLA_NOPE + hd * MLA_V
        _put_v(vmla_ref, hd, kv[:, vlo:vlo + MLA_V], tr)


def _mla_proj_kernel(x_ref, gmix_ref, wq_ref, gq_ref, wuq_ref, wkv_ref, gkv_ref, wukv_ref, c_ref, s1_ref, s2_ref,
                     h_ref, q_ref, ckv_ref, krope_ref, kmla_ref, vmla_ref, *, scale, tr):
    h = _rms(x_ref[...], gmix_ref[...]).astype(BF16)
    h_ref[...] = h
    c, s1, s2 = c_ref[...], s1_ref[...], s2_ref[...]
    _qproj_body(h, wq_ref, gq_ref, wuq_ref, c, s1, s2, q_ref, scale, tr)
    z = _dot(h, wkv_ref[...])
    ckv = _rms(z[:, :MLA_KV_LORA], gkv_ref[...])
    ckv_ref[...] = ckv
    t = _rope_tile(z[:, MLA_KV_LORA:], c, s1, s2, MLA_ROPE // 2)
    krope_ref[...] = t[:, :MLA_ROPE]
    _kv_expand(ckv, t, wukv_ref, kmla_ref, vmla_ref, tr)


def _mla_proj(x, gmix, wq, gq, wuq, wkv, gkv, wukv, tabs, bm, scale, tr):
    m, d = x.shape
    nt = tabs[0].shape[0] // bm
    full = lambda a: pl.BlockSpec(a.shape, lambda i: (0, 0))
    tab = pl.BlockSpec((bm, LANES), lambda i: (i % nt, 0))
    row = lambda n: pl.BlockSpec((bm, n), lambda i: (i, 0))
    nk = MLA_HEADS * MLA_QK_PAD
    qspec, qshape = _out(m, wuq.shape[1], bm, BF16, tr)
    vspec, vshape = _out(m, _v_width(MLA_HEADS, MLA_V, tr), bm, BF16, tr)
    return pl.pallas_call(
        functools.partial(_mla_proj_kernel, scale=scale, tr=tr),
        grid=(m // bm,),
        in_specs=[row(d), full(gmix), full(wq), full(gq), full(wuq), full(wkv), full(gkv), full(wukv), tab, tab, tab],
        out_specs=[row(d), qspec, row(MLA_KV_LORA), row(MLA_ROPE), row(nk), vspec],
        out_shape=[jax.ShapeDtypeStruct((m, d), BF16), qshape,
                   jax.ShapeDtypeStruct((m, MLA_KV_LORA), F32), jax.ShapeDtypeStruct((m, MLA_ROPE), F32),
                   jax.ShapeDtypeStruct((m, nk), BF16), vshape],
        compiler_params=_params("parallel"),
        name="norm_mla_proj",
    )(x, gmix, wq, gq, wuq, wkv, gkv, wukv, *tabs)


def _kvcache_kernel(ckv_ref, krope_ref, wukv_ref, kmla_ref, vmla_ref):
    _kv_expand(ckv_ref[...], krope_ref[...], wukv_ref, kmla_ref, vmla_ref, False)


def _kvcache_expand(ckv, krope_pad, wukv, bm):
    m = ckv.shape[0]
    row = lambda n: pl.BlockSpec((bm, n), lambda i: (i, 0))
    nk, nv = MLA_HEADS * MLA_QK_PAD, MLA_HEADS * MLA_V
    return pl.pallas_call(
        _kvcache_kernel,
        grid=(m // bm,),
        in_specs=[row(MLA_KV_LORA), row(LANES), pl.BlockSpec(wukv.shape, lambda i: (0, 0))],
        out_specs=[row(nk), row(nv)],
        out_shape=[jax.ShapeDtypeStruct((m, nk), BF16), jax.ShapeDtypeStruct((m, nv), BF16)],
        compiler_params=_params("parallel"),
        name="mla_kv_cache_expand",
    )(ckv, krope_pad, wukv)


def _diff_proj_kernel(h_ref, wdq_ref, wdk_ref, wdv_ref, c_ref, s1_ref, s2_ref,
                      q1_ref, q2_ref, dk_ref, dkb_ref, dv_ref, dvb_ref, *, scale, tr):
    h = h_ref[...]
    zq = _dot(h, wdq_ref[...])
    zk = _dot(h, wdk_ref[...])
    zv = _dot(h, wdv_ref[...])
    dv_ref[...] = zv
    c, s1, s2 = c_ref[...], s1_ref[...], s2_ref[...]
    first = lax.broadcasted_iota(jnp.int32, (1, LANES), 1) < DIFF_DH
    for hd in range(DIFF_HEADS):
        sl = slice(hd * LANES, (hd + 1) * LANES)
        q = _rope_tile(zq[:, sl], c, s1, s2, DIFF_ROT // 2) * scale
        _put(q1_ref, hd * LANES, jnp.where(first, q, 0.0), tr)
        _put(q2_ref, hd * LANES, jnp.where(first, 0.0, q), tr)
        k = _rope_tile(zk[:, sl], c, s1, s2, DIFF_ROT // 2)
        dk_ref[:, sl] = k
        dkb_ref[:, sl] = k.astype(BF16)
        _put_v(dvb_ref, hd, zv[:, sl], tr)


def _diff_proj(h, wdq, wdk, wdv, tabs, bm, scale, tr):
    m, d = h.shape
    nt = tabs[0].shape[0] // bm
    n = wdq.shape[1]
    full = lambda a: pl.BlockSpec(a.shape, lambda i: (0, 0))
    tab = pl.BlockSpec((bm, LANES), lambda i: (i % nt, 0))
    row = lambda w: pl.BlockSpec((bm, w), lambda i: (i, 0))
    qspec, qshape = _out(m, n, bm, BF16, tr)
    vspec, vshape = _out(m, _v_width(DIFF_HEADS, DIFF_VD, tr), bm, BF16, tr)
    f32_rows, bf16_rows = jax.ShapeDtypeStruct((m, n), F32), jax.ShapeDtypeStruct((m, n), BF16)
    return pl.pallas_call(
        functools.partial(_diff_proj_kernel, scale=scale, tr=tr),
        grid=(m // bm,),
        in_specs=[row(d), full(wdq), full(wdk), full(wdv), tab, tab, tab],
        out_specs=[qspec, qspec, row(n), row(n), row(n), vspec],
        out_shape=[qshape, qshape, f32_rows, bf16_rows, f32_rows, vshape],
        compiler_params=_params("parallel"),
        name="diff_proj",
    )(h, wdq, wdk, wdv, *tabs)


def _merge_kernel(h_ref, oa_ref, ob_ref, wga_ref, wgb_ref, wa_ref, wb_ref, o_ref):
    h = h_ref[...]
    ga = jax.nn.sigmoid(_dot(h, wga_ref[...]))
    gb = jax.nn.sigmoid(_dot(h, wgb_ref[...]))
    ya = _dot(oa_ref[...], wa_ref[...])
    yb = _dot(ob_ref[...], wb_ref[...])
    o_ref[...] = (ga * ya + gb * yb).astype(o_ref.dtype)


def _merge(h, oa, ob, wga, wgb, wa, wb, bm, bn):
    m, d = h.shape
    n = wga.shape[1]
    row = lambda a: pl.BlockSpec((bm, a.shape[1]), lambda i, j: (i, 0))
    col = lambda a: pl.BlockSpec((a.shape[0], bn), lambda i, j: (0, j))
    return pl.pallas_call(
        _merge_kernel,
        grid=(m // bm, n // bn),
        in_specs=[row(h), row(oa), row(ob), col(wga), col(wgb), col(wa), col(wb)],
        out_specs=pl.BlockSpec((bm, bn), lambda i, j: (i, j)),
        out_shape=jax.ShapeDtypeStruct((m, n), BF16),
        compiler_params=_params("parallel", "arbitrary"),
        name="gated_merge",
    )(h, oa, ob, wga, wgb, wa, wb)


def _outproj_kernel(x_ref, mg_ref, wo_ref, g_ref, x2_ref, h2_ref):
    x2 = x_ref[...] + _dot(mg_ref[...], wo_ref[...])
    x2_ref[...] = x2
    h2_ref[...] = _rms(x2, g_ref[...]).astype(BF16)


def _outproj(x, mg, wo, g, bm):
    m, d = x.shape
    row = pl.BlockSpec((bm, d), lambda i: (i, 0))
    return pl.pallas_call(
        _outproj_kernel,
        grid=(m // bm,),
        in_specs=[row, row, pl.BlockSpec(wo.shape, lambda i: (0, 0)), pl.BlockSpec((1, d), lambda i: (0, 0))],
        out_specs=[row, row],
        out_shape=[jax.ShapeDtypeStruct((m, d), F32), jax.ShapeDtypeStruct((m, d), BF16)],
        compiler_params=_params("parallel"),
        name="out_proj_residual",
    )(x, mg, wo, g)


def _ffn_kernel(h2_ref, wg_ref, wu_ref, wd_ref, x2_ref, gf_ref, y_ref):
    f = pl.program_id(1)

    @pl.when(f == 0)
    def _():
        y_ref[...] = x2_ref[...]

    h2 = h2_ref[...]
    a = (jax.nn.silu(_dot(h2, wg_ref[...])) * _dot(h2, wu_ref[...])).astype(BF16)
    y_ref[...] += _dot(a, wd_ref[...])

    @pl.when(f == pl.num_programs(1) - 1)
    def _():
        y_ref[...] = _rms(y_ref[...], gf_ref[...])


def _ffn(h2, w_in, w_out, x2, gf, bm, bf):
    m, d = h2.shape
    dff = w_out.shape[0]
    nf = dff // bf
    row = pl.BlockSpec((bm, d), lambda i, f: (i, 0))
    return pl.pallas_call(
        _ffn_kernel,
        grid=(m // bm, nf),
        in_specs=[row,
                  pl.BlockSpec((d, bf), lambda i, f: (0, f)),
                  pl.BlockSpec((d, bf), lambda i, f: (0, f + nf)),
                  pl.BlockSpec((bf, d), lambda i, f: (f, 0)),
                  row,
                  pl.BlockSpec((1, d), lambda i, f: (0, 0))],
        out_specs=row,
        out_shape=jax.ShapeDtypeStruct((m, d), F32),
        compiler_params=_params("parallel", "arbitrary"),
        name="ffn_swiglu_final_norm",
    )(h2, w_in, w_in, w_out, x2, gf)


def _softmax_step(s, v, m, l, acc):
    m_new = jnp.maximum(m, jnp.max(s, axis=-1, keepdims=True))
    alpha = jnp.exp2(m - m_new)
    p = jnp.exp2(s - m_new)
    l = alpha * l + jnp.sum(p, axis=-1, keepdims=True)
    acc = alpha * acc + _dot(p.astype(BF16), v)
    return m_new, l, acc


def _chunk_mask(tq, tk, q0, k0):
    qp = q0 + lax.broadcasted_iota(jnp.int32, (tq, tk), 0)
    kp = k0 + lax.broadcasted_iota(jnp.int32, (tq, tk), 1)
    return (kp // CHUNK) <= (qp // CHUNK)


def _softmax_init(tq, dv):
    return (jnp.full((tq, 1), NEG_INF, F32), jnp.zeros((tq, 1), F32), jnp.zeros((tq, dv), F32))


def _chunk_mask_t(t):
    kp = lax.broadcasted_iota(jnp.int32, (t, t), 0)
    qp = lax.broadcasted_iota(jnp.int32, (t, t), 1)
    return (kp // CHUNK) <= (qp // CHUNK)


def _probs_t(s, smax, m):
    m_new = jnp.maximum(m, smax)
    return jnp.exp2(s - m_new).astype(BF16), jnp.exp2(m - m_new), m_new


def _attention_scratch(t, dv, streams):
    return [pltpu.VMEM((streams, 2, t, t), F32),
            pltpu.VMEM((streams, dv + VT_ONES, t), F32),
            pltpu.VMEM((streams, 8, t), F32)]


def _causal_attention_t(n, scores, next_scores, values, finish, s_ref, acc_ref, stat_ref, t, streams):
    row_m, row_smax = 0, 1

    def stat(st, r):
        return stat_ref[st, r:r + 1, :]

    def put_scores(st, slot, s):
        s_ref[st, slot] = s
        stat_ref[st, row_smax + slot:row_smax + slot + 1, :] = jnp.max(s, axis=0, keepdims=True)

    def accumulate(st, j, s, smax):
        p, alpha, m = _probs_t(s, smax, stat(st, row_m))
        acc = alpha * acc_ref[st] + _dot(values(j, st), p)
        stat_ref[st, row_m:row_m + 1, :] = m
        return acc

    @pl.when(n == 0)
    def _():
        for st in range(streams):
            put_scores(st, 0, scores(0, st))

    for st in range(streams):
        acc_ref[st] = jnp.zeros(acc_ref.shape[1:], F32)
        stat_ref[st, row_m:row_m + 1, :] = jnp.full((1, t), NEG_INF, F32)

    def run(first):
        def step(tau, par):
            for st in range(streams):
                put_scores(st, (first + par) % 2, scores(tau, st))
                prev = (first + par + 1) % 2
                acc_ref[st] = accumulate(st, tau - 1, s_ref[st, prev], stat(st, row_smax + prev))

        unroll = 4

        def trip(u, carry):
            for k in range(unroll):
                step(unroll * u + 1 + k, (1 + k) % 2)
            return carry

        lax.fori_loop(0, n // unroll, trip, 0)
        done = (n // unroll) * unroll

        @pl.when(n - done >= 2)
        def _():
            step(done + 1, 1)
            step(done + 2, 0)

        @pl.when(n % 2 == 1)
        def _():
            step(n, 1)

        def tail(par):
            diag = (first + par) % 2
            for st in range(streams):
                put_scores(st, 1 - diag, next_scores(st))
            mask = _chunk_mask_t(t)
            res = []
            for st in range(streams):
                s = jnp.where(mask, s_ref[st, diag], NEG_INF)
                res.append(accumulate(st, n, s, jnp.max(s, axis=0, keepdims=True)))
            finish(res)

        for par in (0, 1):
            pl.when(n % 2 == par)(functools.partial(tail, par))

    first = ((n + 1) // 2) % 2
    for f in (0, 1):
        pl.when(first == f)(functools.partial(run, f))


def _normalised(acc, dv):
    return (acc[:dv] / acc[dv:dv + 1]).T


def _mla_prompt_kernel(qt_ref, qn_ref, k_ref, vt_ref, o_ref, *scratch, t, heads):
    def block_scores(q_ref, j, hd):
        k = k_ref[0, pl.ds(pl.multiple_of(j * t, t), t), hd * MLA_QK_PAD:(hd + 1) * MLA_QK_PAD]
        return _dot(k, q_ref[0, hd * MLA_QK_PAD:(hd + 1) * MLA_QK_PAD, :])

    scores = functools.partial(block_scores, qt_ref)
    next_scores = functools.partial(block_scores, qn_ref, 0)

    def values(j, hd):
        return vt_ref[j, hd * (MLA_V + VT_ONES):(hd + 1) * (MLA_V + VT_ONES), :]

    def finish(res):
        for hd, acc in enumerate(res):
            o_ref[0, :, hd * MLA_V:(hd + 1) * MLA_V] = _normalised(acc, MLA_V).astype(o_ref.dtype)

    _causal_attention_t(pl.program_id(2), scores, next_scores, values, finish, *scratch, t, heads)


def _mla_prompt(qt, k, vt, t, heads):
    b, l, _ = k.shape
    nq = l // t
    qblock = lambda step: pl.BlockSpec((1, heads * MLA_QK_PAD, t),
                                       lambda b, h, i: (b * nq + jnp.minimum(i + step, nq - 1), h, 0))
    return pl.pallas_call(
        functools.partial(_mla_prompt_kernel, t=t, heads=heads),
        grid=(b, MLA_HEADS // heads, nq),
        in_specs=[qblock(0), qblock(1),
                  pl.BlockSpec((1, l, heads * MLA_QK_PAD), lambda b, h, i: (b, 0, h)),
                  pl.BlockSpec((nq, heads * (MLA_V + VT_ONES), t), lambda b, h, i: (b, h, 0))],
        out_specs=pl.BlockSpec((1, t, heads * MLA_V), lambda b, h, i: (b, i, h)),
        out_shape=jax.ShapeDtypeStruct((b, l, MLA_HEADS * MLA_V), BF16),
        scratch_shapes=_attention_scratch(t, MLA_V, heads),
        compiler_params=_params("parallel", "parallel", "arbitrary"),
        name="mla_attention_prompt",
    )(qt, qt, k, vt)


def _lambda(lq1_ref, lk1_ref, lq2_ref, lk2_ref, lambda_init):
    a = jnp.sum(lq1_ref[...] * lk1_ref[...], axis=-1, keepdims=True)
    b = jnp.sum(lq2_ref[...] * lk2_ref[...], axis=-1, keepdims=True)
    return jnp.exp(a) - jnp.exp(b) + lambda_init


def _diff_finish(c1, c2, lam, g, lambda_init):
    o = c1[2] / c1[1] - lam * (c2[2] / c2[1])
    return _rms(o, g) * (1.0 - lambda_init)


def _diff_prompt_kernel(q1t_ref, q2t_ref, q1n_ref, q2n_ref, k_ref, vt_ref, lq1_ref, lk1_ref, lq2_ref, lk2_ref,
                        g_ref, o_ref, *scratch, t, lambda_init):
    qt_refs, qn_refs = (q1t_ref, q2t_ref), (q1n_ref, q2n_ref)

    def scores(j, st):
        return _dot(k_ref[0, pl.ds(pl.multiple_of(j * t, t), t), :], qt_refs[st][0])

    def next_scores(st):
        return _dot(k_ref[0, 0:t, :], qn_refs[st][0])

    def values(j, st):
        return vt_ref[j]

    def finish(res):
        lam = _lambda(lq1_ref, lk1_ref, lq2_ref, lk2_ref, lambda_init)
        o = _normalised(res[0], DIFF_VD) - lam * _normalised(res[1], DIFF_VD)
        o_ref[0] = (_rms(o, g_ref[...]) * (1.0 - lambda_init)).astype(o_ref.dtype)

    _causal_attention_t(pl.program_id(2), scores, next_scores, values, finish, *scratch, t, 2)


def _diff_prompt(q1t, q2t, k, vt, lams, g, t, lambda_init):
    b, l, _ = k.shape
    nq = l // t
    qs = pl.BlockSpec((1, LANES, t), lambda b, h, i: (b * nq + i, h, 0))
    qn = pl.BlockSpec((1, LANES, t), lambda b, h, i: (b * nq + jnp.minimum(i + 1, nq - 1), h, 0))
    small = lambda a: pl.BlockSpec(a.shape, lambda b, h, i: (0, 0))
    return pl.pallas_call(
        functools.partial(_diff_prompt_kernel, t=t, lambda_init=lambda_init),
        grid=(b, DIFF_HEADS, nq),
        in_specs=[qs, qs, qn, qn,
                  pl.BlockSpec((1, l, LANES), lambda b, h, i: (b, 0, h)),
                  pl.BlockSpec((nq, DIFF_VD + VT_ONES, t), lambda b, h, i: (b, h, 0))]
                 + [small(a) for a in lams] + [small(g)],
        out_specs=pl.BlockSpec((1, t, DIFF_VD), lambda b, h, i: (b, i, h)),
        out_shape=jax.ShapeDtypeStruct((b, l, DIFF_HEADS * DIFF_VD), BF16),
        scratch_shapes=_attention_scratch(t, DIFF_VD, 2),
        compiler_params=_params("parallel", "parallel", "arbitrary"),
        name="diff_attention_prompt",
    )(q1t, q2t, q1t, q2t, k, vt, *lams, g)


def _two_part_softmax(q, kc, vc, kn, vn, mask_c, mask_n):
    tq = q.shape[0]
    carry = _softmax_init(tq, vc.shape[1])
    carry = _softmax_step(jnp.where(mask_c, _dot_nt(q, kc), NEG_INF), vc, *carry)
    return _softmax_step(jnp.where(mask_n, _dot_nt(q, kn), NEG_INF), vn, *carry)


def _mla_sample_kernel(q_ref, kc_ref, vc_ref, kn_ref, vn_ref, o_ref, *, past):
    tq, tc = q_ref.shape[1], kc_ref.shape[1]
    mask_c, mask_n = _chunk_mask(tq, tc, past, 0), _chunk_mask(tq, tq, past, past)
    for hd in range(MLA_HEADS):
        qk = slice(hd * MLA_QK_PAD, (hd + 1) * MLA_QK_PAD)
        v = slice(hd * MLA_V, (hd + 1) * MLA_V)
        _, l, acc = _two_part_softmax(q_ref[0, :, qk], kc_ref[0, :, qk], vc_ref[0, :, v], kn_ref[0, :, qk],
                                      vn_ref[0, :, v], mask_c, mask_n)
        o_ref[0, :, v] = (acc / l).astype(o_ref.dtype)


def _mla_sample(q, kc, vc, kn, vn):
    b, tq, _ = q.shape
    blk = lambda a: pl.BlockSpec((1,) + a.shape[1:], lambda b: (b, 0, 0))
    return pl.pallas_call(
        functools.partial(_mla_sample_kernel, past=kc.shape[1]),
        grid=(b,),
        in_specs=[blk(q), blk(kc), blk(vc), blk(kn), blk(vn)],
        out_specs=pl.BlockSpec((1, tq, MLA_HEADS * MLA_V), lambda b: (b, 0, 0)),
        out_shape=jax.ShapeDtypeStruct((b, tq, MLA_HEADS * MLA_V), BF16),
        compiler_params=_params("parallel"),
        name="mla_attention_sample",
    )(q, kc, vc, kn, vn)


def _diff_sample_kernel(q1_ref, q2_ref, kc_ref, vc_ref, kn_ref, vn_ref,
                        lq1_ref, lk1_ref, lq2_ref, lk2_ref, g_ref, o_ref, *, past, lambda_init):
    tq = q1_ref.shape[1]
    mask_c, mask_n = _chunk_mask(tq, past, past, 0), _chunk_mask(tq, tq, past, past)
    lam = _lambda(lq1_ref, lk1_ref, lq2_ref, lk2_ref, lambda_init)
    for hd in range(DIFF_HEADS):
        rows = pl.ds(hd, past, stride=DIFF_HEADS)
        cols = slice(hd * LANES, (hd + 1) * LANES)
        kc, vc = kc_ref[0, rows, :].astype(BF16), vc_ref[0, rows, :].astype(BF16)
        kn, vn = kn_ref[0, :, cols], vn_ref[0, :, cols]
        c1 = _two_part_softmax(q1_ref[0, :, cols], kc, vc, kn, vn, mask_c, mask_n)
        c2 = _two_part_softmax(q2_ref[0, :, cols], kc, vc, kn, vn, mask_c, mask_n)
        o_ref[0, :, cols] = _diff_finish(c1, c2, lam, g_ref[...], lambda_init).astype(o_ref.dtype)


def _diff_sample(q1, q2, kc, vc, kn, vn, lams, g, lambda_init, layer):
    b, tq, _ = q1.shape
    past = kc.shape[1] // DIFF_HEADS
    blk = lambda a: pl.BlockSpec((1,) + a.shape[1:], lambda b: (b, 0, 0))
    cache = pl.BlockSpec((1,) + kc.shape[1:], lambda b: (layer * q1.shape[0] + b, 0, 0))
    small = lambda a: pl.BlockSpec(a.shape, lambda b: (0, 0))
    return pl.pallas_call(
        functools.partial(_diff_sample_kernel, past=past, lambda_init=lambda_init),
        grid=(b,),
        in_specs=[blk(q1), blk(q2), cache, cache, blk(kn), blk(vn)] + [small(a) for a in lams] + [small(g)],
        out_specs=blk(q1),
        out_shape=jax.ShapeDtypeStruct((b, tq, DIFF_HEADS * DIFF_VD), BF16),
        compiler_params=_params("parallel"),
        name="diff_attention_sample",
    )(q1, q2, kc, vc, kn, vn, *lams, g)


def _rope_tables(pos, theta, rot_dim, period):
    half = rot_dim // 2
    inv = 1.0 / (jnp.float32(theta) ** (jnp.arange(half, dtype=F32) / half))
    ang = pos.astype(F32)[:, None] * inv[None, :]
    cos, sin = jnp.cos(ang), jnp.sin(ang)
    n = pos.shape[0]
    rest = period - rot_dim
    c = jnp.concatenate([cos, cos, jnp.ones((n, rest), F32)], axis=1)
    s1 = jnp.concatenate([-sin, jnp.zeros((n, half + rest), F32)], axis=1)
    s2 = jnp.concatenate([jnp.zeros((n, half), F32), sin, jnp.zeros((n, rest), F32)], axis=1)
    reps = LANES // period
    return tuple(jnp.tile(a, (1, reps)) for a in (c, s1, s2))


def _layer_weights(w_in, mla_w_uq, mla_w_ukv):
    c_kv = MLA_Q_LORA
    c_kr = c_kv + MLA_KV_LORA
    c_dq = c_kr + MLA_ROPE
    dqk = DIFF_HEADS * 2 * DIFF_DH
    c_dk = c_dq + dqk
    c_dv = c_dk + dqk
    c_ga = c_dv + DIFF_HEADS * DIFF_VD
    d = w_in.shape[0]
    c_gb = c_ga + d
    cols = lambda lo, hi: w_in[:, lo:hi].astype(BF16)
    wq = cols(0, c_kv)
    wkv = jnp.pad(cols(c_kv, c_dq), ((0, 0), (0, LANES - MLA_ROPE)))
    wdq, wdk, wdv = cols(c_dq, c_dk), cols(c_dk, c_dv), cols(c_dv, c_ga)
    wga, wgb = cols(c_ga, c_gb), cols(c_gb, w_in.shape[1])
    uq = mla_w_uq.astype(BF16).reshape(MLA_Q_LORA, MLA_HEADS, MLA_NOPE + MLA_ROPE)
    uq = jnp.pad(uq, ((0, 0), (0, 0), (0, MLA_QK_PAD - MLA_NOPE - MLA_ROPE))).reshape(MLA_Q_LORA, -1)
    ukv = mla_w_ukv.astype(BF16).reshape(MLA_KV_LORA, MLA_HEADS, MLA_NOPE + MLA_V)
    ukv = jnp.concatenate([ukv[:, :, :MLA_NOPE].reshape(MLA_KV_LORA, -1),
                           ukv[:, :, MLA_NOPE:].reshape(MLA_KV_LORA, -1)], axis=1)
    return wq, wkv, wdq, wdk, wdv, wga, wgb, uq, ukv


def _block(m, want):
    return want if m % want == 0 else m


def kernel(x_prompt, x_sample, cache_mla_ckv, cache_mla_krope, cache_diff_k, cache_diff_v, norm_mix, w_in,
           mla_q_norm, mla_w_uq, mla_kv_norm, mla_w_ukv, diff_lq1, diff_lk1, diff_lq2, diff_lk2, diff_subln,
           w_branch_a, w_branch_b, w_out, norm_ffn, w_ffn_in, w_ffn_out, norm_final):
    bp, lp, d = x_prompt.shape
    bs, ls, _ = x_sample.shape
    depth, _, past, _ = cache_mla_ckv.shape
    assert depth == 1, "the FFN kernel fuses the final norm, so it serves the last (only) layer"
    mp, ms = bp * lp, bs * ls
    t_attn = 512

    pos_p = jnp.arange(lp)
    pos_s = past + jnp.arange(ls)
    tabs_mla_p = _rope_tables(pos_p, MLA_THETA, MLA_ROPE, LANES)
    tabs_mla_s = tuple(jnp.tile(a, (bs, 1)) for a in _rope_tables(pos_s, MLA_THETA, MLA_ROPE, LANES))
    tabs_dif_p = _rope_tables(pos_p, ROPE_THETA, DIFF_ROT, DIFF_DH)
    tabs_dif_s = tuple(jnp.tile(a, (bs, 1)) for a in _rope_tables(pos_s, ROPE_THETA, DIFF_ROT, DIFF_DH))
    q_scale_mla = (MLA_NOPE + MLA_ROPE) ** -0.5 * LOG2E
    q_scale_dif = DIFF_DH ** -0.5 * LOG2E
    gfinal = norm_final.reshape(1, d)

    xp = x_prompt.reshape(mp, d)
    xs = x_sample.reshape(ms, d)
    rows_p, rows_s = [], []
    for l in range(depth):
        lambda_init = 0.8 - 0.6 * math.exp(-0.3 * l)
        wq, wkv, wdq, wdk, wdv, wga, wgb, uq, ukv = _layer_weights(w_in[l], mla_w_uq[l], mla_w_ukv[l])
        gmix, gq, gkv = norm_mix[l].reshape(1, -1), mla_q_norm[l].reshape(1, -1), mla_kv_norm[l].reshape(1, -1)
        lams = tuple(a[l].reshape(1, -1) for a in (diff_lq1, diff_lk1, diff_lq2, diff_lk2))
        gsub = diff_subln[l].reshape(1, -1)
        wa, wbr, wo = w_branch_a[l].astype(BF16), w_branch_b[l].astype(BF16), w_out[l].astype(BF16)
        wfi, wfo = w_ffn_in[l].astype(BF16), w_ffn_out[l].astype(BF16)
        gffn = norm_ffn[l].reshape(1, -1)

        def stage1(x, bm, tabs_mla, tabs_dif, tr):
            h, q, ckv, krope, kmla, vmla = _mla_proj(x, gmix, wq, gq, uq, wkv, gkv, ukv, tabs_mla, bm,
                                                     q_scale_mla, tr)
            q1, q2, dk, dkb, dv, dvb = _diff_proj(h, wdq, wdk, wdv, tabs_dif, bm, q_scale_dif, tr)
            return h, q, ckv, krope, kmla, vmla, q1, q2, dk, dkb, dv, dvb

        def stage2(x, h, oa, ob, bm, bn, bf):
            mg = _merge(h, oa, ob, wga, wgb, wa, wbr, bm, bn)
            x2, h2 = _outproj(x, mg, wo, gffn, min(bm, 512))
            return _ffn(h2, wfi, wfo, x2, gfinal, bm, bf)

        h, q, ckv, krope, kmla, vmla, q1, q2, dk, dkb, dv, dvb = stage1(xp, t_attn, tabs_mla_p, tabs_dif_p, True)
        r3 = lambda a: a.reshape(bp, lp, -1)
        oa = _mla_prompt(q, r3(kmla), vmla, t_attn, 2)
        ob = _diff_prompt(q1, q2, r3(dkb), dvb, lams, gsub, t_attn, lambda_init)
        xp = stage2(xp, h, oa.reshape(mp, -1), ob.reshape(mp, -1), _block(mp, 1024), 512, 256)
        rows_p.append((ckv.reshape(bp, lp, -1), krope.reshape(bp, lp, -1),
                       dk.reshape(bp, lp, DIFF_HEADS, -1), dv.reshape(bp, lp, DIFF_HEADS, -1)))

        h, q, ckv, krope, kmla, vmla, q1, q2, dk, dkb, dv, dvb = stage1(xs, ms, tabs_mla_s, tabs_dif_s, False)
        kr_pad = jnp.pad(cache_mla_krope[l].reshape(bs * past, MLA_ROPE), ((0, 0), (0, LANES - MLA_ROPE)))
        kc, vc = _kvcache_expand(cache_mla_ckv[l].reshape(bs * past, -1), kr_pad, ukv, _block(bs * past, 512))
        r3 = lambda a: a.reshape(bs, ls, -1)
        c3 = lambda a: a.reshape(bs, past, -1)
        oa = _mla_sample(r3(q), c3(kc), c3(vc), r3(kmla), r3(vmla))
        call = lambda a: a.reshape(depth * bs, past * DIFF_HEADS, -1)
        ob = _diff_sample(r3(q1), r3(q2), call(cache_diff_k), call(cache_diff_v), r3(dkb), r3(dvb),
                          lams, gsub, lambda_init, l)
        xs = stage2(xs, h, oa.reshape(ms, -1), ob.reshape(ms, -1), ms, 512, 512)
        rows_s.append((ckv.reshape(bs, ls, -1), krope.reshape(bs, ls, -1),
                       dk.reshape(bs, ls, DIFF_HEADS, -1), dv.reshape(bs, ls, DIFF_HEADS, -1)))

    y_prompt = xp.reshape(bp, lp, d)
    y_sample = xs.reshape(bs, ls, d)
    stack = lambda rows, i: jnp.stack([r[i] for r in rows], axis=0)
    return (y_prompt, y_sample,
            stack(rows_p, 0), stack(rows_p, 1), stack(rows_p, 2), stack(rows_p, 3),
            stack(rows_s, 0), stack(rows_s, 1), stack(rows_s, 2), stack(rows_s, 3))
```

```python
import functools
import math

import jax
import jax.numpy as jnp
from jax import lax
from jax.experimental import pallas as pl
from jax.experimental.pallas import tpu as pltpu

F32 = jnp.float32
BF16 = jnp.bfloat16

CHUNK = 64
EPS = 1e-6
NEG_INF = -1e30

MLA_HEADS = 8
MLA_Q_LORA = 512
MLA_KV_LORA = 256
MLA_NOPE = 128
MLA_ROPE = 64
MLA_V = 128
MLA_THETA = 10000.0
MLA_QK_PAD = 256

DIFF_HEADS = 8
DIFF_DH = 64
DIFF_VD = 2 * DIFF_DH
DIFF_ROT = DIFF_DH // 4
ROPE_THETA = 500000.0

LANES = 128
VT_ONES = 16
LOG2E = math.log2(math.e)
VMEM_LIMIT = 56 * 1024 * 1024


def _params(*sem):
    return pltpu.CompilerParams(dimension_semantics=sem, vmem_limit_bytes=VMEM_LIMIT)


def _dot(a, b):
    return jnp.dot(a, b, preferred_element_type=F32)


def _dot_nt(a, b):
    return lax.dot_general(a, b, (((1,), (1,)), ((), ())), preferred_element_type=F32)


def _rms(x, g):
    return x * lax.rsqrt(jnp.mean(x * x, axis=-1, keepdims=True) + EPS) * g


def _rope_tile(t, c, s1, s2, half):
    return t * c + pltpu.roll(t, LANES - half, 1) * s1 + pltpu.roll(t, half, 1) * s2


def _put(ref, lo, val, tr):
    w = val.shape[1]
    if tr:
        ref[0, lo:lo + w, :] = val.T.astype(ref.dtype)
    else:
        ref[:, lo:lo + w] = val.astype(ref.dtype)


def _put_v(ref, hd, val, tr):
    bm, w = val.shape
    if tr:
        lo = hd * (w + VT_ONES)
        ref[0, lo:lo + w, :] = val.T.astype(ref.dtype)
        ref[0, lo + w:lo + w + VT_ONES, :] = jnp.ones((VT_ONES, bm), ref.dtype)
    else:
        ref[:, hd * w:(hd + 1) * w] = val.astype(ref.dtype)


def _v_width(heads, dv, tr):
    return heads * (dv + VT_ONES) if tr else heads * dv


def _out(m, n, bm, dtype, tr):
    if tr:
        return pl.BlockSpec((1, n, bm), lambda i: (i, 0, 0)), jax.ShapeDtypeStruct((m // bm, n, bm), dtype)
    return pl.BlockSpec((bm, n), lambda i: (i, 0)), jax.ShapeDtypeStruct((m, n), dtype)


def _qproj_body(h, wq_ref, gq_ref, wuq_ref, c, s1, s2, o_ref, scale, tr):
    qlat = _rms(_dot(h, wq_ref[...]), gq_ref[...]).astype(BF16)
    q = _dot(qlat, wuq_ref[...])
    for hd in range(MLA_HEADS):
        lo = hd * MLA_QK_PAD
        _put(o_ref, lo, q[:, lo:lo + LANES] * scale, tr)
        t = _rope_tile(q[:, lo + LANES:lo + 2 * LANES], c, s1, s2, MLA_ROPE // 2)
        _put(o_ref, lo + LANES, t * scale, tr)


def _kv_expand(ckv, krope_tile, wukv_ref, kmla_ref, vmla_ref, tr):
    kv = _dot(ckv.astype(BF16), wukv_ref[...])
    kr = krope_tile.astype(BF16)
    for hd in range(MLA_HEADS):
        lo = hd * MLA_QK_PAD
        kmla_ref[:, lo:lo + LANES] = kv[:, hd * MLA_NOPE:(hd + 1) * MLA_NOPE].astype(BF16)
        kmla_ref[:, lo + LANES:lo + 2 * LANES] = kr
        vlo = MLA_HEADS * MLA_NOPE + hd * MLA_V
        _put_v(vmla_ref, hd, kv[:, vlo:vlo + MLA_V], tr)


def _mla_proj_kernel(x_ref, gmix_ref, wq_ref, gq_ref, wuq_ref, wkv_ref, gkv_ref, wukv_ref, c_ref, s1_ref, s2_ref,
                     h_ref, q_ref, ckv_ref, krope_ref, kmla_ref, vmla_ref, *, scale, tr):
    h = _rms(x_ref[...], gmix_ref[...]).astype(BF16)
    h_ref[...] = h
    c, s1, s2 = c_ref[...], s1_ref[...], s2_ref[...]
    _qproj_body(h, wq_ref, gq_ref, wuq_ref, c, s1, s2, q_ref, scale, tr)
    z = _dot(h, wkv_ref[...])
    ckv = _rms(z[:, :MLA_KV_LORA], gkv_ref[...])
    ckv_ref[...] = ckv
    t = _rope_tile(z[:, MLA_KV_LORA:], c, s1, s2, MLA_ROPE // 2)
    krope_ref[...] = t[:, :MLA_ROPE]
    _kv_expand(ckv, t, wukv_ref, kmla_ref, vmla_ref, tr)


def _mla_proj(x, gmix, wq, gq, wuq, wkv, gkv, wukv, tabs, bm, scale, tr):
    m, d = x.shape
    nt = tabs[0].shape[0] // bm
    full = lambda a: pl.BlockSpec(a.shape, lambda i: (0, 0))
    tab = pl.BlockSpec((bm, LANES), lambda i: (i % nt, 0))
    row = lambda n: pl.BlockSpec((bm, n), lambda i: (i, 0))
    nk = MLA_HEADS * MLA_QK_PAD
    qspec, qshape = _out(m, wuq.shape[1], bm, BF16, tr)
    vspec, vshape = _out(m, _v_width(MLA_HEADS, MLA_V, tr), bm, BF16, tr)
    return pl.pallas_call(
        functools.partial(_mla_proj_kernel, scale=scale, tr=tr),
        grid=(m // bm,),
        in_specs=[row(d), full(gmix), full(wq), full(gq), full(wuq), full(wkv), full(gkv), full(wukv), tab, tab, tab],
        out_specs=[row(d), qspec, row(MLA_KV_LORA), row(MLA_ROPE), row(nk), vspec],
        out_shape=[jax.ShapeDtypeStruct((m, d), BF16), qshape,
                   jax.ShapeDtypeStruct((m, MLA_KV_LORA), F32), jax.ShapeDtypeStruct((m, MLA_ROPE), F32),
                   jax.ShapeDtypeStruct((m, nk), BF16), vshape],
        compiler_params=_params("parallel"),
        name="norm_mla_proj",
    )(x, gmix, wq, gq, wuq, wkv, gkv, wukv, *tabs)


def _kvcache_kernel(ckv_ref, krope_ref, wukv_ref, kmla_ref, vmla_ref):
    _kv_expand(ckv_ref[...], krope_ref[...], wukv_ref, kmla_ref, vmla_ref, False)


def _kvcache_expand(ckv, krope_pad, wukv, bm):
    m = ckv.shape[0]
    row = lambda n: pl.BlockSpec((bm, n), lambda i: (i, 0))
    nk, nv = MLA_HEADS * MLA_QK_PAD, MLA_HEADS * MLA_V
    return pl.pallas_call(
        _kvcache_kernel,
        grid=(m // bm,),
        in_specs=[row(MLA_KV_LORA), row(LANES), pl.BlockSpec(wukv.shape, lambda i: (0, 0))],
        out_specs=[row(nk), row(nv)],
        out_shape=[jax.ShapeDtypeStruct((m, nk), BF16), jax.ShapeDtypeStruct((m, nv), BF16)],
        compiler_params=_params("parallel"),
        name="mla_kv_cache_expand",
    )(ckv, krope_pad, wukv)


def _diff_proj_kernel(h_ref, wdq_ref, wdk_ref, wdv_ref, c_ref, s1_ref, s2_ref,
                      q1_ref, q2_ref, dk_ref, dkb_ref, dv_ref, dvb_ref, *, scale, tr):
    h = h_ref[...]
    zq = _dot(h, wdq_ref[...])
    zk = _dot(h, wdk_ref[...])
    zv = _dot(h, wdv_ref[...])
    dv_ref[...] = zv
    c, s1, s2 = c_ref[...], s1_ref[...], s2_ref[...]
    first = lax.broadcasted_iota(jnp.int32, (1, LANES), 1) < DIFF_DH
    for hd in range(DIFF_HEADS):
        sl = slice(hd * LANES, (hd + 1) * LANES)
        q = _rope_tile(zq[:, sl], c, s1, s2, DIFF_ROT // 2) * scale
        _put(q1_ref, hd * LANES, jnp.where(first, q, 0.0), tr)
        _put(q2_ref, hd * LANES, jnp.where(first, 0.0, q), tr)
        k = _rope_tile(zk[:, sl], c, s1, s2, DIFF_ROT // 2)
        dk_ref[:, sl] = k
        dkb_ref[:, sl] = k.astype(BF16)
        _put_v(dvb_ref, hd, zv[:, sl], tr)


def _diff_proj(h, wdq, wdk, wdv, tabs, bm, scale, tr):
    m, d = h.shape
    nt = tabs[0].shape[0] // bm
    n = wdq.shape[1]
    full = lambda a: pl.BlockSpec(a.shape, lambda i: (0, 0))
    tab = pl.BlockSpec((bm, LANES), lambda i: (i % nt, 0))
    row = lambda w: pl.BlockSpec((bm, w), lambda i: (i, 0))
    qspec, qshape = _out(m, n, bm, BF16, tr)
    vspec, vshape = _out(m, _v_width(DIFF_HEADS, DIFF_VD, tr), bm, BF16, tr)
    f32_rows, bf16_rows = jax.ShapeDtypeStruct((m, n), F32), jax.ShapeDtypeStruct((m, n), BF16)
    return pl.pallas_call(
        functools.partial(_diff_proj_kernel, scale=scale, tr=tr),
        grid=(m // bm,),
        in_specs=[row(d), full(wdq), full(wdk), full(wdv), tab, tab, tab],
        out_specs=[qspec, qspec, row(n), row(n), row(n), vspec],
        out_shape=[qshape, qshape, f32_rows, bf16_rows, f32_rows, vshape],
        compiler_params=_params("parallel"),
        name="diff_proj",
    )(h, wdq, wdk, wdv, *tabs)


def _merge_kernel(h_ref, oa_ref, ob_ref, wga_ref, wgb_ref, wa_ref, wb_ref, o_ref):
    h = h_ref[...]
    ga = jax.nn.sigmoid(_dot(h, wga_ref[...]))
    gb = jax.nn.sigmoid(_dot(h, wgb_ref[...]))
    ya = _dot(oa_ref[...], wa_ref[...])
    yb = _dot(ob_ref[...], wb_ref[...])
    o_ref[...] = (ga * ya + gb * yb).astype(o_ref.dtype)


def _merge(h, oa, ob, wga, wgb, wa, wb, bm, bn):
    m, d = h.shape
    n = wga.shape[1]
    row = lambda a: pl.BlockSpec((bm, a.shape[1]), lambda i, j: (i, 0))
    col = lambda a: pl.BlockSpec((a.shape[0], bn), lambda i, j: (0, j))
    return pl.pallas_call(
        _merge_kernel,
        grid=(m // bm, n // bn),
        in_specs=[row(h), row(oa), row(ob), col(wga), col(wgb), col(wa), col(wb)],
        out_specs=pl.BlockSpec((bm, bn), lambda i, j: (i, j)),
        out_shape=jax.ShapeDtypeStruct((m, n), BF16),
        compiler_params=_params("parallel", "arbitrary"),
        name="gated_merge",
    )(h, oa, ob, wga, wgb, wa, wb)


def _outproj_kernel(x_ref, mg_ref, wo_ref, g_ref, x2_ref, h2_ref):
    x2 = x_ref[...] + _dot(mg_ref[...], wo_ref[...])
    x2_ref[...] = x2
    h2_ref[...] = _rms(x2, g_ref[...]).astype(BF16)


def _outproj(x, mg, wo, g, bm):
    m, d = x.shape
    row = pl.BlockSpec((bm, d), lambda i: (i, 0))
    return pl.pallas_call(
        _outproj_kernel,
        grid=(m // bm,),
        in_specs=[row, row, pl.BlockSpec(wo.shape, lambda i: (0, 0)), pl.BlockSpec((1, d), lambda i: (0, 0))],
        out_specs=[row, row],
        out_shape=[jax.ShapeDtypeStruct((m, d), F32), jax.ShapeDtypeStruct((m, d), BF16)],
        compiler_params=_params("parallel"),
        name="out_proj_residual",
    )(x, mg, wo, g)


def _ffn_kernel(h2_ref, wg_ref, wu_ref, wd_ref, x2_ref, gf_ref, y_ref):
    f = pl.program_id(1)

    @pl.when(f == 0)
    def _():
        y_ref[...] = x2_ref[...]

    h2 = h2_ref[...]
    a = (jax.nn.silu(_dot(h2, wg_ref[...])) * _dot(h2, wu_ref[...])).astype(BF16)
    y_ref[...] += _dot(a, wd_ref[...])

    @pl.when(f == pl.num_programs(1) - 1)
    def _():
        y_ref[...] = _rms(y_ref[...], gf_ref[...])


def _ffn(h2, w_in, w_out, x2, gf, bm, bf):
    m, d = h2.shape
    dff = w_out.shape[0]
    nf = dff // bf
    row = pl.BlockSpec((bm, d), lambda i, f: (i, 0))
    return pl.pallas_call(
        _ffn_kernel,
        grid=(m // bm, nf),
        in_specs=[row,
                  pl.BlockSpec((d, bf), lambda i, f: (0, f)),
                  pl.BlockSpec((d, bf), lambda i, f: (0, f + nf)),
                  pl.BlockSpec((bf, d), lambda i, f: (f, 0)),
                  row,
                  pl.BlockSpec((1, d), lambda i, f: (0, 0))],
        out_specs=row,
        out_shape=jax.ShapeDtypeStruct((m, d), F32),
        compiler_params=_params("parallel", "arbitrary"),
        name="ffn_swiglu_final_norm",
    )(h2, w_in, w_in, w_out, x2, gf)


def _softmax_step(s, v, m, l, acc):
    m_new = jnp.maximum(m, jnp.max(s, axis=-1, keepdims=True))
    alpha = jnp.exp2(m - m_new)
    p = jnp.exp2(s - m_new)
    l = alpha * l + jnp.sum(p, axis=-1, keepdims=True)
    acc = alpha * acc + _dot(p.astype(BF16), v)
    return m_new, l, acc


def _chunk_mask(tq, tk, q0, k0):
    qp = q0 + lax.broadcasted_iota(jnp.int32, (tq, tk), 0)
    kp = k0 + lax.broadcasted_iota(jnp.int32, (tq, tk), 1)
    return (kp // CHUNK) <= (qp // CHUNK)


def _softmax_init(tq, dv):
    return (jnp.full((tq, 1), NEG_INF, F32), jnp.zeros((tq, 1), F32), jnp.zeros((tq, dv), F32))


def _chunk_mask_t(t):
    kp = lax.broadcasted_iota(jnp.int32, (t, t), 0)
    qp = lax.broadcasted_iota(jnp.int32, (t, t), 1)
    return (kp // CHUNK) <= (qp // CHUNK)


def _probs_t(s, smax, m):
    m_new = jnp.maximum(m, smax)
    return jnp.exp2(s - m_new).astype(BF16), jnp.exp2(m - m_new), m_new


def _attention_scratch(t, dv, streams):
    return [pltpu.VMEM((streams, 2, t, t), F32),
            pltpu.VMEM((streams, dv + VT_ONES, t), F32),
            pltpu.VMEM((streams, 8, t), F32)]


def _causal_attention_t(n, scores, next_scores, values, finish, s_ref, acc_ref, stat_ref, t, streams):
    row_m, row_smax = 0, 1

    def stat(st, r):
        return stat_ref[st, r:r + 1, :]

    def put_scores(st, slot, s):
        s_ref[st, slot] = s
        stat_ref[st, row_smax + slot:row_smax + slot + 1, :] = jnp.max(s, axis=0, keepdims=True)

    def accumulate(st, j, s, smax):
        p, alpha, m = _probs_t(s, smax, stat(st, row_m))
        acc = alpha * acc_ref[st] + _dot(values(j, st), p)
        stat_ref[st, row_m:row_m + 1, :] = m
        return acc

    @pl.when(n == 0)
    def _():
        for st in range(streams):
            put_scores(st, 0, scores(0, st))

    for st in range(streams):
        acc_ref[st] = jnp.zeros(acc_ref.shape[1:], F32)
        stat_ref[st, row_m:row_m + 1, :] = jnp.full((1, t), NEG_INF, F32)

    def run(first):
        def step(tau, par):
            for st in range(streams):
                put_scores(st, (first + par) % 2, scores(tau, st))
                prev = (first + par + 1) % 2
                acc_ref[st] = accumulate(st, tau - 1, s_ref[st, prev], stat(st, row_smax + prev))

        unroll = 4

        def trip(u, carry):
            for k in range(unroll):
                step(unroll * u + 1 + k, (1 + k) % 2)
            return carry

        lax.fori_loop(0, n // unroll, trip, 0)
        done = (n // unroll) * unroll

        @pl.when(n - done >= 2)
        def _():
            step(done + 1, 1)
            step(done + 2, 0)

        @pl.when(n % 2 == 1)
        def _():
            step(n, 1)

        def tail(par):
            diag = (first + par) % 2
            for st in range(streams):
                put_scores(st, 1 - diag, next_scores(st))
            mask = _chunk_mask_t(t)
            res = []
            for st in range(streams):
                s = jnp.where(mask, s_ref[st, diag], NEG_INF)
                res.append(accumulate(st, n, s, jnp.max(s, axis=0, keepdims=True)))
            finish(res)

        for par in (0, 1):
            pl.when(n % 2 == par)(functools.partial(tail, par))

    first = ((n + 1) // 2) % 2
    for f in (0, 1):
        pl.when(first == f)(functools.partial(run, f))


def _normalised(acc, dv):
    return (acc[:dv] / acc[dv:dv + 1]).T


def _mla_prompt_kernel(qt_ref, qn_ref, k_ref, vt_ref, o_ref, *scratch, t, heads):
    def block_scores(q_ref, j, hd):
        k = k_ref[0, pl.ds(pl.multiple_of(j * t, t), t), hd * MLA_QK_PAD:(hd + 1) * MLA_QK_PAD]
        return _dot(k, q_ref[0, hd * MLA_QK_PAD:(hd + 1) * MLA_QK_PAD, :])

    scores = functools.partial(block_scores, qt_ref)
    next_scores = functools.partial(block_scores, qn_ref, 0)

    def values(j, hd):
        return vt_ref[j, hd * (MLA_V + VT_ONES):(hd + 1) * (MLA_V + VT_ONES), :]

    def finish(res):
        for hd, acc in enumerate(res):
            o_ref[0, :, hd * MLA_V:(hd + 1) * MLA_V] = _normalised(acc, MLA_V).astype(o_ref.dtype)

    _causal_attention_t(pl.program_id(2), scores, next_scores, values, finish, *scratch, t, heads)


def _mla_prompt(qt, k, vt, t, heads):
    b, l, _ = k.shape
    nq = l // t
    qblock = lambda step: pl.BlockSpec((1, heads * MLA_QK_PAD, t),
                                       lambda b, h, i: (b * nq + jnp.minimum(i + step, nq - 1), h, 0))
    return pl.pallas_call(
        functools.partial(_mla_prompt_kernel, t=t, heads=heads),
        grid=(b, MLA_HEADS // heads, nq),
        in_specs=[qblock(0), qblock(1),
                  pl.BlockSpec((1, l, heads * MLA_QK_PAD), lambda b, h, i: (b, 0, h)),
                  pl.BlockSpec((nq, heads * (MLA_V + VT_ONES), t), lambda b, h, i: (b, h, 0))],
        out_specs=pl.BlockSpec((1, t, heads * MLA_V), lambda b, h, i: (b, i, h)),
        out_shape=jax.ShapeDtypeStruct((b, l, MLA_HEADS * MLA_V), BF16),
        scratch_shapes=_attention_scratch(t, MLA_V, heads),
        compiler_params=_params("parallel", "parallel", "arbitrary"),
        name="mla_attention_prompt",
    )(qt, qt, k, vt)


def _lambda(lq1_ref, lk1_ref, lq2_ref, lk2_ref, lambda_init):
    a = jnp.sum(lq1_ref[...] * lk1_ref[...], axis=-1, keepdims=True)
    b = jnp.sum(lq2_ref[...] * lk2_ref[...], axis=-1, keepdims=True)
    return jnp.exp(a) - jnp.exp(b) + lambda_init


def _diff_finish(c1, c2, lam, g, lambda_init):
    o = c1[2] / c1[1] - lam * (c2[2] / c2[1])
    return _rms(o, g) * (1.0 - lambda_init)


def _diff_prompt_kernel(q1t_ref, q2t_ref, q1n_ref, q2n_ref, k_ref, vt_ref, lq1_ref, lk1_ref, lq2_ref, lk2_ref,
                        g_ref, o_ref, *scratch, t, heads, lambda_init):
    qt_refs, qn_refs = (q1t_ref, q2t_ref), (q1n_ref, q2n_ref)

    def block_scores(q_refs, rows, st):
        cols = slice((st // 2) * LANES, (st // 2 + 1) * LANES)
        return _dot(k_ref[0, rows, cols], q_refs[st % 2][0, cols, :])

    def scores(j, st):
        return block_scores(qt_refs, pl.ds(pl.multiple_of(j * t, t), t), st)

    def next_scores(st):
        return block_scores(qn_refs, slice(0, t), st)

    def values(j, st):
        return vt_ref[j, (st // 2) * (DIFF_VD + VT_ONES):(st // 2 + 1) * (DIFF_VD + VT_ONES), :]

    def finish(res):
        lam = _lambda(lq1_ref, lk1_ref, lq2_ref, lk2_ref, lambda_init)
        for hd in range(heads):
            o = _normalised(res[2 * hd], DIFF_VD) - lam * _normalised(res[2 * hd + 1], DIFF_VD)
            o = _rms(o, g_ref[...]) * (1.0 - lambda_init)
            o_ref[0, :, hd * DIFF_VD:(hd + 1) * DIFF_VD] = o.astype(o_ref.dtype)

    _causal_attention_t(pl.program_id(2), scores, next_scores, values, finish, *scratch, t, 2 * heads)


def _diff_prompt(q1t, q2t, k, vt, lams, g, t, heads, lambda_init):
    b, l, _ = k.shape
    nq = l // t
    qs = pl.BlockSpec((1, heads * LANES, t), lambda b, h, i: (b * nq + i, h, 0))
    qn = pl.BlockSpec((1, heads * LANES, t), lambda b, h, i: (b * nq + jnp.minimum(i + 1, nq - 1), h, 0))
    small = lambda a: pl.BlockSpec(a.shape, lambda b, h, i: (0, 0))
    return pl.pallas_call(
        functools.partial(_diff_prompt_kernel, t=t, heads=heads, lambda_init=lambda_init),
        grid=(b, DIFF_HEADS // heads, nq),
        in_specs=[qs, qs, qn, qn,
                  pl.BlockSpec((1, l, heads * LANES), lambda b, h, i: (b, 0, h)),
                  pl.BlockSpec((nq, heads * (DIFF_VD + VT_ONES), t), lambda b, h, i: (b, h, 0))]
                 + [small(a) for a in lams] + [small(g)],
        out_specs=pl.BlockSpec((1, t, heads * DIFF_VD), lambda b, h, i: (b, i, h)),
        out_shape=jax.ShapeDtypeStruct((b, l, DIFF_HEADS * DIFF_VD), BF16),
        scratch_shapes=_attention_scratch(t, DIFF_VD, 2 * heads),
        compiler_params=_params("parallel", "parallel", "arbitrary"),
        name="diff_attention_prompt",
    )(q1t, q2t, q1t, q2t, k, vt, *lams, g)


def _two_part_softmax(q, kc, vc, kn, vn, mask_c, mask_n):
    tq = q.shape[0]
    carry = _softmax_init(tq, vc.shape[1])
    carry = _softmax_step(jnp.where(mask_c, _dot_nt(q, kc), NEG_INF), vc, *carry)
    return _softmax_step(jnp.where(mask_n, _dot_nt(q, kn), NEG_INF), vn, *carry)


def _mla_sample_kernel(q_ref, kc_ref, vc_ref, kn_ref, vn_ref, o_ref, *, past):
    tq, tc = q_ref.shape[1], kc_ref.shape[1]
    mask_c, mask_n = _chunk_mask(tq, tc, past, 0), _chunk_mask(tq, tq, past, past)
    for hd in range(MLA_HEADS):
        qk = slice(hd * MLA_QK_PAD, (hd + 1) * MLA_QK_PAD)
        v = slice(hd * MLA_V, (hd + 1) * MLA_V)
        _, l, acc = _two_part_softmax(q_ref[0, :, qk], kc_ref[0, :, qk], vc_ref[0, :, v], kn_ref[0, :, qk],
                                      vn_ref[0, :, v], mask_c, mask_n)
        o_ref[0, :, v] = (acc / l).astype(o_ref.dtype)


def _mla_sample(q, kc, vc, kn, vn):
    b, tq, _ = q.shape
    blk = lambda a: pl.BlockSpec((1,) + a.shape[1:], lambda b: (b, 0, 0))
    return pl.pallas_call(
        functools.partial(_mla_sample_kernel, past=kc.shape[1]),
        grid=(b,),
        in_specs=[blk(q), blk(kc), blk(vc), blk(kn), blk(vn)],
        out_specs=pl.BlockSpec((1, tq, MLA_HEADS * MLA_V), lambda b: (b, 0, 0)),
        out_shape=jax.ShapeDtypeStruct((b, tq, MLA_HEADS * MLA_V), BF16),
        compiler_params=_params("parallel"),
        name="mla_attention_sample",
    )(q, kc, vc, kn, vn)


def _diff_sample_kernel(q1_ref, q2_ref, kc_ref, vc_ref, kn_ref, vn_ref,
                        lq1_ref, lk1_ref, lq2_ref, lk2_ref, g_ref, o_ref, *, past, lambda_init):
    tq = q1_ref.shape[1]
    mask_c, mask_n = _chunk_mask(tq, past, past, 0), _chunk_mask(tq, tq, past, past)
    lam = _lambda(lq1_ref, lk1_ref, lq2_ref, lk2_ref, lambda_init)
    for hd in range(DIFF_HEADS):
        rows = pl.ds(hd, past, stride=DIFF_HEADS)
        cols = slice(hd * LANES, (hd + 1) * LANES)
        kc, vc = kc_ref[0, rows, :].astype(BF16), vc_ref[0, rows, :].astype(BF16)
        kn, vn = kn_ref[0, :, cols], vn_ref[0, :, cols]
        c1 = _two_part_softmax(q1_ref[0, :, cols], kc, vc, kn, vn, mask_c, mask_n)
        c2 = _two_part_softmax(q2_ref[0, :, cols], kc, vc, kn, vn, mask_c, mask_n)
        o_ref[0, :, cols] = _diff_finish(c1, c2, lam, g_ref[...], lambda_init).astype(o_ref.dtype)


def _diff_sample(q1, q2, kc, vc, kn, vn, lams, g, lambda_init, layer):
    b, tq, _ = q1.shape
    past = kc.shape[1] // DIFF_HEADS
    blk = lambda a: pl.BlockSpec((1,) + a.shape[1:], lambda b: (b, 0, 0))
    cache = pl.BlockSpec((1,) + kc.shape[1:], lambda b: (layer * q1.shape[0] + b, 0, 0))
    small = lambda a: pl.BlockSpec(a.shape, lambda b: (0, 0))
    return pl.pallas_call(
        functools.partial(_diff_sample_kernel, past=past, lambda_init=lambda_init),
        grid=(b,),
        in_specs=[blk(q1), blk(q2), cache, cache, blk(kn), blk(vn)] + [small(a) for a in lams] + [small(g)],
        out_specs=blk(q1),
        out_shape=jax.ShapeDtypeStruct((b, tq, DIFF_HEADS * DIFF_VD), BF16),
        compiler_params=_params("parallel"),
        name="diff_attention_sample",
    )(q1, q2, kc, vc, kn, vn, *lams, g)


def _rope_tables(pos, theta, rot_dim, period):
    half = rot_dim // 2
    inv = 1.0 / (jnp.float32(theta) ** (jnp.arange(half, dtype=F32) / half))
    ang = pos.astype(F32)[:, None] * inv[None, :]
    cos, sin = jnp.cos(ang), jnp.sin(ang)
    n = pos.shape[0]
    rest = period - rot_dim
    c = jnp.concatenate([cos, cos, jnp.ones((n, rest), F32)], axis=1)
    s1 = jnp.concatenate([-sin, jnp.zeros((n, half + rest), F32)], axis=1)
    s2 = jnp.concatenate([jnp.zeros((n, half), F32), sin, jnp.zeros((n, rest), F32)], axis=1)
    reps = LANES // period
    return tuple(jnp.tile(a, (1, reps)) for a in (c, s1, s2))


def _layer_weights(w_in, mla_w_uq, mla_w_ukv):
    c_kv = MLA_Q_LORA
    c_kr = c_kv + MLA_KV_LORA
    c_dq = c_kr + MLA_ROPE
    dqk = DIFF_HEADS * 2 * DIFF_DH
    c_dk = c_dq + dqk
    c_dv = c_dk + dqk
    c_ga = c_dv + DIFF_HEADS * DIFF_VD
    d = w_in.shape[0]
    c_gb = c_ga + d
    cols = lambda lo, hi: w_in[:, lo:hi].astype(BF16)
    wq = cols(0, c_kv)
    wkv = jnp.pad(cols(c_kv, c_dq), ((0, 0), (0, LANES - MLA_ROPE)))
    wdq, wdk, wdv = cols(c_dq, c_dk), cols(c_dk, c_dv), cols(c_dv, c_ga)
    wga, wgb = cols(c_ga, c_gb), cols(c_gb, w_in.shape[1])
    uq = mla_w_uq.astype(BF16).reshape(MLA_Q_LORA, MLA_HEADS, MLA_NOPE + MLA_ROPE)
    uq = jnp.pad(uq, ((0, 0), (0, 0), (0, MLA_QK_PAD - MLA_NOPE - MLA_ROPE))).reshape(MLA_Q_LORA, -1)
    ukv = mla_w_ukv.astype(BF16).reshape(MLA_KV_LORA, MLA_HEADS, MLA_NOPE + MLA_V)
    ukv = jnp.concatenate([ukv[:, :, :MLA_NOPE].reshape(MLA_KV_LORA, -1),
                           ukv[:, :, MLA_NOPE:].reshape(MLA_KV_LORA, -1)], axis=1)
    return wq, wkv, wdq, wdk, wdv, wga, wgb, uq, ukv


def _block(m, want):
    return want if m % want == 0 else m


def kernel(x_prompt, x_sample, cache_mla_ckv, cache_mla_krope, cache_diff_k, cache_diff_v, norm_mix, w_in,
           mla_q_norm, mla_w_uq, mla_kv_norm, mla_w_ukv, diff_lq1, diff_lk1, diff_lq2, diff_lk2, diff_subln,
           w_branch_a, w_branch_b, w_out, norm_ffn, w_ffn_in, w_ffn_out, norm_final):
    bp, lp, d = x_prompt.shape
    bs, ls, _ = x_sample.shape
    depth, _, past, _ = cache_mla_ckv.shape
    assert depth == 1, "the FFN kernel fuses the final norm, so it serves the last (only) layer"
    mp, ms = bp * lp, bs * ls
    t_attn = 512

    pos_p = jnp.arange(lp)
    pos_s = past + jnp.arange(ls)
    tabs_mla_p = _rope_tables(pos_p, MLA_THETA, MLA_ROPE, LANES)
    tabs_mla_s = tuple(jnp.tile(a, (bs, 1)) for a in _rope_tables(pos_s, MLA_THETA, MLA_ROPE, LANES))
    tabs_dif_p = _rope_tables(pos_p, ROPE_THETA, DIFF_ROT, DIFF_DH)
    tabs_dif_s = tuple(jnp.tile(a, (bs, 1)) for a in _rope_tables(pos_s, ROPE_THETA, DIFF_ROT, DIFF_DH))
    q_scale_mla = (MLA_NOPE + MLA_ROPE) ** -0.5 * LOG2E
    q_scale_dif = DIFF_DH ** -0.5 * LOG2E
    gfinal = norm_final.reshape(1, d)

    xp = x_prompt.reshape(mp, d)
    xs = x_sample.reshape(ms, d)
    rows_p, rows_s = [], []
    for l in range(depth):
        lambda_init = 0.8 - 0.6 * math.exp(-0.3 * l)
        wq, wkv, wdq, wdk, wdv, wga, wgb, uq, ukv = _layer_weights(w_in[l], mla_w_uq[l], mla_w_ukv[l])
        gmix, gq, gkv = norm_mix[l].reshape(1, -1), mla_q_norm[l].reshape(1, -1), mla_kv_norm[l].reshape(1, -1)
        lams = tuple(a[l].reshape(1, -1) for a in (diff_lq1, diff_lk1, diff_lq2, diff_lk2))
        gsub = diff_subln[l].reshape(1, -1)
        wa, wbr, wo = w_branch_a[l].astype(BF16), w_branch_b[l].astype(BF16), w_out[l].astype(BF16)
        wfi, wfo = w_ffn_in[l].astype(BF16), w_ffn_out[l].astype(BF16)
        gffn = norm_ffn[l].reshape(1, -1)

        def stage1(x, bm, tabs_mla, tabs_dif, tr):
            h, q, ckv, krope, kmla, vmla = _mla_proj(x, gmix, wq, gq, uq, wkv, gkv, ukv, tabs_mla, bm,
                                                     q_scale_mla, tr)
            q1, q2, dk, dkb, dv, dvb = _diff_proj(h, wdq, wdk, wdv, tabs_dif, bm, q_scale_dif, tr)
            return h, q, ckv, krope, kmla, vmla, q1, q2, dk, dkb, dv, dvb

        def stage2(x, h, oa, ob, bm, bn, bf):
            mg = _merge(h, oa, ob, wga, wgb, wa, wbr, bm, bn)
            x2, h2 = _outproj(x, mg, wo, gffn, min(bm, 512))
            return _ffn(h2, wfi, wfo, x2, gfinal, bm, bf)

        h, q, ckv, krope, kmla, vmla, q1, q2, dk, dkb, dv, dvb = stage1(xp, t_attn, tabs_mla_p, tabs_dif_p, True)
        r3 = lambda a: a.reshape(bp, lp, -1)
        oa = _mla_prompt(q, r3(kmla), vmla, t_attn, 2)
        ob = _diff_prompt(q1, q2, r3(dkb), dvb, lams, gsub, t_attn, 2, lambda_init)
        xp = stage2(xp, h, oa.reshape(mp, -1), ob.reshape(mp, -1), _block(mp, 1024), 512, 256)
        rows_p.append((ckv.reshape(bp, lp, -1), krope.reshape(bp, lp, -1),
                       dk.reshape(bp, lp, DIFF_HEADS, -1), dv.reshape(bp, lp, DIFF_HEADS, -1)))

        h, q, ckv, krope, kmla, vmla, q1, q2, dk, dkb, dv, dvb = stage1(xs, ms, tabs_mla_s, tabs_dif_s, False)
        kr_pad = jnp.pad(cache_mla_krope[l].reshape(bs * past, MLA_ROPE), ((0, 0), (0, LANES - MLA_ROPE)))
        kc, vc = _kvcache_expand(cache_mla_ckv[l].reshape(bs * past, -1), kr_pad, ukv, _block(bs * past, 512))
        r3 = lambda a: a.reshape(bs, ls, -1)
        c3 = lambda a: a.reshape(bs, past, -1)
        oa = _mla_sample(r3(q), c3(kc), c3(vc), r3(kmla), r3(vmla))
        call = lambda a: a.reshape(depth * bs, past * DIFF_HEADS, -1)
        ob = _diff_sample(r3(q1), r3(q2), call(cache_diff_k), call(cache_diff_v), r3(dkb), r3(dvb),
                          lams, gsub, lambda_init, l)
        xs = stage2(xs, h, oa.reshape(ms, -1), ob.reshape(ms, -1), ms, 512, 512)
        rows_s.append((ckv.reshape(bs, ls, -1), krope.reshape(bs, ls, -1),
                       dk.reshape(bs, ls, DIFF_HEADS, -1), dv.reshape(bs, ls, DIFF_HEADS, -1)))

    y_prompt = xp.reshape(bp, lp, d)
    y_sample = xs.reshape(bs, ls, d)
    stack = lambda rows, i: jnp.stack([r[i] for r in rows], axis=0)
    return (y_prompt, y_sample,
            stack(rows_p, 0), stack(rows_p, 1), stack(rows_p, 2), stack(rows_p, 3),
            stack(rows_s, 0), stack(rows_s, 1), stack(rows_s, 2), stack(rows_s, 3))
```

```python
import functools
import math

import jax
import jax.numpy as jnp
from jax import lax
from jax.experimental import pallas as pl
from jax.experimental.pallas import tpu as pltpu

F32 = jnp.float32
BF16 = jnp.bfloat16

CHUNK = 64
EPS = 1e-6
NEG_INF = -1e30

MLA_HEADS = 8
MLA_Q_LORA = 512
MLA_KV_LORA = 256
MLA_NOPE = 128
MLA_ROPE = 64
MLA_V = 128
MLA_THETA = 10000.0
MLA_QK_PAD = 256

DIFF_HEADS = 8
DIFF_DH = 64
DIFF_VD = 2 * DIFF_DH
DIFF_ROT = DIFF_DH // 4
ROPE_THETA = 500000.0

LANES = 128
VT_ONES = 16
LOG2E = math.log2(math.e)
VMEM_LIMIT = 56 * 1024 * 1024


def _params(*sem):
    return pltpu.CompilerParams(dimension_semantics=sem, vmem_limit_bytes=VMEM_LIMIT)


def _dot(a, b):
    return jnp.dot(a, b, preferred_element_type=F32)


def _dot_nt(a, b):
    return lax.dot_general(a, b, (((1,), (1,)), ((), ())), preferred_element_type=F32)


def _rms(x, g):
    return x * lax.rsqrt(jnp.mean(x * x, axis=-1, keepdims=True) + EPS) * g


def _rope_tile(t, c, s1, s2, half):
    return t * c + pltpu.roll(t, LANES - half, 1) * s1 + pltpu.roll(t, half, 1) * s2


def _put(ref, lo, val, tr):
    w = val.shape[1]
    if tr:
        ref[0, lo:lo + w, :] = val.T.astype(ref.dtype)
    else:
        ref[:, lo:lo + w] = val.astype(ref.dtype)


def _put_v(ref, hd, val, tr):
    bm, w = val.shape
    if tr:
        lo = hd * (w + VT_ONES)
        ref[0, lo:lo + w, :] = val.T.astype(ref.dtype)
        ref[0, lo + w:lo + w + VT_ONES, :] = jnp.ones((VT_ONES, bm), ref.dtype)
    else:
        ref[:, hd * w:(hd + 1) * w] = val.astype(ref.dtype)


def _v_width(heads, dv, tr):
    return heads * (dv + VT_ONES) if tr else heads * dv


def _out(m, n, bm, dtype, tr):
    if tr:
        return pl.BlockSpec((1, n, bm), lambda i: (i, 0, 0)), jax.ShapeDtypeStruct((m // bm, n, bm), dtype)
    return pl.BlockSpec((bm, n), lambda i: (i, 0)), jax.ShapeDtypeStruct((m, n), dtype)


def _qproj_body(h, wq_ref, gq_ref, wuq_ref, c, s1, s2, o_ref, scale, tr):
    qlat = _rms(_dot(h, wq_ref[...]), gq_ref[...]).astype(BF16)
    q = _dot(qlat, wuq_ref[...])
    for hd in range(MLA_HEADS):
        lo = hd * MLA_QK_PAD
        _put(o_ref, lo, q[:, lo:lo + LANES] * scale, tr)
        t = _rope_tile(q[:, lo + LANES:lo + 2 * LANES], c, s1, s2, MLA_ROPE // 2)
        _put(o_ref, lo + LANES, t * scale, tr)


def _kv_expand(ckv, krope_tile, wukv_ref, kmla_ref, vmla_ref, tr):
    kv = _dot(ckv.astype(BF16), wukv_ref[...])
    kr = krope_tile.astype(BF16)
    for hd in range(MLA_HEADS):
        lo = hd * MLA_QK_PAD
        kmla_ref[:, lo:lo + LANES] = kv[:, hd * MLA_NOPE:(hd + 1) * MLA_NOPE].astype(BF16)
        kmla_ref[:, lo + LANES:lo + 2 * LANES] = kr
        vlo = MLA_HEADS * MLA_NOPE + hd * MLA_V
        _put_v(vmla_ref, hd, kv[:, vlo:vlo + MLA_V], tr)


def _mla_proj_kernel(x_ref, gmix_ref, wq_ref, gq_ref, wuq_ref, wkv_ref, gkv_ref, wukv_ref, c_ref, s1_ref, s2_ref,
                     h_ref, q_ref, ckv_ref, krope_ref, kmla_ref, vmla_ref, *, scale, tr):
    h = _rms(x_ref[...], gmix_ref[...]).astype(BF16)
    h_ref[...] = h
    c, s1, s2 = c_ref[...], s1_ref[...], s2_ref[...]
    _qproj_body(h, wq_ref, gq_ref, wuq_ref, c, s1, s2, q_ref, scale, tr)
    z = _dot(h, wkv_ref[...])
    ckv = _rms(z[:, :MLA_KV_LORA], gkv_ref[...])
    ckv_ref[...] = ckv
    t = _rope_tile(z[:, MLA_KV_LORA:], c, s1, s2, MLA_ROPE // 2)
    krope_ref[...] = t[:, :MLA_ROPE]
    _kv_expand(ckv, t, wukv_ref, kmla_ref, vmla_ref, tr)


def _mla_proj(x, gmix, wq, gq, wuq, wkv, gkv, wukv, tabs, bm, scale, tr):
    m, d = x.shape
    nt = tabs[0].shape[0] // bm
    full = lambda a: pl.BlockSpec(a.shape, lambda i: (0, 0))
    tab = pl.BlockSpec((bm, LANES), lambda i: (i % nt, 0))
    row = lambda n: pl.BlockSpec((bm, n), lambda i: (i, 0))
    nk = MLA_HEADS * MLA_QK_PAD
    qspec, qshape = _out(m, wuq.shape[1], bm, BF16, tr)
    vspec, vshape = _out(m, _v_width(MLA_HEADS, MLA_V, tr), bm, BF16, tr)
    return pl.pallas_call(
        functools.partial(_mla_proj_kernel, scale=scale, tr=tr),
        grid=(m // bm,),
        in_specs=[row(d), full(gmix), full(wq), full(gq), full(wuq), full(wkv), full(gkv), full(wukv), tab, tab, tab],
        out_specs=[row(d), qspec, row(MLA_KV_LORA), row(MLA_ROPE), row(nk), vspec],
        out_shape=[jax.ShapeDtypeStruct((m, d), BF16), qshape,
                   jax.ShapeDtypeStruct((m, MLA_KV_LORA), F32), jax.ShapeDtypeStruct((m, MLA_ROPE), F32),
                   jax.ShapeDtypeStruct((m, nk), BF16), vshape],
        compiler_params=_params("parallel"),
        name="norm_mla_proj",
    )(x, gmix, wq, gq, wuq, wkv, gkv, wukv, *tabs)


def _kvcache_kernel(ckv_ref, krope_ref, wukv_ref, kmla_ref, vmla_ref):
    _kv_expand(ckv_ref[...], krope_ref[...], wukv_ref, kmla_ref, vmla_ref, False)


def _kvcache_expand(ckv, krope_pad, wukv, bm):
    m = ckv.shape[0]
    row = lambda n: pl.BlockSpec((bm, n), lambda i: (i, 0))
    nk, nv = MLA_HEADS * MLA_QK_PAD, MLA_HEADS * MLA_V
    return pl.pallas_call(
        _kvcache_kernel,
        grid=(m // bm,),
        in_specs=[row(MLA_KV_LORA), row(LANES), pl.BlockSpec(wukv.shape, lambda i: (0, 0))],
        out_specs=[row(nk), row(nv)],
        out_shape=[jax.ShapeDtypeStruct((m, nk), BF16), jax.ShapeDtypeStruct((m, nv), BF16)],
        compiler_params=_params("parallel"),
        name="mla_kv_cache_expand",
    )(ckv, krope_pad, wukv)


def _diff_proj_kernel(h_ref, wdq_ref, wdk_ref, wdv_ref, c_ref, s1_ref, s2_ref,
                      q1_ref, q2_ref, dk_ref, dkb_ref, dv_ref, dvb_ref, *, scale, tr):
    h = h_ref[...]
    zq = _dot(h, wdq_ref[...])
    zk = _dot(h, wdk_ref[...])
    zv = _dot(h, wdv_ref[...])
    dv_ref[...] = zv
    c, s1, s2 = c_ref[...], s1_ref[...], s2_ref[...]
    first = lax.broadcasted_iota(jnp.int32, (1, LANES), 1) < DIFF_DH
    for hd in range(DIFF_HEADS):
        sl = slice(hd * LANES, (hd + 1) * LANES)
        q = _rope_tile(zq[:, sl], c, s1, s2, DIFF_ROT // 2) * scale
        _put(q1_ref, hd * LANES, jnp.where(first, q, 0.0), tr)
        _put(q2_ref, hd * LANES, jnp.where(first, 0.0, q), tr)
        k = _rope_tile(zk[:, sl], c, s1, s2, DIFF_ROT // 2)
        dk_ref[:, sl] = k
        dkb_ref[:, sl] = k.astype(BF16)
        _put_v(dvb_ref, hd, zv[:, sl], tr)


def _diff_proj(h, wdq, wdk, wdv, tabs, bm, scale, tr):
    m, d = h.shape
    nt = tabs[0].shape[0] // bm
    n = wdq.shape[1]
    full = lambda a: pl.BlockSpec(a.shape, lambda i: (0, 0))
    tab = pl.BlockSpec((bm, LANES), lambda i: (i % nt, 0))
    row = lambda w: pl.BlockSpec((bm, w), lambda i: (i, 0))
    qspec, qshape = _out(m, n, bm, BF16, tr)
    vspec, vshape = _out(m, _v_width(DIFF_HEADS, DIFF_VD, tr), bm, BF16, tr)
    f32_rows, bf16_rows = jax.ShapeDtypeStruct((m, n), F32), jax.ShapeDtypeStruct((m, n), BF16)
    return pl.pallas_call(
        functools.partial(_diff_proj_kernel, scale=scale, tr=tr),
        grid=(m // bm,),
        in_specs=[row(d), full(wdq), full(wdk), full(wdv), tab, tab, tab],
        out_specs=[qspec, qspec, row(n), row(n), row(n), vspec],
        out_shape=[qshape, qshape, f32_rows, bf16_rows, f32_rows, vshape],
        compiler_params=_params("parallel"),
        name="diff_proj",
    )(h, wdq, wdk, wdv, *tabs)


def _merge_kernel(h_ref, oa_ref, ob_ref, wga_ref, wgb_ref, wa_ref, wb_ref, o_ref):
    h = h_ref[...]
    ga = jax.nn.sigmoid(_dot(h, wga_ref[...]))
    gb = jax.nn.sigmoid(_dot(h, wgb_ref[...]))
    ya = _dot(oa_ref[...], wa_ref[...])
    yb = _dot(ob_ref[...], wb_ref[...])
    o_ref[...] = (ga * ya + gb * yb).astype(o_ref.dtype)


def _merge(h, oa, ob, wga, wgb, wa, wb, bm, bn):
    m, d = h.shape
    n = wga.shape[1]
    row = lambda a: pl.BlockSpec((bm, a.shape[1]), lambda i, j: (i, 0))
    col = lambda a: pl.BlockSpec((a.shape[0], bn), lambda i, j: (0, j))
    return pl.pallas_call(
        _merge_kernel,
        grid=(m // bm, n // bn),
        in_specs=[row(h), row(oa), row(ob), col(wga), col(wgb), col(wa), col(wb)],
        out_specs=pl.BlockSpec((bm, bn), lambda i, j: (i, j)),
        out_shape=jax.ShapeDtypeStruct((m, n), BF16),
        compiler_params=_params("parallel", "arbitrary"),
        name="gated_merge",
    )(h, oa, ob, wga, wgb, wa, wb)


def _outproj_kernel(x_ref, mg_ref, wo_ref, g_ref, x2_ref, h2_ref):
    x2 = x_ref[...] + _dot(mg_ref[...], wo_ref[...])
    x2_ref[...] = x2
    h2_ref[...] = _rms(x2, g_ref[...]).astype(BF16)


def _outproj(x, mg, wo, g, bm):
    m, d = x.shape
    row = pl.BlockSpec((bm, d), lambda i: (i, 0))
    return pl.pallas_call(
        _outproj_kernel,
        grid=(m // bm,),
        in_specs=[row, row, pl.BlockSpec(wo.shape, lambda i: (0, 0)), pl.BlockSpec((1, d), lambda i: (0, 0))],
        out_specs=[row, row],
        out_shape=[jax.ShapeDtypeStruct((m, d), F32), jax.ShapeDtypeStruct((m, d), BF16)],
        compiler_params=_params("parallel"),
        name="out_proj_residual",
    )(x, mg, wo, g)


def _ffn_kernel(h2_ref, wg_ref, wu_ref, wd_ref, x2_ref, gf_ref, y_ref):
    f = pl.program_id(1)

    @pl.when(f == 0)
    def _():
        y_ref[...] = x2_ref[...]

    h2 = h2_ref[...]
    a = (jax.nn.silu(_dot(h2, wg_ref[...])) * _dot(h2, wu_ref[...])).astype(BF16)
    y_ref[...] += _dot(a, wd_ref[...])

    @pl.when(f == pl.num_programs(1) - 1)
    def _():
        y_ref[...] = _rms(y_ref[...], gf_ref[...])


def _ffn(h2, w_in, w_out, x2, gf, bm, bf):
    m, d = h2.shape
    dff = w_out.shape[0]
    nf = dff // bf
    row = pl.BlockSpec((bm, d), lambda i, f: (i, 0))
    return pl.pallas_call(
        _ffn_kernel,
        grid=(m // bm, nf),
        in_specs=[row,
                  pl.BlockSpec((d, bf), lambda i, f: (0, f)),
                  pl.BlockSpec((d, bf), lambda i, f: (0, f + nf)),
                  pl.BlockSpec((bf, d), lambda i, f: (f, 0)),
                  row,
                  pl.BlockSpec((1, d), lambda i, f: (0, 0))],
        out_specs=row,
        out_shape=jax.ShapeDtypeStruct((m, d), F32),
        compiler_params=_params("parallel", "arbitrary"),
        name="ffn_swiglu_final_norm",
    )(h2, w_in, w_in, w_out, x2, gf)


def _softmax_step(s, v, m, l, acc):
    m_new = jnp.maximum(m, jnp.max(s, axis=-1, keepdims=True))
    alpha = jnp.exp2(m - m_new)
    p = jnp.exp2(s - m_new)
    l = alpha * l + jnp.sum(p, axis=-1, keepdims=True)
    acc = alpha * acc + _dot(p.astype(BF16), v)
    return m_new, l, acc


def _chunk_mask(tq, tk, q0, k0):
    qp = q0 + lax.broadcasted_iota(jnp.int32, (tq, tk), 0)
    kp = k0 + lax.broadcasted_iota(jnp.int32, (tq, tk), 1)
    return (kp // CHUNK) <= (qp // CHUNK)


def _softmax_init(tq, dv):
    return (jnp.full((tq, 1), NEG_INF, F32), jnp.zeros((tq, 1), F32), jnp.zeros((tq, dv), F32))


def _chunk_mask_t(t):
    kp = lax.broadcasted_iota(jnp.int32, (t, t), 0)
    qp = lax.broadcasted_iota(jnp.int32, (t, t), 1)
    return (kp // CHUNK) <= (qp // CHUNK)


def _probs_t(s, smax, m):
    m_new = jnp.maximum(m, smax)
    return jnp.exp2(s - m_new).astype(BF16), jnp.exp2(m - m_new), m_new


def _attention_scratch(t, dv, streams):
    return [pltpu.VMEM((streams, 2, t, t), F32),
            pltpu.VMEM((streams, dv + VT_ONES, t), F32),
            pltpu.VMEM((streams, 8, t), F32)]


def _causal_attention_t(n, scores, next_scores, values, finish, s_ref, acc_ref, stat_ref, t, streams):
    row_m, row_smax = 0, 1

    def stat(st, r):
        return stat_ref[st, r:r + 1, :]

    def put_scores(st, slot, s):
        s_ref[st, slot] = s
        stat_ref[st, row_smax + slot:row_smax + slot + 1, :] = jnp.max(s, axis=0, keepdims=True)

    def accumulate(st, j, s, smax):
        p, alpha, m = _probs_t(s, smax, stat(st, row_m))
        acc = alpha * acc_ref[st] + _dot(values(j, st), p)
        stat_ref[st, row_m:row_m + 1, :] = m
        return acc

    @pl.when(n == 0)
    def _():
        for st in range(streams):
            put_scores(st, 0, scores(0, st))

    for st in range(streams):
        acc_ref[st] = jnp.zeros(acc_ref.shape[1:], F32)
        stat_ref[st, row_m:row_m + 1, :] = jnp.full((1, t), NEG_INF, F32)

    def run(first):
        def step(tau, par):
            for st in range(streams):
                put_scores(st, (first + par) % 2, scores(tau, st))
                prev = (first + par + 1) % 2
                acc_ref[st] = accumulate(st, tau - 1, s_ref[st, prev], stat(st, row_smax + prev))

        unroll = 4

        def trip(u, carry):
            for k in range(unroll):
                step(unroll * u + 1 + k, (1 + k) % 2)
            return carry

        lax.fori_loop(0, n // unroll, trip, 0)
        done = (n // unroll) * unroll

        @pl.when(n - done >= 2)
        def _():
            step(done + 1, 1)
            step(done + 2, 0)

        @pl.when(n % 2 == 1)
        def _():
            step(n, 1)

        def tail(par):
            diag = (first + par) % 2
            for st in range(streams):
                put_scores(st, 1 - diag, next_scores(st))
            mask = _chunk_mask_t(t)
            res = []
            for st in range(streams):
                s = jnp.where(mask, s_ref[st, diag], NEG_INF)
                res.append(accumulate(st, n, s, jnp.max(s, axis=0, keepdims=True)))
            finish(res)

        for par in (0, 1):
            pl.when(n % 2 == par)(functools.partial(tail, par))

    first = ((n + 1) // 2) % 2
    for f in (0, 1):
        pl.when(first == f)(functools.partial(run, f))


def _normalised(acc, dv):
    return (acc[:dv] / acc[dv:dv + 1]).T


def _mla_prompt_kernel(qt_ref, qn_ref, k_ref, vt_ref, o_ref, *scratch, t, heads):
    def block_scores(q_ref, j, hd):
        k = k_ref[0, pl.ds(pl.multiple_of(j * t, t), t), hd * MLA_QK_PAD:(hd + 1) * MLA_QK_PAD]
        return _dot(k, q_ref[0, hd * MLA_QK_PAD:(hd + 1) * MLA_QK_PAD, :])

    scores = functools.partial(block_scores, qt_ref)
    next_scores = functools.partial(block_scores, qn_ref, 0)

    def values(j, hd):
        return vt_ref[j, hd * (MLA_V + VT_ONES):(hd + 1) * (MLA_V + VT_ONES), :]

    def finish(res):
        for hd, acc in enumerate(res):
            o_ref[0, :, hd * MLA_V:(hd + 1) * MLA_V] = _normalised(acc, MLA_V).astype(o_ref.dtype)

    _causal_attention_t(pl.program_id(2), scores, next_scores, values, finish, *scratch, t, heads)


def _mla_prompt(qt, k, vt, t, heads):
    b, l, _ = k.shape
    nq = l // t
    qblock = lambda step: pl.BlockSpec((1, heads * MLA_QK_PAD, t),
                                       lambda b, h, i: (b * nq + jnp.minimum(i + step, nq - 1), h, 0))
    return pl.pallas_call(
        functools.partial(_mla_prompt_kernel, t=t, heads=heads),
        grid=(b, MLA_HEADS // heads, nq),
        in_specs=[qblock(0), qblock(1),
                  pl.BlockSpec((1, l, heads * MLA_QK_PAD), lambda b, h, i: (b, 0, h), pipeline_mode=pl.Buffered(1)),
                  pl.BlockSpec((nq, heads * (MLA_V + VT_ONES), t), lambda b, h, i: (b, h, 0),
                               pipeline_mode=pl.Buffered(1))],
        out_specs=pl.BlockSpec((1, t, heads * MLA_V), lambda b, h, i: (b, i, h)),
        out_shape=jax.ShapeDtypeStruct((b, l, MLA_HEADS * MLA_V), BF16),
        scratch_shapes=_attention_scratch(t, MLA_V, heads),
        compiler_params=_params("parallel", "parallel", "arbitrary"),
        name="mla_attention_prompt",
    )(qt, qt, k, vt)


def _lambda(lq1_ref, lk1_ref, lq2_ref, lk2_ref, lambda_init):
    a = jnp.sum(lq1_ref[...] * lk1_ref[...], axis=-1, keepdims=True)
    b = jnp.sum(lq2_ref[...] * lk2_ref[...], axis=-1, keepdims=True)
    return jnp.exp(a) - jnp.exp(b) + lambda_init


def _diff_finish(c1, c2, lam, g, lambda_init):
    o = c1[2] / c1[1] - lam * (c2[2] / c2[1])
    return _rms(o, g) * (1.0 - lambda_init)


def _diff_prompt_kernel(q1t_ref, q2t_ref, q1n_ref, q2n_ref, k_ref, vt_ref, lq1_ref, lk1_ref, lq2_ref, lk2_ref,
                        g_ref, o_ref, *scratch, t, heads, lambda_init):
    qt_refs, qn_refs = (q1t_ref, q2t_ref), (q1n_ref, q2n_ref)

    def block_scores(q_refs, rows, st):
        cols = slice((st // 2) * LANES, (st // 2 + 1) * LANES)
        return _dot(k_ref[0, rows, cols], q_refs[st % 2][0, cols, :])

    def scores(j, st):
        return block_scores(qt_refs, pl.ds(pl.multiple_of(j * t, t), t), st)

    def next_scores(st):
        return block_scores(qn_refs, slice(0, t), st)

    def values(j, st):
        return vt_ref[j, (st // 2) * (DIFF_VD + VT_ONES):(st // 2 + 1) * (DIFF_VD + VT_ONES), :]

    def finish(res):
        lam = _lambda(lq1_ref, lk1_ref, lq2_ref, lk2_ref, lambda_init)
        for hd in range(heads):
            o = _normalised(res[2 * hd], DIFF_VD) - lam * _normalised(res[2 * hd + 1], DIFF_VD)
            o = _rms(o, g_ref[...]) * (1.0 - lambda_init)
            o_ref[0, :, hd * DIFF_VD:(hd + 1) * DIFF_VD] = o.astype(o_ref.dtype)

    _causal_attention_t(pl.program_id(2), scores, next_scores, values, finish, *scratch, t, 2 * heads)


def _diff_prompt(q1t, q2t, k, vt, lams, g, t, heads, lambda_init):
    b, l, _ = k.shape
    nq = l // t
    qs = pl.BlockSpec((1, heads * LANES, t), lambda b, h, i: (b * nq + i, h, 0))
    qn = pl.BlockSpec((1, heads * LANES, t), lambda b, h, i: (b * nq + jnp.minimum(i + 1, nq - 1), h, 0))
    small = lambda a: pl.BlockSpec(a.shape, lambda b, h, i: (0, 0))
    return pl.pallas_call(
        functools.partial(_diff_prompt_kernel, t=t, heads=heads, lambda_init=lambda_init),
        grid=(b, DIFF_HEADS // heads, nq),
        in_specs=[qs, qs, qn, qn,
                  pl.BlockSpec((1, l, heads * LANES), lambda b, h, i: (b, 0, h)),
                  pl.BlockSpec((nq, heads * (DIFF_VD + VT_ONES), t), lambda b, h, i: (b, h, 0))]
                 + [small(a) for a in lams] + [small(g)],
        out_specs=pl.BlockSpec((1, t, heads * DIFF_VD), lambda b, h, i: (b, i, h)),
        out_shape=jax.ShapeDtypeStruct((b, l, DIFF_HEADS * DIFF_VD), BF16),
        scratch_shapes=_attention_scratch(t, DIFF_VD, 2 * heads),
        compiler_params=_params("parallel", "parallel", "arbitrary"),
        name="diff_attention_prompt",
    )(q1t, q2t, q1t, q2t, k, vt, *lams, g)


def _two_part_softmax(q, kc, vc, kn, vn, mask_c, mask_n):
    tq = q.shape[0]
    carry = _softmax_init(tq, vc.shape[1])
    carry = _softmax_step(jnp.where(mask_c, _dot_nt(q, kc), NEG_INF), vc, *carry)
    return _softmax_step(jnp.where(mask_n, _dot_nt(q, kn), NEG_INF), vn, *carry)


def _mla_sample_kernel(q_ref, kc_ref, vc_ref, kn_ref, vn_ref, o_ref, *, past):
    tq, tc = q_ref.shape[1], kc_ref.shape[1]
    mask_c, mask_n = _chunk_mask(tq, tc, past, 0), _chunk_mask(tq, tq, past, past)
    for hd in range(MLA_HEADS):
        qk = slice(hd * MLA_QK_PAD, (hd + 1) * MLA_QK_PAD)
        v = slice(hd * MLA_V, (hd + 1) * MLA_V)
        _, l, acc = _two_part_softmax(q_ref[0, :, qk], kc_ref[0, :, qk], vc_ref[0, :, v], kn_ref[0, :, qk],
                                      vn_ref[0, :, v], mask_c, mask_n)
        o_ref[0, :, v] = (acc / l).astype(o_ref.dtype)


def _mla_sample(q, kc, vc, kn, vn):
    b, tq, _ = q.shape
    blk = lambda a: pl.BlockSpec((1,) + a.shape[1:], lambda b: (b, 0, 0))
    return pl.pallas_call(
        functools.partial(_mla_sample_kernel, past=kc.shape[1]),
        grid=(b,),
        in_specs=[blk(q), blk(kc), blk(vc), blk(kn), blk(vn)],
        out_specs=pl.BlockSpec((1, tq, MLA_HEADS * MLA_V), lambda b: (b, 0, 0)),
        out_shape=jax.ShapeDtypeStruct((b, tq, MLA_HEADS * MLA_V), BF16),
        compiler_params=_params("parallel"),
        name="mla_attention_sample",
    )(q, kc, vc, kn, vn)


def _diff_sample_kernel(q1_ref, q2_ref, kc_ref, vc_ref, kn_ref, vn_ref,
                        lq1_ref, lk1_ref, lq2_ref, lk2_ref, g_ref, o_ref, *, past, lambda_init):
    tq = q1_ref.shape[1]
    mask_c, mask_n = _chunk_mask(tq, past, past, 0), _chunk_mask(tq, tq, past, past)
    lam = _lambda(lq1_ref, lk1_ref, lq2_ref, lk2_ref, lambda_init)
    for hd in range(DIFF_HEADS):
        rows = pl.ds(hd, past, stride=DIFF_HEADS)
        cols = slice(hd * LANES, (hd + 1) * LANES)
        kc, vc = kc_ref[0, rows, :].astype(BF16), vc_ref[0, rows, :].astype(BF16)
        kn, vn = kn_ref[0, :, cols], vn_ref[0, :, cols]
        c1 = _two_part_softmax(q1_ref[0, :, cols], kc, vc, kn, vn, mask_c, mask_n)
        c2 = _two_part_softmax(q2_ref[0, :, cols], kc, vc, kn, vn, mask_c, mask_n)
        o_ref[0, :, cols] = _diff_finish(c1, c2, lam, g_ref[...], lambda_init).astype(o_ref.dtype)


def _diff_sample(q1, q2, kc, vc, kn, vn, lams, g, lambda_init, layer):
    b, tq, _ = q1.shape
    past = kc.shape[1] // DIFF_HEADS
    blk = lambda a: pl.BlockSpec((1,) + a.shape[1:], lambda b: (b, 0, 0))
    cache = pl.BlockSpec((1,) + kc.shape[1:], lambda b: (layer * q1.shape[0] + b, 0, 0))
    small = lambda a: pl.BlockSpec(a.shape, lambda b: (0, 0))
    return pl.pallas_call(
        functools.partial(_diff_sample_kernel, past=past, lambda_init=lambda_init),
        grid=(b,),
        in_specs=[blk(q1), blk(q2), cache, cache, blk(kn), blk(vn)] + [small(a) for a in lams] + [small(g)],
        out_specs=blk(q1),
        out_shape=jax.ShapeDtypeStruct((b, tq, DIFF_HEADS * DIFF_VD), BF16),
        compiler_params=_params("parallel"),
        name="diff_attention_sample",
    )(q1, q2, kc, vc, kn, vn, *lams, g)


def _rope_tables(pos, theta, rot_dim, period):
    half = rot_dim // 2
    inv = 1.0 / (jnp.float32(theta) ** (jnp.arange(half, dtype=F32) / half))
    ang = pos.astype(F32)[:, None] * inv[None, :]
    cos, sin = jnp.cos(ang), jnp.sin(ang)
    n = pos.shape[0]
    rest = period - rot_dim
    c = jnp.concatenate([cos, cos, jnp.ones((n, rest), F32)], axis=1)
    s1 = jnp.concatenate([-sin, jnp.zeros((n, half + rest), F32)], axis=1)
    s2 = jnp.concatenate([jnp.zeros((n, half), F32), sin, jnp.zeros((n, rest), F32)], axis=1)
    reps = LANES // period
    return tuple(jnp.tile(a, (1, reps)) for a in (c, s1, s2))


def _layer_weights(w_in, mla_w_uq, mla_w_ukv):
    c_kv = MLA_Q_LORA
    c_kr = c_kv + MLA_KV_LORA
    c_dq = c_kr + MLA_ROPE
    dqk = DIFF_HEADS * 2 * DIFF_DH
    c_dk = c_dq + dqk
    c_dv = c_dk + dqk
    c_ga = c_dv + DIFF_HEADS * DIFF_VD
    d = w_in.shape[0]
    c_gb = c_ga + d
    cols = lambda lo, hi: w_in[:, lo:hi].astype(BF16)
    wq = cols(0, c_kv)
    wkv = jnp.pad(cols(c_kv, c_dq), ((0, 0), (0, LANES - MLA_ROPE)))
    wdq, wdk, wdv = cols(c_dq, c_dk), cols(c_dk, c_dv), cols(c_dv, c_ga)
    wga, wgb = cols(c_ga, c_gb), cols(c_gb, w_in.shape[1])
    uq = mla_w_uq.astype(BF16).reshape(MLA_Q_LORA, MLA_HEADS, MLA_NOPE + MLA_ROPE)
    uq = jnp.pad(uq, ((0, 0), (0, 0), (0, MLA_QK_PAD - MLA_NOPE - MLA_ROPE))).reshape(MLA_Q_LORA, -1)
    ukv = mla_w_ukv.astype(BF16).reshape(MLA_KV_LORA, MLA_HEADS, MLA_NOPE + MLA_V)
    ukv = jnp.concatenate([ukv[:, :, :MLA_NOPE].reshape(MLA_KV_LORA, -1),
                           ukv[:, :, MLA_NOPE:].reshape(MLA_KV_LORA, -1)], axis=1)
    return wq, wkv, wdq, wdk, wdv, wga, wgb, uq, ukv


def _block(m, want):
    return want if m % want == 0 else m


def kernel(x_prompt, x_sample, cache_mla_ckv, cache_mla_krope, cache_diff_k, cache_diff_v, norm_mix, w_in,
           mla_q_norm, mla_w_uq, mla_kv_norm, mla_w_ukv, diff_lq1, diff_lk1, diff_lq2, diff_lk2, diff_subln,
           w_branch_a, w_branch_b, w_out, norm_ffn, w_ffn_in, w_ffn_out, norm_final):
    bp, lp, d = x_prompt.shape
    bs, ls, _ = x_sample.shape
    depth, _, past, _ = cache_mla_ckv.shape
    assert depth == 1, "the FFN kernel fuses the final norm, so it serves the last (only) layer"
    mp, ms = bp * lp, bs * ls
    t_attn = 512

    pos_p = jnp.arange(lp)
    pos_s = past + jnp.arange(ls)
    tabs_mla_p = _rope_tables(pos_p, MLA_THETA, MLA_ROPE, LANES)
    tabs_mla_s = tuple(jnp.tile(a, (bs, 1)) for a in _rope_tables(pos_s, MLA_THETA, MLA_ROPE, LANES))
    tabs_dif_p = _rope_tables(pos_p, ROPE_THETA, DIFF_ROT, DIFF_DH)
    tabs_dif_s = tuple(jnp.tile(a, (bs, 1)) for a in _rope_tables(pos_s, ROPE_THETA, DIFF_ROT, DIFF_DH))
    q_scale_mla = (MLA_NOPE + MLA_ROPE) ** -0.5 * LOG2E
    q_scale_dif = DIFF_DH ** -0.5 * LOG2E
    gfinal = norm_final.reshape(1, d)

    xp = x_prompt.reshape(mp, d)
    xs = x_sample.reshape(ms, d)
    rows_p, rows_s = [], []
    for l in range(depth):
        lambda_init = 0.8 - 0.6 * math.exp(-0.3 * l)
        wq, wkv, wdq, wdk, wdv, wga, wgb, uq, ukv = _layer_weights(w_in[l], mla_w_uq[l], mla_w_ukv[l])
        gmix, gq, gkv = norm_mix[l].reshape(1, -1), mla_q_norm[l].reshape(1, -1), mla_kv_norm[l].reshape(1, -1)
        lams = tuple(a[l].reshape(1, -1) for a in (diff_lq1, diff_lk1, diff_lq2, diff_lk2))
        gsub = diff_subln[l].reshape(1, -1)
        wa, wbr, wo = w_branch_a[l].astype(BF16), w_branch_b[l].astype(BF16), w_out[l].astype(BF16)
        wfi, wfo = w_ffn_in[l].astype(BF16), w_ffn_out[l].astype(BF16)
        gffn = norm_ffn[l].reshape(1, -1)

        def stage1(x, bm, tabs_mla, tabs_dif, tr):
            h, q, ckv, krope, kmla, vmla = _mla_proj(x, gmix, wq, gq, uq, wkv, gkv, ukv, tabs_mla, bm,
                                                     q_scale_mla, tr)
            q1, q2, dk, dkb, dv, dvb = _diff_proj(h, wdq, wdk, wdv, tabs_dif, bm, q_scale_dif, tr)
            return h, q, ckv, krope, kmla, vmla, q1, q2, dk, dkb, dv, dvb

        def stage2(x, h, oa, ob, bm, bn, bf):
            mg = _merge(h, oa, ob, wga, wgb, wa, wbr, bm, bn)
            x2, h2 = _outproj(x, mg, wo, gffn, min(bm, 512))
            return _ffn(h2, wfi, wfo, x2, gfinal, bm, bf)

        h, q, ckv, krope, kmla, vmla, q1, q2, dk, dkb, dv, dvb = stage1(xp, t_attn, tabs_mla_p, tabs_dif_p, True)
        r3 = lambda a: a.reshape(bp, lp, -1)
        oa = _mla_prompt(q, r3(kmla), vmla, t_attn, 4)
        ob = _diff_prompt(q1, q2, r3(dkb), dvb, lams, gsub, t_attn, 2, lambda_init)
        xp = stage2(xp, h, oa.reshape(mp, -1), ob.reshape(mp, -1), _block(mp, 1024), 512, 256)
        rows_p.append((ckv.reshape(bp, lp, -1), krope.reshape(bp, lp, -1),
                       dk.reshape(bp, lp, DIFF_HEADS, -1), dv.reshape(bp, lp, DIFF_HEADS, -1)))

        h, q, ckv, krope, kmla, vmla, q1, q2, dk, dkb, dv, dvb = stage1(xs, ms, tabs_mla_s, tabs_dif_s, False)
        kr_pad = jnp.pad(cache_mla_krope[l].reshape(bs * past, MLA_ROPE), ((0, 0), (0, LANES - MLA_ROPE)))
        kc, vc = _kvcache_expand(cache_mla_ckv[l].reshape(bs * past, -1), kr_pad, ukv, _block(bs * past, 512))
        r3 = lambda a: a.reshape(bs, ls, -1)
        c3 = lambda a: a.reshape(bs, past, -1)
        oa = _mla_sample(r3(q), c3(kc), c3(vc), r3(kmla), r3(vmla))
        call = lambda a: a.reshape(depth * bs, past * DIFF_HEADS, -1)
        ob = _diff_sample(r3(q1), r3(q2), call(cache_diff_k), call(cache_diff_v), r3(dkb), r3(dvb),
                          lams, gsub, lambda_init, l)
        xs = stage2(xs, h, oa.reshape(ms, -1), ob.reshape(ms, -1), ms, 512, 512)
        rows_s.append((ckv.reshape(bs, ls, -1), krope.reshape(bs, ls, -1),
                       dk.reshape(bs, ls, DIFF_HEADS, -1), dv.reshape(bs, ls, DIFF_HEADS, -1)))

    y_prompt = xp.reshape(bp, lp, d)
    y_sample = xs.reshape(bs, ls, d)
    stack = lambda rows, i: jnp.stack([r[i] for r in rows], axis=0)
    return (y_prompt, y_sample,
            stack(rows_p, 0), stack(rows_p, 1), stack(rows_p, 2), stack(rows_p, 3),
            stack(rows_s, 0), stack(rows_s, 1), stack(rows_s, 2), stack(rows_s, 3))
```

```python
import functools
import math

import jax
import jax.numpy as jnp
from jax import lax
from jax.experimental import pallas as pl
from jax.experimental.pallas import tpu as pltpu

F32 = jnp.float32
BF16 = jnp.bfloat16

CHUNK = 64
EPS = 1e-6
NEG_INF = -1e30

MLA_HEADS = 8
MLA_Q_LORA = 512
MLA_KV_LORA = 256
MLA_NOPE = 128
MLA_ROPE = 64
MLA_V = 128
MLA_THETA = 10000.0
MLA_QK_PAD = 256

DIFF_HEADS = 8
DIFF_DH = 64
DIFF_VD = 2 * DIFF_DH
DIFF_ROT = DIFF_DH // 4
ROPE_THETA = 500000.0

LANES = 128
VT_ONES = 16
LOG2E = math.log2(math.e)
VMEM_LIMIT = 56 * 1024 * 1024


def _params(*sem):
    return pltpu.CompilerParams(dimension_semantics=sem, vmem_limit_bytes=VMEM_LIMIT)


def _dot(a, b):
    return jnp.dot(a, b, preferred_element_type=F32)


def _dot_nt(a, b):
    return lax.dot_general(a, b, (((1,), (1,)), ((), ())), preferred_element_type=F32)


def _rms(x, g):
    return x * lax.rsqrt(jnp.mean(x * x, axis=-1, keepdims=True) + EPS) * g


def _rope_tile(t, c, s1, s2, half):
    return t * c + pltpu.roll(t, LANES - half, 1) * s1 + pltpu.roll(t, half, 1) * s2


def _put(ref, lo, val, tr):
    w = val.shape[1]
    if tr:
        ref[0, lo:lo + w, :] = val.T.astype(ref.dtype)
    else:
        ref[:, lo:lo + w] = val.astype(ref.dtype)


def _put_v(ref, hd, val, tr):
    bm, w = val.shape
    if tr:
        lo = hd * (w + VT_ONES)
        ref[0, lo:lo + w, :] = val.T.astype(ref.dtype)
        ref[0, lo + w:lo + w + VT_ONES, :] = jnp.ones((VT_ONES, bm), ref.dtype)
    else:
        ref[:, hd * w:(hd + 1) * w] = val.astype(ref.dtype)


def _v_width(heads, dv, tr):
    return heads * (dv + VT_ONES) if tr else heads * dv


def _out(m, n, bm, dtype, tr):
    if tr:
        return pl.BlockSpec((1, n, bm), lambda i: (i, 0, 0)), jax.ShapeDtypeStruct((m // bm, n, bm), dtype)
    return pl.BlockSpec((bm, n), lambda i: (i, 0)), jax.ShapeDtypeStruct((m, n), dtype)


def _qproj_body(h, wq_ref, gq_ref, wuq_ref, c, s1, s2, o_ref, scale, tr):
    qlat = _rms(_dot(h, wq_ref[...]), gq_ref[...]).astype(BF16)
    q = _dot(qlat, wuq_ref[...])
    for hd in range(MLA_HEADS):
        lo = hd * MLA_QK_PAD
        _put(o_ref, lo, q[:, lo:lo + LANES] * scale, tr)
        t = _rope_tile(q[:, lo + LANES:lo + 2 * LANES], c, s1, s2, MLA_ROPE // 2)
        _put(o_ref, lo + LANES, t * scale, tr)


def _kv_expand(ckv, krope_tile, wukv_ref, kmla_ref, vmla_ref, tr):
    kv = _dot(ckv.astype(BF16), wukv_ref[...])
    kr = krope_tile.astype(BF16)
    for hd in range(MLA_HEADS):
        lo = hd * MLA_QK_PAD
        kmla_ref[:, lo:lo + LANES] = kv[:, hd * MLA_NOPE:(hd + 1) * MLA_NOPE].astype(BF16)
        kmla_ref[:, lo + LANES:lo + 2 * LANES] = kr
        vlo = MLA_HEADS * MLA_NOPE + hd * MLA_V
        _put_v(vmla_ref, hd, kv[:, vlo:vlo + MLA_V], tr)


def _mla_proj_kernel(x_ref, gmix_ref, wq_ref, gq_ref, wuq_ref, wkv_ref, gkv_ref, wukv_ref, c_ref, s1_ref, s2_ref,
                     h_ref, q_ref, ckv_ref, krope_ref, kmla_ref, vmla_ref, *, scale, tr):
    h = _rms(x_ref[...], gmix_ref[...]).astype(BF16)
    h_ref[...] = h
    c, s1, s2 = c_ref[...], s1_ref[...], s2_ref[...]
    _qproj_body(h, wq_ref, gq_ref, wuq_ref, c, s1, s2, q_ref, scale, tr)
    z = _dot(h, wkv_ref[...])
    ckv = _rms(z[:, :MLA_KV_LORA], gkv_ref[...])
    ckv_ref[...] = ckv
    t = _rope_tile(z[:, MLA_KV_LORA:], c, s1, s2, MLA_ROPE // 2)
    krope_ref[...] = t[:, :MLA_ROPE]
    _kv_expand(ckv, t, wukv_ref, kmla_ref, vmla_ref, tr)


def _mla_proj(x, gmix, wq, gq, wuq, wkv, gkv, wukv, tabs, bm, scale, tr):
    m, d = x.shape
    nt = tabs[0].shape[0] // bm
    full = lambda a: pl.BlockSpec(a.shape, lambda i: (0, 0))
    tab = pl.BlockSpec((bm, LANES), lambda i: (i % nt, 0))
    row = lambda n: pl.BlockSpec((bm, n), lambda i: (i, 0))
    nk = MLA_HEADS * MLA_QK_PAD
    qspec, qshape = _out(m, wuq.shape[1], bm, BF16, tr)
    vspec, vshape = _out(m, _v_width(MLA_HEADS, MLA_V, tr), bm, BF16, tr)
    return pl.pallas_call(
        functools.partial(_mla_proj_kernel, scale=scale, tr=tr),
        grid=(m // bm,),
        in_specs=[row(d), full(gmix), full(wq), full(gq), full(wuq), full(wkv), full(gkv), full(wukv), tab, tab, tab],
        out_specs=[row(d), qspec, row(MLA_KV_LORA), row(MLA_ROPE), row(nk), vspec],
        out_shape=[jax.ShapeDtypeStruct((m, d), BF16), qshape,
                   jax.ShapeDtypeStruct((m, MLA_KV_LORA), F32), jax.ShapeDtypeStruct((m, MLA_ROPE), F32),
                   jax.ShapeDtypeStruct((m, nk), BF16), vshape],
        compiler_params=_params("parallel"),
        name="norm_mla_proj",
    )(x, gmix, wq, gq, wuq, wkv, gkv, wukv, *tabs)


def _kvcache_kernel(ckv_ref, krope_ref, wukv_ref, kmla_ref, vmla_ref):
    _kv_expand(ckv_ref[...], krope_ref[...], wukv_ref, kmla_ref, vmla_ref, False)


def _kvcache_expand(ckv, krope_pad, wukv, bm):
    m = ckv.shape[0]
    row = lambda n: pl.BlockSpec((bm, n), lambda i: (i, 0))
    nk, nv = MLA_HEADS * MLA_QK_PAD, MLA_HEADS * MLA_V
    return pl.pallas_call(
        _kvcache_kernel,
        grid=(m // bm,),
        in_specs=[row(MLA_KV_LORA), row(LANES), pl.BlockSpec(wukv.shape, lambda i: (0, 0))],
        out_specs=[row(nk), row(nv)],
        out_shape=[jax.ShapeDtypeStruct((m, nk), BF16), jax.ShapeDtypeStruct((m, nv), BF16)],
        compiler_params=_params("parallel"),
        name="mla_kv_cache_expand",
    )(ckv, krope_pad, wukv)


def _diff_proj_kernel(h_ref, wdq_ref, wdk_ref, wdv_ref, c_ref, s1_ref, s2_ref,
                      q1_ref, q2_ref, dk_ref, dkb_ref, dv_ref, dvb_ref, *, scale, tr):
    h = h_ref[...]
    zq = _dot(h, wdq_ref[...])
    zk = _dot(h, wdk_ref[...])
    zv = _dot(h, wdv_ref[...])
    dv_ref[...] = zv
    c, s1, s2 = c_ref[...], s1_ref[...], s2_ref[...]
    first = lax.broadcasted_iota(jnp.int32, (1, LANES), 1) < DIFF_DH
    for hd in range(DIFF_HEADS):
        sl = slice(hd * LANES, (hd + 1) * LANES)
        q = _rope_tile(zq[:, sl], c, s1, s2, DIFF_ROT // 2) * scale
        _put(q1_ref, hd * LANES, jnp.where(first, q, 0.0), tr)
        _put(q2_ref, hd * LANES, jnp.where(first, 0.0, q), tr)
        k = _rope_tile(zk[:, sl], c, s1, s2, DIFF_ROT // 2)
        dk_ref[:, sl] = k
        dkb_ref[:, sl] = k.astype(BF16)
        _put_v(dvb_ref, hd, zv[:, sl], tr)


def _diff_proj(h, wdq, wdk, wdv, tabs, bm, scale, tr):
    m, d = h.shape
    nt = tabs[0].shape[0] // bm
    n = wdq.shape[1]
    full = lambda a: pl.BlockSpec(a.shape, lambda i: (0, 0))
    tab = pl.BlockSpec((bm, LANES), lambda i: (i % nt, 0))
    row = lambda w: pl.BlockSpec((bm, w), lambda i: (i, 0))
    qspec, qshape = _out(m, n, bm, BF16, tr)
    vspec, vshape = _out(m, _v_width(DIFF_HEADS, DIFF_VD, tr), bm, BF16, tr)
    f32_rows, bf16_rows = jax.ShapeDtypeStruct((m, n), F32), jax.ShapeDtypeStruct((m, n), BF16)
    return pl.pallas_call(
        functools.partial(_diff_proj_kernel, scale=scale, tr=tr),
        grid=(m // bm,),
        in_specs=[row(d), full(wdq), full(wdk), full(wdv), tab, tab, tab],
        out_specs=[qspec, qspec, row(n), row(n), row(n), vspec],
        out_shape=[qshape, qshape, f32_rows, bf16_rows, f32_rows, vshape],
        compiler_params=_params("parallel"),
        name="diff_proj",
    )(h, wdq, wdk, wdv, *tabs)


def _merge_kernel(h_ref, oa_ref, ob_ref, wga_ref, wgb_ref, wa_ref, wb_ref, o_ref):
    h = h_ref[...]
    ga = jax.nn.sigmoid(_dot(h, wga_ref[...]))
    gb = jax.nn.sigmoid(_dot(h, wgb_ref[...]))
    ya = _dot(oa_ref[...], wa_ref[...])
    yb = _dot(ob_ref[...], wb_ref[...])
    o_ref[...] = (ga * ya + gb * yb).astype(o_ref.dtype)


def _merge(h, oa, ob, wga, wgb, wa, wb, bm, bn):
    m, d = h.shape
    n = wga.shape[1]
    row = lambda a: pl.BlockSpec((bm, a.shape[1]), lambda i, j: (i, 0))
    col = lambda a: pl.BlockSpec((a.shape[0], bn), lambda i, j: (0, j))
    return pl.pallas_call(
        _merge_kernel,
        grid=(m // bm, n // bn),
        in_specs=[row(h), row(oa), row(ob), col(wga), col(wgb), col(wa), col(wb)],
        out_specs=pl.BlockSpec((bm, bn), lambda i, j: (i, j)),
        out_shape=jax.ShapeDtypeStruct((m, n), BF16),
        compiler_params=_params("parallel", "arbitrary"),
        name="gated_merge",
    )(h, oa, ob, wga, wgb, wa, wb)


def _outproj_kernel(x_ref, mg_ref, wo_ref, g_ref, x2_ref, h2_ref):
    x2 = x_ref[...] + _dot(mg_ref[...], wo_ref[...])
    x2_ref[...] = x2
    h2_ref[...] = _rms(x2, g_ref[...]).astype(BF16)


def _outproj(x, mg, wo, g, bm):
    m, d = x.shape
    row = pl.BlockSpec((bm, d), lambda i: (i, 0))
    return pl.pallas_call(
        _outproj_kernel,
        grid=(m // bm,),
        in_specs=[row, row, pl.BlockSpec(wo.shape, lambda i: (0, 0)), pl.BlockSpec((1, d), lambda i: (0, 0))],
        out_specs=[row, row],
        out_shape=[jax.ShapeDtypeStruct((m, d), F32), jax.ShapeDtypeStruct((m, d), BF16)],
        compiler_params=_params("parallel"),
        name="out_proj_residual",
    )(x, mg, wo, g)


def _ffn_kernel(h2_ref, wg_ref, wu_ref, wd_ref, x2_ref, gf_ref, y_ref):
    f = pl.program_id(1)

    @pl.when(f == 0)
    def _():
        y_ref[...] = x2_ref[...]

    h2 = h2_ref[...]
    a = (jax.nn.silu(_dot(h2, wg_ref[...])) * _dot(h2, wu_ref[...])).astype(BF16)
    y_ref[...] += _dot(a, wd_ref[...])

    @pl.when(f == pl.num_programs(1) - 1)
    def _():
        y_ref[...] = _rms(y_ref[...], gf_ref[...])


def _ffn(h2, w_in, w_out, x2, gf, bm, bf):
    m, d = h2.shape
    dff = w_out.shape[0]
    nf = dff // bf
    row = pl.BlockSpec((bm, d), lambda i, f: (i, 0))
    return pl.pallas_call(
        _ffn_kernel,
        grid=(m // bm, nf),
        in_specs=[row,
                  pl.BlockSpec((d, bf), lambda i, f: (0, f)),
                  pl.BlockSpec((d, bf), lambda i, f: (0, f + nf)),
                  pl.BlockSpec((bf, d), lambda i, f: (f, 0)),
                  row,
                  pl.BlockSpec((1, d), lambda i, f: (0, 0))],
        out_specs=row,
        out_shape=jax.ShapeDtypeStruct((m, d), F32),
        compiler_params=_params("parallel", "arbitrary"),
        name="ffn_swiglu_final_norm",
    )(h2, w_in, w_in, w_out, x2, gf)


def _softmax_step(s, v, m, l, acc):
    m_new = jnp.maximum(m, jnp.max(s, axis=-1, keepdims=True))
    alpha = jnp.exp2(m - m_new)
    p = jnp.exp2(s - m_new)
    l = alpha * l + jnp.sum(p, axis=-1, keepdims=True)
    acc = alpha * acc + _dot(p.astype(BF16), v)
    return m_new, l, acc


def _chunk_mask(tq, tk, q0, k0):
    qp = q0 + lax.broadcasted_iota(jnp.int32, (tq, tk), 0)
    kp = k0 + lax.broadcasted_iota(jnp.int32, (tq, tk), 1)
    return (kp // CHUNK) <= (qp // CHUNK)


def _softmax_init(tq, dv):
    return (jnp.full((tq, 1), NEG_INF, F32), jnp.zeros((tq, 1), F32), jnp.zeros((tq, dv), F32))


def _chunk_mask_t(t):
    kp = lax.broadcasted_iota(jnp.int32, (t, t), 0)
    qp = lax.broadcasted_iota(jnp.int32, (t, t), 1)
    return (kp // CHUNK) <= (qp // CHUNK)


def _probs_t(s, smax, m):
    m_new = jnp.maximum(m, smax)
    return jnp.exp2(s - m_new).astype(BF16), jnp.exp2(m - m_new), m_new


def _attention_scratch(t, dv, streams):
    return [pltpu.VMEM((streams, 2, t, t), F32),
            pltpu.VMEM((streams, dv + VT_ONES, t), F32),
            pltpu.VMEM((streams, 8, t), F32)]


def _causal_attention_t(n, scores, next_scores, values, finish, s_ref, acc_ref, stat_ref, t, streams):
    row_m, row_smax = 0, 1

    def stat(st, r):
        return stat_ref[st, r:r + 1, :]

    def put_scores(st, slot, s):
        s_ref[st, slot] = s
        stat_ref[st, row_smax + slot:row_smax + slot + 1, :] = jnp.max(s, axis=0, keepdims=True)

    def accumulate(st, j, s, smax):
        p, alpha, m = _probs_t(s, smax, stat(st, row_m))
        acc = alpha * acc_ref[st] + _dot(values(j, st), p)
        stat_ref[st, row_m:row_m + 1, :] = m
        return acc

    @pl.when(n == 0)
    def _():
        for st in range(streams):
            put_scores(st, 0, scores(0, st))

    for st in range(streams):
        acc_ref[st] = jnp.zeros(acc_ref.shape[1:], F32)
        stat_ref[st, row_m:row_m + 1, :] = jnp.full((1, t), NEG_INF, F32)

    def run(first):
        def step(tau, par):
            for st in range(streams):
                put_scores(st, (first + par) % 2, scores(tau, st))
                prev = (first + par + 1) % 2
                acc_ref[st] = accumulate(st, tau - 1, s_ref[st, prev], stat(st, row_smax + prev))

        unroll = 4

        def trip(u, carry):
            for k in range(unroll):
                step(unroll * u + 1 + k, (1 + k) % 2)
            return carry

        lax.fori_loop(0, n // unroll, trip, 0)
        done = (n // unroll) * unroll

        @pl.when(n - done >= 2)
        def _():
            step(done + 1, 1)
            step(done + 2, 0)

        @pl.when(n % 2 == 1)
        def _():
            step(n, 1)

        def tail(par):
            diag = (first + par) % 2
            for st in range(streams):
                put_scores(st, 1 - diag, next_scores(st))
            mask = _chunk_mask_t(t)
            res = []
            for st in range(streams):
                s = jnp.where(mask, s_ref[st, diag], NEG_INF)
                res.append(accumulate(st, n, s, jnp.max(s, axis=0, keepdims=True)))
            finish(res)

        for par in (0, 1):
            pl.when(n % 2 == par)(functools.partial(tail, par))

    first = ((n + 1) // 2) % 2
    for f in (0, 1):
        pl.when(first == f)(functools.partial(run, f))


def _normalised(acc, dv):
    return (acc[:dv] / acc[dv:dv + 1]).T


def _mla_prompt_kernel(qt_ref, qn_ref, k_ref, vt_ref, o_ref, *scratch, t, heads):
    def block_scores(q_ref, j, hd):
        k = k_ref[0, pl.ds(pl.multiple_of(j * t, t), t), hd * MLA_QK_PAD:(hd + 1) * MLA_QK_PAD]
        return _dot(k, q_ref[0, hd * MLA_QK_PAD:(hd + 1) * MLA_QK_PAD, :])

    scores = functools.partial(block_scores, qt_ref)
    next_scores = functools.partial(block_scores, qn_ref, 0)

    def values(j, hd):
        return vt_ref[j, hd * (MLA_V + VT_ONES):(hd + 1) * (MLA_V + VT_ONES), :]

    def finish(res):
        for hd, acc in enumerate(res):
            o_ref[0, :, hd * MLA_V:(hd + 1) * MLA_V] = _normalised(acc, MLA_V).astype(o_ref.dtype)

    _causal_attention_t(pl.program_id(2), scores, next_scores, values, finish, *scratch, t, heads)


def _mla_prompt(qt, k, vt, t, heads):
    b, l, _ = k.shape
    nq = l // t
    qblock = lambda step: pl.BlockSpec((1, heads * MLA_QK_PAD, t),
                                       lambda b, h, i: (b * nq + jnp.minimum(i + step, nq - 1), h, 0))
    return pl.pallas_call(
        functools.partial(_mla_prompt_kernel, t=t, heads=heads),
        grid=(b, MLA_HEADS // heads, nq),
        in_specs=[qblock(0), qblock(1),
                  pl.BlockSpec((1, l, heads * MLA_QK_PAD), lambda b, h, i: (b, 0, h), pipeline_mode=pl.Buffered(1)),
                  pl.BlockSpec((nq, heads * (MLA_V + VT_ONES), t), lambda b, h, i: (b, h, 0),
                               pipeline_mode=pl.Buffered(1))],
        out_specs=pl.BlockSpec((1, t, heads * MLA_V), lambda b, h, i: (b, i, h)),
        out_shape=jax.ShapeDtypeStruct((b, l, MLA_HEADS * MLA_V), BF16),
        scratch_shapes=_attention_scratch(t, MLA_V, heads),
        compiler_params=_params("parallel", "parallel", "arbitrary"),
        name="mla_attention_prompt",
    )(qt, qt, k, vt)


def _lambda(lq1_ref, lk1_ref, lq2_ref, lk2_ref, lambda_init):
    a = jnp.sum(lq1_ref[...] * lk1_ref[...], axis=-1, keepdims=True)
    b = jnp.sum(lq2_ref[...] * lk2_ref[...], axis=-1, keepdims=True)
    return jnp.exp(a) - jnp.exp(b) + lambda_init


def _diff_finish(c1, c2, lam, g, lambda_init):
    o = c1[2] / c1[1] - lam * (c2[2] / c2[1])
    return _rms(o, g) * (1.0 - lambda_init)


def _diff_prompt_kernel(q1t_ref, q2t_ref, q1n_ref, q2n_ref, k_ref, vt_ref, lq1_ref, lk1_ref, lq2_ref, lk2_ref,
                        g_ref, o_ref, *scratch, t, heads, lambda_init):
    qt_refs, qn_refs = (q1t_ref, q2t_ref), (q1n_ref, q2n_ref)

    def block_scores(q_refs, rows, st):
        cols = slice((st // 2) * LANES, (st // 2 + 1) * LANES)
        return _dot(k_ref[0, rows, cols], q_refs[st % 2][0, cols, :])

    def scores(j, st):
        return block_scores(qt_refs, pl.ds(pl.multiple_of(j * t, t), t), st)

    def next_scores(st):
        return block_scores(qn_refs, slice(0, t), st)

    def values(j, st):
        return vt_ref[j, (st // 2) * (DIFF_VD + VT_ONES):(st // 2 + 1) * (DIFF_VD + VT_ONES), :]

    def finish(res):
        lam = _lambda(lq1_ref, lk1_ref, lq2_ref, lk2_ref, lambda_init)
        for hd in range(heads):
            o = _normalised(res[2 * hd], DIFF_VD) - lam * _normalised(res[2 * hd + 1], DIFF_VD)
            o = _rms(o, g_ref[...]) * (1.0 - lambda_init)
            o_ref[0, :, hd * DIFF_VD:(hd + 1) * DIFF_VD] = o.astype(o_ref.dtype)

    _causal_attention_t(pl.program_id(2), scores, next_scores, values, finish, *scratch, t, 2 * heads)


def _diff_prompt(q1t, q2t, k, vt, lams, g, t, heads, lambda_init):
    b, l, _ = k.shape
    nq = l // t
    qs = pl.BlockSpec((1, heads * LANES, t), lambda b, h, i: (b * nq + i, h, 0))
    qn = pl.BlockSpec((1, heads * LANES, t), lambda b, h, i: (b * nq + jnp.minimum(i + 1, nq - 1), h, 0))
    small = lambda a: pl.BlockSpec(a.shape, lambda b, h, i: (0, 0))
    return pl.pallas_call(
        functools.partial(_diff_prompt_kernel, t=t, heads=heads, lambda_init=lambda_init),
        grid=(b, DIFF_HEADS // heads, nq),
        in_specs=[qs, qs, qn, qn,
                  pl.BlockSpec((1, l, heads * LANES), lambda b, h, i: (b, 0, h)),
                  pl.BlockSpec((nq, heads * (DIFF_VD + VT_ONES), t), lambda b, h, i: (b, h, 0))]
                 + [small(a) for a in lams] + [small(g)],
        out_specs=pl.BlockSpec((1, t, heads * DIFF_VD), lambda b, h, i: (b, i, h)),
        out_shape=jax.ShapeDtypeStruct((b, l, DIFF_HEADS * DIFF_VD), BF16),
        scratch_shapes=_attention_scratch(t, DIFF_VD, 2 * heads),
        compiler_params=_params("parallel", "parallel", "arbitrary"),
        name="diff_attention_prompt",
    )(q1t, q2t, q1t, q2t, k, vt, *lams, g)


def _two_part_softmax(q, kc, vc, kn, vn, mask_c, mask_n):
    tq = q.shape[0]
    carry = _softmax_init(tq, vc.shape[1])
    carry = _softmax_step(jnp.where(mask_c, _dot_nt(q, kc), NEG_INF), vc, *carry)
    return _softmax_step(jnp.where(mask_n, _dot_nt(q, kn), NEG_INF), vn, *carry)


def _mla_sample_kernel(q_ref, kc_ref, vc_ref, kn_ref, vn_ref, o_ref, *, past):
    tq, tc = q_ref.shape[1], kc_ref.shape[1]
    mask_c, mask_n = _chunk_mask(tq, tc, past, 0), _chunk_mask(tq, tq, past, past)
    for hd in range(MLA_HEADS):
        qk = slice(hd * MLA_QK_PAD, (hd + 1) * MLA_QK_PAD)
        v = slice(hd * MLA_V, (hd + 1) * MLA_V)
        _, l, acc = _two_part_softmax(q_ref[0, :, qk], kc_ref[0, :, qk], vc_ref[0, :, v], kn_ref[0, :, qk],
                                      vn_ref[0, :, v], mask_c, mask_n)
        o_ref[0, :, v] = (acc / l).astype(o_ref.dtype)


def _mla_sample(q, kc, vc, kn, vn):
    b, tq, _ = q.shape
    blk = lambda a: pl.BlockSpec((1,) + a.shape[1:], lambda b: (b, 0, 0))
    return pl.pallas_call(
        functools.partial(_mla_sample_kernel, past=kc.shape[1]),
        grid=(b,),
        in_specs=[blk(q), blk(kc), blk(vc), blk(kn), blk(vn)],
        out_specs=pl.BlockSpec((1, tq, MLA_HEADS * MLA_V), lambda b: (b, 0, 0)),
        out_shape=jax.ShapeDtypeStruct((b, tq, MLA_HEADS * MLA_V), BF16),
        compiler_params=_params("parallel"),
        name="mla_attention_sample",
    )(q, kc, vc, kn, vn)


def _diff_sample_kernel(q1_ref, q2_ref, kc_ref, vc_ref, kn_ref, vn_ref,
                        lq1_ref, lk1_ref, lq2_ref, lk2_ref, g_ref, o_ref, *, past, lambda_init):
    tq = q1_ref.shape[1]
    mask_c, mask_n = _chunk_mask(tq, past, past, 0), _chunk_mask(tq, tq, past, past)
    lam = _lambda(lq1_ref, lk1_ref, lq2_ref, lk2_ref, lambda_init)
    for hd in range(DIFF_HEADS):
        rows = pl.ds(hd, past, stride=DIFF_HEADS)
        cols = slice(hd * LANES, (hd + 1) * LANES)
        kc, vc = kc_ref[0, rows, :].astype(BF16), vc_ref[0, rows, :].astype(BF16)
        kn, vn = kn_ref[0, :, cols], vn_ref[0, :, cols]
        c1 = _two_part_softmax(q1_ref[0, :, cols], kc, vc, kn, vn, mask_c, mask_n)
        c2 = _two_part_softmax(q2_ref[0, :, cols], kc, vc, kn, vn, mask_c, mask_n)
        o_ref[0, :, cols] = _diff_finish(c1, c2, lam, g_ref[...], lambda_init).astype(o_ref.dtype)


def _diff_sample(q1, q2, kc, vc, kn, vn, lams, g, lambda_init, layer):
    b, tq, _ = q1.shape
    past = kc.shape[1] // DIFF_HEADS
    blk = lambda a: pl.BlockSpec((1,) + a.shape[1:], lambda b: (b, 0, 0))
    cache = pl.BlockSpec((1,) + kc.shape[1:], lambda b: (layer * q1.shape[0] + b, 0, 0))
    small = lambda a: pl.BlockSpec(a.shape, lambda b: (0, 0))
    return pl.pallas_call(
        functools.partial(_diff_sample_kernel, past=past, lambda_init=lambda_init),
        grid=(b,),
        in_specs=[blk(q1), blk(q2), cache, cache, blk(kn), blk(vn)] + [small(a) for a in lams] + [small(g)],
        out_specs=blk(q1),
        out_shape=jax.ShapeDtypeStruct((b, tq, DIFF_HEADS * DIFF_VD), BF16),
        compiler_params=_params("parallel"),
        name="diff_attention_sample",
    )(q1, q2, kc, vc, kn, vn, *lams, g)


def _rope_tables(pos, theta, rot_dim, period):
    half = rot_dim // 2
    inv = 1.0 / (jnp.float32(theta) ** (jnp.arange(half, dtype=F32) / half))
    ang = pos.astype(F32)[:, None] * inv[None, :]
    cos, sin = jnp.cos(ang), jnp.sin(ang)
    n = pos.shape[0]
    rest = period - rot_dim
    c = jnp.concatenate([cos, cos, jnp.ones((n, rest), F32)], axis=1)
    s1 = jnp.concatenate([-sin, jnp.zeros((n, half + rest), F32)], axis=1)
    s2 = jnp.concatenate([jnp.zeros((n, half), F32), sin, jnp.zeros((n, rest), F32)], axis=1)
    reps = LANES // period
    return tuple(jnp.tile(a, (1, reps)) for a in (c, s1, s2))


def _layer_weights(w_in, mla_w_uq, mla_w_ukv):
    c_kv = MLA_Q_LORA
    c_kr = c_kv + MLA_KV_LORA
    c_dq = c_kr + MLA_ROPE
    dqk = DIFF_HEADS * 2 * DIFF_DH
    c_dk = c_dq + dqk
    c_dv = c_dk + dqk
    c_ga = c_dv + DIFF_HEADS * DIFF_VD
    d = w_in.shape[0]
    c_gb = c_ga + d
    cols = lambda lo, hi: w_in[:, lo:hi].astype(BF16)
    wq = cols(0, c_kv)
    wkv = jnp.pad(cols(c_kv, c_dq), ((0, 0), (0, LANES - MLA_ROPE)))
    wdq, wdk, wdv = cols(c_dq, c_dk), cols(c_dk, c_dv), cols(c_dv, c_ga)
    wga, wgb = cols(c_ga, c_gb), cols(c_gb, w_in.shape[1])
    uq = mla_w_uq.astype(BF16).reshape(MLA_Q_LORA, MLA_HEADS, MLA_NOPE + MLA_ROPE)
    uq = jnp.pad(uq, ((0, 0), (0, 0), (0, MLA_QK_PAD - MLA_NOPE - MLA_ROPE))).reshape(MLA_Q_LORA, -1)
    ukv = mla_w_ukv.astype(BF16).reshape(MLA_KV_LORA, MLA_HEADS, MLA_NOPE + MLA_V)
    ukv = jnp.concatenate([ukv[:, :, :MLA_NOPE].reshape(MLA_KV_LORA, -1),
                           ukv[:, :, MLA_NOPE:].reshape(MLA_KV_LORA, -1)], axis=1)
    return wq, wkv, wdq, wdk, wdv, wga, wgb, uq, ukv


def _block(m, want):
    return want if m % want == 0 else m


def kernel(x_prompt, x_sample, cache_mla_ckv, cache_mla_krope, cache_diff_k, cache_diff_v, norm_mix, w_in,
           mla_q_norm, mla_w_uq, mla_kv_norm, mla_w_ukv, diff_lq1, diff_lk1, diff_lq2, diff_lk2, diff_subln,
           w_branch_a, w_branch_b, w_out, norm_ffn, w_ffn_in, w_ffn_out, norm_final):
    bp, lp, d = x_prompt.shape
    bs, ls, _ = x_sample.shape
    depth, _, past, _ = cache_mla_ckv.shape
    assert depth == 1, "the FFN kernel fuses the final norm, so it serves the last (only) layer"
    mp, ms = bp * lp, bs * ls
    t_attn = 512

    pos_p = jnp.arange(lp)
    pos_s = past + jnp.arange(ls)
    tabs_mla_p = _rope_tables(pos_p, MLA_THETA, MLA_ROPE, LANES)
    tabs_mla_s = tuple(jnp.tile(a, (bs, 1)) for a in _rope_tables(pos_s, MLA_THETA, MLA_ROPE, LANES))
    tabs_dif_p = _rope_tables(pos_p, ROPE_THETA, DIFF_ROT, DIFF_DH)
    tabs_dif_s = tuple(jnp.tile(a, (bs, 1)) for a in _rope_tables(pos_s, ROPE_THETA, DIFF_ROT, DIFF_DH))
    q_scale_mla = (MLA_NOPE + MLA_ROPE) ** -0.5 * LOG2E
    q_scale_dif = DIFF_DH ** -0.5 * LOG2E
    gfinal = norm_final.reshape(1, d)

    xp = x_prompt.reshape(mp, d)
    xs = x_sample.reshape(ms, d)
    rows_p, rows_s = [], []
    for l in range(depth):
        lambda_init = 0.8 - 0.6 * math.exp(-0.3 * l)
        wq, wkv, wdq, wdk, wdv, wga, wgb, uq, ukv = _layer_weights(w_in[l], mla_w_uq[l], mla_w_ukv[l])
        gmix, gq, gkv = norm_mix[l].reshape(1, -1), mla_q_norm[l].reshape(1, -1), mla_kv_norm[l].reshape(1, -1)
        lams = tuple(a[l].reshape(1, -1) for a in (diff_lq1, diff_lk1, diff_lq2, diff_lk2))
        gsub = diff_subln[l].reshape(1, -1)
        wa, wbr, wo = w_branch_a[l].astype(BF16), w_branch_b[l].astype(BF16), w_out[l].astype(BF16)
        wfi, wfo = w_ffn_in[l].astype(BF16), w_ffn_out[l].astype(BF16)
        gffn = norm_ffn[l].reshape(1, -1)

        def stage1(x, bm, tabs_mla, tabs_dif, tr):
            h, q, ckv, krope, kmla, vmla = _mla_proj(x, gmix, wq, gq, uq, wkv, gkv, ukv, tabs_mla, bm,
                                                     q_scale_mla, tr)
            q1, q2, dk, dkb, dv, dvb = _diff_proj(h, wdq, wdk, wdv, tabs_dif, bm, q_scale_dif, tr)
            return h, q, ckv, krope, kmla, vmla, q1, q2, dk, dkb, dv, dvb

        def stage2(x, h, oa, ob, bm, bn, bf):
            mg = _merge(h, oa, ob, wga, wgb, wa, wbr, bm, bn)
            x2, h2 = _outproj(x, mg, wo, gffn, min(bm, 512))
            return _ffn(h2, wfi, wfo, x2, gfinal, bm, bf)

        h, q, ckv, krope, kmla, vmla, q1, q2, dk, dkb, dv, dvb = stage1(xp, t_attn, tabs_mla_p, tabs_dif_p, True)
        r3 = lambda a: a.reshape(bp, lp, -1)
        oa = _mla_prompt(q, r3(kmla), vmla, t_attn, 4)
        ob = _diff_prompt(q1, q2, r3(dkb), dvb, lams, gsub, t_attn, 2, lambda_init)
        xp = stage2(xp, h, oa.reshape(mp, -1), ob.reshape(mp, -1), _block(mp, 1024), 1024, 256)
        rows_p.append((ckv.reshape(bp, lp, -1), krope.reshape(bp, lp, -1),
                       dk.reshape(bp, lp, DIFF_HEADS, -1), dv.reshape(bp, lp, DIFF_HEADS, -1)))

        h, q, ckv, krope, kmla, vmla, q1, q2, dk, dkb, dv, dvb = stage1(xs, ms, tabs_mla_s, tabs_dif_s, False)
        kr_pad = jnp.pad(cache_mla_krope[l].reshape(bs * past, MLA_ROPE), ((0, 0), (0, LANES - MLA_ROPE)))
        kc, vc = _kvcache_expand(cache_mla_ckv[l].reshape(bs * past, -1), kr_pad, ukv, _block(bs * past, 2048))
        r3 = lambda a: a.reshape(bs, ls, -1)
        c3 = lambda a: a.reshape(bs, past, -1)
        oa = _mla_sample(r3(q), c3(kc), c3(vc), r3(kmla), r3(vmla))
        call = lambda a: a.reshape(depth * bs, past * DIFF_HEADS, -1)
        ob = _diff_sample(r3(q1), r3(q2), call(cache_diff_k), call(cache_diff_v), r3(dkb), r3(dvb),
                          lams, gsub, lambda_init, l)
        xs = stage2(xs, h, oa.reshape(ms, -1), ob.reshape(ms, -1), ms, 512, 512)
        rows_s.append((ckv.reshape(bs, ls, -1), krope.reshape(bs, ls, -1),
                       dk.reshape(bs, ls, DIFF_HEADS, -1), dv.reshape(bs, ls, DIFF_HEADS, -1)))

    y_prompt = xp.reshape(bp, lp, d)
    y_sample = xs.reshape(bs, ls, d)
    stack = lambda rows, i: jnp.stack([r[i] for r in rows], axis=0)
    return (y_prompt, y_sample,
            stack(rows_p, 0), stack(rows_p, 1), stack(rows_p, 2), stack(rows_p, 3),
            stack(rows_s, 0), stack(rows_s, 1), stack(rows_s, 2), stack(rows_s, 3))
```

```python
import functools
import math

import jax
import jax.numpy as jnp
from jax import lax
from jax.experimental import pallas as pl
from jax.experimental.pallas import tpu as pltpu

F32 = jnp.float32
BF16 = jnp.bfloat16

CHUNK = 64
EPS = 1e-6
NEG_INF = -1e30

MLA_HEADS = 8
MLA_Q_LORA = 512
MLA_KV_LORA = 256
MLA_NOPE = 128
MLA_ROPE = 64
MLA_V = 128
MLA_THETA = 10000.0
MLA_QK_PAD = 256

DIFF_HEADS = 8
DIFF_DH = 64
DIFF_VD = 2 * DIFF_DH
DIFF_ROT = DIFF_DH // 4
ROPE_THETA = 500000.0

LANES = 128
VT_ONES = 16
LOG2E = math.log2(math.e)
VMEM_LIMIT = 56 * 1024 * 1024


def _params(*sem):
    return pltpu.CompilerParams(dimension_semantics=sem, vmem_limit_bytes=VMEM_LIMIT)


def _dot(a, b):
    return jnp.dot(a, b, preferred_element_type=F32)


def _dot_nt(a, b):
    return lax.dot_general(a, b, (((1,), (1,)), ((), ())), preferred_element_type=F32)


def _rms(x, g):
    return x * lax.rsqrt(jnp.mean(x * x, axis=-1, keepdims=True) + EPS) * g


def _rope_tile(t, c, s1, s2, half):
    return t * c + pltpu.roll(t, LANES - half, 1) * s1 + pltpu.roll(t, half, 1) * s2


def _put(ref, lo, val, tr):
    w = val.shape[1]
    if tr:
        ref[0, lo:lo + w, :] = val.T.astype(ref.dtype)
    else:
        ref[:, lo:lo + w] = val.astype(ref.dtype)


def _put_v(ref, hd, val, tr):
    bm, w = val.shape
    if tr:
        lo = hd * (w + VT_ONES)
        ref[0, lo:lo + w, :] = val.T.astype(ref.dtype)
        ref[0, lo + w:lo + w + VT_ONES, :] = jnp.ones((VT_ONES, bm), ref.dtype)
    else:
        ref[:, hd * w:(hd + 1) * w] = val.astype(ref.dtype)


def _v_width(heads, dv, tr):
    return heads * (dv + VT_ONES) if tr else heads * dv


def _out(m, n, bm, dtype, tr):
    if tr:
        return pl.BlockSpec((1, n, bm), lambda i: (i, 0, 0)), jax.ShapeDtypeStruct((m // bm, n, bm), dtype)
    return pl.BlockSpec((bm, n), lambda i: (i, 0)), jax.ShapeDtypeStruct((m, n), dtype)


def _qproj_body(h, wq_ref, gq_ref, wuq_ref, c, s1, s2, o_ref, scale, tr):
    qlat = _rms(_dot(h, wq_ref[...]), gq_ref[...]).astype(BF16)
    q = _dot(qlat, wuq_ref[...])
    for hd in range(MLA_HEADS):
        lo = hd * MLA_QK_PAD
        _put(o_ref, lo, q[:, lo:lo + LANES] * scale, tr)
        t = _rope_tile(q[:, lo + LANES:lo + 2 * LANES], c, s1, s2, MLA_ROPE // 2)
        _put(o_ref, lo + LANES, t * scale, tr)


def _kv_expand(ckv, krope_tile, wukv_ref, kmla_ref, vmla_ref, tr):
    kv = _dot(ckv.astype(BF16), wukv_ref[...])
    kr = krope_tile.astype(BF16)
    for hd in range(MLA_HEADS):
        lo = hd * MLA_QK_PAD
        kmla_ref[:, lo:lo + LANES] = kv[:, hd * MLA_NOPE:(hd + 1) * MLA_NOPE].astype(BF16)
        kmla_ref[:, lo + LANES:lo + 2 * LANES] = kr
        vlo = MLA_HEADS * MLA_NOPE + hd * MLA_V
        _put_v(vmla_ref, hd, kv[:, vlo:vlo + MLA_V], tr)


def _mla_proj_kernel(x_ref, gmix_ref, wq_ref, gq_ref, wuq_ref, wkv_ref, gkv_ref, wukv_ref, c_ref, s1_ref, s2_ref,
                     h_ref, q_ref, ckv_ref, krope_ref, kmla_ref, vmla_ref, *, scale, tr):
    h = _rms(x_ref[...], gmix_ref[...]).astype(BF16)
    h_ref[...] = h
    c, s1, s2 = c_ref[...], s1_ref[...], s2_ref[...]
    _qproj_body(h, wq_ref, gq_ref, wuq_ref, c, s1, s2, q_ref, scale, tr)
    z = _dot(h, wkv_ref[...])
    ckv = _rms(z[:, :MLA_KV_LORA], gkv_ref[...])
    ckv_ref[...] = ckv
    t = _rope_tile(z[:, MLA_KV_LORA:], c, s1, s2, MLA_ROPE // 2)
    krope_ref[...] = t[:, :MLA_ROPE]
    _kv_expand(ckv, t, wukv_ref, kmla_ref, vmla_ref, tr)


def _mla_proj(x, gmix, wq, gq, wuq, wkv, gkv, wukv, tabs, bm, scale, tr):
    m, d = x.shape
    nt = tabs[0].shape[0] // bm
    full = lambda a: pl.BlockSpec(a.shape, lambda i: (0, 0))
    tab = pl.BlockSpec((bm, LANES), lambda i: (i % nt, 0))
    row = lambda n: pl.BlockSpec((bm, n), lambda i: (i, 0))
    nk = MLA_HEADS * MLA_QK_PAD
    qspec, qshape = _out(m, wuq.shape[1], bm, BF16, tr)
    vspec, vshape = _out(m, _v_width(MLA_HEADS, MLA_V, tr), bm, BF16, tr)
    return pl.pallas_call(
        functools.partial(_mla_proj_kernel, scale=scale, tr=tr),
        grid=(m // bm,),
        in_specs=[row(d), full(gmix), full(wq), full(gq), full(wuq), full(wkv), full(gkv), full(wukv), tab, tab, tab],
        out_specs=[row(d), qspec, row(MLA_KV_LORA), row(MLA_ROPE), row(nk), vspec],
        out_shape=[jax.ShapeDtypeStruct((m, d), BF16), qshape,
                   jax.ShapeDtypeStruct((m, MLA_KV_LORA), F32), jax.ShapeDtypeStruct((m, MLA_ROPE), F32),
                   jax.ShapeDtypeStruct((m, nk), BF16), vshape],
        compiler_params=_params("parallel"),
        name="norm_mla_proj",
    )(x, gmix, wq, gq, wuq, wkv, gkv, wukv, *tabs)


def _diff_proj_kernel(h_ref, wdq_ref, wdk_ref, wdv_ref, c_ref, s1_ref, s2_ref,
                      q1_ref, q2_ref, dk_ref, dkb_ref, dv_ref, dvb_ref, *, scale, tr):
    h = h_ref[...]
    zq = _dot(h, wdq_ref[...])
    zk = _dot(h, wdk_ref[...])
    zv = _dot(h, wdv_ref[...])
    dv_ref[...] = zv
    c, s1, s2 = c_ref[...], s1_ref[...], s2_ref[...]
    first = lax.broadcasted_iota(jnp.int32, (1, LANES), 1) < DIFF_DH
    for hd in range(DIFF_HEADS):
        sl = slice(hd * LANES, (hd + 1) * LANES)
        q = _rope_tile(zq[:, sl], c, s1, s2, DIFF_ROT // 2) * scale
        _put(q1_ref, hd * LANES, jnp.where(first, q, 0.0), tr)
        _put(q2_ref, hd * LANES, jnp.where(first, 0.0, q), tr)
        k = _rope_tile(zk[:, sl], c, s1, s2, DIFF_ROT // 2)
        dk_ref[:, sl] = k
        dkb_ref[:, sl] = k.astype(BF16)
        _put_v(dvb_ref, hd, zv[:, sl], tr)


def _diff_proj(h, wdq, wdk, wdv, tabs, bm, scale, tr):
    m, d = h.shape
    nt = tabs[0].shape[0] // bm
    n = wdq.shape[1]
    full = lambda a: pl.BlockSpec(a.shape, lambda i: (0, 0))
    tab = pl.BlockSpec((bm, LANES), lambda i: (i % nt, 0))
    row = lambda w: pl.BlockSpec((bm, w), lambda i: (i, 0))
    qspec, qshape = _out(m, n, bm, BF16, tr)
    vspec, vshape = _out(m, _v_width(DIFF_HEADS, DIFF_VD, tr), bm, BF16, tr)
    f32_rows, bf16_rows = jax.ShapeDtypeStruct((m, n), F32), jax.ShapeDtypeStruct((m, n), BF16)
    return pl.pallas_call(
        functools.partial(_diff_proj_kernel, scale=scale, tr=tr),
        grid=(m // bm,),
        in_specs=[row(d), full(wdq), full(wdk), full(wdv), tab, tab, tab],
        out_specs=[qspec, qspec, row(n), row(n), row(n), vspec],
        out_shape=[qshape, qshape, f32_rows, bf16_rows, f32_rows, vshape],
        compiler_params=_params("parallel"),
        name="diff_proj",
    )(h, wdq, wdk, wdv, *tabs)


def _merge_kernel(h_ref, oa_ref, ob_ref, wga_ref, wgb_ref, wa_ref, wb_ref, o_ref):
    h = h_ref[...]
    ga = jax.nn.sigmoid(_dot(h, wga_ref[...]))
    gb = jax.nn.sigmoid(_dot(h, wgb_ref[...]))
    ya = _dot(oa_ref[...], wa_ref[...])
    yb = _dot(ob_ref[...], wb_ref[...])
    o_ref[...] = (ga * ya + gb * yb).astype(o_ref.dtype)


def _merge(h, oa, ob, wga, wgb, wa, wb, bm, bn):
    m, d = h.shape
    n = wga.shape[1]
    row = lambda a: pl.BlockSpec((bm, a.shape[1]), lambda i, j: (i, 0))
    col = lambda a: pl.BlockSpec((a.shape[0], bn), lambda i, j: (0, j))
    return pl.pallas_call(
        _merge_kernel,
        grid=(m // bm, n // bn),
        in_specs=[row(h), row(oa), row(ob), col(wga), col(wgb), col(wa), col(wb)],
        out_specs=pl.BlockSpec((bm, bn), lambda i, j: (i, j)),
        out_shape=jax.ShapeDtypeStruct((m, n), BF16),
        compiler_params=_params("parallel", "arbitrary"),
        name="gated_merge",
    )(h, oa, ob, wga, wgb, wa, wb)


def _outproj_kernel(x_ref, mg_ref, wo_ref, g_ref, x2_ref, h2_ref):
    x2 = x_ref[...] + _dot(mg_ref[...], wo_ref[...])
    x2_ref[...] = x2
    h2_ref[...] = _rms(x2, g_ref[...]).astype(BF16)


def _outproj(x, mg, wo, g, bm):
    m, d = x.shape
    row = pl.BlockSpec((bm, d), lambda i: (i, 0))
    return pl.pallas_call(
        _outproj_kernel,
        grid=(m // bm,),
        in_specs=[row, row, pl.BlockSpec(wo.shape, lambda i: (0, 0)), pl.BlockSpec((1, d), lambda i: (0, 0))],
        out_specs=[row, row],
        out_shape=[jax.ShapeDtypeStruct((m, d), F32), jax.ShapeDtypeStruct((m, d), BF16)],
        compiler_params=_params("parallel"),
        name="out_proj_residual",
    )(x, mg, wo, g)


def _ffn_kernel(h2_ref, wg_ref, wu_ref, wd_ref, x2_ref, gf_ref, y_ref):
    f = pl.program_id(1)

    @pl.when(f == 0)
    def _():
        y_ref[...] = x2_ref[...]

    h2 = h2_ref[...]
    a = (jax.nn.silu(_dot(h2, wg_ref[...])) * _dot(h2, wu_ref[...])).astype(BF16)
    y_ref[...] += _dot(a, wd_ref[...])

    @pl.when(f == pl.num_programs(1) - 1)
    def _():
        y_ref[...] = _rms(y_ref[...], gf_ref[...])


def _ffn(h2, w_in, w_out, x2, gf, bm, bf):
    m, d = h2.shape
    dff = w_out.shape[0]
    nf = dff // bf
    row = pl.BlockSpec((bm, d), lambda i, f: (i, 0))
    return pl.pallas_call(
        _ffn_kernel,
        grid=(m // bm, nf),
        in_specs=[row,
                  pl.BlockSpec((d, bf), lambda i, f: (0, f)),
                  pl.BlockSpec((d, bf), lambda i, f: (0, f + nf)),
                  pl.BlockSpec((bf, d), lambda i, f: (f, 0)),
                  row,
                  pl.BlockSpec((1, d), lambda i, f: (0, 0))],
        out_specs=row,
        out_shape=jax.ShapeDtypeStruct((m, d), F32),
        compiler_params=_params("parallel", "arbitrary"),
        name="ffn_swiglu_final_norm",
    )(h2, w_in, w_in, w_out, x2, gf)


def _softmax_step(s, v, m, l, acc):
    m_new = jnp.maximum(m, jnp.max(s, axis=-1, keepdims=True))
    alpha = jnp.exp2(m - m_new)
    p = jnp.exp2(s - m_new)
    l = alpha * l + jnp.sum(p, axis=-1, keepdims=True)
    acc = alpha * acc + _dot(p.astype(BF16), v)
    return m_new, l, acc


def _chunk_mask(tq, tk, q0, k0):
    qp = q0 + lax.broadcasted_iota(jnp.int32, (tq, tk), 0)
    kp = k0 + lax.broadcasted_iota(jnp.int32, (tq, tk), 1)
    return (kp // CHUNK) <= (qp // CHUNK)


def _softmax_init(tq, dv):
    return (jnp.full((tq, 1), NEG_INF, F32), jnp.zeros((tq, 1), F32), jnp.zeros((tq, dv), F32))


def _chunk_mask_t(t):
    kp = lax.broadcasted_iota(jnp.int32, (t, t), 0)
    qp = lax.broadcasted_iota(jnp.int32, (t, t), 1)
    return (kp // CHUNK) <= (qp // CHUNK)


def _probs_t(s, smax, m):
    m_new = jnp.maximum(m, smax)
    return jnp.exp2(s - m_new).astype(BF16), jnp.exp2(m - m_new), m_new


def _attention_scratch(t, dv, streams):
    return [pltpu.VMEM((streams, 2, t, t), F32),
            pltpu.VMEM((streams, dv + VT_ONES, t), F32),
            pltpu.VMEM((streams, 8, t), F32)]


def _causal_attention_t(n, scores, next_scores, values, finish, s_ref, acc_ref, stat_ref, t, streams):
    row_m, row_smax = 0, 1

    def stat(st, r):
        return stat_ref[st, r:r + 1, :]

    def put_scores(st, slot, s):
        s_ref[st, slot] = s
        stat_ref[st, row_smax + slot:row_smax + slot + 1, :] = jnp.max(s, axis=0, keepdims=True)

    def accumulate(st, j, s, smax):
        p, alpha, m = _probs_t(s, smax, stat(st, row_m))
        acc = alpha * acc_ref[st] + _dot(values(j, st), p)
        stat_ref[st, row_m:row_m + 1, :] = m
        return acc

    @pl.when(n == 0)
    def _():
        for st in range(streams):
            put_scores(st, 0, scores(0, st))

    for st in range(streams):
        acc_ref[st] = jnp.zeros(acc_ref.shape[1:], F32)
        stat_ref[st, row_m:row_m + 1, :] = jnp.full((1, t), NEG_INF, F32)

    def run(first):
        def step(tau, par):
            for st in range(streams):
                put_scores(st, (first + par) % 2, scores(tau, st))
                prev = (first + par + 1) % 2
                acc_ref[st] = accumulate(st, tau - 1, s_ref[st, prev], stat(st, row_smax + prev))

        unroll = 4

        def trip(u, carry):
            for k in range(unroll):
                step(unroll * u + 1 + k, (1 + k) % 2)
            return carry

        lax.fori_loop(0, n // unroll, trip, 0)
        done = (n // unroll) * unroll

        @pl.when(n - done >= 2)
        def _():
            step(done + 1, 1)
            step(done + 2, 0)

        @pl.when(n % 2 == 1)
        def _():
            step(n, 1)

        def tail(par):
            diag = (first + par) % 2
            for st in range(streams):
                put_scores(st, 1 - diag, next_scores(st))
            mask = _chunk_mask_t(t)
            res = []
            for st in range(streams):
                s = jnp.where(mask, s_ref[st, diag], NEG_INF)
                res.append(accumulate(st, n, s, jnp.max(s, axis=0, keepdims=True)))
            finish(res)

        for par in (0, 1):
            pl.when(n % 2 == par)(functools.partial(tail, par))

    first = ((n + 1) // 2) % 2
    for f in (0, 1):
        pl.when(first == f)(functools.partial(run, f))


def _normalised(acc, dv):
    return (acc[:dv] / acc[dv:dv + 1]).T


def _mla_prompt_kernel(qt_ref, qn_ref, k_ref, vt_ref, o_ref, *scratch, t, heads):
    def block_scores(q_ref, j, hd):
        k = k_ref[0, pl.ds(pl.multiple_of(j * t, t), t), hd * MLA_QK_PAD:(hd + 1) * MLA_QK_PAD]
        return _dot(k, q_ref[0, hd * MLA_QK_PAD:(hd + 1) * MLA_QK_PAD, :])

    scores = functools.partial(block_scores, qt_ref)
    next_scores = functools.partial(block_scores, qn_ref, 0)

    def values(j, hd):
        return vt_ref[j, hd * (MLA_V + VT_ONES):(hd + 1) * (MLA_V + VT_ONES), :]

    def finish(res):
        for hd, acc in enumerate(res):
            o_ref[0, :, hd * MLA_V:(hd + 1) * MLA_V] = _normalised(acc, MLA_V).astype(o_ref.dtype)

    _causal_attention_t(pl.program_id(2), scores, next_scores, values, finish, *scratch, t, heads)


def _mla_prompt(qt, k, vt, t, heads):
    b, l, _ = k.shape
    nq = l // t
    qblock = lambda step: pl.BlockSpec((1, heads * MLA_QK_PAD, t),
                                       lambda b, h, i: (b * nq + jnp.minimum(i + step, nq - 1), h, 0))
    return pl.pallas_call(
        functools.partial(_mla_prompt_kernel, t=t, heads=heads),
        grid=(b, MLA_HEADS // heads, nq),
        in_specs=[qblock(0), qblock(1),
                  pl.BlockSpec((1, l, heads * MLA_QK_PAD), lambda b, h, i: (b, 0, h), pipeline_mode=pl.Buffered(1)),
                  pl.BlockSpec((nq, heads * (MLA_V + VT_ONES), t), lambda b, h, i: (b, h, 0),
                               pipeline_mode=pl.Buffered(1))],
        out_specs=pl.BlockSpec((1, t, heads * MLA_V), lambda b, h, i: (b, i, h)),
        out_shape=jax.ShapeDtypeStruct((b, l, MLA_HEADS * MLA_V), BF16),
        scratch_shapes=_attention_scratch(t, MLA_V, heads),
        compiler_params=_params("parallel", "parallel", "arbitrary"),
        name="mla_attention_prompt",
    )(qt, qt, k, vt)


def _lambda(lq1_ref, lk1_ref, lq2_ref, lk2_ref, lambda_init):
    a = jnp.sum(lq1_ref[...] * lk1_ref[...], axis=-1, keepdims=True)
    b = jnp.sum(lq2_ref[...] * lk2_ref[...], axis=-1, keepdims=True)
    return jnp.exp(a) - jnp.exp(b) + lambda_init


def _diff_finish(c1, c2, lam, g, lambda_init):
    o = c1[2] / c1[1] - lam * (c2[2] / c2[1])
    return _rms(o, g) * (1.0 - lambda_init)


def _diff_prompt_kernel(q1t_ref, q2t_ref, q1n_ref, q2n_ref, k_ref, vt_ref, lq1_ref, lk1_ref, lq2_ref, lk2_ref,
                        g_ref, o_ref, *scratch, t, heads, lambda_init):
    qt_refs, qn_refs = (q1t_ref, q2t_ref), (q1n_ref, q2n_ref)

    def block_scores(q_refs, rows, st):
        cols = slice((st // 2) * LANES, (st // 2 + 1) * LANES)
        return _dot(k_ref[0, rows, cols], q_refs[st % 2][0, cols, :])

    def scores(j, st):
        return block_scores(qt_refs, pl.ds(pl.multiple_of(j * t, t), t), st)

    def next_scores(st):
        return block_scores(qn_refs, slice(0, t), st)

    def values(j, st):
        return vt_ref[j, (st // 2) * (DIFF_VD + VT_ONES):(st // 2 + 1) * (DIFF_VD + VT_ONES), :]

    def finish(res):
        lam = _lambda(lq1_ref, lk1_ref, lq2_ref, lk2_ref, lambda_init)
        for hd in range(heads):
            o = _normalised(res[2 * hd], DIFF_VD) - lam * _normalised(res[2 * hd + 1], DIFF_VD)
            o = _rms(o, g_ref[...]) * (1.0 - lambda_init)
            o_ref[0, :, hd * DIFF_VD:(hd + 1) * DIFF_VD] = o.astype(o_ref.dtype)

    _causal_attention_t(pl.program_id(2), scores, next_scores, values, finish, *scratch, t, 2 * heads)


def _diff_prompt(q1t, q2t, k, vt, lams, g, t, heads, lambda_init):
    b, l, _ = k.shape
    nq = l // t
    qs = pl.BlockSpec((1, heads * LANES, t), lambda b, h, i: (b * nq + i, h, 0))
    qn = pl.BlockSpec((1, heads * LANES, t), lambda b, h, i: (b * nq + jnp.minimum(i + 1, nq - 1), h, 0))
    small = lambda a: pl.BlockSpec(a.shape, lambda b, h, i: (0, 0))
    return pl.pallas_call(
        functools.partial(_diff_prompt_kernel, t=t, heads=heads, lambda_init=lambda_init),
        grid=(b, DIFF_HEADS // heads, nq),
        in_specs=[qs, qs, qn, qn,
                  pl.BlockSpec((1, l, heads * LANES), lambda b, h, i: (b, 0, h)),
                  pl.BlockSpec((nq, heads * (DIFF_VD + VT_ONES), t), lambda b, h, i: (b, h, 0))]
                 + [small(a) for a in lams] + [small(g)],
        out_specs=pl.BlockSpec((1, t, heads * DIFF_VD), lambda b, h, i: (b, i, h)),
        out_shape=jax.ShapeDtypeStruct((b, l, DIFF_HEADS * DIFF_VD), BF16),
        scratch_shapes=_attention_scratch(t, DIFF_VD, 2 * heads),
        compiler_params=_params("parallel", "parallel", "arbitrary"),
        name="diff_attention_prompt",
    )(q1t, q2t, q1t, q2t, k, vt, *lams, g)


def _two_part_softmax(q, kc, vc, kn, vn, mask_c, mask_n):
    tq = q.shape[0]
    carry = _softmax_init(tq, vc.shape[1])
    carry = _softmax_step(jnp.where(mask_c, _dot_nt(q, kc), NEG_INF), vc, *carry)
    return _softmax_step(jnp.where(mask_n, _dot_nt(q, kn), NEG_INF), vn, *carry)


def _mla_sample_kernel(q_ref, latc_ref, krc_ref, latn_ref, kn_ref, wukv_ref, o_ref, *, past):
    tq = q_ref.shape[1]
    latc, latn = latc_ref[0].astype(BF16), latn_ref[0].astype(BF16)
    krc, krn = krc_ref[0].astype(BF16), kn_ref[0, :, LANES:2 * LANES]
    qa, qr = [], []
    for hd in range(MLA_HEADS):
        lo = hd * MLA_QK_PAD
        wk = wukv_ref[:, hd * MLA_NOPE:(hd + 1) * MLA_NOPE]
        qa.append(_dot_nt(q_ref[0, :, lo:lo + LANES], wk).astype(BF16))
        qr.append(q_ref[0, :, lo + LANES:lo + 2 * LANES])
    qa, qr = jnp.concatenate(qa, axis=0), jnp.concatenate(qr, axis=0)
    rows = MLA_HEADS * tq

    def mask(tk, k0):
        qp = past + lax.broadcasted_iota(jnp.int32, (rows, tk), 0) % tq
        kp = k0 + lax.broadcasted_iota(jnp.int32, (rows, tk), 1)
        return (kp // CHUNK) <= (qp // CHUNK)

    carry = _softmax_init(rows, MLA_KV_LORA)
    s = _dot_nt(qa, latc) + _dot_nt(qr, krc)
    carry = _softmax_step(jnp.where(mask(past, 0), s, NEG_INF), latc, *carry)
    s = _dot_nt(qa, latn) + _dot_nt(qr, krn)
    _, l, acc = _softmax_step(jnp.where(mask(tq, past), s, NEG_INF), latn, *carry)
    olat = (acc / l).astype(BF16)
    v0 = MLA_HEADS * MLA_NOPE
    for hd in range(MLA_HEADS):
        wv = wukv_ref[:, v0 + hd * MLA_V:v0 + (hd + 1) * MLA_V]
        o_ref[0, :, hd * MLA_V:(hd + 1) * MLA_V] = _dot(olat[hd * tq:(hd + 1) * tq], wv).astype(o_ref.dtype)


def _mla_sample(q, latc, krc, latn, kn, wukv, layer):
    b, tq, _ = q.shape
    blk = lambda a: pl.BlockSpec((1,) + a.shape[1:], lambda b: (b, 0, 0))
    cache = lambda a: pl.BlockSpec((1,) + a.shape[1:], lambda b: (layer * q.shape[0] + b, 0, 0))
    return pl.pallas_call(
        functools.partial(_mla_sample_kernel, past=latc.shape[1]),
        grid=(b,),
        in_specs=[blk(q), cache(latc), cache(krc), blk(latn), blk(kn), pl.BlockSpec(wukv.shape, lambda b: (0, 0))],
        out_specs=pl.BlockSpec((1, tq, MLA_HEADS * MLA_V), lambda b: (b, 0, 0)),
        out_shape=jax.ShapeDtypeStruct((b, tq, MLA_HEADS * MLA_V), BF16),
        compiler_params=_params("parallel"),
        name="mla_attention_sample",
    )(q, latc, krc, latn, kn, wukv)


def _diff_sample_kernel(q1_ref, q2_ref, kc_ref, vc_ref, kn_ref, vn_ref,
                        lq1_ref, lk1_ref, lq2_ref, lk2_ref, g_ref, o_ref, *, past, lambda_init):
    tq = q1_ref.shape[1]
    mask_c, mask_n = _chunk_mask(tq, past, past, 0), _chunk_mask(tq, tq, past, past)
    lam = _lambda(lq1_ref, lk1_ref, lq2_ref, lk2_ref, lambda_init)
    for hd in range(DIFF_HEADS):
        rows = pl.ds(hd, past, stride=DIFF_HEADS)
        cols = slice(hd * LANES, (hd + 1) * LANES)
        kc, vc = kc_ref[0, rows, :].astype(BF16), vc_ref[0, rows, :].astype(BF16)
        kn, vn = kn_ref[0, :, cols], vn_ref[0, :, cols]
        c1 = _two_part_softmax(q1_ref[0, :, cols], kc, vc, kn, vn, mask_c, mask_n)
        c2 = _two_part_softmax(q2_ref[0, :, cols], kc, vc, kn, vn, mask_c, mask_n)
        o_ref[0, :, cols] = _diff_finish(c1, c2, lam, g_ref[...], lambda_init).astype(o_ref.dtype)


def _diff_sample(q1, q2, kc, vc, kn, vn, lams, g, lambda_init, layer):
    b, tq, _ = q1.shape
    past = kc.shape[1] // DIFF_HEADS
    blk = lambda a: pl.BlockSpec((1,) + a.shape[1:], lambda b: (b, 0, 0))
    cache = pl.BlockSpec((1,) + kc.shape[1:], lambda b: (layer * q1.shape[0] + b, 0, 0))
    small = lambda a: pl.BlockSpec(a.shape, lambda b: (0, 0))
    return pl.pallas_call(
        functools.partial(_diff_sample_kernel, past=past, lambda_init=lambda_init),
        grid=(b,),
        in_specs=[blk(q1), blk(q2), cache, cache, blk(kn), blk(vn)] + [small(a) for a in lams] + [small(g)],
        out_specs=blk(q1),
        out_shape=jax.ShapeDtypeStruct((b, tq, DIFF_HEADS * DIFF_VD), BF16),
        compiler_params=_params("parallel"),
        name="diff_attention_sample",
    )(q1, q2, kc, vc, kn, vn, *lams, g)


def _rope_tables(pos, theta, rot_dim, period):
    half = rot_dim // 2
    inv = 1.0 / (jnp.float32(theta) ** (jnp.arange(half, dtype=F32) / half))
    ang = pos.astype(F32)[:, None] * inv[None, :]
    cos, sin = jnp.cos(ang), jnp.sin(ang)
    n = pos.shape[0]
    rest = period - rot_dim
    c = jnp.concatenate([cos, cos, jnp.ones((n, rest), F32)], axis=1)
    s1 = jnp.concatenate([-sin, jnp.zeros((n, half + rest), F32)], axis=1)
    s2 = jnp.concatenate([jnp.zeros((n, half), F32), sin, jnp.zeros((n, rest), F32)], axis=1)
    reps = LANES // period
    return tuple(jnp.tile(a, (1, reps)) for a in (c, s1, s2))


def _layer_weights(w_in, mla_w_uq, mla_w_ukv):
    c_kv = MLA_Q_LORA
    c_kr = c_kv + MLA_KV_LORA
    c_dq = c_kr + MLA_ROPE
    dqk = DIFF_HEADS * 2 * DIFF_DH
    c_dk = c_dq + dqk
    c_dv = c_dk + dqk
    c_ga = c_dv + DIFF_HEADS * DIFF_VD
    d = w_in.shape[0]
    c_gb = c_ga + d
    cols = lambda lo, hi: w_in[:, lo:hi].astype(BF16)
    wq = cols(0, c_kv)
    wkv = jnp.pad(cols(c_kv, c_dq), ((0, 0), (0, LANES - MLA_ROPE)))
    wdq, wdk, wdv = cols(c_dq, c_dk), cols(c_dk, c_dv), cols(c_dv, c_ga)
    wga, wgb = cols(c_ga, c_gb), cols(c_gb, w_in.shape[1])
    uq = mla_w_uq.astype(BF16).reshape(MLA_Q_LORA, MLA_HEADS, MLA_NOPE + MLA_ROPE)
    uq = jnp.pad(uq, ((0, 0), (0, 0), (0, MLA_QK_PAD - MLA_NOPE - MLA_ROPE))).reshape(MLA_Q_LORA, -1)
    ukv = mla_w_ukv.astype(BF16).reshape(MLA_KV_LORA, MLA_HEADS, MLA_NOPE + MLA_V)
    ukv = jnp.concatenate([ukv[:, :, :MLA_NOPE].reshape(MLA_KV_LORA, -1),
                           ukv[:, :, MLA_NOPE:].reshape(MLA_KV_LORA, -1)], axis=1)
    return wq, wkv, wdq, wdk, wdv, wga, wgb, uq, ukv


def _block(m, want):
    return want if m % want == 0 else m


def kernel(x_prompt, x_sample, cache_mla_ckv, cache_mla_krope, cache_diff_k, cache_diff_v, norm_mix, w_in,
           mla_q_norm, mla_w_uq, mla_kv_norm, mla_w_ukv, diff_lq1, diff_lk1, diff_lq2, diff_lk2, diff_subln,
           w_branch_a, w_branch_b, w_out, norm_ffn, w_ffn_in, w_ffn_out, norm_final):
    bp, lp, d = x_prompt.shape
    bs, ls, _ = x_sample.shape
    depth, _, past, _ = cache_mla_ckv.shape
    assert depth == 1, "the FFN kernel fuses the final norm, so it serves the last (only) layer"
    mp, ms = bp * lp, bs * ls
    t_attn = 512

    pos_p = jnp.arange(lp)
    pos_s = past + jnp.arange(ls)
    tabs_mla_p = _rope_tables(pos_p, MLA_THETA, MLA_ROPE, LANES)
    tabs_mla_s = tuple(jnp.tile(a, (bs, 1)) for a in _rope_tables(pos_s, MLA_THETA, MLA_ROPE, LANES))
    tabs_dif_p = _rope_tables(pos_p, ROPE_THETA, DIFF_ROT, DIFF_DH)
    tabs_dif_s = tuple(jnp.tile(a, (bs, 1)) for a in _rope_tables(pos_s, ROPE_THETA, DIFF_ROT, DIFF_DH))
    q_scale_mla = (MLA_NOPE + MLA_ROPE) ** -0.5 * LOG2E
    q_scale_dif = DIFF_DH ** -0.5 * LOG2E
    gfinal = norm_final.reshape(1, d)

    xp = x_prompt.reshape(mp, d)
    xs = x_sample.reshape(ms, d)
    rows_p, rows_s = [], []
    for l in range(depth):
        lambda_init = 0.8 - 0.6 * math.exp(-0.3 * l)
        wq, wkv, wdq, wdk, wdv, wga, wgb, uq, ukv = _layer_weights(w_in[l], mla_w_uq[l], mla_w_ukv[l])
        gmix, gq, gkv = norm_mix[l].reshape(1, -1), mla_q_norm[l].reshape(1, -1), mla_kv_norm[l].reshape(1, -1)
        lams = tuple(a[l].reshape(1, -1) for a in (diff_lq1, diff_lk1, diff_lq2, diff_lk2))
        gsub = diff_subln[l].reshape(1, -1)
        wa, wbr, wo = w_branch_a[l].astype(BF16), w_branch_b[l].astype(BF16), w_out[l].astype(BF16)
        wfi, wfo = w_ffn_in[l].astype(BF16), w_ffn_out[l].astype(BF16)
        gffn = norm_ffn[l].reshape(1, -1)

        def stage1(x, bm, tabs_mla, tabs_dif, tr):
            h, q, ckv, krope, kmla, vmla = _mla_proj(x, gmix, wq, gq, uq, wkv, gkv, ukv, tabs_mla, bm,
                                                     q_scale_mla, tr)
            q1, q2, dk, dkb, dv, dvb = _diff_proj(h, wdq, wdk, wdv, tabs_dif, bm, q_scale_dif, tr)
            return h, q, ckv, krope, kmla, vmla, q1, q2, dk, dkb, dv, dvb

        def stage2(x, h, oa, ob, bm, bn, bf):
            mg = _merge(h, oa, ob, wga, wgb, wa, wbr, bm, bn)
            x2, h2 = _outproj(x, mg, wo, gffn, min(bm, 512))
            return _ffn(h2, wfi, wfo, x2, gfinal, bm, bf)

        h, q, ckv, krope, kmla, vmla, q1, q2, dk, dkb, dv, dvb = stage1(xp, t_attn, tabs_mla_p, tabs_dif_p, True)
        r3 = lambda a: a.reshape(bp, lp, -1)
        oa = _mla_prompt(q, r3(kmla), vmla, t_attn, 4)
        ob = _diff_prompt(q1, q2, r3(dkb), dvb, lams, gsub, t_attn, 2, lambda_init)
        xp = stage2(xp, h, oa.reshape(mp, -1), ob.reshape(mp, -1), _block(mp, 1024), 1024, 256)
        rows_p.append((ckv.reshape(bp, lp, -1), krope.reshape(bp, lp, -1),
                       dk.reshape(bp, lp, DIFF_HEADS, -1), dv.reshape(bp, lp, DIFF_HEADS, -1)))

        h, q, ckv, krope, kmla, vmla, q1, q2, dk, dkb, dv, dvb = stage1(xs, ms, tabs_mla_s, tabs_dif_s, False)
        kr_pad = jnp.pad(cache_mla_krope.reshape(depth * bs, past, MLA_ROPE), ((0, 0), (0, 0), (0, LANES - MLA_ROPE)))
        r3 = lambda a: a.reshape(bs, ls, -1)
        oa = _mla_sample(r3(q), cache_mla_ckv.reshape(depth * bs, past, -1), kr_pad, r3(ckv), r3(kmla), ukv, l)
        call = lambda a: a.reshape(depth * bs, past * DIFF_HEADS, -1)
        ob = _diff_sample(r3(q1), r3(q2), call(cache_diff_k), call(cache_diff_v), r3(dkb), r3(dvb),
                          lams, gsub, lambda_init, l)
        xs = stage2(xs, h, oa.reshape(ms, -1), ob.reshape(ms, -1), ms, 512, 512)
        rows_s.append((ckv.reshape(bs, ls, -1), krope.reshape(bs, ls, -1),
                       dk.reshape(bs, ls, DIFF_HEADS, -1), dv.reshape(bs, ls, DIFF_HEADS, -1)))

    y_prompt = xp.reshape(bp, lp, d)
    y_sample = xs.reshape(bs, ls, d)
    stack = lambda rows, i: jnp.stack([r[i] for r in rows], axis=0)
    return (y_prompt, y_sample,
            stack(rows_p, 0), stack(rows_p, 1), stack(rows_p, 2), stack(rows_p, 3),
            stack(rows_s, 0), stack(rows_s, 1), stack(rows_s, 2), stack(rows_s, 3))
```

```python
import functools
import math

import jax
import jax.numpy as jnp
from jax import lax
from jax.experimental import pallas as pl
from jax.experimental.pallas import tpu as pltpu

F32 = jnp.float32
BF16 = jnp.bfloat16

CHUNK = 64
EPS = 1e-6
NEG_INF = -1e30

MLA_HEADS = 8
MLA_Q_LORA = 512
MLA_KV_LORA = 256
MLA_NOPE = 128
MLA_ROPE = 64
MLA_V = 128
MLA_THETA = 10000.0
MLA_QK_PAD = 256

DIFF_HEADS = 8
DIFF_DH = 64
DIFF_VD = 2 * DIFF_DH
DIFF_ROT = DIFF_DH // 4
ROPE_THETA = 500000.0

LANES = 128
VT_ONES = 16
LOG2E = math.log2(math.e)
VMEM_LIMIT = 56 * 1024 * 1024


def _params(*sem):
    return pltpu.CompilerParams(dimension_semantics=sem, vmem_limit_bytes=VMEM_LIMIT)


def _dot(a, b):
    return jnp.dot(a, b, preferred_element_type=F32)


def _dot_nt(a, b):
    return lax.dot_general(a, b, (((1,), (1,)), ((), ())), preferred_element_type=F32)


def _rms(x, g):
    return x * lax.rsqrt(jnp.mean(x * x, axis=-1, keepdims=True) + EPS) * g


def _rope_tile(t, c, s1, s2, half):
    return t * c + pltpu.roll(t, LANES - half, 1) * s1 + pltpu.roll(t, half, 1) * s2


def _put(ref, lo, val, tr):
    w = val.shape[1]
    if tr:
        ref[0, lo:lo + w, :] = val.T.astype(ref.dtype)
    else:
        ref[:, lo:lo + w] = val.astype(ref.dtype)


def _put_v(ref, hd, val, tr):
    bm, w = val.shape
    if tr:
        lo = hd * (w + VT_ONES)
        ref[0, lo:lo + w, :] = val.T.astype(ref.dtype)
        ref[0, lo + w:lo + w + VT_ONES, :] = jnp.ones((VT_ONES, bm), ref.dtype)
    else:
        ref[:, hd * w:(hd + 1) * w] = val.astype(ref.dtype)


def _v_width(heads, dv, tr):
    return heads * (dv + VT_ONES) if tr else heads * dv


def _out(m, n, bm, dtype, tr):
    if tr:
        return pl.BlockSpec((1, n, bm), lambda i: (i, 0, 0)), jax.ShapeDtypeStruct((m // bm, n, bm), dtype)
    return pl.BlockSpec((bm, n), lambda i: (i, 0)), jax.ShapeDtypeStruct((m, n), dtype)


def _qproj_body(h, wq_ref, gq_ref, wuq_ref, c, s1, s2, o_ref, scale, tr):
    qlat = _rms(_dot(h, wq_ref[...]), gq_ref[...]).astype(BF16)
    q = _dot(qlat, wuq_ref[...])
    for hd in range(MLA_HEADS):
        lo = hd * MLA_QK_PAD
        _put(o_ref, lo, q[:, lo:lo + LANES] * scale, tr)
        t = _rope_tile(q[:, lo + LANES:lo + 2 * LANES], c, s1, s2, MLA_ROPE // 2)
        _put(o_ref, lo + LANES, t * scale, tr)


def _kv_expand(ckv, krope_tile, wukv_ref, kmla_ref, vmla_ref, tr):
    kv = _dot(ckv.astype(BF16), wukv_ref[...])
    kr = krope_tile.astype(BF16)
    for hd in range(MLA_HEADS):
        lo = hd * MLA_QK_PAD
        kmla_ref[:, lo:lo + LANES] = kv[:, hd * MLA_NOPE:(hd + 1) * MLA_NOPE].astype(BF16)
        kmla_ref[:, lo + LANES:lo + 2 * LANES] = kr
        vlo = MLA_HEADS * MLA_NOPE + hd * MLA_V
        _put_v(vmla_ref, hd, kv[:, vlo:vlo + MLA_V], tr)


def _mla_proj_kernel(x_ref, gmix_ref, wq_ref, gq_ref, wuq_ref, wkv_ref, gkv_ref, wukv_ref, c_ref, s1_ref, s2_ref,
                     h_ref, q_ref, ckv_ref, krope_ref, kmla_ref, vmla_ref, *, scale, tr):
    h = _rms(x_ref[...], gmix_ref[...]).astype(BF16)
    h_ref[...] = h
    c, s1, s2 = c_ref[...], s1_ref[...], s2_ref[...]
    _qproj_body(h, wq_ref, gq_ref, wuq_ref, c, s1, s2, q_ref, scale, tr)
    z = _dot(h, wkv_ref[...])
    ckv = _rms(z[:, :MLA_KV_LORA], gkv_ref[...])
    ckv_ref[...] = ckv
    t = _rope_tile(z[:, MLA_KV_LORA:], c, s1, s2, MLA_ROPE // 2)
    krope_ref[...] = t[:, :MLA_ROPE]
    _kv_expand(ckv, t, wukv_ref, kmla_ref, vmla_ref, tr)


def _mla_proj(x, gmix, wq, gq, wuq, wkv, gkv, wukv, tabs, bm, scale, tr):
    m, d = x.shape
    nt = tabs[0].shape[0] // bm
    full = lambda a: pl.BlockSpec(a.shape, lambda i: (0, 0))
    tab = pl.BlockSpec((bm, LANES), lambda i: (i % nt, 0))
    row = lambda n: pl.BlockSpec((bm, n), lambda i: (i, 0))
    nk = MLA_HEADS * MLA_QK_PAD
    qspec, qshape = _out(m, wuq.shape[1], bm, BF16, tr)
    vspec, vshape = _out(m, _v_width(MLA_HEADS, MLA_V, tr), bm, BF16, tr)
    return pl.pallas_call(
        functools.partial(_mla_proj_kernel, scale=scale, tr=tr),
        grid=(m // bm,),
        in_specs=[row(d), full(gmix), full(wq), full(gq), full(wuq), full(wkv), full(gkv), full(wukv), tab, tab, tab],
        out_specs=[row(d), qspec, row(MLA_KV_LORA), row(MLA_ROPE), row(nk), vspec],
        out_shape=[jax.ShapeDtypeStruct((m, d), BF16), qshape,
                   jax.ShapeDtypeStruct((m, MLA_KV_LORA), F32), jax.ShapeDtypeStruct((m, MLA_ROPE), F32),
                   jax.ShapeDtypeStruct((m, nk), BF16), vshape],
        compiler_params=_params("parallel"),
        name="norm_mla_proj",
    )(x, gmix, wq, gq, wuq, wkv, gkv, wukv, *tabs)


def _diff_proj_kernel(h_ref, wdq_ref, wdk_ref, wdv_ref, c_ref, s1_ref, s2_ref,
                      q1_ref, q2_ref, dk_ref, dkb_ref, dv_ref, dvb_ref, *, scale, tr):
    h = h_ref[...]
    zq = _dot(h, wdq_ref[...])
    zk = _dot(h, wdk_ref[...])
    zv = _dot(h, wdv_ref[...])
    dv_ref[...] = zv
    c, s1, s2 = c_ref[...], s1_ref[...], s2_ref[...]
    first = lax.broadcasted_iota(jnp.int32, (1, LANES), 1) < DIFF_DH
    for hd in range(DIFF_HEADS):
        sl = slice(hd * LANES, (hd + 1) * LANES)
        q = _rope_tile(zq[:, sl], c, s1, s2, DIFF_ROT // 2) * scale
        _put(q1_ref, hd * LANES, jnp.where(first, q, 0.0), tr)
        _put(q2_ref, hd * LANES, jnp.where(first, 0.0, q), tr)
        k = _rope_tile(zk[:, sl], c, s1, s2, DIFF_ROT // 2)
        dk_ref[:, sl] = k
        dkb_ref[:, sl] = k.astype(BF16)
        _put_v(dvb_ref, hd, zv[:, sl], tr)


def _diff_proj(h, wdq, wdk, wdv, tabs, bm, scale, tr):
    m, d = h.shape
    nt = tabs[0].shape[0] // bm
    n = wdq.shape[1]
    full = lambda a: pl.BlockSpec(a.shape, lambda i: (0, 0))
    tab = pl.BlockSpec((bm, LANES), lambda i: (i % nt, 0))
    row = lambda w: pl.BlockSpec((bm, w), lambda i: (i, 0))
    qspec, qshape = _out(m, n, bm, BF16, tr)
    vspec, vshape = _out(m, _v_width(DIFF_HEADS, DIFF_VD, tr), bm, BF16, tr)
    f32_rows, bf16_rows = jax.ShapeDtypeStruct((m, n), F32), jax.ShapeDtypeStruct((m, n), BF16)
    return pl.pallas_call(
        functools.partial(_diff_proj_kernel, scale=scale, tr=tr),
        grid=(m // bm,),
        in_specs=[row(d), full(wdq), full(wdk), full(wdv), tab, tab, tab],
        out_specs=[qspec, qspec, row(n), row(n), row(n), vspec],
        out_shape=[qshape, qshape, f32_rows, bf16_rows, f32_rows, vshape],
        compiler_params=_params("parallel"),
        name="diff_proj",
    )(h, wdq, wdk, wdv, *tabs)


def _merge_kernel(h_ref, oa_ref, ob_ref, wga_ref, wgb_ref, wa_ref, wb_ref, o_ref):
    h = h_ref[...]
    ga = jax.nn.sigmoid(_dot(h, wga_ref[...]))
    gb = jax.nn.sigmoid(_dot(h, wgb_ref[...]))
    ya = _dot(oa_ref[...], wa_ref[...])
    yb = _dot(ob_ref[...], wb_ref[...])
    o_ref[...] = (ga * ya + gb * yb).astype(o_ref.dtype)


def _merge(h, oa, ob, wga, wgb, wa, wb, bm, bn):
    m, d = h.shape
    n = wga.shape[1]
    row = lambda a: pl.BlockSpec((bm, a.shape[1]), lambda i, j: (i, 0))
    col = lambda a: pl.BlockSpec((a.shape[0], bn), lambda i, j: (0, j))
    return pl.pallas_call(
        _merge_kernel,
        grid=(m // bm, n // bn),
        in_specs=[row(h), row(oa), row(ob), col(wga), col(wgb), col(wa), col(wb)],
        out_specs=pl.BlockSpec((bm, bn), lambda i, j: (i, j)),
        out_shape=jax.ShapeDtypeStruct((m, n), BF16),
        compiler_params=_params("parallel", "arbitrary"),
        name="gated_merge",
    )(h, oa, ob, wga, wgb, wa, wb)


def _outproj_kernel(x_ref, mg_ref, wo_ref, g_ref, x2_ref, h2_ref):
    x2 = x_ref[...] + _dot(mg_ref[...], wo_ref[...])
    x2_ref[...] = x2
    h2_ref[...] = _rms(x2, g_ref[...]).astype(BF16)


def _outproj(x, mg, wo, g, bm):
    m, d = x.shape
    row = pl.BlockSpec((bm, d), lambda i: (i, 0))
    return pl.pallas_call(
        _outproj_kernel,
        grid=(m // bm,),
        in_specs=[row, row, pl.BlockSpec(wo.shape, lambda i: (0, 0)), pl.BlockSpec((1, d), lambda i: (0, 0))],
        out_specs=[row, row],
        out_shape=[jax.ShapeDtypeStruct((m, d), F32), jax.ShapeDtypeStruct((m, d), BF16)],
        compiler_params=_params("parallel"),
        name="out_proj_residual",
    )(x, mg, wo, g)


def _ffn_kernel(h2_ref, wg_ref, wu_ref, wd_ref, x2_ref, gf_ref, y_ref):
    f = pl.program_id(1)

    @pl.when(f == 0)
    def _():
        y_ref[...] = x2_ref[...]

    h2 = h2_ref[...]
    a = (jax.nn.silu(_dot(h2, wg_ref[...])) * _dot(h2, wu_ref[...])).astype(BF16)
    y_ref[...] += _dot(a, wd_ref[...])

    @pl.when(f == pl.num_programs(1) - 1)
    def _():
        y_ref[...] = _rms(y_ref[...], gf_ref[...])


def _ffn(h2, w_in, w_out, x2, gf, bm, bf):
    m, d = h2.shape
    dff = w_out.shape[0]
    nf = dff // bf
    row = pl.BlockSpec((bm, d), lambda i, f: (i, 0))
    return pl.pallas_call(
        _ffn_kernel,
        grid=(m // bm, nf),
        in_specs=[row,
                  pl.BlockSpec((d, bf), lambda i, f: (0, f)),
                  pl.BlockSpec((d, bf), lambda i, f: (0, f + nf)),
                  pl.BlockSpec((bf, d), lambda i, f: (f, 0)),
                  row,
                  pl.BlockSpec((1, d), lambda i, f: (0, 0))],
        out_specs=row,
        out_shape=jax.ShapeDtypeStruct((m, d), F32),
        compiler_params=_params("parallel", "arbitrary"),
        name="ffn_swiglu_final_norm",
    )(h2, w_in, w_in, w_out, x2, gf)


def _softmax_step(s, v, m, l, acc):
    m_new = jnp.maximum(m, jnp.max(s, axis=-1, keepdims=True))
    alpha = jnp.exp2(m - m_new)
    p = jnp.exp2(s - m_new)
    l = alpha * l + jnp.sum(p, axis=-1, keepdims=True)
    acc = alpha * acc + _dot(p.astype(BF16), v)
    return m_new, l, acc


def _chunk_mask(tq, tk, q0, k0):
    qp = q0 + lax.broadcasted_iota(jnp.int32, (tq, tk), 0)
    kp = k0 + lax.broadcasted_iota(jnp.int32, (tq, tk), 1)
    return (kp // CHUNK) <= (qp // CHUNK)


def _softmax_init(tq, dv):
    return (jnp.full((tq, 1), NEG_INF, F32), jnp.zeros((tq, 1), F32), jnp.zeros((tq, dv), F32))


def _chunk_mask_t(t):
    kp = lax.broadcasted_iota(jnp.int32, (t, t), 0)
    qp = lax.broadcasted_iota(jnp.int32, (t, t), 1)
    return (kp // CHUNK) <= (qp // CHUNK)


def _probs_t(s, smax, m):
    m_new = jnp.maximum(m, smax)
    return jnp.exp2(s - m_new).astype(BF16), jnp.exp2(m - m_new), m_new


def _attention_scratch(t, dv, streams):
    return [pltpu.VMEM((streams, 2, t, t), F32),
            pltpu.VMEM((streams, dv + VT_ONES, t), F32),
            pltpu.VMEM((streams, 8, t), F32)]


def _causal_attention_t(n, scores, next_scores, values, finish, s_ref, acc_ref, stat_ref, t, streams):
    row_m, row_smax = 0, 1

    def stat(st, r):
        return stat_ref[st, r:r + 1, :]

    def put_scores(st, slot, s):
        s_ref[st, slot] = s
        stat_ref[st, row_smax + slot:row_smax + slot + 1, :] = jnp.max(s, axis=0, keepdims=True)

    def accumulate(st, j, s, smax):
        p, alpha, m = _probs_t(s, smax, stat(st, row_m))
        acc = alpha * acc_ref[st] + _dot(values(j, st), p)
        stat_ref[st, row_m:row_m + 1, :] = m
        return acc

    @pl.when(n == 0)
    def _():
        for st in range(streams):
            put_scores(st, 0, scores(0, st))

    for st in range(streams):
        acc_ref[st] = jnp.zeros(acc_ref.shape[1:], F32)
        stat_ref[st, row_m:row_m + 1, :] = jnp.full((1, t), NEG_INF, F32)

    def run(first):
        def step(tau, par):
            for st in range(streams):
                put_scores(st, (first + par) % 2, scores(tau, st))
                prev = (first + par + 1) % 2
                acc_ref[st] = accumulate(st, tau - 1, s_ref[st, prev], stat(st, row_smax + prev))

        unroll = 4

        def trip(u, carry):
            for k in range(unroll):
                step(unroll * u + 1 + k, (1 + k) % 2)
            return carry

        lax.fori_loop(0, n // unroll, trip, 0)
        done = (n // unroll) * unroll

        @pl.when(n - done >= 2)
        def _():
            step(done + 1, 1)
            step(done + 2, 0)

        @pl.when(n % 2 == 1)
        def _():
            step(n, 1)

        def tail(par):
            diag = (first + par) % 2
            for st in range(streams):
                put_scores(st, 1 - diag, next_scores(st))
            mask = _chunk_mask_t(t)
            res = []
            for st in range(streams):
                s = jnp.where(mask, s_ref[st, diag], NEG_INF)
                res.append(accumulate(st, n, s, jnp.max(s, axis=0, keepdims=True)))
            finish(res)

        for par in (0, 1):
            pl.when(n % 2 == par)(functools.partial(tail, par))

    first = ((n + 1) // 2) % 2
    for f in (0, 1):
        pl.when(first == f)(functools.partial(run, f))


def _normalised(acc, dv):
    return (acc[:dv] / acc[dv:dv + 1]).T


def _mla_prompt_kernel(qt_ref, qn_ref, k_ref, vt_ref, o_ref, *scratch, t, heads):
    def block_scores(q_ref, j, hd):
        k = k_ref[0, pl.ds(pl.multiple_of(j * t, t), t), hd * MLA_QK_PAD:(hd + 1) * MLA_QK_PAD]
        return _dot(k, q_ref[0, hd * MLA_QK_PAD:(hd + 1) * MLA_QK_PAD, :])

    scores = functools.partial(block_scores, qt_ref)
    next_scores = functools.partial(block_scores, qn_ref, 0)

    def values(j, hd):
        return vt_ref[j, hd * (MLA_V + VT_ONES):(hd + 1) * (MLA_V + VT_ONES), :]

    def finish(res):
        for hd, acc in enumerate(res):
            o_ref[0, :, hd * MLA_V:(hd + 1) * MLA_V] = _normalised(acc, MLA_V).astype(o_ref.dtype)

    _causal_attention_t(pl.program_id(2), scores, next_scores, values, finish, *scratch, t, heads)


def _mla_prompt(qt, k, vt, t, heads):
    b, l, _ = k.shape
    nq = l // t
    qblock = lambda step: pl.BlockSpec((1, heads * MLA_QK_PAD, t),
                                       lambda b, h, i: (b * nq + jnp.minimum(i + step, nq - 1), h, 0))
    return pl.pallas_call(
        functools.partial(_mla_prompt_kernel, t=t, heads=heads),
        grid=(b, MLA_HEADS // heads, nq),
        in_specs=[qblock(0), qblock(1),
                  pl.BlockSpec((1, l, heads * MLA_QK_PAD), lambda b, h, i: (b, 0, h), pipeline_mode=pl.Buffered(1)),
                  pl.BlockSpec((nq, heads * (MLA_V + VT_ONES), t), lambda b, h, i: (b, h, 0),
                               pipeline_mode=pl.Buffered(1))],
        out_specs=pl.BlockSpec((1, t, heads * MLA_V), lambda b, h, i: (b, i, h)),
        out_shape=jax.ShapeDtypeStruct((b, l, MLA_HEADS * MLA_V), BF16),
        scratch_shapes=_attention_scratch(t, MLA_V, heads),
        compiler_params=_params("parallel", "parallel", "arbitrary"),
        name="mla_attention_prompt",
    )(qt, qt, k, vt)


def _lambda(lq1_ref, lk1_ref, lq2_ref, lk2_ref, lambda_init):
    a = jnp.sum(lq1_ref[...] * lk1_ref[...], axis=-1, keepdims=True)
    b = jnp.sum(lq2_ref[...] * lk2_ref[...], axis=-1, keepdims=True)
    return jnp.exp(a) - jnp.exp(b) + lambda_init


def _diff_prompt_kernel(q1t_ref, q2t_ref, q1n_ref, q2n_ref, k_ref, vt_ref, lq1_ref, lk1_ref, lq2_ref, lk2_ref,
                        g_ref, o_ref, *scratch, t, heads, lambda_init):
    qt_refs, qn_refs = (q1t_ref, q2t_ref), (q1n_ref, q2n_ref)

    def block_scores(q_refs, rows, st):
        cols = slice((st // 2) * LANES, (st // 2 + 1) * LANES)
        return _dot(k_ref[0, rows, cols], q_refs[st % 2][0, cols, :])

    def scores(j, st):
        return block_scores(qt_refs, pl.ds(pl.multiple_of(j * t, t), t), st)

    def next_scores(st):
        return block_scores(qn_refs, slice(0, t), st)

    def values(j, st):
        return vt_ref[j, (st // 2) * (DIFF_VD + VT_ONES):(st // 2 + 1) * (DIFF_VD + VT_ONES), :]

    def finish(res):
        lam = _lambda(lq1_ref, lk1_ref, lq2_ref, lk2_ref, lambda_init)
        for hd in range(heads):
            o = _normalised(res[2 * hd], DIFF_VD) - lam * _normalised(res[2 * hd + 1], DIFF_VD)
            o = _rms(o, g_ref[...]) * (1.0 - lambda_init)
            o_ref[0, :, hd * DIFF_VD:(hd + 1) * DIFF_VD] = o.astype(o_ref.dtype)

    _causal_attention_t(pl.program_id(2), scores, next_scores, values, finish, *scratch, t, 2 * heads)


def _diff_prompt(q1t, q2t, k, vt, lams, g, t, heads, lambda_init):
    b, l, _ = k.shape
    nq = l // t
    qs = pl.BlockSpec((1, heads * LANES, t), lambda b, h, i: (b * nq + i, h, 0))
    qn = pl.BlockSpec((1, heads * LANES, t), lambda b, h, i: (b * nq + jnp.minimum(i + 1, nq - 1), h, 0))
    small = lambda a: pl.BlockSpec(a.shape, lambda b, h, i: (0, 0))
    return pl.pallas_call(
        functools.partial(_diff_prompt_kernel, t=t, heads=heads, lambda_init=lambda_init),
        grid=(b, DIFF_HEADS // heads, nq),
        in_specs=[qs, qs, qn, qn,
                  pl.BlockSpec((1, l, heads * LANES), lambda b, h, i: (b, 0, h)),
                  pl.BlockSpec((nq, heads * (DIFF_VD + VT_ONES), t), lambda b, h, i: (b, h, 0))]
                 + [small(a) for a in lams] + [small(g)],
        out_specs=pl.BlockSpec((1, t, heads * DIFF_VD), lambda b, h, i: (b, i, h)),
        out_shape=jax.ShapeDtypeStruct((b, l, DIFF_HEADS * DIFF_VD), BF16),
        scratch_shapes=_attention_scratch(t, DIFF_VD, 2 * heads),
        compiler_params=_params("parallel", "parallel", "arbitrary"),
        name="diff_attention_prompt",
    )(q1t, q2t, q1t, q2t, k, vt, *lams, g)


def _two_part_softmax(q, kc, vc, kn, vn, mask_c, mask_n):
    tq = q.shape[0]
    carry = _softmax_init(tq, vc.shape[1])
    carry = _softmax_step(jnp.where(mask_c, _dot_nt(q, kc), NEG_INF), vc, *carry)
    return _softmax_step(jnp.where(mask_n, _dot_nt(q, kn), NEG_INF), vn, *carry)


def _mla_sample_kernel(q_ref, latc_ref, krc_ref, latn_ref, kn_ref, wukv_ref, o_ref, *, past):
    tq = q_ref.shape[1]
    latc, latn = latc_ref[0].astype(BF16), latn_ref[0].astype(BF16)
    krc, krn = krc_ref[0].astype(BF16), kn_ref[0, :, LANES:2 * LANES]
    qa, qr = [], []
    for hd in range(MLA_HEADS):
        lo = hd * MLA_QK_PAD
        wk = wukv_ref[:, hd * MLA_NOPE:(hd + 1) * MLA_NOPE]
        qa.append(_dot_nt(q_ref[0, :, lo:lo + LANES], wk).astype(BF16))
        qr.append(q_ref[0, :, lo + LANES:lo + 2 * LANES])
    qa, qr = jnp.concatenate(qa, axis=0), jnp.concatenate(qr, axis=0)
    rows = MLA_HEADS * tq

    def mask(tk, k0):
        qp = past + lax.broadcasted_iota(jnp.int32, (rows, tk), 0) % tq
        kp = k0 + lax.broadcasted_iota(jnp.int32, (rows, tk), 1)
        return (kp // CHUNK) <= (qp // CHUNK)

    carry = _softmax_init(rows, MLA_KV_LORA)
    s = _dot_nt(qa, latc) + _dot_nt(qr, krc)
    carry = _softmax_step(jnp.where(mask(past, 0), s, NEG_INF), latc, *carry)
    s = _dot_nt(qa, latn) + _dot_nt(qr, krn)
    _, l, acc = _softmax_step(jnp.where(mask(tq, past), s, NEG_INF), latn, *carry)
    olat = (acc / l).astype(BF16)
    v0 = MLA_HEADS * MLA_NOPE
    for hd in range(MLA_HEADS):
        wv = wukv_ref[:, v0 + hd * MLA_V:v0 + (hd + 1) * MLA_V]
        o_ref[0, :, hd * MLA_V:(hd + 1) * MLA_V] = _dot(olat[hd * tq:(hd + 1) * tq], wv).astype(o_ref.dtype)


def _mla_sample(q, latc, krc, latn, kn, wukv, layer):
    b, tq, _ = q.shape
    blk = lambda a: pl.BlockSpec((1,) + a.shape[1:], lambda b: (b, 0, 0))
    cache = lambda a: pl.BlockSpec((1,) + a.shape[1:], lambda b: (layer * q.shape[0] + b, 0, 0))
    return pl.pallas_call(
        functools.partial(_mla_sample_kernel, past=latc.shape[1]),
        grid=(b,),
        in_specs=[blk(q), cache(latc), cache(krc), blk(latn), blk(kn), pl.BlockSpec(wukv.shape, lambda b: (0, 0))],
        out_specs=pl.BlockSpec((1, tq, MLA_HEADS * MLA_V), lambda b: (b, 0, 0)),
        out_shape=jax.ShapeDtypeStruct((b, tq, MLA_HEADS * MLA_V), BF16),
        compiler_params=_params("parallel"),
        name="mla_attention_sample",
    )(q, latc, krc, latn, kn, wukv)


def _diff_sample_kernel(q1_ref, q2_ref, kc_ref, vc_ref, kn_ref, vn_ref,
                        lq1_ref, lk1_ref, lq2_ref, lk2_ref, g_ref, o_ref, *, past, lambda_init):
    tq = q1_ref.shape[1]
    mask_c, mask_n = (jnp.concatenate([m, m], axis=0)
                      for m in (_chunk_mask(tq, past, past, 0), _chunk_mask(tq, tq, past, past)))
    lam = _lambda(lq1_ref, lk1_ref, lq2_ref, lk2_ref, lambda_init)
    for hd in range(DIFF_HEADS):
        rows = pl.ds(hd, past, stride=DIFF_HEADS)
        cols = slice(hd * LANES, (hd + 1) * LANES)
        kc, vc = kc_ref[0, rows, :].astype(BF16), vc_ref[0, rows, :].astype(BF16)
        q = jnp.concatenate([q1_ref[0, :, cols], q2_ref[0, :, cols]], axis=0)
        _, l, acc = _two_part_softmax(q, kc, vc, kn_ref[0, :, cols], vn_ref[0, :, cols], mask_c, mask_n)
        o = acc / l
        o = _rms(o[:tq] - lam * o[tq:], g_ref[...]) * (1.0 - lambda_init)
        o_ref[0, :, cols] = o.astype(o_ref.dtype)


def _diff_sample(q1, q2, kc, vc, kn, vn, lams, g, lambda_init, layer):
    b, tq, _ = q1.shape
    past = kc.shape[1] // DIFF_HEADS
    blk = lambda a: pl.BlockSpec((1,) + a.shape[1:], lambda b: (b, 0, 0))
    cache = pl.BlockSpec((1,) + kc.shape[1:], lambda b: (layer * q1.shape[0] + b, 0, 0))
    small = lambda a: pl.BlockSpec(a.shape, lambda b: (0, 0))
    return pl.pallas_call(
        functools.partial(_diff_sample_kernel, past=past, lambda_init=lambda_init),
        grid=(b,),
        in_specs=[blk(q1), blk(q2), cache, cache, blk(kn), blk(vn)] + [small(a) for a in lams] + [small(g)],
        out_specs=blk(q1),
        out_shape=jax.ShapeDtypeStruct((b, tq, DIFF_HEADS * DIFF_VD), BF16),
        compiler_params=_params("parallel"),
        name="diff_attention_sample",
    )(q1, q2, kc, vc, kn, vn, *lams, g)


def _rope_tables(pos, theta, rot_dim, period):
    half = rot_dim // 2
    inv = 1.0 / (jnp.float32(theta) ** (jnp.arange(half, dtype=F32) / half))
    ang = pos.astype(F32)[:, None] * inv[None, :]
    cos, sin = jnp.cos(ang), jnp.sin(ang)
    n = pos.shape[0]
    rest = period - rot_dim
    c = jnp.concatenate([cos, cos, jnp.ones((n, rest), F32)], axis=1)
    s1 = jnp.concatenate([-sin, jnp.zeros((n, half + rest), F32)], axis=1)
    s2 = jnp.concatenate([jnp.zeros((n, half), F32), sin, jnp.zeros((n, rest), F32)], axis=1)
    reps = LANES // period
    return tuple(jnp.tile(a, (1, reps)) for a in (c, s1, s2))


def _layer_weights(w_in, mla_w_uq, mla_w_ukv):
    c_kv = MLA_Q_LORA
    c_kr = c_kv + MLA_KV_LORA
    c_dq = c_kr + MLA_ROPE
    dqk = DIFF_HEADS * 2 * DIFF_DH
    c_dk = c_dq + dqk
    c_dv = c_dk + dqk
    c_ga = c_dv + DIFF_HEADS * DIFF_VD
    d = w_in.shape[0]
    c_gb = c_ga + d
    cols = lambda lo, hi: w_in[:, lo:hi].astype(BF16)
    wq = cols(0, c_kv)
    wkv = jnp.pad(cols(c_kv, c_dq), ((0, 0), (0, LANES - MLA_ROPE)))
    wdq, wdk, wdv = cols(c_dq, c_dk), cols(c_dk, c_dv), cols(c_dv, c_ga)
    wga, wgb = cols(c_ga, c_gb), cols(c_gb, w_in.shape[1])
    uq = mla_w_uq.astype(BF16).reshape(MLA_Q_LORA, MLA_HEADS, MLA_NOPE + MLA_ROPE)
    uq = jnp.pad(uq, ((0, 0), (0, 0), (0, MLA_QK_PAD - MLA_NOPE - MLA_ROPE))).reshape(MLA_Q_LORA, -1)
    ukv = mla_w_ukv.astype(BF16).reshape(MLA_KV_LORA, MLA_HEADS, MLA_NOPE + MLA_V)
    ukv = jnp.concatenate([ukv[:, :, :MLA_NOPE].reshape(MLA_KV_LORA, -1),
                           ukv[:, :, MLA_NOPE:].reshape(MLA_KV_LORA, -1)], axis=1)
    return wq, wkv, wdq, wdk, wdv, wga, wgb, uq, ukv


def _block(m, want):
    return want if m % want == 0 else m


def kernel(x_prompt, x_sample, cache_mla_ckv, cache_mla_krope, cache_diff_k, cache_diff_v, norm_mix, w_in,
           mla_q_norm, mla_w_uq, mla_kv_norm, mla_w_ukv, diff_lq1, diff_lk1, diff_lq2, diff_lk2, diff_subln,
           w_branch_a, w_branch_b, w_out, norm_ffn, w_ffn_in, w_ffn_out, norm_final):
    bp, lp, d = x_prompt.shape
    bs, ls, _ = x_sample.shape
    depth, _, past, _ = cache_mla_ckv.shape
    assert depth == 1, "the FFN kernel fuses the final norm, so it serves the last (only) layer"
    mp, ms = bp * lp, bs * ls
    t_attn = 512

    pos_p = jnp.arange(lp)
    pos_s = past + jnp.arange(ls)
    tabs_mla_p = _rope_tables(pos_p, MLA_THETA, MLA_ROPE, LANES)
    tabs_mla_s = tuple(jnp.tile(a, (bs, 1)) for a in _rope_tables(pos_s, MLA_THETA, MLA_ROPE, LANES))
    tabs_dif_p = _rope_tables(pos_p, ROPE_THETA, DIFF_ROT, DIFF_DH)
    tabs_dif_s = tuple(jnp.tile(a, (bs, 1)) for a in _rope_tables(pos_s, ROPE_THETA, DIFF_ROT, DIFF_DH))
    q_scale_mla = (MLA_NOPE + MLA_ROPE) ** -0.5 * LOG2E
    q_scale_dif = DIFF_DH ** -0.5 * LOG2E
    gfinal = norm_final.reshape(1, d)

    xp = x_prompt.reshape(mp, d)
    xs = x_sample.reshape(ms, d)
    rows_p, rows_s = [], []
    for l in range(depth):
        lambda_init = 0.8 - 0.6 * math.exp(-0.3 * l)
        wq, wkv, wdq, wdk, wdv, wga, wgb, uq, ukv = _layer_weights(w_in[l], mla_w_uq[l], mla_w_ukv[l])
        gmix, gq, gkv = norm_mix[l].reshape(1, -1), mla_q_norm[l].reshape(1, -1), mla_kv_norm[l].reshape(1, -1)
        lams = tuple(a[l].reshape(1, -1) for a in (diff_lq1, diff_lk1, diff_lq2, diff_lk2))
        gsub = diff_subln[l].reshape(1, -1)
        wa, wbr, wo = w_branch_a[l].astype(BF16), w_branch_b[l].astype(BF16), w_out[l].astype(BF16)
        wfi, wfo = w_ffn_in[l].astype(BF16), w_ffn_out[l].astype(BF16)
        gffn = norm_ffn[l].reshape(1, -1)

        def stage1(x, bm, tabs_mla, tabs_dif, tr):
            h, q, ckv, krope, kmla, vmla = _mla_proj(x, gmix, wq, gq, uq, wkv, gkv, ukv, tabs_mla, bm,
                                                     q_scale_mla, tr)
            q1, q2, dk, dkb, dv, dvb = _diff_proj(h, wdq, wdk, wdv, tabs_dif, bm, q_scale_dif, tr)
            return h, q, ckv, krope, kmla, vmla, q1, q2, dk, dkb, dv, dvb

        def stage2(x, h, oa, ob, bm, bn, bf):
            mg = _merge(h, oa, ob, wga, wgb, wa, wbr, bm, bn)
            x2, h2 = _outproj(x, mg, wo, gffn, min(bm, 512))
            return _ffn(h2, wfi, wfo, x2, gfinal, bm, bf)

        h, q, ckv, krope, kmla, vmla, q1, q2, dk, dkb, dv, dvb = stage1(xp, t_attn, tabs_mla_p, tabs_dif_p, True)
        r3 = lambda a: a.reshape(bp, lp, -1)
        oa = _mla_prompt(q, r3(kmla), vmla, t_attn, 4)
        ob = _diff_prompt(q1, q2, r3(dkb), dvb, lams, gsub, t_attn, 2, lambda_init)
        xp = stage2(xp, h, oa.reshape(mp, -1), ob.reshape(mp, -1), _block(mp, 1024), 1024, 256)
        rows_p.append((ckv.reshape(bp, lp, -1), krope.reshape(bp, lp, -1),
                       dk.reshape(bp, lp, DIFF_HEADS, -1), dv.reshape(bp, lp, DIFF_HEADS, -1)))

        h, q, ckv, krope, kmla, vmla, q1, q2, dk, dkb, dv, dvb = stage1(xs, ms, tabs_mla_s, tabs_dif_s, False)
        kr_pad = jnp.pad(cache_mla_krope.reshape(depth * bs, past, MLA_ROPE), ((0, 0), (0, 0), (0, LANES - MLA_ROPE)))
        r3 = lambda a: a.reshape(bs, ls, -1)
        oa = _mla_sample(r3(q), cache_mla_ckv.reshape(depth * bs, past, -1), kr_pad, r3(ckv), r3(kmla), ukv, l)
        call = lambda a: a.reshape(depth * bs, past * DIFF_HEADS, -1)
        ob = _diff_sample(r3(q1), r3(q2), call(cache_diff_k), call(cache_diff_v), r3(dkb), r3(dvb),
                          lams, gsub, lambda_init, l)
        xs = stage2(xs, h, oa.reshape(ms, -1), ob.reshape(ms, -1), ms, 512, 512)
        rows_s.append((ckv.reshape(bs, ls, -1), krope.reshape(bs, ls, -1),
                       dk.reshape(bs, ls, DIFF_HEADS, -1), dv.reshape(bs, ls, DIFF_HEADS, -1)))

    y_prompt = xp.reshape(bp, lp, d)
    y_sample = xs.reshape(bs, ls, d)
    stack = lambda rows, i: jnp.stack([r[i] for r in rows], axis=0)
    return (y_prompt, y_sample,
            stack(rows_p, 0), stack(rows_p, 1), stack(rows_p, 2), stack(rows_p, 3),
            stack(rows_s, 0), stack(rows_s, 1), stack(rows_s, 2), stack(rows_s, 3))
```

```python
import functools
import math

import jax
import jax.numpy as jnp
from jax import lax
from jax.experimental import pallas as pl
from jax.experimental.pallas import tpu as pltpu

F32 = jnp.float32
BF16 = jnp.bfloat16

CHUNK = 64
EPS = 1e-6
NEG_INF = -1e30

MLA_HEADS = 8
MLA_Q_LORA = 512
MLA_KV_LORA = 256
MLA_NOPE = 128
MLA_ROPE = 64
MLA_V = 128
MLA_THETA = 10000.0
MLA_QK_PAD = 256

DIFF_HEADS = 8
DIFF_DH = 64
DIFF_VD = 2 * DIFF_DH
DIFF_ROT = DIFF_DH // 4
ROPE_THETA = 500000.0

LANES = 128
VT_ONES = 16
LOG2E = math.log2(math.e)
VMEM_LIMIT = 56 * 1024 * 1024


def _params(*sem):
    return pltpu.CompilerParams(dimension_semantics=sem, vmem_limit_bytes=VMEM_LIMIT)


def _dot(a, b):
    return jnp.dot(a, b, preferred_element_type=F32)


def _dot_nt(a, b):
    return lax.dot_general(a, b, (((1,), (1,)), ((), ())), preferred_element_type=F32)


def _rms(x, g):
    return x * lax.rsqrt(jnp.mean(x * x, axis=-1, keepdims=True) + EPS) * g


def _rope_tile(t, c, s1, s2, half):
    return t * c + pltpu.roll(t, LANES - half, 1) * s1 + pltpu.roll(t, half, 1) * s2


def _put(ref, lo, val, tr):
    w = val.shape[1]
    if tr:
        ref[0, lo:lo + w, :] = val.T.astype(ref.dtype)
    else:
        ref[:, lo:lo + w] = val.astype(ref.dtype)


def _put_v(ref, hd, val, tr):
    bm, w = val.shape
    if tr:
        lo = hd * (w + VT_ONES)
        ref[0, lo:lo + w, :] = val.T.astype(ref.dtype)
        ref[0, lo + w:lo + w + VT_ONES, :] = jnp.ones((VT_ONES, bm), ref.dtype)
    else:
        ref[:, hd * w:(hd + 1) * w] = val.astype(ref.dtype)


def _v_width(heads, dv, tr):
    return heads * (dv + VT_ONES) if tr else heads * dv


def _out(m, n, bm, dtype, tr):
    if tr:
        return pl.BlockSpec((1, n, bm), lambda i: (i, 0, 0)), jax.ShapeDtypeStruct((m // bm, n, bm), dtype)
    return pl.BlockSpec((bm, n), lambda i: (i, 0)), jax.ShapeDtypeStruct((m, n), dtype)


def _qproj_body(h, wq_ref, gq_ref, wuq_ref, c, s1, s2, o_ref, scale, tr):
    qlat = _rms(_dot(h, wq_ref[...]), gq_ref[...]).astype(BF16)
    q = _dot(qlat, wuq_ref[...])
    for hd in range(MLA_HEADS):
        lo = hd * MLA_QK_PAD
        _put(o_ref, lo, q[:, lo:lo + LANES] * scale, tr)
        t = _rope_tile(q[:, lo + LANES:lo + 2 * LANES], c, s1, s2, MLA_ROPE // 2)
        _put(o_ref, lo + LANES, t * scale, tr)


def _kv_expand(ckv, krope_tile, wukv_ref, kmla_ref, vmla_ref, tr):
    kv = _dot(ckv.astype(BF16), wukv_ref[...])
    kr = krope_tile.astype(BF16)
    for hd in range(MLA_HEADS):
        lo = hd * MLA_QK_PAD
        kmla_ref[:, lo:lo + LANES] = kv[:, hd * MLA_NOPE:(hd + 1) * MLA_NOPE].astype(BF16)
        kmla_ref[:, lo + LANES:lo + 2 * LANES] = kr
        vlo = MLA_HEADS * MLA_NOPE + hd * MLA_V
        _put_v(vmla_ref, hd, kv[:, vlo:vlo + MLA_V], tr)


def _mla_proj_kernel(x_ref, gmix_ref, wq_ref, gq_ref, wuq_ref, wkv_ref, gkv_ref, wukv_ref, c_ref, s1_ref, s2_ref,
                     h_ref, q_ref, ckv_ref, krope_ref, kmla_ref, vmla_ref, *, scale, tr):
    h = _rms(x_ref[...], gmix_ref[...]).astype(BF16)
    h_ref[...] = h
    c, s1, s2 = c_ref[...], s1_ref[...], s2_ref[...]
    _qproj_body(h, wq_ref, gq_ref, wuq_ref, c, s1, s2, q_ref, scale, tr)
    z = _dot(h, wkv_ref[...])
    ckv = _rms(z[:, :MLA_KV_LORA], gkv_ref[...])
    ckv_ref[...] = ckv
    t = _rope_tile(z[:, MLA_KV_LORA:], c, s1, s2, MLA_ROPE // 2)
    krope_ref[...] = t[:, :MLA_ROPE]
    _kv_expand(ckv, t, wukv_ref, kmla_ref, vmla_ref, tr)


def _mla_proj(x, gmix, wq, gq, wuq, wkv, gkv, wukv, tabs, bm, scale, tr):
    m, d = x.shape
    nt = tabs[0].shape[0] // bm
    full = lambda a: pl.BlockSpec(a.shape, lambda i: (0, 0))
    tab = pl.BlockSpec((bm, LANES), lambda i: (i % nt, 0))
    row = lambda n: pl.BlockSpec((bm, n), lambda i: (i, 0))
    nk = MLA_HEADS * MLA_QK_PAD
    qspec, qshape = _out(m, wuq.shape[1], bm, BF16, tr)
    vspec, vshape = _out(m, _v_width(MLA_HEADS, MLA_V, tr), bm, BF16, tr)
    return pl.pallas_call(
        functools.partial(_mla_proj_kernel, scale=scale, tr=tr),
        grid=(m // bm,),
        in_specs=[row(d), full(gmix), full(wq), full(gq), full(wuq), full(wkv), full(gkv), full(wukv), tab, tab, tab],
        out_specs=[row(d), qspec, row(MLA_KV_LORA), row(MLA_ROPE), row(nk), vspec],
        out_shape=[jax.ShapeDtypeStruct((m, d), BF16), qshape,
                   jax.ShapeDtypeStruct((m, MLA_KV_LORA), F32), jax.ShapeDtypeStruct((m, MLA_ROPE), F32),
                   jax.ShapeDtypeStruct((m, nk), BF16), vshape],
        compiler_params=_params("parallel"),
        name="norm_mla_proj",
    )(x, gmix, wq, gq, wuq, wkv, gkv, wukv, *tabs)


def _diff_proj_kernel(h_ref, wdq_ref, wdk_ref, wdv_ref, c_ref, s1_ref, s2_ref,
                      q1_ref, q2_ref, dk_ref, dkb_ref, dv_ref, dvb_ref, *, scale, tr):
    h = h_ref[...]
    zq = _dot(h, wdq_ref[...])
    zk = _dot(h, wdk_ref[...])
    zv = _dot(h, wdv_ref[...])
    dv_ref[...] = zv
    c, s1, s2 = c_ref[...], s1_ref[...], s2_ref[...]
    first = lax.broadcasted_iota(jnp.int32, (1, LANES), 1) < DIFF_DH
    for hd in range(DIFF_HEADS):
        sl = slice(hd * LANES, (hd + 1) * LANES)
        q = _rope_tile(zq[:, sl], c, s1, s2, DIFF_ROT // 2) * scale
        _put(q1_ref, hd * LANES, jnp.where(first, q, 0.0), tr)
        _put(q2_ref, hd * LANES, jnp.where(first, 0.0, q), tr)
        k = _rope_tile(zk[:, sl], c, s1, s2, DIFF_ROT // 2)
        dk_ref[:, sl] = k
        dkb_ref[:, sl] = k.astype(BF16)
        _put_v(dvb_ref, hd, zv[:, sl], tr)


def _diff_proj(h, wdq, wdk, wdv, tabs, bm, scale, tr):
    m, d = h.shape
    nt = tabs[0].shape[0] // bm
    n = wdq.shape[1]
    full = lambda a: pl.BlockSpec(a.shape, lambda i: (0, 0))
    tab = pl.BlockSpec((bm, LANES), lambda i: (i % nt, 0))
    row = lambda w: pl.BlockSpec((bm, w), lambda i: (i, 0))
    qspec, qshape = _out(m, n, bm, BF16, tr)
    vspec, vshape = _out(m, _v_width(DIFF_HEADS, DIFF_VD, tr), bm, BF16, tr)
    f32_rows, bf16_rows = jax.ShapeDtypeStruct((m, n), F32), jax.ShapeDtypeStruct((m, n), BF16)
    return pl.pallas_call(
        functools.partial(_diff_proj_kernel, scale=scale, tr=tr),
        grid=(m // bm,),
        in_specs=[row(d), full(wdq), full(wdk), full(wdv), tab, tab, tab],
        out_specs=[qspec, qspec, row(n), row(n), row(n), vspec],
        out_shape=[qshape, qshape, f32_rows, bf16_rows, f32_rows, vshape],
        compiler_params=_params("parallel"),
        name="diff_proj",
    )(h, wdq, wdk, wdv, *tabs)


def _merge_kernel(h_ref, oa_ref, ob_ref, wga_ref, wgb_ref, wa_ref, wb_ref, o_ref):
    h = h_ref[...]
    ga = jax.nn.sigmoid(_dot(h, wga_ref[...]))
    gb = jax.nn.sigmoid(_dot(h, wgb_ref[...]))
    ya = _dot(oa_ref[...], wa_ref[...])
    yb = _dot(ob_ref[...], wb_ref[...])
    o_ref[...] = (ga * ya + gb * yb).astype(o_ref.dtype)


def _merge(h, oa, ob, wga, wgb, wa, wb, bm, bn):
    m, d = h.shape
    n = wga.shape[1]
    row = lambda a: pl.BlockSpec((bm, a.shape[1]), lambda i, j: (i, 0))
    col = lambda a: pl.BlockSpec((a.shape[0], bn), lambda i, j: (0, j))
    return pl.pallas_call(
        _merge_kernel,
        grid=(m // bm, n // bn),
        in_specs=[row(h), row(oa), row(ob), col(wga), col(wgb), col(wa), col(wb)],
        out_specs=pl.BlockSpec((bm, bn), lambda i, j: (i, j)),
        out_shape=jax.ShapeDtypeStruct((m, n), BF16),
        compiler_params=_params("parallel", "arbitrary"),
        name="gated_merge",
    )(h, oa, ob, wga, wgb, wa, wb)


def _outproj_kernel(x_ref, mg_ref, wo_ref, g_ref, x2_ref, h2_ref):
    x2 = x_ref[...] + _dot(mg_ref[...], wo_ref[...])
    x2_ref[...] = x2
    h2_ref[...] = _rms(x2, g_ref[...]).astype(BF16)


def _outproj(x, mg, wo, g, bm):
    m, d = x.shape
    row = pl.BlockSpec((bm, d), lambda i: (i, 0))
    return pl.pallas_call(
        _outproj_kernel,
        grid=(m // bm,),
        in_specs=[row, row, pl.BlockSpec(wo.shape, lambda i: (0, 0)), pl.BlockSpec((1, d), lambda i: (0, 0))],
        out_specs=[row, row],
        out_shape=[jax.ShapeDtypeStruct((m, d), F32), jax.ShapeDtypeStruct((m, d), BF16)],
        compiler_params=_params("parallel"),
        name="out_proj_residual",
    )(x, mg, wo, g)


def _ffn_kernel(h2_ref, wg_ref, wu_ref, wd_ref, x2_ref, gf_ref, y_ref):
    f = pl.program_id(1)

    @pl.when(f == 0)
    def _():
        y_ref[...] = x2_ref[...]

    h2 = h2_ref[...]
    a = (jax.nn.silu(_dot(h2, wg_ref[...])) * _dot(h2, wu_ref[...])).astype(BF16)
    y_ref[...] += _dot(a, wd_ref[...])

    @pl.when(f == pl.num_programs(1) - 1)
    def _():
        y_ref[...] = _rms(y_ref[...], gf_ref[...])


def _ffn(h2, w_in, w_out, x2, gf, bm, bf):
    m, d = h2.shape
    dff = w_out.shape[0]
    nf = dff // bf
    row = pl.BlockSpec((bm, d), lambda i, f: (i, 0))
    return pl.pallas_call(
        _ffn_kernel,
        grid=(m // bm, nf),
        in_specs=[row,
                  pl.BlockSpec((d, bf), lambda i, f: (0, f)),
                  pl.BlockSpec((d, bf), lambda i, f: (0, f + nf)),
                  pl.BlockSpec((bf, d), lambda i, f: (f, 0)),
                  row,
                  pl.BlockSpec((1, d), lambda i, f: (0, 0))],
        out_specs=row,
        out_shape=jax.ShapeDtypeStruct((m, d), F32),
        compiler_params=_params("parallel", "arbitrary"),
        name="ffn_swiglu_final_norm",
    )(h2, w_in, w_in, w_out, x2, gf)


def _softmax_step(s, v, m, l, acc):
    m_new = jnp.maximum(m, jnp.max(s, axis=-1, keepdims=True))
    alpha = jnp.exp2(m - m_new)
    p = jnp.exp2(s - m_new)
    l = alpha * l + jnp.sum(p, axis=-1, keepdims=True)
    acc = alpha * acc + _dot(p.astype(BF16), v)
    return m_new, l, acc


def _chunk_mask(tq, tk, q0, k0):
    qp = q0 + lax.broadcasted_iota(jnp.int32, (tq, tk), 0)
    kp = k0 + lax.broadcasted_iota(jnp.int32, (tq, tk), 1)
    return (kp // CHUNK) <= (qp // CHUNK)


def _softmax_init(tq, dv):
    return (jnp.full((tq, 1), NEG_INF, F32), jnp.zeros((tq, 1), F32), jnp.zeros((tq, dv), F32))


def _chunk_mask_t(t):
    kp = lax.broadcasted_iota(jnp.int32, (t, t), 0)
    qp = lax.broadcasted_iota(jnp.int32, (t, t), 1)
    return (kp // CHUNK) <= (qp // CHUNK)


def _probs_t(s, smax, m):
    m_new = jnp.maximum(m, smax)
    return jnp.exp2(s - m_new).astype(BF16), jnp.exp2(m - m_new), m_new


def _attention_scratch(t, dv, streams):
    return [pltpu.VMEM((streams, 2, t, t), F32),
            pltpu.VMEM((streams, dv + VT_ONES, t), F32),
            pltpu.VMEM((streams, 8, t), F32)]


def _causal_attention_t(n, scores, next_scores, values, finish, s_ref, acc_ref, stat_ref, t, streams):
    row_m, row_smax = 0, 1

    def stat(st, r):
        return stat_ref[st, r:r + 1, :]

    def put_scores(st, slot, s):
        s_ref[st, slot] = s
        stat_ref[st, row_smax + slot:row_smax + slot + 1, :] = jnp.max(s, axis=0, keepdims=True)

    def accumulate(st, j, s, smax):
        p, alpha, m = _probs_t(s, smax, stat(st, row_m))
        acc = alpha * acc_ref[st] + _dot(values(j, st), p)
        stat_ref[st, row_m:row_m + 1, :] = m
        return acc

    @pl.when(n == 0)
    def _():
        for st in range(streams):
            put_scores(st, 0, scores(0, st))

    for st in range(streams):
        acc_ref[st] = jnp.zeros(acc_ref.shape[1:], F32)
        stat_ref[st, row_m:row_m + 1, :] = jnp.full((1, t), NEG_INF, F32)

    def run(first):
        def step(tau, par):
            for st in range(streams):
                put_scores(st, (first + par) % 2, scores(tau, st))
                prev = (first + par + 1) % 2
                acc_ref[st] = accumulate(st, tau - 1, s_ref[st, prev], stat(st, row_smax + prev))

        unroll = 4

        def trip(u, carry):
            for k in range(unroll):
                step(unroll * u + 1 + k, (1 + k) % 2)
            return carry

        lax.fori_loop(0, n // unroll, trip, 0)
        done = (n // unroll) * unroll

        @pl.when(n - done >= 2)
        def _():
            step(done + 1, 1)
            step(done + 2, 0)

        @pl.when(n % 2 == 1)
        def _():
            step(n, 1)

        def tail(par):
            diag = (first + par) % 2
            for st in range(streams):
                put_scores(st, 1 - diag, next_scores(st))
            mask = _chunk_mask_t(t)
            res = []
            for st in range(streams):
                halves = []
                for c in range(0, t, t // 2):
                    rows, cols = slice(0, c + t // 2), slice(c, c + t // 2)
                    s = jnp.where(mask[rows, cols], s_ref[st, diag, rows, cols], NEG_INF)
                    p, alpha, _ = _probs_t(s, jnp.max(s, axis=0, keepdims=True), stat_ref[st, row_m:row_m + 1, cols])
                    halves.append(alpha * acc_ref[st, :, cols] + _dot(values(n, st)[:, rows], p))
                res.append(jnp.concatenate(halves, axis=1))
            finish(res)

        for par in (0, 1):
            pl.when(n % 2 == par)(functools.partial(tail, par))

    first = ((n + 1) // 2) % 2
    for f in (0, 1):
        pl.when(first == f)(functools.partial(run, f))


def _normalised(acc, dv):
    return (acc[:dv] / acc[dv:dv + 1]).T


def _mla_prompt_kernel(qt_ref, qn_ref, k_ref, vt_ref, o_ref, *scratch, t, heads):
    def block_scores(q_ref, j, hd):
        k = k_ref[0, pl.ds(pl.multiple_of(j * t, t), t), hd * MLA_QK_PAD:(hd + 1) * MLA_QK_PAD]
        return _dot(k, q_ref[0, hd * MLA_QK_PAD:(hd + 1) * MLA_QK_PAD, :])

    scores = functools.partial(block_scores, qt_ref)
    next_scores = functools.partial(block_scores, qn_ref, 0)

    def values(j, hd):
        return vt_ref[j, hd * (MLA_V + VT_ONES):(hd + 1) * (MLA_V + VT_ONES), :]

    def finish(res):
        for hd, acc in enumerate(res):
            o_ref[0, :, hd * MLA_V:(hd + 1) * MLA_V] = _normalised(acc, MLA_V).astype(o_ref.dtype)

    _causal_attention_t(pl.program_id(2), scores, next_scores, values, finish, *scratch, t, heads)


def _mla_prompt(qt, k, vt, t, heads):
    b, l, _ = k.shape
    nq = l // t
    qblock = lambda step: pl.BlockSpec((1, heads * MLA_QK_PAD, t),
                                       lambda b, h, i: (b * nq + jnp.minimum(i + step, nq - 1), h, 0))
    return pl.pallas_call(
        functools.partial(_mla_prompt_kernel, t=t, heads=heads),
        grid=(b, MLA_HEADS // heads, nq),
        in_specs=[qblock(0), qblock(1),
                  pl.BlockSpec((1, l, heads * MLA_QK_PAD), lambda b, h, i: (b, 0, h), pipeline_mode=pl.Buffered(1)),
                  pl.BlockSpec((nq, heads * (MLA_V + VT_ONES), t), lambda b, h, i: (b, h, 0),
                               pipeline_mode=pl.Buffered(1))],
        out_specs=pl.BlockSpec((1, t, heads * MLA_V), lambda b, h, i: (b, i, h)),
        out_shape=jax.ShapeDtypeStruct((b, l, MLA_HEADS * MLA_V), BF16),
        scratch_shapes=_attention_scratch(t, MLA_V, heads),
        compiler_params=_params("parallel", "parallel", "arbitrary"),
        name="mla_attention_prompt",
    )(qt, qt, k, vt)


def _lambda(lq1_ref, lk1_ref, lq2_ref, lk2_ref, lambda_init):
    a = jnp.sum(lq1_ref[...] * lk1_ref[...], axis=-1, keepdims=True)
    b = jnp.sum(lq2_ref[...] * lk2_ref[...], axis=-1, keepdims=True)
    return jnp.exp(a) - jnp.exp(b) + lambda_init


def _diff_prompt_kernel(q1t_ref, q2t_ref, q1n_ref, q2n_ref, k_ref, vt_ref, lq1_ref, lk1_ref, lq2_ref, lk2_ref,
                        g_ref, o_ref, *scratch, t, heads, lambda_init):
    qt_refs, qn_refs = (q1t_ref, q2t_ref), (q1n_ref, q2n_ref)

    def block_scores(q_refs, rows, st):
        cols = slice((st // 2) * LANES, (st // 2 + 1) * LANES)
        return _dot(k_ref[0, rows, cols], q_refs[st % 2][0, cols, :])

    def scores(j, st):
        return block_scores(qt_refs, pl.ds(pl.multiple_of(j * t, t), t), st)

    def next_scores(st):
        return block_scores(qn_refs, slice(0, t), st)

    def values(j, st):
        return vt_ref[j, (st // 2) * (DIFF_VD + VT_ONES):(st // 2 + 1) * (DIFF_VD + VT_ONES), :]

    def finish(res):
        lam = _lambda(lq1_ref, lk1_ref, lq2_ref, lk2_ref, lambda_init)
        for hd in range(heads):
            o = _normalised(res[2 * hd], DIFF_VD) - lam * _normalised(res[2 * hd + 1], DIFF_VD)
            o = _rms(o, g_ref[...]) * (1.0 - lambda_init)
            o_ref[0, :, hd * DIFF_VD:(hd + 1) * DIFF_VD] = o.astype(o_ref.dtype)

    _causal_attention_t(pl.program_id(2), scores, next_scores, values, finish, *scratch, t, 2 * heads)


def _diff_prompt(q1t, q2t, k, vt, lams, g, t, heads, lambda_init):
    b, l, _ = k.shape
    nq = l // t
    qs = pl.BlockSpec((1, heads * LANES, t), lambda b, h, i: (b * nq + i, h, 0))
    qn = pl.BlockSpec((1, heads * LANES, t), lambda b, h, i: (b * nq + jnp.minimum(i + 1, nq - 1), h, 0))
    small = lambda a: pl.BlockSpec(a.shape, lambda b, h, i: (0, 0))
    return pl.pallas_call(
        functools.partial(_diff_prompt_kernel, t=t, heads=heads, lambda_init=lambda_init),
        grid=(b, DIFF_HEADS // heads, nq),
        in_specs=[qs, qs, qn, qn,
                  pl.BlockSpec((1, l, heads * LANES), lambda b, h, i: (b, 0, h)),
                  pl.BlockSpec((nq, heads * (DIFF_VD + VT_ONES), t), lambda b, h, i: (b, h, 0))]
                 + [small(a) for a in lams] + [small(g)],
        out_specs=pl.BlockSpec((1, t, heads * DIFF_VD), lambda b, h, i: (b, i, h)),
        out_shape=jax.ShapeDtypeStruct((b, l, DIFF_HEADS * DIFF_VD), BF16),
        scratch_shapes=_attention_scratch(t, DIFF_VD, 2 * heads),
        compiler_params=_params("parallel", "parallel", "arbitrary"),
        name="diff_attention_prompt",
    )(q1t, q2t, q1t, q2t, k, vt, *lams, g)


def _two_part_softmax(q, kc, vc, kn, vn, mask_c, mask_n):
    tq = q.shape[0]
    carry = _softmax_init(tq, vc.shape[1])
    carry = _softmax_step(jnp.where(mask_c, _dot_nt(q, kc), NEG_INF), vc, *carry)
    return _softmax_step(jnp.where(mask_n, _dot_nt(q, kn), NEG_INF), vn, *carry)


def _mla_sample_kernel(q_ref, latc_ref, krc_ref, latn_ref, kn_ref, wukv_ref, o_ref, *, past):
    tq = q_ref.shape[1]
    latc, latn = latc_ref[0].astype(BF16), latn_ref[0].astype(BF16)
    krc, krn = krc_ref[0].astype(BF16), kn_ref[0, :, LANES:2 * LANES]
    qa, qr = [], []
    for hd in range(MLA_HEADS):
        lo = hd * MLA_QK_PAD
        wk = wukv_ref[:, hd * MLA_NOPE:(hd + 1) * MLA_NOPE]
        qa.append(_dot_nt(q_ref[0, :, lo:lo + LANES], wk).astype(BF16))
        qr.append(q_ref[0, :, lo + LANES:lo + 2 * LANES])
    qa, qr = jnp.concatenate(qa, axis=0), jnp.concatenate(qr, axis=0)
    rows = MLA_HEADS * tq

    def mask(tk, k0):
        qp = past + lax.broadcasted_iota(jnp.int32, (rows, tk), 0) % tq
        kp = k0 + lax.broadcasted_iota(jnp.int32, (rows, tk), 1)
        return (kp // CHUNK) <= (qp // CHUNK)

    carry = _softmax_init(rows, MLA_KV_LORA)
    s = _dot_nt(qa, latc) + _dot_nt(qr, krc)
    carry = _softmax_step(jnp.where(mask(past, 0), s, NEG_INF), latc, *carry)
    s = _dot_nt(qa, latn) + _dot_nt(qr, krn)
    _, l, acc = _softmax_step(jnp.where(mask(tq, past), s, NEG_INF), latn, *carry)
    olat = (acc / l).astype(BF16)
    v0 = MLA_HEADS * MLA_NOPE
    for hd in range(MLA_HEADS):
        wv = wukv_ref[:, v0 + hd * MLA_V:v0 + (hd + 1) * MLA_V]
        o_ref[0, :, hd * MLA_V:(hd + 1) * MLA_V] = _dot(olat[hd * tq:(hd + 1) * tq], wv).astype(o_ref.dtype)


def _mla_sample(q, latc, krc, latn, kn, wukv, layer):
    b, tq, _ = q.shape
    blk = lambda a: pl.BlockSpec((1,) + a.shape[1:], lambda b: (b, 0, 0))
    cache = lambda a: pl.BlockSpec((1,) + a.shape[1:], lambda b: (layer * q.shape[0] + b, 0, 0))
    return pl.pallas_call(
        functools.partial(_mla_sample_kernel, past=latc.shape[1]),
        grid=(b,),
        in_specs=[blk(q), cache(latc), cache(krc), blk(latn), blk(kn), pl.BlockSpec(wukv.shape, lambda b: (0, 0))],
        out_specs=pl.BlockSpec((1, tq, MLA_HEADS * MLA_V), lambda b: (b, 0, 0)),
        out_shape=jax.ShapeDtypeStruct((b, tq, MLA_HEADS * MLA_V), BF16),
        compiler_params=_params("parallel"),
        name="mla_attention_sample",
    )(q, latc, krc, latn, kn, wukv)


def _diff_sample_kernel(q1_ref, q2_ref, kc_ref, vc_ref, kn_ref, vn_ref,
                        lq1_ref, lk1_ref, lq2_ref, lk2_ref, g_ref, o_ref, *, past, lambda_init):
    tq = q1_ref.shape[1]
    mask_c, mask_n = (jnp.concatenate([m, m], axis=0)
                      for m in (_chunk_mask(tq, past, past, 0), _chunk_mask(tq, tq, past, past)))
    lam = _lambda(lq1_ref, lk1_ref, lq2_ref, lk2_ref, lambda_init)
    for hd in range(DIFF_HEADS):
        rows = pl.ds(hd, past, stride=DIFF_HEADS)
        cols = slice(hd * LANES, (hd + 1) * LANES)
        kc, vc = kc_ref[0, rows, :].astype(BF16), vc_ref[0, rows, :].astype(BF16)
        q = jnp.concatenate([q1_ref[0, :, cols], q2_ref[0, :, cols]], axis=0)
        _, l, acc = _two_part_softmax(q, kc, vc, kn_ref[0, :, cols], vn_ref[0, :, cols], mask_c, mask_n)
        o = acc / l
        o = _rms(o[:tq] - lam * o[tq:], g_ref[...]) * (1.0 - lambda_init)
        o_ref[0, :, cols] = o.astype(o_ref.dtype)


def _diff_sample(q1, q2, kc, vc, kn, vn, lams, g, lambda_init, layer):
    b, tq, _ = q1.shape
    past = kc.shape[1] // DIFF_HEADS
    blk = lambda a: pl.BlockSpec((1,) + a.shape[1:], lambda b: (b, 0, 0))
    cache = pl.BlockSpec((1,) + kc.shape[1:], lambda b: (layer * q1.shape[0] + b, 0, 0))
    small = lambda a: pl.BlockSpec(a.shape, lambda b: (0, 0))
    return pl.pallas_call(
        functools.partial(_diff_sample_kernel, past=past, lambda_init=lambda_init),
        grid=(b,),
        in_specs=[blk(q1), blk(q2), cache, cache, blk(kn), blk(vn)] + [small(a) for a in lams] + [small(g)],
        out_specs=blk(q1),
        out_shape=jax.ShapeDtypeStruct((b, tq, DIFF_HEADS * DIFF_VD), BF16),
        compiler_params=_params("parallel"),
        name="diff_attention_sample",
    )(q1, q2, kc, vc, kn, vn, *lams, g)


def _rope_tables(pos, theta, rot_dim, period):
    half = rot_dim // 2
    inv = 1.0 / (jnp.float32(theta) ** (jnp.arange(half, dtype=F32) / half))
    ang = pos.astype(F32)[:, None] * inv[None, :]
    cos, sin = jnp.cos(ang), jnp.sin(ang)
    n = pos.shape[0]
    rest = period - rot_dim
    c = jnp.concatenate([cos, cos, jnp.ones((n, rest), F32)], axis=1)
    s1 = jnp.concatenate([-sin, jnp.zeros((n, half + rest), F32)], axis=1)
    s2 = jnp.concatenate([jnp.zeros((n, half), F32), sin, jnp.zeros((n, rest), F32)], axis=1)
    reps = LANES // period
    return tuple(jnp.tile(a, (1, reps)) for a in (c, s1, s2))


def _layer_weights(w_in, mla_w_uq, mla_w_ukv):
    c_kv = MLA_Q_LORA
    c_kr = c_kv + MLA_KV_LORA
    c_dq = c_kr + MLA_ROPE
    dqk = DIFF_HEADS * 2 * DIFF_DH
    c_dk = c_dq + dqk
    c_dv = c_dk + dqk
    c_ga = c_dv + DIFF_HEADS * DIFF_VD
    d = w_in.shape[0]
    c_gb = c_ga + d
    cols = lambda lo, hi: w_in[:, lo:hi].astype(BF16)
    wq = cols(0, c_kv)
    wkv = jnp.pad(cols(c_kv, c_dq), ((0, 0), (0, LANES - MLA_ROPE)))
    wdq, wdk, wdv = cols(c_dq, c_dk), cols(c_dk, c_dv), cols(c_dv, c_ga)
    wga, wgb = cols(c_ga, c_gb), cols(c_gb, w_in.shape[1])
    uq = mla_w_uq.astype(BF16).reshape(MLA_Q_LORA, MLA_HEADS, MLA_NOPE + MLA_ROPE)
    uq = jnp.pad(uq, ((0, 0), (0, 0), (0, MLA_QK_PAD - MLA_NOPE - MLA_ROPE))).reshape(MLA_Q_LORA, -1)
    ukv = mla_w_ukv.astype(BF16).reshape(MLA_KV_LORA, MLA_HEADS, MLA_NOPE + MLA_V)
    ukv = jnp.concatenate([ukv[:, :, :MLA_NOPE].reshape(MLA_KV_LORA, -1),
                           ukv[:, :, MLA_NOPE:].reshape(MLA_KV_LORA, -1)], axis=1)
    return wq, wkv, wdq, wdk, wdv, wga, wgb, uq, ukv


def _block(m, want):
    return want if m % want == 0 else m


def kernel(x_prompt, x_sample, cache_mla_ckv, cache_mla_krope, cache_diff_k, cache_diff_v, norm_mix, w_in,
           mla_q_norm, mla_w_uq, mla_kv_norm, mla_w_ukv, diff_lq1, diff_lk1, diff_lq2, diff_lk2, diff_subln,
           w_branch_a, w_branch_b, w_out, norm_ffn, w_ffn_in, w_ffn_out, norm_final):
    bp, lp, d = x_prompt.shape
    bs, ls, _ = x_sample.shape
    depth, _, past, _ = cache_mla_ckv.shape
    assert depth == 1, "the FFN kernel fuses the final norm, so it serves the last (only) layer"
    mp, ms = bp * lp, bs * ls
    t_attn = 512

    pos_p = jnp.arange(lp)
    pos_s = past + jnp.arange(ls)
    tabs_mla_p = _rope_tables(pos_p, MLA_THETA, MLA_ROPE, LANES)
    tabs_mla_s = tuple(jnp.tile(a, (bs, 1)) for a in _rope_tables(pos_s, MLA_THETA, MLA_ROPE, LANES))
    tabs_dif_p = _rope_tables(pos_p, ROPE_THETA, DIFF_ROT, DIFF_DH)
    tabs_dif_s = tuple(jnp.tile(a, (bs, 1)) for a in _rope_tables(pos_s, ROPE_THETA, DIFF_ROT, DIFF_DH))
    q_scale_mla = (MLA_NOPE + MLA_ROPE) ** -0.5 * LOG2E
    q_scale_dif = DIFF_DH ** -0.5 * LOG2E
    gfinal = norm_final.reshape(1, d)

    xp = x_prompt.reshape(mp, d)
    xs = x_sample.reshape(ms, d)
    rows_p, rows_s = [], []
    for l in range(depth):
        lambda_init = 0.8 - 0.6 * math.exp(-0.3 * l)
        wq, wkv, wdq, wdk, wdv, wga, wgb, uq, ukv = _layer_weights(w_in[l], mla_w_uq[l], mla_w_ukv[l])
        gmix, gq, gkv = norm_mix[l].reshape(1, -1), mla_q_norm[l].reshape(1, -1), mla_kv_norm[l].reshape(1, -1)
        lams = tuple(a[l].reshape(1, -1) for a in (diff_lq1, diff_lk1, diff_lq2, diff_lk2))
        gsub = diff_subln[l].reshape(1, -1)
        wa, wbr, wo = w_branch_a[l].astype(BF16), w_branch_b[l].astype(BF16), w_out[l].astype(BF16)
        wfi, wfo = w_ffn_in[l].astype(BF16), w_ffn_out[l].astype(BF16)
        gffn = norm_ffn[l].reshape(1, -1)

        def stage1(x, bm, tabs_mla, tabs_dif, tr):
            h, q, ckv, krope, kmla, vmla = _mla_proj(x, gmix, wq, gq, uq, wkv, gkv, ukv, tabs_mla, bm,
                                                     q_scale_mla, tr)
            q1, q2, dk, dkb, dv, dvb = _diff_proj(h, wdq, wdk, wdv, tabs_dif, bm, q_scale_dif, tr)
            return h, q, ckv, krope, kmla, vmla, q1, q2, dk, dkb, dv, dvb

        def stage2(x, h, oa, ob, bm, bn, bf):
            mg = _merge(h, oa, ob, wga, wgb, wa, wbr, bm, bn)
            x2, h2 = _outproj(x, mg, wo, gffn, min(bm, 512))
            return _ffn(h2, wfi, wfo, x2, gfinal, bm, bf)

        h, q, ckv, krope, kmla, vmla, q1, q2, dk, dkb, dv, dvb = stage1(xp, t_attn, tabs_mla_p, tabs_dif_p, True)
        r3 = lambda a: a.reshape(bp, lp, -1)
        oa = _mla_prompt(q, r3(kmla), vmla, t_attn, 4)
        ob = _diff_prompt(q1, q2, r3(dkb), dvb, lams, gsub, t_attn, 2, lambda_init)
        xp = stage2(xp, h, oa.reshape(mp, -1), ob.reshape(mp, -1), _block(mp, 1024), 1024, 256)
        rows_p.append((ckv.reshape(bp, lp, -1), krope.reshape(bp, lp, -1),
                       dk.reshape(bp, lp, DIFF_HEADS, -1), dv.reshape(bp, lp, DIFF_HEADS, -1)))

        h, q, ckv, krope, kmla, vmla, q1, q2, dk, dkb, dv, dvb = stage1(xs, ms, tabs_mla_s, tabs_dif_s, False)
        kr_pad = jnp.pad(cache_mla_krope.reshape(depth * bs, past, MLA_ROPE), ((0, 0), (0, 0), (0, LANES - MLA_ROPE)))
        r3 = lambda a: a.reshape(bs, ls, -1)
        oa = _mla_sample(r3(q), cache_mla_ckv.reshape(depth * bs, past, -1), kr_pad, r3(ckv), r3(kmla), ukv, l)
        call = lambda a: a.reshape(depth * bs, past * DIFF_HEADS, -1)
        ob = _diff_sample(r3(q1), r3(q2), call(cache_diff_k), call(cache_diff_v), r3(dkb), r3(dvb),
                          lams, gsub, lambda_init, l)
        xs = stage2(xs, h, oa.reshape(ms, -1), ob.reshape(ms, -1), ms, 512, 512)
        rows_s.append((ckv.reshape(bs, ls, -1), krope.reshape(bs, ls, -1),
                       dk.reshape(bs, ls, DIFF_HEADS, -1), dv.reshape(bs, ls, DIFF_HEADS, -1)))

    y_prompt = xp.reshape(bp, lp, d)
    y_sample = xs.reshape(bs, ls, d)
    stack = lambda rows, i: jnp.stack([r[i] for r in rows], axis=0)
    return (y_prompt, y_sample,
            stack(rows_p, 0), stack(rows_p, 1), stack(rows_p, 2), stack(rows_p, 3),
            stack(rows_s, 0), stack(rows_s, 1), stack(rows_s, 2), stack(rows_s, 3))
```

```python
import functools
import math

import jax
import jax.numpy as jnp
from jax import lax
from jax.experimental import pallas as pl
from jax.experimental.pallas import tpu as pltpu

F32 = jnp.float32
BF16 = jnp.bfloat16

CHUNK = 64
EPS = 1e-6
NEG_INF = -1e30

MLA_HEADS = 8
MLA_Q_LORA = 512
MLA_KV_LORA = 256
MLA_NOPE = 128
MLA_ROPE = 64
MLA_V = 128
MLA_THETA = 10000.0
MLA_QK_PAD = 256

DIFF_HEADS = 8
DIFF_DH = 64
DIFF_VD = 2 * DIFF_DH
DIFF_ROT = DIFF_DH // 4
ROPE_THETA = 500000.0

LANES = 128
VT_ONES = 16
LOG2E = math.log2(math.e)
VMEM_LIMIT = 56 * 1024 * 1024


def _params(*sem):
    return pltpu.CompilerParams(dimension_semantics=sem, vmem_limit_bytes=VMEM_LIMIT)


def _dot(a, b):
    return jnp.dot(a, b, preferred_element_type=F32)


def _dot_nt(a, b):
    return lax.dot_general(a, b, (((1,), (1,)), ((), ())), preferred_element_type=F32)


def _rms(x, g):
    return x * lax.rsqrt(jnp.mean(x * x, axis=-1, keepdims=True) + EPS) * g


def _rope_tile(t, c, s1, s2, half):
    return t * c + pltpu.roll(t, LANES - half, 1) * s1 + pltpu.roll(t, half, 1) * s2


def _put(ref, lo, val, tr):
    w = val.shape[1]
    if tr:
        ref[0, lo:lo + w, :] = val.T.astype(ref.dtype)
    else:
        ref[:, lo:lo + w] = val.astype(ref.dtype)


def _put_v(ref, hd, val, tr):
    bm, w = val.shape
    if tr:
        lo = hd * (w + VT_ONES)
        ref[0, lo:lo + w, :] = val.T.astype(ref.dtype)
        ref[0, lo + w:lo + w + VT_ONES, :] = jnp.ones((VT_ONES, bm), ref.dtype)
    else:
        ref[:, hd * w:(hd + 1) * w] = val.astype(ref.dtype)


def _v_width(heads, dv, tr):
    return heads * (dv + VT_ONES) if tr else heads * dv


def _out(m, n, bm, dtype, tr):
    if tr:
        return pl.BlockSpec((1, n, bm), lambda i: (i, 0, 0)), jax.ShapeDtypeStruct((m // bm, n, bm), dtype)
    return pl.BlockSpec((bm, n), lambda i: (i, 0)), jax.ShapeDtypeStruct((m, n), dtype)


def _qproj_body(h, wq_ref, gq_ref, wuq_ref, c, s1, s2, o_ref, scale, tr):
    qlat = _rms(_dot(h, wq_ref[...]), gq_ref[...]).astype(BF16)
    q = _dot(qlat, wuq_ref[...])
    for hd in range(MLA_HEADS):
        lo = hd * MLA_QK_PAD
        _put(o_ref, lo, q[:, lo:lo + LANES] * scale, tr)
        t = _rope_tile(q[:, lo + LANES:lo + 2 * LANES], c, s1, s2, MLA_ROPE // 2)
        _put(o_ref, lo + LANES, t * scale, tr)


def _kv_expand(ckv, krope_tile, wukv_ref, kmla_ref, vmla_ref, tr):
    kv = _dot(ckv.astype(BF16), wukv_ref[...])
    kr = krope_tile.astype(BF16)
    for hd in range(MLA_HEADS):
        lo = hd * MLA_QK_PAD
        kmla_ref[:, lo:lo + LANES] = kv[:, hd * MLA_NOPE:(hd + 1) * MLA_NOPE].astype(BF16)
        kmla_ref[:, lo + LANES:lo + 2 * LANES] = kr
        vlo = MLA_HEADS * MLA_NOPE + hd * MLA_V
        _put_v(vmla_ref, hd, kv[:, vlo:vlo + MLA_V], tr)


def _mla_proj_kernel(x_ref, gmix_ref, wq_ref, gq_ref, wuq_ref, wkv_ref, gkv_ref, wukv_ref, c_ref, s1_ref, s2_ref,
                     h_ref, q_ref, ckv_ref, krope_ref, kmla_ref, vmla_ref, *, scale, tr):
    h = _rms(x_ref[...], gmix_ref[...]).astype(BF16)
    h_ref[...] = h
    c, s1, s2 = c_ref[...], s1_ref[...], s2_ref[...]
    _qproj_body(h, wq_ref, gq_ref, wuq_ref, c, s1, s2, q_ref, scale, tr)
    z = _dot(h, wkv_ref[...])
    ckv = _rms(z[:, :MLA_KV_LORA], gkv_ref[...])
    ckv_ref[...] = ckv
    t = _rope_tile(z[:, MLA_KV_LORA:], c, s1, s2, MLA_ROPE // 2)
    krope_ref[...] = t[:, :MLA_ROPE]
    _kv_expand(ckv, t, wukv_ref, kmla_ref, vmla_ref, tr)


def _mla_proj(x, gmix, wq, gq, wuq, wkv, gkv, wukv, tabs, bm, scale, tr):
    m, d = x.shape
    nt = tabs[0].shape[0] // bm
    full = lambda a: pl.BlockSpec(a.shape, lambda i: (0, 0))
    tab = pl.BlockSpec((bm, LANES), lambda i: (i % nt, 0))
    row = lambda n: pl.BlockSpec((bm, n), lambda i: (i, 0))
    nk = MLA_HEADS * MLA_QK_PAD
    qspec, qshape = _out(m, wuq.shape[1], bm, BF16, tr)
    vspec, vshape = _out(m, _v_width(MLA_HEADS, MLA_V, tr), bm, BF16, tr)
    return pl.pallas_call(
        functools.partial(_mla_proj_kernel, scale=scale, tr=tr),
        grid=(m // bm,),
        in_specs=[row(d), full(gmix), full(wq), full(gq), full(wuq), full(wkv), full(gkv), full(wukv), tab, tab, tab],
        out_specs=[row(d), qspec, row(MLA_KV_LORA), row(MLA_ROPE), row(nk), vspec],
        out_shape=[jax.ShapeDtypeStruct((m, d), BF16), qshape,
                   jax.ShapeDtypeStruct((m, MLA_KV_LORA), F32), jax.ShapeDtypeStruct((m, MLA_ROPE), F32),
                   jax.ShapeDtypeStruct((m, nk), BF16), vshape],
        compiler_params=_params("parallel"),
        name="norm_mla_proj",
    )(x, gmix, wq, gq, wuq, wkv, gkv, wukv, *tabs)


def _diff_proj_kernel(h_ref, wdq_ref, wdk_ref, wdv_ref, c_ref, s1_ref, s2_ref,
                      q1_ref, q2_ref, dk_ref, dkb_ref, dv_ref, dvb_ref, *, scale, tr):
    h = h_ref[...]
    zq = _dot(h, wdq_ref[...])
    zk = _dot(h, wdk_ref[...])
    zv = _dot(h, wdv_ref[...])
    dv_ref[...] = zv
    c, s1, s2 = c_ref[...], s1_ref[...], s2_ref[...]
    first = lax.broadcasted_iota(jnp.int32, (1, LANES), 1) < DIFF_DH
    for hd in range(DIFF_HEADS):
        sl = slice(hd * LANES, (hd + 1) * LANES)
        q = _rope_tile(zq[:, sl], c, s1, s2, DIFF_ROT // 2) * scale
        _put(q1_ref, hd * LANES, jnp.where(first, q, 0.0), tr)
        _put(q2_ref, hd * LANES, jnp.where(first, 0.0, q), tr)
        k = _rope_tile(zk[:, sl], c, s1, s2, DIFF_ROT // 2)
        dk_ref[:, sl] = k
        dkb_ref[:, sl] = k.astype(BF16)
        _put_v(dvb_ref, hd, zv[:, sl], tr)


def _diff_proj(h, wdq, wdk, wdv, tabs, bm, scale, tr):
    m, d = h.shape
    nt = tabs[0].shape[0] // bm
    n = wdq.shape[1]
    full = lambda a: pl.BlockSpec(a.shape, lambda i: (0, 0))
    tab = pl.BlockSpec((bm, LANES), lambda i: (i % nt, 0))
    row = lambda w: pl.BlockSpec((bm, w), lambda i: (i, 0))
    qspec, qshape = _out(m, n, bm, BF16, tr)
    vspec, vshape = _out(m, _v_width(DIFF_HEADS, DIFF_VD, tr), bm, BF16, tr)
    f32_rows, bf16_rows = jax.ShapeDtypeStruct((m, n), F32), jax.ShapeDtypeStruct((m, n), BF16)
    return pl.pallas_call(
        functools.partial(_diff_proj_kernel, scale=scale, tr=tr),
        grid=(m // bm,),
        in_specs=[row(d), full(wdq), full(wdk), full(wdv), tab, tab, tab],
        out_specs=[qspec, qspec, row(n), row(n), row(n), vspec],
        out_shape=[qshape, qshape, f32_rows, bf16_rows, f32_rows, vshape],
        compiler_params=_params("parallel"),
        name="diff_proj",
    )(h, wdq, wdk, wdv, *tabs)


def _merge_kernel(h_ref, oa_ref, ob_ref, wga_ref, wgb_ref, wa_ref, wb_ref, o_ref):
    h = h_ref[...]
    ga = jax.nn.sigmoid(_dot(h, wga_ref[...]))
    gb = jax.nn.sigmoid(_dot(h, wgb_ref[...]))
    ya = _dot(oa_ref[...], wa_ref[...])
    yb = _dot(ob_ref[...], wb_ref[...])
    o_ref[...] = (ga * ya + gb * yb).astype(o_ref.dtype)


def _merge(h, oa, ob, wga, wgb, wa, wb, bm, bn):
    m, d = h.shape
    n = wga.shape[1]
    row = lambda a: pl.BlockSpec((bm, a.shape[1]), lambda i, j: (i, 0))
    col = lambda a: pl.BlockSpec((a.shape[0], bn), lambda i, j: (0, j))
    return pl.pallas_call(
        _merge_kernel,
        grid=(m // bm, n // bn),
        in_specs=[row(h), row(oa), row(ob), col(wga), col(wgb), col(wa), col(wb)],
        out_specs=pl.BlockSpec((bm, bn), lambda i, j: (i, j)),
        out_shape=jax.ShapeDtypeStruct((m, n), BF16),
        compiler_params=_params("parallel", "arbitrary"),
        name="gated_merge",
    )(h, oa, ob, wga, wgb, wa, wb)


def _outproj_kernel(x_ref, mg_ref, wo_ref, g_ref, x2_ref, h2_ref):
    x2 = x_ref[...] + _dot(mg_ref[...], wo_ref[...])
    x2_ref[...] = x2
    h2_ref[...] = _rms(x2, g_ref[...]).astype(BF16)


def _outproj(x, mg, wo, g, bm):
    m, d = x.shape
    row = pl.BlockSpec((bm, d), lambda i: (i, 0))
    return pl.pallas_call(
        _outproj_kernel,
        grid=(m // bm,),
        in_specs=[row, row, pl.BlockSpec(wo.shape, lambda i: (0, 0)), pl.BlockSpec((1, d), lambda i: (0, 0))],
        out_specs=[row, row],
        out_shape=[jax.ShapeDtypeStruct((m, d), F32), jax.ShapeDtypeStruct((m, d), BF16)],
        compiler_params=_params("parallel"),
        name="out_proj_residual",
    )(x, mg, wo, g)


def _ffn_kernel(h2_ref, wg_ref, wu_ref, wd_ref, x2_ref, gf_ref, y_ref):
    f = pl.program_id(1)

    @pl.when(f == 0)
    def _():
        y_ref[...] = x2_ref[...]

    h2 = h2_ref[...]
    a = (jax.nn.silu(_dot(h2, wg_ref[...])) * _dot(h2, wu_ref[...])).astype(BF16)
    y_ref[...] += _dot(a, wd_ref[...])

    @pl.when(f == pl.num_programs(1) - 1)
    def _():
        y_ref[...] = _rms(y_ref[...], gf_ref[...])


def _ffn(h2, w_in, w_out, x2, gf, bm, bf):
    m, d = h2.shape
    dff = w_out.shape[0]
    nf = dff // bf
    row = pl.BlockSpec((bm, d), lambda i, f: (i, 0))
    return pl.pallas_call(
        _ffn_kernel,
        grid=(m // bm, nf),
        in_specs=[row,
                  pl.BlockSpec((d, bf), lambda i, f: (0, f)),
                  pl.BlockSpec((d, bf), lambda i, f: (0, f + nf)),
                  pl.BlockSpec((bf, d), lambda i, f: (f, 0)),
                  row,
                  pl.BlockSpec((1, d), lambda i, f: (0, 0))],
        out_specs=row,
        out_shape=jax.ShapeDtypeStruct((m, d), F32),
        compiler_params=_params("parallel", "arbitrary"),
        name="ffn_swiglu_final_norm",
    )(h2, w_in, w_in, w_out, x2, gf)


def _softmax_step(s, v, m, l, acc):
    m_new = jnp.maximum(m, jnp.max(s, axis=-1, keepdims=True))
    alpha = jnp.exp2(m - m_new)
    p = jnp.exp2(s - m_new)
    l = alpha * l + jnp.sum(p, axis=-1, keepdims=True)
    acc = alpha * acc + _dot(p.astype(BF16), v)
    return m_new, l, acc


def _chunk_mask(tq, tk, q0, k0):
    qp = q0 + lax.broadcasted_iota(jnp.int32, (tq, tk), 0)
    kp = k0 + lax.broadcasted_iota(jnp.int32, (tq, tk), 1)
    return (kp // CHUNK) <= (qp // CHUNK)


def _softmax_init(tq, dv):
    return (jnp.full((tq, 1), NEG_INF, F32), jnp.zeros((tq, 1), F32), jnp.zeros((tq, dv), F32))


def _chunk_mask_t(t):
    kp = lax.broadcasted_iota(jnp.int32, (t, t), 0)
    qp = lax.broadcasted_iota(jnp.int32, (t, t), 1)
    return (kp // CHUNK) <= (qp // CHUNK)


def _probs_t(s, smax, m):
    m_new = jnp.maximum(m, smax)
    return jnp.exp2(s - m_new).astype(BF16), jnp.exp2(m - m_new), m_new


def _attention_scratch(t, dv, streams):
    return [pltpu.VMEM((streams, 2, t, t), F32),
            pltpu.VMEM((streams, dv + VT_ONES, t), F32),
            pltpu.VMEM((streams, 8, t), F32)]


def _causal_attention_t(n, last, scores, next_scores, values, finish, s_ref, acc_ref, stat_ref, t, streams):
    row_m, row_smax = 0, 1

    def stat(st, r):
        return stat_ref[st, r:r + 1, :]

    def put_scores(st, slot, s):
        s_ref[st, slot] = s
        stat_ref[st, row_smax + slot:row_smax + slot + 1, :] = jnp.max(s, axis=0, keepdims=True)

    def masked(s):
        return jnp.where(_chunk_mask_t(t), s, NEG_INF)

    def key_of(item):
        return jnp.where(item == 0, n, item - 1)

    def accumulate(st, item, s, smax):
        p, alpha, m = _probs_t(s, smax, stat(st, row_m))
        acc = alpha * acc_ref[st] + _dot(values(key_of(item), st), p)
        stat_ref[st, row_m:row_m + 1, :] = m
        return acc

    @pl.when(n == 0)
    def _():
        for st in range(streams):
            put_scores(st, 0, masked(scores(0, st)))

    for st in range(streams):
        acc_ref[st] = jnp.zeros(acc_ref.shape[1:], F32)
        stat_ref[st, row_m:row_m + 1, :] = jnp.full((1, t), NEG_INF, F32)

    def run(first):
        def step(tau, par):
            for st in range(streams):
                put_scores(st, (first + par) % 2, scores(tau - 1, st))
                prev = (first + par + 1) % 2
                acc_ref[st] = accumulate(st, tau - 1, s_ref[st, prev], stat(st, row_smax + prev))

        unroll = 4

        def trip(u, carry):
            for k in range(unroll):
                step(unroll * u + 1 + k, (1 + k) % 2)
            return carry

        lax.fori_loop(0, n // unroll, trip, 0)
        done = (n // unroll) * unroll

        @pl.when(n - done >= 2)
        def _():
            step(done + 1, 1)
            step(done + 2, 0)

        @pl.when(n % 2 == 1)
        def _():
            step(n, 1)

        def tail(par):
            slot = (first + par) % 2
            for st in range(streams):
                put_scores(st, 1 - slot, masked(next_scores(jnp.minimum(n + 1, last), st)))
            res = []
            for st in range(streams):
                vt = values(key_of(n), st)
                halves = []
                for c in range(0, t, t // 2):
                    cols = slice(c, c + t // 2)
                    p, alpha, _ = _probs_t(s_ref[st, slot, :, cols],
                                           stat_ref[st, row_smax + slot:row_smax + slot + 1, cols],
                                           stat_ref[st, row_m:row_m + 1, cols])
                    halves.append(alpha * acc_ref[st, :, cols] + _dot(vt, p))
                res.append(jnp.concatenate(halves, axis=1))
            finish(res)

        for par in (0, 1):
            pl.when(n % 2 == par)(functools.partial(tail, par))

    first = ((n + 1) // 2) % 2
    for f in (0, 1):
        pl.when(first == f)(functools.partial(run, f))


def _normalised(acc, dv):
    return (acc[:dv] / acc[dv:dv + 1]).T


def _mla_prompt_kernel(qt_ref, qn_ref, k_ref, vt_ref, o_ref, *scratch, t, heads):
    def block_scores(q_ref, j, hd):
        k = k_ref[0, pl.ds(pl.multiple_of(j * t, t), t), hd * MLA_QK_PAD:(hd + 1) * MLA_QK_PAD]
        return _dot(k, q_ref[0, hd * MLA_QK_PAD:(hd + 1) * MLA_QK_PAD, :])

    scores = functools.partial(block_scores, qt_ref)
    next_scores = functools.partial(block_scores, qn_ref)

    def values(j, hd):
        return vt_ref[j, hd * (MLA_V + VT_ONES):(hd + 1) * (MLA_V + VT_ONES), :]

    def finish(res):
        for hd, acc in enumerate(res):
            o_ref[0, :, hd * MLA_V:(hd + 1) * MLA_V] = _normalised(acc, MLA_V).astype(o_ref.dtype)

    _causal_attention_t(pl.program_id(2), pl.num_programs(2) - 1, scores, next_scores, values, finish,
                        *scratch, t, heads)


def _mla_prompt(qt, k, vt, t, heads):
    b, l, _ = k.shape
    nq = l // t
    qblock = lambda step: pl.BlockSpec((1, heads * MLA_QK_PAD, t),
                                       lambda b, h, i: (b * nq + jnp.minimum(i + step, nq - 1), h, 0))
    return pl.pallas_call(
        functools.partial(_mla_prompt_kernel, t=t, heads=heads),
        grid=(b, MLA_HEADS // heads, nq),
        in_specs=[qblock(0), qblock(1),
                  pl.BlockSpec((1, l, heads * MLA_QK_PAD), lambda b, h, i: (b, 0, h), pipeline_mode=pl.Buffered(1)),
                  pl.BlockSpec((nq, heads * (MLA_V + VT_ONES), t), lambda b, h, i: (b, h, 0),
                               pipeline_mode=pl.Buffered(1))],
        out_specs=pl.BlockSpec((1, t, heads * MLA_V), lambda b, h, i: (b, i, h)),
        out_shape=jax.ShapeDtypeStruct((b, l, MLA_HEADS * MLA_V), BF16),
        scratch_shapes=_attention_scratch(t, MLA_V, heads),
        compiler_params=_params("parallel", "parallel", "arbitrary"),
        name="mla_attention_prompt",
    )(qt, qt, k, vt)


def _lambda(lq1_ref, lk1_ref, lq2_ref, lk2_ref, lambda_init):
    a = jnp.sum(lq1_ref[...] * lk1_ref[...], axis=-1, keepdims=True)
    b = jnp.sum(lq2_ref[...] * lk2_ref[...], axis=-1, keepdims=True)
    return jnp.exp(a) - jnp.exp(b) + lambda_init


def _diff_prompt_kernel(q1t_ref, q2t_ref, q1n_ref, q2n_ref, k_ref, vt_ref, lq1_ref, lk1_ref, lq2_ref, lk2_ref,
                        g_ref, o_ref, *scratch, t, heads, lambda_init):
    qt_refs, qn_refs = (q1t_ref, q2t_ref), (q1n_ref, q2n_ref)

    def block_scores(q_refs, rows, st):
        cols = slice((st // 2) * LANES, (st // 2 + 1) * LANES)
        return _dot(k_ref[0, rows, cols], q_refs[st % 2][0, cols, :])

    def scores(j, st):
        return block_scores(qt_refs, pl.ds(pl.multiple_of(j * t, t), t), st)

    def next_scores(j, st):
        return block_scores(qn_refs, pl.ds(pl.multiple_of(j * t, t), t), st)

    def values(j, st):
        return vt_ref[j, (st // 2) * (DIFF_VD + VT_ONES):(st // 2 + 1) * (DIFF_VD + VT_ONES), :]

    def finish(res):
        lam = _lambda(lq1_ref, lk1_ref, lq2_ref, lk2_ref, lambda_init)
        for hd in range(heads):
            o = _normalised(res[2 * hd], DIFF_VD) - lam * _normalised(res[2 * hd + 1], DIFF_VD)
            o = _rms(o, g_ref[...]) * (1.0 - lambda_init)
            o_ref[0, :, hd * DIFF_VD:(hd + 1) * DIFF_VD] = o.astype(o_ref.dtype)

    _causal_attention_t(pl.program_id(2), pl.num_programs(2) - 1, scores, next_scores, values, finish,
                        *scratch, t, 2 * heads)


def _diff_prompt(q1t, q2t, k, vt, lams, g, t, heads, lambda_init):
    b, l, _ = k.shape
    nq = l // t
    qs = pl.BlockSpec((1, heads * LANES, t), lambda b, h, i: (b * nq + i, h, 0))
    qn = pl.BlockSpec((1, heads * LANES, t), lambda b, h, i: (b * nq + jnp.minimum(i + 1, nq - 1), h, 0))
    small = lambda a: pl.BlockSpec(a.shape, lambda b, h, i: (0, 0))
    return pl.pallas_call(
        functools.partial(_diff_prompt_kernel, t=t, heads=heads, lambda_init=lambda_init),
        grid=(b, DIFF_HEADS // heads, nq),
        in_specs=[qs, qs, qn, qn,
                  pl.BlockSpec((1, l, heads * LANES), lambda b, h, i: (b, 0, h)),
                  pl.BlockSpec((nq, heads * (DIFF_VD + VT_ONES), t), lambda b, h, i: (b, h, 0))]
                 + [small(a) for a in lams] + [small(g)],
        out_specs=pl.BlockSpec((1, t, heads * DIFF_VD), lambda b, h, i: (b, i, h)),
        out_shape=jax.ShapeDtypeStruct((b, l, DIFF_HEADS * DIFF_VD), BF16),
        scratch_shapes=_attention_scratch(t, DIFF_VD, 2 * heads),
        compiler_params=_params("parallel", "parallel", "arbitrary"),
        name="diff_attention_prompt",
    )(q1t, q2t, q1t, q2t, k, vt, *lams, g)


def _two_part_softmax(q, kc, vc, kn, vn, mask_c, mask_n):
    tq = q.shape[0]
    carry = _softmax_init(tq, vc.shape[1])
    carry = _softmax_step(jnp.where(mask_c, _dot_nt(q, kc), NEG_INF), vc, *carry)
    return _softmax_step(jnp.where(mask_n, _dot_nt(q, kn), NEG_INF), vn, *carry)


def _mla_sample_kernel(q_ref, latc_ref, krc_ref, latn_ref, kn_ref, wukv_ref, o_ref, *, past):
    tq = q_ref.shape[1]
    latc, latn = latc_ref[0].astype(BF16), latn_ref[0].astype(BF16)
    krc, krn = krc_ref[0].astype(BF16), kn_ref[0, :, LANES:2 * LANES]
    qa, qr = [], []
    for hd in range(MLA_HEADS):
        lo = hd * MLA_QK_PAD
        wk = wukv_ref[:, hd * MLA_NOPE:(hd + 1) * MLA_NOPE]
        qa.append(_dot_nt(q_ref[0, :, lo:lo + LANES], wk).astype(BF16))
        qr.append(q_ref[0, :, lo + LANES:lo + 2 * LANES])
    qa, qr = jnp.concatenate(qa, axis=0), jnp.concatenate(qr, axis=0)
    rows = MLA_HEADS * tq

    def mask(tk, k0):
        qp = past + lax.broadcasted_iota(jnp.int32, (rows, tk), 0) % tq
        kp = k0 + lax.broadcasted_iota(jnp.int32, (rows, tk), 1)
        return (kp // CHUNK) <= (qp // CHUNK)

    carry = _softmax_init(rows, MLA_KV_LORA)
    s = _dot_nt(qa, latc) + _dot_nt(qr, krc)
    carry = _softmax_step(jnp.where(mask(past, 0), s, NEG_INF), latc, *carry)
    s = _dot_nt(qa, latn) + _dot_nt(qr, krn)
    _, l, acc = _softmax_step(jnp.where(mask(tq, past), s, NEG_INF), latn, *carry)
    olat = (acc / l).astype(BF16)
    v0 = MLA_HEADS * MLA_NOPE
    for hd in range(MLA_HEADS):
        wv = wukv_ref[:, v0 + hd * MLA_V:v0 + (hd + 1) * MLA_V]
        o_ref[0, :, hd * MLA_V:(hd + 1) * MLA_V] = _dot(olat[hd * tq:(hd + 1) * tq], wv).astype(o_ref.dtype)


def _mla_sample(q, latc, krc, latn, kn, wukv, layer):
    b, tq, _ = q.shape
    blk = lambda a: pl.BlockSpec((1,) + a.shape[1:], lambda b: (b, 0, 0))
    cache = lambda a: pl.BlockSpec((1,) + a.shape[1:], lambda b: (layer * q.shape[0] + b, 0, 0))
    return pl.pallas_call(
        functools.partial(_mla_sample_kernel, past=latc.shape[1]),
        grid=(b,),
        in_specs=[blk(q), cache(latc), cache(krc), blk(latn), blk(kn), pl.BlockSpec(wukv.shape, lambda b: (0, 0))],
        out_specs=pl.BlockSpec((1, tq, MLA_HEADS * MLA_V), lambda b: (b, 0, 0)),
        out_shape=jax.ShapeDtypeStruct((b, tq, MLA_HEADS * MLA_V), BF16),
        compiler_params=_params("parallel"),
        name="mla_attention_sample",
    )(q, latc, krc, latn, kn, wukv)


def _diff_sample_kernel(q1_ref, q2_ref, kc_ref, vc_ref, kn_ref, vn_ref,
                        lq1_ref, lk1_ref, lq2_ref, lk2_ref, g_ref, o_ref, *, past, lambda_init):
    tq = q1_ref.shape[1]
    mask_c, mask_n = (jnp.concatenate([m, m], axis=0)
                      for m in (_chunk_mask(tq, past, past, 0), _chunk_mask(tq, tq, past, past)))
    lam = _lambda(lq1_ref, lk1_ref, lq2_ref, lk2_ref, lambda_init)
    for hd in range(DIFF_HEADS):
        rows = pl.ds(hd, past, stride=DIFF_HEADS)
        cols = slice(hd * LANES, (hd + 1) * LANES)
        kc, vc = kc_ref[0, rows, :].astype(BF16), vc_ref[0, rows, :].astype(BF16)
        q = jnp.concatenate([q1_ref[0, :, cols], q2_ref[0, :, cols]], axis=0)
        _, l, acc = _two_part_softmax(q, kc, vc, kn_ref[0, :, cols], vn_ref[0, :, cols], mask_c, mask_n)
        o = acc / l
        o = _rms(o[:tq] - lam * o[tq:], g_ref[...]) * (1.0 - lambda_init)
        o_ref[0, :, cols] = o.astype(o_ref.dtype)


def _diff_sample(q1, q2, kc, vc, kn, vn, lams, g, lambda_init, layer):
    b, tq, _ = q1.shape
    past = kc.shape[1] // DIFF_HEADS
    blk = lambda a: pl.BlockSpec((1,) + a.shape[1:], lambda b: (b, 0, 0))
    cache = pl.BlockSpec((1,) + kc.shape[1:], lambda b: (layer * q1.shape[0] + b, 0, 0))
    small = lambda a: pl.BlockSpec(a.shape, lambda b: (0, 0))
    return pl.pallas_call(
        functools.partial(_diff_sample_kernel, past=past, lambda_init=lambda_init),
        grid=(b,),
        in_specs=[blk(q1), blk(q2), cache, cache, blk(kn), blk(vn)] + [small(a) for a in lams] + [small(g)],
        out_specs=blk(q1),
        out_shape=jax.ShapeDtypeStruct((b, tq, DIFF_HEADS * DIFF_VD), BF16),
        compiler_params=_params("parallel"),
        name="diff_attention_sample",
    )(q1, q2, kc, vc, kn, vn, *lams, g)


def _rope_tables(pos, theta, rot_dim, period):
    half = rot_dim // 2
    inv = 1.0 / (jnp.float32(theta) ** (jnp.arange(half, dtype=F32) / half))
    ang = pos.astype(F32)[:, None] * inv[None, :]
    cos, sin = jnp.cos(ang), jnp.sin(ang)
    n = pos.shape[0]
    rest = period - rot_dim
    c = jnp.concatenate([cos, cos, jnp.ones((n, rest), F32)], axis=1)
    s1 = jnp.concatenate([-sin, jnp.zeros((n, half + rest), F32)], axis=1)
    s2 = jnp.concatenate([jnp.zeros((n, half), F32), sin, jnp.zeros((n, rest), F32)], axis=1)
    reps = LANES // period
    return tuple(jnp.tile(a, (1, reps)) for a in (c, s1, s2))


def _layer_weights(w_in, mla_w_uq, mla_w_ukv):
    c_kv = MLA_Q_LORA
    c_kr = c_kv + MLA_KV_LORA
    c_dq = c_kr + MLA_ROPE
    dqk = DIFF_HEADS * 2 * DIFF_DH
    c_dk = c_dq + dqk
    c_dv = c_dk + dqk
    c_ga = c_dv + DIFF_HEADS * DIFF_VD
    d = w_in.shape[0]
    c_gb = c_ga + d
    cols = lambda lo, hi: w_in[:, lo:hi].astype(BF16)
    wq = cols(0, c_kv)
    wkv = jnp.pad(cols(c_kv, c_dq), ((0, 0), (0, LANES - MLA_ROPE)))
    wdq, wdk, wdv = cols(c_dq, c_dk), cols(c_dk, c_dv), cols(c_dv, c_ga)
    wga, wgb = cols(c_ga, c_gb), cols(c_gb, w_in.shape[1])
    uq = mla_w_uq.astype(BF16).reshape(MLA_Q_LORA, MLA_HEADS, MLA_NOPE + MLA_ROPE)
    uq = jnp.pad(uq, ((0, 0), (0, 0), (0, MLA_QK_PAD - MLA_NOPE - MLA_ROPE))).reshape(MLA_Q_LORA, -1)
    ukv = mla_w_ukv.astype(BF16).reshape(MLA_KV_LORA, MLA_HEADS, MLA_NOPE + MLA_V)
    ukv = jnp.concatenate([ukv[:, :, :MLA_NOPE].reshape(MLA_KV_LORA, -1),
                           ukv[:, :, MLA_NOPE:].reshape(MLA_KV_LORA, -1)], axis=1)
    return wq, wkv, wdq, wdk, wdv, wga, wgb, uq, ukv


def _block(m, want):
    return want if m % want == 0 else m


def kernel(x_prompt, x_sample, cache_mla_ckv, cache_mla_krope, cache_diff_k, cache_diff_v, norm_mix, w_in,
           mla_q_norm, mla_w_uq, mla_kv_norm, mla_w_ukv, diff_lq1, diff_lk1, diff_lq2, diff_lk2, diff_subln,
           w_branch_a, w_branch_b, w_out, norm_ffn, w_ffn_in, w_ffn_out, norm_final):
    bp, lp, d = x_prompt.shape
    bs, ls, _ = x_sample.shape
    depth, _, past, _ = cache_mla_ckv.shape
    assert depth == 1, "the FFN kernel fuses the final norm, so it serves the last (only) layer"
    mp, ms = bp * lp, bs * ls
    t_attn = 512

    pos_p = jnp.arange(lp)
    pos_s = past + jnp.arange(ls)
    tabs_mla_p = _rope_tables(pos_p, MLA_THETA, MLA_ROPE, LANES)
    tabs_mla_s = tuple(jnp.tile(a, (bs, 1)) for a in _rope_tables(pos_s, MLA_THETA, MLA_ROPE, LANES))
    tabs_dif_p = _rope_tables(pos_p, ROPE_THETA, DIFF_ROT, DIFF_DH)
    tabs_dif_s = tuple(jnp.tile(a, (bs, 1)) for a in _rope_tables(pos_s, ROPE_THETA, DIFF_ROT, DIFF_DH))
    q_scale_mla = (MLA_NOPE + MLA_ROPE) ** -0.5 * LOG2E
    q_scale_dif = DIFF_DH ** -0.5 * LOG2E
    gfinal = norm_final.reshape(1, d)

    xp = x_prompt.reshape(mp, d)
    xs = x_sample.reshape(ms, d)
    rows_p, rows_s = [], []
    for l in range(depth):
        lambda_init = 0.8 - 0.6 * math.exp(-0.3 * l)
        wq, wkv, wdq, wdk, wdv, wga, wgb, uq, ukv = _layer_weights(w_in[l], mla_w_uq[l], mla_w_ukv[l])
        gmix, gq, gkv = norm_mix[l].reshape(1, -1), mla_q_norm[l].reshape(1, -1), mla_kv_norm[l].reshape(1, -1)
        lams = tuple(a[l].reshape(1, -1) for a in (diff_lq1, diff_lk1, diff_lq2, diff_lk2))
        gsub = diff_subln[l].reshape(1, -1)
        wa, wbr, wo = w_branch_a[l].astype(BF16), w_branch_b[l].astype(BF16), w_out[l].astype(BF16)
        wfi, wfo = w_ffn_in[l].astype(BF16), w_ffn_out[l].astype(BF16)
        gffn = norm_ffn[l].reshape(1, -1)

        def stage1(x, bm, tabs_mla, tabs_dif, tr):
            h, q, ckv, krope, kmla, vmla = _mla_proj(x, gmix, wq, gq, uq, wkv, gkv, ukv, tabs_mla, bm,
                                                     q_scale_mla, tr)
            q1, q2, dk, dkb, dv, dvb = _diff_proj(h, wdq, wdk, wdv, tabs_dif, bm, q_scale_dif, tr)
            return h, q, ckv, krope, kmla, vmla, q1, q2, dk, dkb, dv, dvb

        def stage2(x, h, oa, ob, bm, bn, bf):
            mg = _merge(h, oa, ob, wga, wgb, wa, wbr, bm, bn)
            x2, h2 = _outproj(x, mg, wo, gffn, min(bm, 512))
            return _ffn(h2, wfi, wfo, x2, gfinal, bm, bf)

        h, q, ckv, krope, kmla, vmla, q1, q2, dk, dkb, dv, dvb = stage1(xp, t_attn, tabs_mla_p, tabs_dif_p, True)
        r3 = lambda a: a.reshape(bp, lp, -1)
        oa = _mla_prompt(q, r3(kmla), vmla, t_attn, 4)
        ob = _diff_prompt(q1, q2, r3(dkb), dvb, lams, gsub, t_attn, 2, lambda_init)
        xp = stage2(xp, h, oa.reshape(mp, -1), ob.reshape(mp, -1), _block(mp, 1024), 1024, 256)
        rows_p.append((ckv.reshape(bp, lp, -1), krope.reshape(bp, lp, -1),
                       dk.reshape(bp, lp, DIFF_HEADS, -1), dv.reshape(bp, lp, DIFF_HEADS, -1)))

        h, q, ckv, krope, kmla, vmla, q1, q2, dk, dkb, dv, dvb = stage1(xs, ms, tabs_mla_s, tabs_dif_s, False)
        kr_pad = jnp.pad(cache_mla_krope.reshape(depth * bs, past, MLA_ROPE), ((0, 0), (0, 0), (0, LANES - MLA_ROPE)))
        r3 = lambda a: a.reshape(bs, ls, -1)
        oa = _mla_sample(r3(q), cache_mla_ckv.reshape(depth * bs, past, -1), kr_pad, r3(ckv), r3(kmla), ukv, l)
        call = lambda a: a.reshape(depth * bs, past * DIFF_HEADS, -1)
        ob = _diff_sample(r3(q1), r3(q2), call(cache_diff_k), call(cache_diff_v), r3(dkb), r3(dvb),
                          lams, gsub, lambda_init, l)
        xs = stage2(xs, h, oa.reshape(ms, -1), ob.reshape(ms, -1), ms, 512, 512)
        rows_s.append((ckv.reshape(bs, ls, -1), krope.reshape(bs, ls, -1),
                       dk.reshape(bs, ls, DIFF_HEADS, -1), dv.reshape(bs, ls, DIFF_HEADS, -1)))

    y_prompt = xp.reshape(bp, lp, d)
    y_sample = xs.reshape(bs, ls, d)
    stack = lambda rows, i: jnp.stack([r[i] for r in rows], axis=0)
    return (y_prompt, y_sample,
            stack(rows_p, 0), stack(rows_p, 1), stack(rows_p, 2), stack(rows_p, 3),
            stack(rows_s, 0), stack(rows_s, 1), stack(rows_s, 2), stack(rows_s, 3))
```

```python
import functools
import math

import jax
import jax.numpy as jnp
from jax import lax
from jax.experimental import pallas as pl
from jax.experimental.pallas import tpu as pltpu

F32 = jnp.float32
BF16 = jnp.bfloat16

CHUNK = 64
EPS = 1e-6
NEG_INF = -1e30

MLA_HEADS = 8
MLA_Q_LORA = 512
MLA_KV_LORA = 256
MLA_NOPE = 128
MLA_ROPE = 64
MLA_V = 128
MLA_THETA = 10000.0
MLA_QK_PAD = 256

DIFF_HEADS = 8
DIFF_DH = 64
DIFF_VD = 2 * DIFF_DH
DIFF_ROT = DIFF_DH // 4
ROPE_THETA = 500000.0

LANES = 128
VT_ONES = 16
LOG2E = math.log2(math.e)
VMEM_LIMIT = 56 * 1024 * 1024


def _params(*sem):
    return pltpu.CompilerParams(dimension_semantics=sem, vmem_limit_bytes=VMEM_LIMIT)


def _dot(a, b):
    return jnp.dot(a, b, preferred_element_type=F32)


def _dot_nt(a, b):
    return lax.dot_general(a, b, (((1,), (1,)), ((), ())), preferred_element_type=F32)


def _rms(x, g):
    return x * lax.rsqrt(jnp.mean(x * x, axis=-1, keepdims=True) + EPS) * g


def _rope_tile(t, c, s1, s2, half):
    return t * c + pltpu.roll(t, LANES - half, 1) * s1 + pltpu.roll(t, half, 1) * s2


def _put(ref, lo, val, tr):
    w = val.shape[1]
    if tr:
        ref[0, lo:lo + w, :] = val.T.astype(ref.dtype)
    else:
        ref[:, lo:lo + w] = val.astype(ref.dtype)


def _put_v(ref, hd, val, tr):
    bm, w = val.shape
    if tr:
        lo = hd * (w + VT_ONES)
        ref[0, lo:lo + w, :] = val.T.astype(ref.dtype)
        ref[0, lo + w:lo + w + VT_ONES, :] = jnp.ones((VT_ONES, bm), ref.dtype)
    else:
        ref[:, hd * w:(hd + 1) * w] = val.astype(ref.dtype)


def _v_width(heads, dv, tr):
    return heads * (dv + VT_ONES) if tr else heads * dv


def _out(m, n, bm, dtype, tr):
    if tr:
        return pl.BlockSpec((1, n, bm), lambda i: (i, 0, 0)), jax.ShapeDtypeStruct((m // bm, n, bm), dtype)
    return pl.BlockSpec((bm, n), lambda i: (i, 0)), jax.ShapeDtypeStruct((m, n), dtype)


def _qproj_body(h, wq_ref, gq_ref, wuq_ref, c, s1, s2, o_ref, scale, tr):
    qlat = _rms(_dot(h, wq_ref[...]), gq_ref[...]).astype(BF16)
    q = _dot(qlat, wuq_ref[...])
    for hd in range(MLA_HEADS):
        lo = hd * MLA_QK_PAD
        _put(o_ref, lo, q[:, lo:lo + LANES] * scale, tr)
        t = _rope_tile(q[:, lo + LANES:lo + 2 * LANES], c, s1, s2, MLA_ROPE // 2)
        _put(o_ref, lo + LANES, t * scale, tr)


def _kv_expand(ckv, krope_tile, wukv_ref, kmla_ref, vmla_ref, tr):
    kv = _dot(ckv.astype(BF16), wukv_ref[...])
    kr = krope_tile.astype(BF16)
    for hd in range(MLA_HEADS):
        lo = hd * MLA_QK_PAD
        kmla_ref[:, lo:lo + LANES] = kv[:, hd * MLA_NOPE:(hd + 1) * MLA_NOPE].astype(BF16)
        kmla_ref[:, lo + LANES:lo + 2 * LANES] = kr
        vlo = MLA_HEADS * MLA_NOPE + hd * MLA_V
        _put_v(vmla_ref, hd, kv[:, vlo:vlo + MLA_V], tr)


def _mla_proj_kernel(x_ref, gmix_ref, wq_ref, gq_ref, wuq_ref, wkv_ref, gkv_ref, wukv_ref, c_ref, s1_ref, s2_ref,
                     h_ref, q_ref, ckv_ref, krope_ref, kmla_ref, vmla_ref, *, scale, tr):
    h = _rms(x_ref[...], gmix_ref[...]).astype(BF16)
    h_ref[...] = h
    c, s1, s2 = c_ref[...], s1_ref[...], s2_ref[...]
    _qproj_body(h, wq_ref, gq_ref, wuq_ref, c, s1, s2, q_ref, scale, tr)
    z = _dot(h, wkv_ref[...])
    ckv = _rms(z[:, :MLA_KV_LORA], gkv_ref[...])
    ckv_ref[...] = ckv
    t = _rope_tile(z[:, MLA_KV_LORA:], c, s1, s2, MLA_ROPE // 2)
    krope_ref[...] = t[:, :MLA_ROPE]
    _kv_expand(ckv, t, wukv_ref, kmla_ref, vmla_ref, tr)


def _mla_proj(x, gmix, wq, gq, wuq, wkv, gkv, wukv, tabs, bm, scale, tr):
    m, d = x.shape
    nt = tabs[0].shape[0] // bm
    full = lambda a: pl.BlockSpec(a.shape, lambda i: (0, 0))
    tab = pl.BlockSpec((bm, LANES), lambda i: (i % nt, 0))
    row = lambda n: pl.BlockSpec((bm, n), lambda i: (i, 0))
    nk = MLA_HEADS * MLA_QK_PAD
    qspec, qshape = _out(m, wuq.shape[1], bm, BF16, tr)
    vspec, vshape = _out(m, _v_width(MLA_HEADS, MLA_V, tr), bm, BF16, tr)
    return pl.pallas_call(
        functools.partial(_mla_proj_kernel, scale=scale, tr=tr),
        grid=(m // bm,),
        in_specs=[row(d), full(gmix), full(wq), full(gq), full(wuq), full(wkv), full(gkv), full(wukv), tab, tab, tab],
        out_specs=[row(d), qspec, row(MLA_KV_LORA), row(MLA_ROPE), row(nk), vspec],
        out_shape=[jax.ShapeDtypeStruct((m, d), BF16), qshape,
                   jax.ShapeDtypeStruct((m, MLA_KV_LORA), F32), jax.ShapeDtypeStruct((m, MLA_ROPE), F32),
                   jax.ShapeDtypeStruct((m, nk), BF16), vshape],
        compiler_params=_params("parallel"),
        name="norm_mla_proj",
    )(x, gmix, wq, gq, wuq, wkv, gkv, wukv, *tabs)


def _diff_proj_kernel(h_ref, wdq_ref, wdk_ref, wdv_ref, c_ref, s1_ref, s2_ref,
                      q1_ref, q2_ref, dk_ref, dkb_ref, dv_ref, dvb_ref, *, scale, tr):
    h = h_ref[...]
    zq = _dot(h, wdq_ref[...])
    zk = _dot(h, wdk_ref[...])
    zv = _dot(h, wdv_ref[...])
    dv_ref[...] = zv
    c, s1, s2 = c_ref[...], s1_ref[...], s2_ref[...]
    first = lax.broadcasted_iota(jnp.int32, (1, LANES), 1) < DIFF_DH
    for hd in range(DIFF_HEADS):
        sl = slice(hd * LANES, (hd + 1) * LANES)
        q = _rope_tile(zq[:, sl], c, s1, s2, DIFF_ROT // 2) * scale
        _put(q1_ref, hd * LANES, jnp.where(first, q, 0.0), tr)
        _put(q2_ref, hd * LANES, jnp.where(first, 0.0, q), tr)
        k = _rope_tile(zk[:, sl], c, s1, s2, DIFF_ROT // 2)
        dk_ref[:, sl] = k
        dkb_ref[:, sl] = k.astype(BF16)
        _put_v(dvb_ref, hd, zv[:, sl], tr)


def _diff_proj(h, wdq, wdk, wdv, tabs, bm, scale, tr):
    m, d = h.shape
    nt = tabs[0].shape[0] // bm
    n = wdq.shape[1]
    full = lambda a: pl.BlockSpec(a.shape, lambda i: (0, 0))
    tab = pl.BlockSpec((bm, LANES), lambda i: (i % nt, 0))
    row = lambda w: pl.BlockSpec((bm, w), lambda i: (i, 0))
    qspec, qshape = _out(m, n, bm, BF16, tr)
    vspec, vshape = _out(m, _v_width(DIFF_HEADS, DIFF_VD, tr), bm, BF16, tr)
    f32_rows, bf16_rows = jax.ShapeDtypeStruct((m, n), F32), jax.ShapeDtypeStruct((m, n), BF16)
    return pl.pallas_call(
        functools.partial(_diff_proj_kernel, scale=scale, tr=tr),
        grid=(m // bm,),
        in_specs=[row(d), full(wdq), full(wdk), full(wdv), tab, tab, tab],
        out_specs=[qspec, qspec, row(n), row(n), row(n), vspec],
        out_shape=[qshape, qshape, f32_rows, bf16_rows, f32_rows, vshape],
        compiler_params=_params("parallel"),
        name="diff_proj",
    )(h, wdq, wdk, wdv, *tabs)


def _merge_kernel(h_ref, oa_ref, ob_ref, wga_ref, wgb_ref, wa_ref, wb_ref, o_ref):
    h = h_ref[...]
    ga = jax.nn.sigmoid(_dot(h, wga_ref[...]))
    gb = jax.nn.sigmoid(_dot(h, wgb_ref[...]))
    ya = _dot(oa_ref[...], wa_ref[...])
    yb = _dot(ob_ref[...], wb_ref[...])
    o_ref[...] = (ga * ya + gb * yb).astype(o_ref.dtype)


def _merge(h, oa, ob, wga, wgb, wa, wb, bm, bn):
    m, d = h.shape
    n = wga.shape[1]
    row = lambda a: pl.BlockSpec((bm, a.shape[1]), lambda i, j: (i, 0))
    col = lambda a: pl.BlockSpec((a.shape[0], bn), lambda i, j: (0, j))
    return pl.pallas_call(
        _merge_kernel,
        grid=(m // bm, n // bn),
        in_specs=[row(h), row(oa), row(ob), col(wga), col(wgb), col(wa), col(wb)],
        out_specs=pl.BlockSpec((bm, bn), lambda i, j: (i, j)),
        out_shape=jax.ShapeDtypeStruct((m, n), BF16),
        compiler_params=_params("parallel", "arbitrary"),
        name="gated_merge",
    )(h, oa, ob, wga, wgb, wa, wb)


def _outproj_kernel(x_ref, mg_ref, wo_ref, g_ref, x2_ref, h2_ref):
    x2 = x_ref[...] + _dot(mg_ref[...], wo_ref[...])
    x2_ref[...] = x2
    h2_ref[...] = _rms(x2, g_ref[...]).astype(BF16)


def _outproj(x, mg, wo, g, bm):
    m, d = x.shape
    row = pl.BlockSpec((bm, d), lambda i: (i, 0))
    return pl.pallas_call(
        _outproj_kernel,
        grid=(m // bm,),
        in_specs=[row, row, pl.BlockSpec(wo.shape, lambda i: (0, 0)), pl.BlockSpec((1, d), lambda i: (0, 0))],
        out_specs=[row, row],
        out_shape=[jax.ShapeDtypeStruct((m, d), F32), jax.ShapeDtypeStruct((m, d), BF16)],
        compiler_params=_params("parallel"),
        name="out_proj_residual",
    )(x, mg, wo, g)


def _ffn_kernel(h2_ref, wg_ref, wu_ref, wd_ref, x2_ref, gf_ref, y_ref):
    f = pl.program_id(1)

    @pl.when(f == 0)
    def _():
        y_ref[...] = x2_ref[...]

    h2 = h2_ref[...]
    a = (jax.nn.silu(_dot(h2, wg_ref[...])) * _dot(h2, wu_ref[...])).astype(BF16)
    y_ref[...] += _dot(a, wd_ref[...])

    @pl.when(f == pl.num_programs(1) - 1)
    def _():
        y_ref[...] = _rms(y_ref[...], gf_ref[...])


def _ffn(h2, w_in, w_out, x2, gf, bm, bf):
    m, d = h2.shape
    dff = w_out.shape[0]
    nf = dff // bf
    row = pl.BlockSpec((bm, d), lambda i, f: (i, 0))
    return pl.pallas_call(
        _ffn_kernel,
        grid=(m // bm, nf),
        in_specs=[pl.BlockSpec((bm, d), lambda i, f: (i, 0), pipeline_mode=pl.Buffered(1)),
                  pl.BlockSpec((d, bf), lambda i, f: (0, f)),
                  pl.BlockSpec((d, bf), lambda i, f: (0, f + nf)),
                  pl.BlockSpec((bf, d), lambda i, f: (f, 0)),
                  row,
                  pl.BlockSpec((1, d), lambda i, f: (0, 0))],
        out_specs=row,
        out_shape=jax.ShapeDtypeStruct((m, d), F32),
        compiler_params=_params("parallel", "arbitrary"),
        name="ffn_swiglu_final_norm",
    )(h2, w_in, w_in, w_out, x2, gf)


def _softmax_step(s, v, m, l, acc):
    m_new = jnp.maximum(m, jnp.max(s, axis=-1, keepdims=True))
    alpha = jnp.exp2(m - m_new)
    p = jnp.exp2(s - m_new)
    l = alpha * l + jnp.sum(p, axis=-1, keepdims=True)
    acc = alpha * acc + _dot(p.astype(BF16), v)
    return m_new, l, acc


def _chunk_mask(tq, tk, q0, k0):
    qp = q0 + lax.broadcasted_iota(jnp.int32, (tq, tk), 0)
    kp = k0 + lax.broadcasted_iota(jnp.int32, (tq, tk), 1)
    return (kp // CHUNK) <= (qp // CHUNK)


def _softmax_init(tq, dv):
    return (jnp.full((tq, 1), NEG_INF, F32), jnp.zeros((tq, 1), F32), jnp.zeros((tq, dv), F32))


def _chunk_mask_t(t):
    kp = lax.broadcasted_iota(jnp.int32, (t, t), 0)
    qp = lax.broadcasted_iota(jnp.int32, (t, t), 1)
    return (kp // CHUNK) <= (qp // CHUNK)


def _probs_t(s, smax, m):
    m_new = jnp.maximum(m, smax)
    return jnp.exp2(s - m_new).astype(BF16), jnp.exp2(m - m_new), m_new


def _attention_scratch(t, dv, streams):
    return [pltpu.VMEM((streams, 2, t, t), F32),
            pltpu.VMEM((streams, dv + VT_ONES, t), F32),
            pltpu.VMEM((streams, 8, t), F32)]


def _causal_attention_t(n, scores, next_scores, values, finish, s_ref, acc_ref, stat_ref, t, streams):
    row_m, row_smax = 0, 1

    def stat(st, r):
        return stat_ref[st, r:r + 1, :]

    def put_scores(st, slot, s):
        s_ref[st, slot] = s
        stat_ref[st, row_smax + slot:row_smax + slot + 1, :] = jnp.max(s, axis=0, keepdims=True)

    def accumulate(st, j, s, smax):
        p, alpha, m = _probs_t(s, smax, stat(st, row_m))
        acc = alpha * acc_ref[st] + _dot(values(j, st), p)
        stat_ref[st, row_m:row_m + 1, :] = m
        return acc

    @pl.when(n == 0)
    def _():
        for st in range(streams):
            put_scores(st, 0, scores(0, st))

    for st in range(streams):
        acc_ref[st] = jnp.zeros(acc_ref.shape[1:], F32)
        stat_ref[st, row_m:row_m + 1, :] = jnp.full((1, t), NEG_INF, F32)

    def run(first):
        def step(tau, par):
            for st in range(streams):
                put_scores(st, (first + par) % 2, scores(tau, st))
                prev = (first + par + 1) % 2
                acc_ref[st] = accumulate(st, tau - 1, s_ref[st, prev], stat(st, row_smax + prev))

        unroll = 4

        def trip(u, carry):
            for k in range(unroll):
                step(unroll * u + 1 + k, (1 + k) % 2)
            return carry

        lax.fori_loop(0, n // unroll, trip, 0)
        done = (n // unroll) * unroll

        @pl.when(n - done >= 2)
        def _():
            step(done + 1, 1)
            step(done + 2, 0)

        @pl.when(n % 2 == 1)
        def _():
            step(n, 1)

        def tail(par):
            diag = (first + par) % 2
            for st in range(streams):
                put_scores(st, 1 - diag, next_scores(st))
            mask = _chunk_mask_t(t)
            res = []
            for st in range(streams):
                halves = []
                for c in range(0, t, t // 2):
                    rows, cols = slice(0, c + t // 2), slice(c, c + t // 2)
                    s = jnp.where(mask[rows, cols], s_ref[st, diag, rows, cols], NEG_INF)
                    p, alpha, _ = _probs_t(s, jnp.max(s, axis=0, keepdims=True), stat_ref[st, row_m:row_m + 1, cols])
                    halves.append(alpha * acc_ref[st, :, cols] + _dot(values(n, st)[:, rows], p))
                res.append(jnp.concatenate(halves, axis=1))
            finish(res)

        for par in (0, 1):
            pl.when(n % 2 == par)(functools.partial(tail, par))

    first = ((n + 1) // 2) % 2
    for f in (0, 1):
        pl.when(first == f)(functools.partial(run, f))


def _normalised(acc, dv):
    return (acc[:dv] / acc[dv:dv + 1]).T


def _mla_prompt_kernel(qt_ref, qn_ref, k_ref, vt_ref, o_ref, *scratch, t, heads):
    def block_scores(q_ref, j, hd):
        k = k_ref[0, pl.ds(pl.multiple_of(j * t, t), t), hd * MLA_QK_PAD:(hd + 1) * MLA_QK_PAD]
        return _dot(k, q_ref[0, hd * MLA_QK_PAD:(hd + 1) * MLA_QK_PAD, :])

    scores = functools.partial(block_scores, qt_ref)
    next_scores = functools.partial(block_scores, qn_ref, 0)

    def values(j, hd):
        return vt_ref[j, hd * (MLA_V + VT_ONES):(hd + 1) * (MLA_V + VT_ONES), :]

    def finish(res):
        for hd, acc in enumerate(res):
            o_ref[0, :, hd * MLA_V:(hd + 1) * MLA_V] = _normalised(acc, MLA_V).astype(o_ref.dtype)

    _causal_attention_t(pl.program_id(2), scores, next_scores, values, finish, *scratch, t, heads)


def _mla_prompt(qt, k, vt, t, heads):
    b, l, _ = k.shape
    nq = l // t
    qblock = lambda step: pl.BlockSpec((1, heads * MLA_QK_PAD, t),
                                       lambda b, h, i: (b * nq + jnp.minimum(i + step, nq - 1), h, 0))
    return pl.pallas_call(
        functools.partial(_mla_prompt_kernel, t=t, heads=heads),
        grid=(b, MLA_HEADS // heads, nq),
        in_specs=[qblock(0), qblock(1),
                  pl.BlockSpec((1, l, heads * MLA_QK_PAD), lambda b, h, i: (b, 0, h), pipeline_mode=pl.Buffered(1)),
                  pl.BlockSpec((nq, heads * (MLA_V + VT_ONES), t), lambda b, h, i: (b, h, 0),
                               pipeline_mode=pl.Buffered(1))],
        out_specs=pl.BlockSpec((1, t, heads * MLA_V), lambda b, h, i: (b, i, h)),
        out_shape=jax.ShapeDtypeStruct((b, l, MLA_HEADS * MLA_V), BF16),
        scratch_shapes=_attention_scratch(t, MLA_V, heads),
        compiler_params=_params("parallel", "parallel", "arbitrary"),
        name="mla_attention_prompt",
    )(qt, qt, k, vt)


def _lambda(lq1_ref, lk1_ref, lq2_ref, lk2_ref, lambda_init):
    a = jnp.sum(lq1_ref[...] * lk1_ref[...], axis=-1, keepdims=True)
    b = jnp.sum(lq2_ref[...] * lk2_ref[...], axis=-1, keepdims=True)
    return jnp.exp(a) - jnp.exp(b) + lambda_init


def _diff_prompt_kernel(q1t_ref, q2t_ref, q1n_ref, q2n_ref, k_ref, vt_ref, lq1_ref, lk1_ref, lq2_ref, lk2_ref,
                        g_ref, o_ref, *scratch, t, heads, lambda_init):
    qt_refs, qn_refs = (q1t_ref, q2t_ref), (q1n_ref, q2n_ref)

    def block_scores(q_refs, rows, st):
        cols = slice((st // 2) * LANES, (st // 2 + 1) * LANES)
        return _dot(k_ref[0, rows, cols], q_refs[st % 2][0, cols, :])

    def scores(j, st):
        return block_scores(qt_refs, pl.ds(pl.multiple_of(j * t, t), t), st)

    def next_scores(st):
        return block_scores(qn_refs, slice(0, t), st)

    def values(j, st):
        return vt_ref[j, (st // 2) * (DIFF_VD + VT_ONES):(st // 2 + 1) * (DIFF_VD + VT_ONES), :]

    def finish(res):
        lam = _lambda(lq1_ref, lk1_ref, lq2_ref, lk2_ref, lambda_init)
        for hd in range(heads):
            o = _normalised(res[2 * hd], DIFF_VD) - lam * _normalised(res[2 * hd + 1], DIFF_VD)
            o = _rms(o, g_ref[...]) * (1.0 - lambda_init)
            o_ref[0, :, hd * DIFF_VD:(hd + 1) * DIFF_VD] = o.astype(o_ref.dtype)

    _causal_attention_t(pl.program_id(2), scores, next_scores, values, finish, *scratch, t, 2 * heads)


def _diff_prompt(q1t, q2t, k, vt, lams, g, t, heads, lambda_init):
    b, l, _ = k.shape
    nq = l // t
    qs = pl.BlockSpec((1, heads * LANES, t), lambda b, h, i: (b * nq + i, h, 0))
    qn = pl.BlockSpec((1, heads * LANES, t), lambda b, h, i: (b * nq + jnp.minimum(i + 1, nq - 1), h, 0))
    small = lambda a: pl.BlockSpec(a.shape, lambda b, h, i: (0, 0))
    return pl.pallas_call(
        functools.partial(_diff_prompt_kernel, t=t, heads=heads, lambda_init=lambda_init),
        grid=(b, DIFF_HEADS // heads, nq),
        in_specs=[qs, qs, qn, qn,
                  pl.BlockSpec((1, l, heads * LANES), lambda b, h, i: (b, 0, h)),
                  pl.BlockSpec((nq, heads * (DIFF_VD + VT_ONES), t), lambda b, h, i: (b, h, 0))]
                 + [small(a) for a in lams] + [small(g)],
        out_specs=pl.BlockSpec((1, t, heads * DIFF_VD), lambda b, h, i: (b, i, h)),
        out_shape=jax.ShapeDtypeStruct((b, l, DIFF_HEADS * DIFF_VD), BF16),
        scratch_shapes=_attention_scratch(t, DIFF_VD, 2 * heads),
        compiler_params=_params("parallel", "parallel", "arbitrary"),
        name="diff_attention_prompt",
    )(q1t, q2t, q1t, q2t, k, vt, *lams, g)


def _two_part_softmax(q, kc, vc, kn, vn, mask_c, mask_n):
    tq = q.shape[0]
    carry = _softmax_init(tq, vc.shape[1])
    carry = _softmax_step(jnp.where(mask_c, _dot_nt(q, kc), NEG_INF), vc, *carry)
    return _softmax_step(jnp.where(mask_n, _dot_nt(q, kn), NEG_INF), vn, *carry)


def _mla_sample_kernel(q_ref, latc_ref, krc_ref, latn_ref, kn_ref, wukv_ref, o_ref, *, past):
    tq = q_ref.shape[1]
    latc, latn = latc_ref[0].astype(BF16), latn_ref[0].astype(BF16)
    krc, krn = krc_ref[0].astype(BF16), kn_ref[0, :, LANES:2 * LANES]
    qa, qr = [], []
    for hd in range(MLA_HEADS):
        lo = hd * MLA_QK_PAD
        wk = wukv_ref[:, hd * MLA_NOPE:(hd + 1) * MLA_NOPE]
        qa.append(_dot_nt(q_ref[0, :, lo:lo + LANES], wk).astype(BF16))
        qr.append(q_ref[0, :, lo + LANES:lo + 2 * LANES])
    qa, qr = jnp.concatenate(qa, axis=0), jnp.concatenate(qr, axis=0)
    rows = MLA_HEADS * tq

    def mask(tk, k0):
        qp = past + lax.broadcasted_iota(jnp.int32, (rows, tk), 0) % tq
        kp = k0 + lax.broadcasted_iota(jnp.int32, (rows, tk), 1)
        return (kp // CHUNK) <= (qp // CHUNK)

    carry = _softmax_init(rows, MLA_KV_LORA)
    s = _dot_nt(qa, latc) + _dot_nt(qr, krc)
    carry = _softmax_step(jnp.where(mask(past, 0), s, NEG_INF), latc, *carry)
    s = _dot_nt(qa, latn) + _dot_nt(qr, krn)
    _, l, acc = _softmax_step(jnp.where(mask(tq, past), s, NEG_INF), latn, *carry)
    olat = (acc / l).astype(BF16)
    v0 = MLA_HEADS * MLA_NOPE
    for hd in range(MLA_HEADS):
        wv = wukv_ref[:, v0 + hd * MLA_V:v0 + (hd + 1) * MLA_V]
        o_ref[0, :, hd * MLA_V:(hd + 1) * MLA_V] = _dot(olat[hd * tq:(hd + 1) * tq], wv).astype(o_ref.dtype)


def _mla_sample(q, latc, krc, latn, kn, wukv, layer):
    b, tq, _ = q.shape
    blk = lambda a: pl.BlockSpec((1,) + a.shape[1:], lambda b: (b, 0, 0))
    cache = lambda a: pl.BlockSpec((1,) + a.shape[1:], lambda b: (layer * q.shape[0] + b, 0, 0))
    return pl.pallas_call(
        functools.partial(_mla_sample_kernel, past=latc.shape[1]),
        grid=(b,),
        in_specs=[blk(q), cache(latc), cache(krc), blk(latn), blk(kn), pl.BlockSpec(wukv.shape, lambda b: (0, 0))],
        out_specs=pl.BlockSpec((1, tq, MLA_HEADS * MLA_V), lambda b: (b, 0, 0)),
        out_shape=jax.ShapeDtypeStruct((b, tq, MLA_HEADS * MLA_V), BF16),
        compiler_params=_params("parallel"),
        name="mla_attention_sample",
    )(q, latc, krc, latn, kn, wukv)


def _diff_sample_kernel(q1_ref, q2_ref, kc_ref, vc_ref, kn_ref, vn_ref,
                        lq1_ref, lk1_ref, lq2_ref, lk2_ref, g_ref, o_ref, *, past, lambda_init):
    tq = q1_ref.shape[1]
    mask_c, mask_n = (jnp.concatenate([m, m], axis=0)
                      for m in (_chunk_mask(tq, past, past, 0), _chunk_mask(tq, tq, past, past)))
    lam = _lambda(lq1_ref, lk1_ref, lq2_ref, lk2_ref, lambda_init)
    for hd in range(DIFF_HEADS):
        rows = pl.ds(hd, past, stride=DIFF_HEADS)
        cols = slice(hd * LANES, (hd + 1) * LANES)
        kc, vc = kc_ref[0, rows, :].astype(BF16), vc_ref[0, rows, :].astype(BF16)
        q = jnp.concatenate([q1_ref[0, :, cols], q2_ref[0, :, cols]], axis=0)
        _, l, acc = _two_part_softmax(q, kc, vc, kn_ref[0, :, cols], vn_ref[0, :, cols], mask_c, mask_n)
        o = acc / l
        o = _rms(o[:tq] - lam * o[tq:], g_ref[...]) * (1.0 - lambda_init)
        o_ref[0, :, cols] = o.astype(o_ref.dtype)


def _diff_sample(q1, q2, kc, vc, kn, vn, lams, g, lambda_init, layer):
    b, tq, _ = q1.shape
    past = kc.shape[1] // DIFF_HEADS
    blk = lambda a: pl.BlockSpec((1,) + a.shape[1:], lambda b: (b, 0, 0))
    cache = pl.BlockSpec((1,) + kc.shape[1:], lambda b: (layer * q1.shape[0] + b, 0, 0))
    small = lambda a: pl.BlockSpec(a.shape, lambda b: (0, 0))
    return pl.pallas_call(
        functools.partial(_diff_sample_kernel, past=past, lambda_init=lambda_init),
        grid=(b,),
        in_specs=[blk(q1), blk(q2), cache, cache, blk(kn), blk(vn)] + [small(a) for a in lams] + [small(g)],
        out_specs=blk(q1),
        out_shape=jax.ShapeDtypeStruct((b, tq, DIFF_HEADS * DIFF_VD), BF16),
        compiler_params=_params("parallel"),
        name="diff_attention_sample",
    )(q1, q2, kc, vc, kn, vn, *lams, g)


def _rope_tables(pos, theta, rot_dim, period):
    half = rot_dim // 2
    inv = 1.0 / (jnp.float32(theta) ** (jnp.arange(half, dtype=F32) / half))
    ang = pos.astype(F32)[:, None] * inv[None, :]
    cos, sin = jnp.cos(ang), jnp.sin(ang)
    n = pos.shape[0]
    rest = period - rot_dim
    c = jnp.concatenate([cos, cos, jnp.ones((n, rest), F32)], axis=1)
    s1 = jnp.concatenate([-sin, jnp.zeros((n, half + rest), F32)], axis=1)
    s2 = jnp.concatenate([jnp.zeros((n, half), F32), sin, jnp.zeros((n, rest), F32)], axis=1)
    reps = LANES // period
    return tuple(jnp.tile(a, (1, reps)) for a in (c, s1, s2))


def _layer_weights(w_in, mla_w_uq, mla_w_ukv):
    c_kv = MLA_Q_LORA
    c_kr = c_kv + MLA_KV_LORA
    c_dq = c_kr + MLA_ROPE
    dqk = DIFF_HEADS * 2 * DIFF_DH
    c_dk = c_dq + dqk
    c_dv = c_dk + dqk
    c_ga = c_dv + DIFF_HEADS * DIFF_VD
    d = w_in.shape[0]
    c_gb = c_ga + d
    cols = lambda lo, hi: w_in[:, lo:hi].astype(BF16)
    wq = cols(0, c_kv)
    wkv = jnp.pad(cols(c_kv, c_dq), ((0, 0), (0, LANES - MLA_ROPE)))
    wdq, wdk, wdv = cols(c_dq, c_dk), cols(c_dk, c_dv), cols(c_dv, c_ga)
    wga, wgb = cols(c_ga, c_gb), cols(c_gb, w_in.shape[1])
    uq = mla_w_uq.astype(BF16).reshape(MLA_Q_LORA, MLA_HEADS, MLA_NOPE + MLA_ROPE)
    uq = jnp.pad(uq, ((0, 0), (0, 0), (0, MLA_QK_PAD - MLA_NOPE - MLA_ROPE))).reshape(MLA_Q_LORA, -1)
    ukv = mla_w_ukv.astype(BF16).reshape(MLA_KV_LORA, MLA_HEADS, MLA_NOPE + MLA_V)
    ukv = jnp.concatenate([ukv[:, :, :MLA_NOPE].reshape(MLA_KV_LORA, -1),
                           ukv[:, :, MLA_NOPE:].reshape(MLA_KV_LORA, -1)], axis=1)
    return wq, wkv, wdq, wdk, wdv, wga, wgb, uq, ukv


def _block(m, want):
    return want if m % want == 0 else m


def kernel(x_prompt, x_sample, cache_mla_ckv, cache_mla_krope, cache_diff_k, cache_diff_v, norm_mix, w_in,
           mla_q_norm, mla_w_uq, mla_kv_norm, mla_w_ukv, diff_lq1, diff_lk1, diff_lq2, diff_lk2, diff_subln,
           w_branch_a, w_branch_b, w_out, norm_ffn, w_ffn_in, w_ffn_out, norm_final):
    bp, lp, d = x_prompt.shape
    bs, ls, _ = x_sample.shape
    depth, _, past, _ = cache_mla_ckv.shape
    assert depth == 1, "the FFN kernel fuses the final norm, so it serves the last (only) layer"
    mp, ms = bp * lp, bs * ls
    t_attn = 512

    pos_p = jnp.arange(lp)
    pos_s = past + jnp.arange(ls)
    tabs_mla_p = _rope_tables(pos_p, MLA_THETA, MLA_ROPE, LANES)
    tabs_mla_s = tuple(jnp.tile(a, (bs, 1)) for a in _rope_tables(pos_s, MLA_THETA, MLA_ROPE, LANES))
    tabs_dif_p = _rope_tables(pos_p, ROPE_THETA, DIFF_ROT, DIFF_DH)
    tabs_dif_s = tuple(jnp.tile(a, (bs, 1)) for a in _rope_tables(pos_s, ROPE_THETA, DIFF_ROT, DIFF_DH))
    q_scale_mla = (MLA_NOPE + MLA_ROPE) ** -0.5 * LOG2E
    q_scale_dif = DIFF_DH ** -0.5 * LOG2E
    gfinal = norm_final.reshape(1, d)

    xp = x_prompt.reshape(mp, d)
    xs = x_sample.reshape(ms, d)
    rows_p, rows_s = [], []
    for l in range(depth):
        lambda_init = 0.8 - 0.6 * math.exp(-0.3 * l)
        wq, wkv, wdq, wdk, wdv, wga, wgb, uq, ukv = _layer_weights(w_in[l], mla_w_uq[l], mla_w_ukv[l])
        gmix, gq, gkv = norm_mix[l].reshape(1, -1), mla_q_norm[l].reshape(1, -1), mla_kv_norm[l].reshape(1, -1)
        lams = tuple(a[l].reshape(1, -1) for a in (diff_lq1, diff_lk1, diff_lq2, diff_lk2))
        gsub = diff_subln[l].reshape(1, -1)
        wa, wbr, wo = w_branch_a[l].astype(BF16), w_branch_b[l].astype(BF16), w_out[l].astype(BF16)
        wfi, wfo = w_ffn_in[l].astype(BF16), w_ffn_out[l].astype(BF16)
        gffn = norm_ffn[l].reshape(1, -1)

        def stage1(x, bm, tabs_mla, tabs_dif, tr):
            h, q, ckv, krope, kmla, vmla = _mla_proj(x, gmix, wq, gq, uq, wkv, gkv, ukv, tabs_mla, bm,
                                                     q_scale_mla, tr)
            q1, q2, dk, dkb, dv, dvb = _diff_proj(h, wdq, wdk, wdv, tabs_dif, bm, q_scale_dif, tr)
            return h, q, ckv, krope, kmla, vmla, q1, q2, dk, dkb, dv, dvb

        def stage2(x, h, oa, ob, bm, bn, bf):
            mg = _merge(h, oa, ob, wga, wgb, wa, wbr, bm, bn)
            x2, h2 = _outproj(x, mg, wo, gffn, min(bm, 512))
            return _ffn(h2, wfi, wfo, x2, gfinal, bm, bf)

        h, q, ckv, krope, kmla, vmla, q1, q2, dk, dkb, dv, dvb = stage1(xp, t_attn, tabs_mla_p, tabs_dif_p, True)
        r3 = lambda a: a.reshape(bp, lp, -1)
        oa = _mla_prompt(q, r3(kmla), vmla, t_attn, 4)
        ob = _diff_prompt(q1, q2, r3(dkb), dvb, lams, gsub, t_attn, 2, lambda_init)
        xp = stage2(xp, h, oa.reshape(mp, -1), ob.reshape(mp, -1), _block(mp, 1024), 1024, 512)
        rows_p.append((ckv.reshape(bp, lp, -1), krope.reshape(bp, lp, -1),
                       dk.reshape(bp, lp, DIFF_HEADS, -1), dv.reshape(bp, lp, DIFF_HEADS, -1)))

        h, q, ckv, krope, kmla, vmla, q1, q2, dk, dkb, dv, dvb = stage1(xs, ms, tabs_mla_s, tabs_dif_s, False)
        kr_pad = jnp.pad(cache_mla_krope.reshape(depth * bs, past, MLA_ROPE), ((0, 0), (0, 0), (0, LANES - MLA_ROPE)))
        r3 = lambda a: a.reshape(bs, ls, -1)
        oa = _mla_sample(r3(q), cache_mla_ckv.reshape(depth * bs, past, -1), kr_pad, r3(ckv), r3(kmla), ukv, l)
        call = lambda a: a.reshape(depth * bs, past * DIFF_HEADS, -1)
        ob = _diff_sample(r3(q1), r3(q2), call(cache_diff_k), call(cache_diff_v), r3(dkb), r3(dvb),
                          lams, gsub, lambda_init, l)
        xs = stage2(xs, h, oa.reshape(ms, -1), ob.reshape(ms, -1), ms, 512, 512)
        rows_s.append((ckv.reshape(bs, ls, -1), krope.reshape(bs, ls, -1),
                       dk.reshape(bs, ls, DIFF_HEADS, -1), dv.reshape(bs, ls, DIFF_HEADS, -1)))

    y_prompt = xp.reshape(bp, lp, d)
    y_sample = xs.reshape(bs, ls, d)
    stack = lambda rows, i: jnp.stack([r[i] for r in rows], axis=0)
    return (y_prompt, y_sample,
            stack(rows_p, 0), stack(rows_p, 1), stack(rows_p, 2), stack(rows_p, 3),
            stack(rows_s, 0), stack(rows_s, 1), stack(rows_s, 2), stack(rows_s, 3))
```

```python
import functools
import math

import jax
import jax.numpy as jnp
from jax import lax
from jax.experimental import pallas as pl
from jax.experimental.pallas import tpu as pltpu

F32 = jnp.float32
BF16 = jnp.bfloat16

CHUNK = 64
EPS = 1e-6
NEG_INF = -1e30

MLA_HEADS = 8
MLA_Q_LORA = 512
MLA_KV_LORA = 256
MLA_NOPE = 128
MLA_ROPE = 64
MLA_V = 128
MLA_THETA = 10000.0
MLA_QK_PAD = 256

DIFF_HEADS = 8
DIFF_DH = 64
DIFF_VD = 2 * DIFF_DH
DIFF_ROT = DIFF_DH // 4
ROPE_THETA = 500000.0

LANES = 128
VT_ONES = 16
LOG2E = math.log2(math.e)
VMEM_LIMIT = 56 * 1024 * 1024


def _params(*sem):
    return pltpu.CompilerParams(dimension_semantics=sem, vmem_limit_bytes=VMEM_LIMIT)


def _dot(a, b):
    return jnp.dot(a, b, preferred_element_type=F32)


def _dot_nt(a, b):
    return lax.dot_general(a, b, (((1,), (1,)), ((), ())), preferred_element_type=F32)


def _rms(x, g):
    return x * lax.rsqrt(jnp.mean(x * x, axis=-1, keepdims=True) + EPS) * g


def _rope_tile(t, c, s1, s2, half):
    return t * c + pltpu.roll(t, LANES - half, 1) * s1 + pltpu.roll(t, half, 1) * s2


def _put(ref, lo, val, tr):
    w = val.shape[1]
    if tr:
        ref[0, lo:lo + w, :] = val.T.astype(ref.dtype)
    else:
        ref[:, lo:lo + w] = val.astype(ref.dtype)


def _put_v(ref, hd, val, tr):
    bm, w = val.shape
    if tr:
        lo = hd * (w + VT_ONES)
        ref[0, lo:lo + w, :] = val.T.astype(ref.dtype)
        ref[0, lo + w:lo + w + VT_ONES, :] = jnp.ones((VT_ONES, bm), ref.dtype)
    else:
        ref[:, hd * w:(hd + 1) * w] = val.astype(ref.dtype)


def _v_width(heads, dv, tr):
    return heads * (dv + VT_ONES) if tr else heads * dv


def _out(m, n, bm, dtype, tr):
    if tr:
        return pl.BlockSpec((1, n, bm), lambda i: (i, 0, 0)), jax.ShapeDtypeStruct((m // bm, n, bm), dtype)
    return pl.BlockSpec((bm, n), lambda i: (i, 0)), jax.ShapeDtypeStruct((m, n), dtype)


def _qproj_body(h, wq_ref, gq_ref, wuq_ref, c, s1, s2, o_ref, scale, tr):
    qlat = _rms(_dot(h, wq_ref[...]), gq_ref[...]).astype(BF16)
    q = _dot(qlat, wuq_ref[...])
    for hd in range(MLA_HEADS):
        lo = hd * MLA_QK_PAD
        _put(o_ref, lo, q[:, lo:lo + LANES] * scale, tr)
        t = _rope_tile(q[:, lo + LANES:lo + 2 * LANES], c, s1, s2, MLA_ROPE // 2)
        _put(o_ref, lo + LANES, t * scale, tr)


def _kv_expand(ckv, krope_tile, wukv_ref, kmla_ref, vmla_ref, tr):
    kv = _dot(ckv.astype(BF16), wukv_ref[...])
    kr = krope_tile.astype(BF16)
    for hd in range(MLA_HEADS):
        lo = hd * MLA_QK_PAD
        kmla_ref[:, lo:lo + LANES] = kv[:, hd * MLA_NOPE:(hd + 1) * MLA_NOPE].astype(BF16)
        kmla_ref[:, lo + LANES:lo + 2 * LANES] = kr
        vlo = MLA_HEADS * MLA_NOPE + hd * MLA_V
        _put_v(vmla_ref, hd, kv[:, vlo:vlo + MLA_V], tr)


def _mla_proj_kernel(x_ref, gmix_ref, wq_ref, gq_ref, wuq_ref, wkv_ref, gkv_ref, wukv_ref, c_ref, s1_ref, s2_ref,
                     h_ref, q_ref, ckv_ref, krope_ref, kmla_ref, vmla_ref, *, scale, tr):
    h = _rms(x_ref[...], gmix_ref[...]).astype(BF16)
    h_ref[...] = h
    c, s1, s2 = c_ref[...], s1_ref[...], s2_ref[...]
    _qproj_body(h, wq_ref, gq_ref, wuq_ref, c, s1, s2, q_ref, scale, tr)
    z = _dot(h, wkv_ref[...])
    ckv = _rms(z[:, :MLA_KV_LORA], gkv_ref[...])
    ckv_ref[...] = ckv
    t = _rope_tile(z[:, MLA_KV_LORA:], c, s1, s2, MLA_ROPE // 2)
    krope_ref[...] = t[:, :MLA_ROPE]
    _kv_expand(ckv, t, wukv_ref, kmla_ref, vmla_ref, tr)


def _mla_proj(x, gmix, wq, gq, wuq, wkv, gkv, wukv, tabs, bm, scale, tr):
    m, d = x.shape
    nt = tabs[0].shape[0] // bm
    full = lambda a: pl.BlockSpec(a.shape, lambda i: (0, 0))
    tab = pl.BlockSpec((bm, LANES), lambda i: (i % nt, 0))
    row = lambda n: pl.BlockSpec((bm, n), lambda i: (i, 0))
    nk = MLA_HEADS * MLA_QK_PAD
    qspec, qshape = _out(m, wuq.shape[1], bm, BF16, tr)
    vspec, vshape = _out(m, _v_width(MLA_HEADS, MLA_V, tr), bm, BF16, tr)
    return pl.pallas_call(
        functools.partial(_mla_proj_kernel, scale=scale, tr=tr),
        grid=(m // bm,),
        in_specs=[row(d), full(gmix), full(wq), full(gq), full(wuq), full(wkv), full(gkv), full(wukv), tab, tab, tab],
        out_specs=[row(d), qspec, row(MLA_KV_LORA), row(MLA_ROPE), row(nk), vspec],
        out_shape=[jax.ShapeDtypeStruct((m, d), BF16), qshape,
                   jax.ShapeDtypeStruct((m, MLA_KV_LORA), F32), jax.ShapeDtypeStruct((m, MLA_ROPE), F32),
                   jax.ShapeDtypeStruct((m, nk), BF16), vshape],
        compiler_params=_params("parallel"),
        name="norm_mla_proj",
    )(x, gmix, wq, gq, wuq, wkv, gkv, wukv, *tabs)


def _diff_proj_kernel(h_ref, wdq_ref, wdk_ref, wdv_ref, c_ref, s1_ref, s2_ref,
                      q1_ref, q2_ref, dk_ref, dkb_ref, dv_ref, dvb_ref, *, scale, tr):
    h = h_ref[...]
    zq = _dot(h, wdq_ref[...])
    zk = _dot(h, wdk_ref[...])
    zv = _dot(h, wdv_ref[...])
    dv_ref[...] = zv
    c, s1, s2 = c_ref[...], s1_ref[...], s2_ref[...]
    first = lax.broadcasted_iota(jnp.int32, (1, LANES), 1) < DIFF_DH
    for hd in range(DIFF_HEADS):
        sl = slice(hd * LANES, (hd + 1) * LANES)
        q = _rope_tile(zq[:, sl], c, s1, s2, DIFF_ROT // 2) * scale
        _put(q1_ref, hd * LANES, jnp.where(first, q, 0.0), tr)
        _put(q2_ref, hd * LANES, jnp.where(first, 0.0, q), tr)
        k = _rope_tile(zk[:, sl], c, s1, s2, DIFF_ROT // 2)
        dk_ref[:, sl] = k
        dkb_ref[:, sl] = k.astype(BF16)
        _put_v(dvb_ref, hd, zv[:, sl], tr)


def _diff_proj(h, wdq, wdk, wdv, tabs, bm, scale, tr):
    m, d = h.shape
    nt = tabs[0].shape[0] // bm
    n = wdq.shape[1]
    full = lambda a: pl.BlockSpec(a.shape, lambda i: (0, 0))
    tab = pl.BlockSpec((bm, LANES), lambda i: (i % nt, 0))
    row = lambda w: pl.BlockSpec((bm, w), lambda i: (i, 0))
    qspec, qshape = _out(m, n, bm, BF16, tr)
    vspec, vshape = _out(m, _v_width(DIFF_HEADS, DIFF_VD, tr), bm, BF16, tr)
    f32_rows, bf16_rows = jax.ShapeDtypeStruct((m, n), F32), jax.ShapeDtypeStruct((m, n), BF16)
    return pl.pallas_call(
        functools.partial(_diff_proj_kernel, scale=scale, tr=tr),
        grid=(m // bm,),
        in_specs=[row(d), full(wdq), full(wdk), full(wdv), tab, tab, tab],
        out_specs=[qspec, qspec, row(n), row(n), row(n), vspec],
        out_shape=[qshape, qshape, f32_rows, bf16_rows, f32_rows, vshape],
        compiler_params=_params("parallel"),
        name="diff_proj",
    )(h, wdq, wdk, wdv, *tabs)


def _merge_kernel(h_ref, oa_ref, ob_ref, wga_ref, wgb_ref, wa_ref, wb_ref, o_ref):
    h = h_ref[...]
    ga = jax.nn.sigmoid(_dot(h, wga_ref[...]))
    gb = jax.nn.sigmoid(_dot(h, wgb_ref[...]))
    ya = _dot(oa_ref[...], wa_ref[...])
    yb = _dot(ob_ref[...], wb_ref[...])
    o_ref[...] = (ga * ya + gb * yb).astype(o_ref.dtype)


def _merge(h, oa, ob, wga, wgb, wa, wb, bm, bn):
    m, d = h.shape
    n = wga.shape[1]
    row = lambda a: pl.BlockSpec((bm, a.shape[1]), lambda i, j: (i, 0))
    col = lambda a: pl.BlockSpec((a.shape[0], bn), lambda i, j: (0, j))
    return pl.pallas_call(
        _merge_kernel,
        grid=(m // bm, n // bn),
        in_specs=[row(h), row(oa), row(ob), col(wga), col(wgb), col(wa), col(wb)],
        out_specs=pl.BlockSpec((bm, bn), lambda i, j: (i, j)),
        out_shape=jax.ShapeDtypeStruct((m, n), BF16),
        compiler_params=_params("parallel", "arbitrary"),
        name="gated_merge",
    )(h, oa, ob, wga, wgb, wa, wb)


def _outproj_kernel(x_ref, mg_ref, wo_ref, g_ref, x2_ref, h2_ref):
    x2 = x_ref[...] + _dot(mg_ref[...], wo_ref[...])
    x2_ref[...] = x2
    h2_ref[...] = _rms(x2, g_ref[...]).astype(BF16)


def _outproj(x, mg, wo, g, bm):
    m, d = x.shape
    row = pl.BlockSpec((bm, d), lambda i: (i, 0))
    return pl.pallas_call(
        _outproj_kernel,
        grid=(m // bm,),
        in_specs=[row, row, pl.BlockSpec(wo.shape, lambda i: (0, 0)), pl.BlockSpec((1, d), lambda i: (0, 0))],
        out_specs=[row, row],
        out_shape=[jax.ShapeDtypeStruct((m, d), F32), jax.ShapeDtypeStruct((m, d), BF16)],
        compiler_params=_params("parallel"),
        name="out_proj_residual",
    )(x, mg, wo, g)


def _ffn_kernel(h2_ref, wg_ref, wu_ref, wd_ref, x2_ref, gf_ref, y_ref):
    f = pl.program_id(1)

    @pl.when(f == 0)
    def _():
        y_ref[...] = x2_ref[...]

    h2 = h2_ref[...]
    a = (jax.nn.silu(_dot(h2, wg_ref[...])) * _dot(h2, wu_ref[...])).astype(BF16)
    y_ref[...] += _dot(a, wd_ref[...])

    @pl.when(f == pl.num_programs(1) - 1)
    def _():
        y_ref[...] = _rms(y_ref[...], gf_ref[...])


def _ffn(h2, w_in, w_out, x2, gf, bm, bf):
    m, d = h2.shape
    dff = w_out.shape[0]
    nf = dff // bf
    row = pl.BlockSpec((bm, d), lambda i, f: (i, 0))
    return pl.pallas_call(
        _ffn_kernel,
        grid=(m // bm, nf),
        in_specs=[row,
                  pl.BlockSpec((d, bf), lambda i, f: (0, f)),
                  pl.BlockSpec((d, bf), lambda i, f: (0, f + nf)),
                  pl.BlockSpec((bf, d), lambda i, f: (f, 0)),
                  row,
                  pl.BlockSpec((1, d), lambda i, f: (0, 0))],
        out_specs=row,
        out_shape=jax.ShapeDtypeStruct((m, d), F32),
        compiler_params=_params("parallel", "arbitrary"),
        name="ffn_swiglu_final_norm",
    )(h2, w_in, w_in, w_out, x2, gf)


def _softmax_step(s, v, m, l, acc):
    m_new = jnp.maximum(m, jnp.max(s, axis=-1, keepdims=True))
    alpha = jnp.exp2(m - m_new)
    p = jnp.exp2(s - m_new)
    l = alpha * l + jnp.sum(p, axis=-1, keepdims=True)
    acc = alpha * acc + _dot(p.astype(BF16), v)
    return m_new, l, acc


def _chunk_mask(tq, tk, q0, k0):
    qp = q0 + lax.broadcasted_iota(jnp.int32, (tq, tk), 0)
    kp = k0 + lax.broadcasted_iota(jnp.int32, (tq, tk), 1)
    return (kp // CHUNK) <= (qp // CHUNK)


def _softmax_init(tq, dv):
    return (jnp.full((tq, 1), NEG_INF, F32), jnp.zeros((tq, 1), F32), jnp.zeros((tq, dv), F32))


def _chunk_mask_t(t):
    kp = lax.broadcasted_iota(jnp.int32, (t, t), 0)
    qp = lax.broadcasted_iota(jnp.int32, (t, t), 1)
    return (kp // CHUNK) <= (qp // CHUNK)


def _probs_t(s, smax, m):
    m_new = jnp.maximum(m, smax)
    return jnp.exp2(s - m_new).astype(BF16), jnp.exp2(m - m_new), m_new


def _attention_scratch(t, dv, streams):
    return [pltpu.VMEM((streams, 2, t, t), F32),
            pltpu.VMEM((streams, dv + VT_ONES, t), F32),
            pltpu.VMEM((streams, 8, t), F32)]


def _causal_attention_t(n, scores, next_scores, values, finish, s_ref, acc_ref, stat_ref, t, streams):
    row_m, row_smax = 0, 1

    def stat(st, r):
        return stat_ref[st, r:r + 1, :]

    def put_scores(st, slot, s):
        s_ref[st, slot] = s
        stat_ref[st, row_smax + slot:row_smax + slot + 1, :] = jnp.max(s, axis=0, keepdims=True)

    def accumulate(st, j, s, smax):
        p, alpha, m = _probs_t(s, smax, stat(st, row_m))
        acc = alpha * acc_ref[st] + _dot(values(j, st), p)
        stat_ref[st, row_m:row_m + 1, :] = m
        return acc

    @pl.when(n == 0)
    def _():
        for st in range(streams):
            put_scores(st, 0, scores(0, st))

    for st in range(streams):
        acc_ref[st] = jnp.zeros(acc_ref.shape[1:], F32)
        stat_ref[st, row_m:row_m + 1, :] = jnp.full((1, t), NEG_INF, F32)

    def run(first):
        def step(tau, par):
            for st in range(streams):
                put_scores(st, (first + par) % 2, scores(tau, st))
                prev = (first + par + 1) % 2
                acc_ref[st] = accumulate(st, tau - 1, s_ref[st, prev], stat(st, row_smax + prev))

        unroll = 4

        def trip(u, carry):
            for k in range(unroll):
                step(unroll * u + 1 + k, (1 + k) % 2)
            return carry

        lax.fori_loop(0, n // unroll, trip, 0)
        done = (n // unroll) * unroll

        @pl.when(n - done >= 2)
        def _():
            step(done + 1, 1)
            step(done + 2, 0)

        @pl.when(n % 2 == 1)
        def _():
            step(n, 1)

        def tail(par):
            diag = (first + par) % 2
            for st in range(streams):
                put_scores(st, 1 - diag, next_scores(st))
            mask = _chunk_mask_t(t)
            res = []
            for st in range(streams):
                halves = []
                for c in range(0, t, t // 2):
                    rows, cols = slice(0, c + t // 2), slice(c, c + t // 2)
                    s = jnp.where(mask[rows, cols], s_ref[st, diag, rows, cols], NEG_INF)
                    p, alpha, _ = _probs_t(s, jnp.max(s, axis=0, keepdims=True), stat_ref[st, row_m:row_m + 1, cols])
                    halves.append(alpha * acc_ref[st, :, cols] + _dot(values(n, st)[:, rows], p))
                res.append(jnp.concatenate(halves, axis=1))
            finish(res)

        for par in (0, 1):
            pl.when(n % 2 == par)(functools.partial(tail, par))

    first = ((n + 1) // 2) % 2
    for f in (0, 1):
        pl.when(first == f)(functools.partial(run, f))


def _normalised(acc, dv):
    return (acc[:dv] / acc[dv:dv + 1]).T


def _mla_prompt_kernel(qt_ref, qn_ref, k_ref, vt_ref, o_ref, *scratch, t, heads):
    def block_scores(q_ref, j, hd):
        k = k_ref[0, pl.ds(pl.multiple_of(j * t, t), t), hd * MLA_QK_PAD:(hd + 1) * MLA_QK_PAD]
        return _dot(k, q_ref[0, hd * MLA_QK_PAD:(hd + 1) * MLA_QK_PAD, :])

    scores = functools.partial(block_scores, qt_ref)
    next_scores = functools.partial(block_scores, qn_ref, 0)

    def values(j, hd):
        return vt_ref[j, hd * (MLA_V + VT_ONES):(hd + 1) * (MLA_V + VT_ONES), :]

    def finish(res):
        for hd, acc in enumerate(res):
            o_ref[0, :, hd * MLA_V:(hd + 1) * MLA_V] = _normalised(acc, MLA_V).astype(o_ref.dtype)

    _causal_attention_t(pl.program_id(2), scores, next_scores, values, finish, *scratch, t, heads)


def _mla_prompt(qt, k, vt, t, heads):
    b, l, _ = k.shape
    nq = l // t
    qblock = lambda step: pl.BlockSpec((1, heads * MLA_QK_PAD, t),
                                       lambda b, h, i: (b * nq + jnp.minimum(i + step, nq - 1), h, 0))
    return pl.pallas_call(
        functools.partial(_mla_prompt_kernel, t=t, heads=heads),
        grid=(b, MLA_HEADS // heads, nq),
        in_specs=[qblock(0), qblock(1),
                  pl.BlockSpec((1, l, heads * MLA_QK_PAD), lambda b, h, i: (b, 0, h), pipeline_mode=pl.Buffered(1)),
                  pl.BlockSpec((nq, heads * (MLA_V + VT_ONES), t), lambda b, h, i: (b, h, 0))],
        out_specs=pl.BlockSpec((1, t, heads * MLA_V), lambda b, h, i: (b, i, h)),
        out_shape=jax.ShapeDtypeStruct((b, l, MLA_HEADS * MLA_V), BF16),
        scratch_shapes=_attention_scratch(t, MLA_V, heads),
        compiler_params=_params("parallel", "parallel", "arbitrary"),
        name="mla_attention_prompt",
    )(qt, qt, k, vt)


def _lambda(lq1_ref, lk1_ref, lq2_ref, lk2_ref, lambda_init):
    a = jnp.sum(lq1_ref[...] * lk1_ref[...], axis=-1, keepdims=True)
    b = jnp.sum(lq2_ref[...] * lk2_ref[...], axis=-1, keepdims=True)
    return jnp.exp(a) - jnp.exp(b) + lambda_init


def _diff_prompt_kernel(q1t_ref, q2t_ref, q1n_ref, q2n_ref, k_ref, vt_ref, lq1_ref, lk1_ref, lq2_ref, lk2_ref,
                        g_ref, o_ref, *scratch, t, heads, lambda_init):
    qt_refs, qn_refs = (q1t_ref, q2t_ref), (q1n_ref, q2n_ref)

    def block_scores(q_refs, rows, st):
        cols = slice((st // 2) * LANES, (st // 2 + 1) * LANES)
        return _dot(k_ref[0, rows, cols], q_refs[st % 2][0, cols, :])

    def scores(j, st):
        return block_scores(qt_refs, pl.ds(pl.multiple_of(j * t, t), t), st)

    def next_scores(st):
        return block_scores(qn_refs, slice(0, t), st)

    def values(j, st):
        return vt_ref[j, (st // 2) * (DIFF_VD + VT_ONES):(st // 2 + 1) * (DIFF_VD + VT_ONES), :]

    def finish(res):
        lam = _lambda(lq1_ref, lk1_ref, lq2_ref, lk2_ref, lambda_init)
        for hd in range(heads):
            o = _normalised(res[2 * hd], DIFF_VD) - lam * _normalised(res[2 * hd + 1], DIFF_VD)
            o = _rms(o, g_ref[...]) * (1.0 - lambda_init)
            o_ref[0, :, hd * DIFF_VD:(hd + 1) * DIFF_VD] = o.astype(o_ref.dtype)

    _causal_attention_t(pl.program_id(2), scores, next_scores, values, finish, *scratch, t, 2 * heads)


def _diff_prompt(q1t, q2t, k, vt, lams, g, t, heads, lambda_init):
    b, l, _ = k.shape
    nq = l // t
    qs = pl.BlockSpec((1, heads * LANES, t), lambda b, h, i: (b * nq + i, h, 0))
    qn = pl.BlockSpec((1, heads * LANES, t), lambda b, h, i: (b * nq + jnp.minimum(i + 1, nq - 1), h, 0))
    small = lambda a: pl.BlockSpec(a.shape, lambda b, h, i: (0, 0))
    return pl.pallas_call(
        functools.partial(_diff_prompt_kernel, t=t, heads=heads, lambda_init=lambda_init),
        grid=(b, DIFF_HEADS // heads, nq),
        in_specs=[qs, qs, qn, qn,
                  pl.BlockSpec((1, l, heads * LANES), lambda b, h, i: (b, 0, h)),
                  pl.BlockSpec((nq, heads * (DIFF_VD + VT_ONES), t), lambda b, h, i: (b, h, 0))]
                 + [small(a) for a in lams] + [small(g)],
        out_specs=pl.BlockSpec((1, t, heads * DIFF_VD), lambda b, h, i: (b, i, h)),
        out_shape=jax.ShapeDtypeStruct((b, l, DIFF_HEADS * DIFF_VD), BF16),
        scratch_shapes=_attention_scratch(t, DIFF_VD, 2 * heads),
        compiler_params=_params("parallel", "parallel", "arbitrary"),
        name="diff_attention_prompt",
    )(q1t, q2t, q1t, q2t, k, vt, *lams, g)


def _two_part_softmax(q, kc, vc, kn, vn, mask_c, mask_n):
    tq = q.shape[0]
    carry = _softmax_init(tq, vc.shape[1])
    carry = _softmax_step(jnp.where(mask_c, _dot_nt(q, kc), NEG_INF), vc, *carry)
    return _softmax_step(jnp.where(mask_n, _dot_nt(q, kn), NEG_INF), vn, *carry)


def _mla_sample_kernel(q_ref, latc_ref, krc_ref, latn_ref, kn_ref, wukv_ref, o_ref, *, past):
    tq = q_ref.shape[1]
    latc, latn = latc_ref[0].astype(BF16), latn_ref[0].astype(BF16)
    krc, krn = krc_ref[0].astype(BF16), kn_ref[0, :, LANES:2 * LANES]
    qa, qr = [], []
    for hd in range(MLA_HEADS):
        lo = hd * MLA_QK_PAD
        wk = wukv_ref[:, hd * MLA_NOPE:(hd + 1) * MLA_NOPE]
        qa.append(_dot_nt(q_ref[0, :, lo:lo + LANES], wk).astype(BF16))
        qr.append(q_ref[0, :, lo + LANES:lo + 2 * LANES])
    qa, qr = jnp.concatenate(qa, axis=0), jnp.concatenate(qr, axis=0)
    rows = MLA_HEADS * tq

    def mask(tk, k0):
        qp = past + lax.broadcasted_iota(jnp.int32, (rows, tk), 0) % tq
        kp = k0 + lax.broadcasted_iota(jnp.int32, (rows, tk), 1)
        return (kp // CHUNK) <= (qp // CHUNK)

    carry = _softmax_init(rows, MLA_KV_LORA)
    s = _dot_nt(qa, latc) + _dot_nt(qr, krc)
    carry = _softmax_step(jnp.where(mask(past, 0), s, NEG_INF), latc, *carry)
    s = _dot_nt(qa, latn) + _dot_nt(qr, krn)
    _, l, acc = _softmax_step(jnp.where(mask(tq, past), s, NEG_INF), latn, *carry)
    olat = (acc / l).astype(BF16)
    v0 = MLA_HEADS * MLA_NOPE
    for hd in range(MLA_HEADS):
        wv = wukv_ref[:, v0 + hd * MLA_V:v0 + (hd + 1) * MLA_V]
        o_ref[0, :, hd * MLA_V:(hd + 1) * MLA_V] = _dot(olat[hd * tq:(hd + 1) * tq], wv).astype(o_ref.dtype)


def _mla_sample(q, latc, krc, latn, kn, wukv, layer):
    b, tq, _ = q.shape
    blk = lambda a: pl.BlockSpec((1,) + a.shape[1:], lambda b: (b, 0, 0))
    cache = lambda a: pl.BlockSpec((1,) + a.shape[1:], lambda b: (layer * q.shape[0] + b, 0, 0))
    return pl.pallas_call(
        functools.partial(_mla_sample_kernel, past=latc.shape[1]),
        grid=(b,),
        in_specs=[blk(q), cache(latc), cache(krc), blk(latn), blk(kn), pl.BlockSpec(wukv.shape, lambda b: (0, 0))],
        out_specs=pl.BlockSpec((1, tq, MLA_HEADS * MLA_V), lambda b: (b, 0, 0)),
        out_shape=jax.ShapeDtypeStruct((b, tq, MLA_HEADS * MLA_V), BF16),
        compiler_params=_params("parallel"),
        name="mla_attention_sample",
    )(q, latc, krc, latn, kn, wukv)


def _diff_sample_kernel(q1_ref, q2_ref, kc_ref, vc_ref, kn_ref, vn_ref,
                        lq1_ref, lk1_ref, lq2_ref, lk2_ref, g_ref, o_ref, *, past, lambda_init):
    tq = q1_ref.shape[1]
    mask_c, mask_n = (jnp.concatenate([m, m], axis=0)
                      for m in (_chunk_mask(tq, past, past, 0), _chunk_mask(tq, tq, past, past)))
    lam = _lambda(lq1_ref, lk1_ref, lq2_ref, lk2_ref, lambda_init)
    for hd in range(DIFF_HEADS):
        rows = pl.ds(hd, past, stride=DIFF_HEADS)
        cols = slice(hd * LANES, (hd + 1) * LANES)
        kc, vc = kc_ref[0, rows, :].astype(BF16), vc_ref[0, rows, :].astype(BF16)
        q = jnp.concatenate([q1_ref[0, :, cols], q2_ref[0, :, cols]], axis=0)
        _, l, acc = _two_part_softmax(q, kc, vc, kn_ref[0, :, cols], vn_ref[0, :, cols], mask_c, mask_n)
        o = acc / l
        o = _rms(o[:tq] - lam * o[tq:], g_ref[...]) * (1.0 - lambda_init)
        o_ref[0, :, cols] = o.astype(o_ref.dtype)


def _diff_sample(q1, q2, kc, vc, kn, vn, lams, g, lambda_init, layer):
    b, tq, _ = q1.shape
    past = kc.shape[1] // DIFF_HEADS
    blk = lambda a: pl.BlockSpec((1,) + a.shape[1:], lambda b: (b, 0, 0))
    cache = pl.BlockSpec((1,) + kc.shape[1:], lambda b: (layer * q1.shape[0] + b, 0, 0))
    small = lambda a: pl.BlockSpec(a.shape, lambda b: (0, 0))
    return pl.pallas_call(
        functools.partial(_diff_sample_kernel, past=past, lambda_init=lambda_init),
        grid=(b,),
        in_specs=[blk(q1), blk(q2), cache, cache, blk(kn), blk(vn)] + [small(a) for a in lams] + [small(g)],
        out_specs=blk(q1),
        out_shape=jax.ShapeDtypeStruct((b, tq, DIFF_HEADS * DIFF_VD), BF16),
        compiler_params=_params("parallel"),
        name="diff_attention_sample",
    )(q1, q2, kc, vc, kn, vn, *lams, g)


def _rope_tables(pos, theta, rot_dim, period):
    half = rot_dim // 2
    inv = 1.0 / (jnp.float32(theta) ** (jnp.arange(half, dtype=F32) / half))
    ang = pos.astype(F32)[:, None] * inv[None, :]
    cos, sin = jnp.cos(ang), jnp.sin(ang)
    n = pos.shape[0]
    rest = period - rot_dim
    c = jnp.concatenate([cos, cos, jnp.ones((n, rest), F32)], axis=1)
    s1 = jnp.concatenate([-sin, jnp.zeros((n, half + rest), F32)], axis=1)
    s2 = jnp.concatenate([jnp.zeros((n, half), F32), sin, jnp.zeros((n, rest), F32)], axis=1)
    reps = LANES // period
    return tuple(jnp.tile(a, (1, reps)) for a in (c, s1, s2))


def _layer_weights(w_in, mla_w_uq, mla_w_ukv):
    c_kv = MLA_Q_LORA
    c_kr = c_kv + MLA_KV_LORA
    c_dq = c_kr + MLA_ROPE
    dqk = DIFF_HEADS * 2 * DIFF_DH
    c_dk = c_dq + dqk
    c_dv = c_dk + dqk
    c_ga = c_dv + DIFF_HEADS * DIFF_VD
    d = w_in.shape[0]
    c_gb = c_ga + d
    cols = lambda lo, hi: w_in[:, lo:hi].astype(BF16)
    wq = cols(0, c_kv)
    wkv = jnp.pad(cols(c_kv, c_dq), ((0, 0), (0, LANES - MLA_ROPE)))
    wdq, wdk, wdv = cols(c_dq, c_dk), cols(c_dk, c_dv), cols(c_dv, c_ga)
    wga, wgb = cols(c_ga, c_gb), cols(c_gb, w_in.shape[1])
    uq = mla_w_uq.astype(BF16).reshape(MLA_Q_LORA, MLA_HEADS, MLA_NOPE + MLA_ROPE)
    uq = jnp.pad(uq, ((0, 0), (0, 0), (0, MLA_QK_PAD - MLA_NOPE - MLA_ROPE))).reshape(MLA_Q_LORA, -1)
    ukv = mla_w_ukv.astype(BF16).reshape(MLA_KV_LORA, MLA_HEADS, MLA_NOPE + MLA_V)
    ukv = jnp.concatenate([ukv[:, :, :MLA_NOPE].reshape(MLA_KV_LORA, -1),
                           ukv[:, :, MLA_NOPE:].reshape(MLA_KV_LORA, -1)], axis=1)
    return wq, wkv, wdq, wdk, wdv, wga, wgb, uq, ukv


def _block(m, want):
    return want if m % want == 0 else m


def kernel(x_prompt, x_sample, cache_mla_ckv, cache_mla_krope, cache_diff_k, cache_diff_v, norm_mix, w_in,
           mla_q_norm, mla_w_uq, mla_kv_norm, mla_w_ukv, diff_lq1, diff_lk1, diff_lq2, diff_lk2, diff_subln,
           w_branch_a, w_branch_b, w_out, norm_ffn, w_ffn_in, w_ffn_out, norm_final):
    bp, lp, d = x_prompt.shape
    bs, ls, _ = x_sample.shape
    depth, _, past, _ = cache_mla_ckv.shape
    assert depth == 1, "the FFN kernel fuses the final norm, so it serves the last (only) layer"
    mp, ms = bp * lp, bs * ls
    t_attn = 512

    pos_p = jnp.arange(lp)
    pos_s = past + jnp.arange(ls)
    tabs_mla_p = _rope_tables(pos_p, MLA_THETA, MLA_ROPE, LANES)
    tabs_mla_s = tuple(jnp.tile(a, (bs, 1)) for a in _rope_tables(pos_s, MLA_THETA, MLA_ROPE, LANES))
    tabs_dif_p = _rope_tables(pos_p, ROPE_THETA, DIFF_ROT, DIFF_DH)
    tabs_dif_s = tuple(jnp.tile(a, (bs, 1)) for a in _rope_tables(pos_s, ROPE_THETA, DIFF_ROT, DIFF_DH))
    q_scale_mla = (MLA_NOPE + MLA_ROPE) ** -0.5 * LOG2E
    q_scale_dif = DIFF_DH ** -0.5 * LOG2E
    gfinal = norm_final.reshape(1, d)

    xp = x_prompt.reshape(mp, d)
    xs = x_sample.reshape(ms, d)
    rows_p, rows_s = [], []
    for l in range(depth):
        lambda_init = 0.8 - 0.6 * math.exp(-0.3 * l)
        wq, wkv, wdq, wdk, wdv, wga, wgb, uq, ukv = _layer_weights(w_in[l], mla_w_uq[l], mla_w_ukv[l])
        gmix, gq, gkv = norm_mix[l].reshape(1, -1), mla_q_norm[l].reshape(1, -1), mla_kv_norm[l].reshape(1, -1)
        lams = tuple(a[l].reshape(1, -1) for a in (diff_lq1, diff_lk1, diff_lq2, diff_lk2))
        gsub = diff_subln[l].reshape(1, -1)
        wa, wbr, wo = w_branch_a[l].astype(BF16), w_branch_b[l].astype(BF16), w_out[l].astype(BF16)
        wfi, wfo = w_ffn_in[l].astype(BF16), w_ffn_out[l].astype(BF16)
        gffn = norm_ffn[l].reshape(1, -1)

        def stage1(x, bm, tabs_mla, tabs_dif, tr):
            h, q, ckv, krope, kmla, vmla = _mla_proj(x, gmix, wq, gq, uq, wkv, gkv, ukv, tabs_mla, bm,
                                                     q_scale_mla, tr)
            q1, q2, dk, dkb, dv, dvb = _diff_proj(h, wdq, wdk, wdv, tabs_dif, bm, q_scale_dif, tr)
            return h, q, ckv, krope, kmla, vmla, q1, q2, dk, dkb, dv, dvb

        def stage2(x, h, oa, ob, bm, bn, bf):
            mg = _merge(h, oa, ob, wga, wgb, wa, wbr, bm, bn)
            x2, h2 = _outproj(x, mg, wo, gffn, min(bm, 512))
            return _ffn(h2, wfi, wfo, x2, gfinal, bm, bf)

        h, q, ckv, krope, kmla, vmla, q1, q2, dk, dkb, dv, dvb = stage1(xp, t_attn, tabs_mla_p, tabs_dif_p, True)
        r3 = lambda a: a.reshape(bp, lp, -1)
        oa = _mla_prompt(q, r3(kmla), vmla, t_attn, 4)
        ob = _diff_prompt(q1, q2, r3(dkb), dvb, lams, gsub, t_attn, 2, lambda_init)
        xp = stage2(xp, h, oa.reshape(mp, -1), ob.reshape(mp, -1), _block(mp, 1024), 1024, 256)
        rows_p.append((ckv.reshape(bp, lp, -1), krope.reshape(bp, lp, -1),
                       dk.reshape(bp, lp, DIFF_HEADS, -1), dv.reshape(bp, lp, DIFF_HEADS, -1)))

        h, q, ckv, krope, kmla, vmla, q1, q2, dk, dkb, dv, dvb = stage1(xs, ms, tabs_mla_s, tabs_dif_s, False)
        kr_pad = jnp.pad(cache_mla_krope.reshape(depth * bs, past, MLA_ROPE), ((0, 0), (0, 0), (0, LANES - MLA_ROPE)))
        r3 = lambda a: a.reshape(bs, ls, -1)
        oa = _mla_sample(r3(q), cache_mla_ckv.reshape(depth * bs, past, -1), kr_pad, r3(ckv), r3(kmla), ukv, l)
        call = lambda a: a.reshape(depth * bs, past * DIFF_HEADS, -1)
        ob = _diff_sample(r3(q1), r3(q2), call(cache_diff_k), call(cache_diff_v), r3(dkb), r3(dvb),
                          lams, gsub, lambda_init, l)
        xs = stage2(xs, h, oa.reshape(ms, -1), ob.reshape(ms, -1), ms, 512, 512)
        rows_s.append((ckv.reshape(bs, ls, -1), krope.reshape(bs, ls, -1),
                       dk.reshape(bs, ls, DIFF_HEADS, -1), dv.reshape(bs, ls, DIFF_HEADS, -1)))

    y_prompt = xp.reshape(bp, lp, d)
    y_sample = xs.reshape(bs, ls, d)
    stack = lambda rows, i: jnp.stack([r[i] for r in rows], axis=0)
    return (y_prompt, y_sample,
            stack(rows_p, 0), stack(rows_p, 1), stack(rows_p, 2), stack(rows_p, 3),
            stack(rows_s, 0), stack(rows_s, 1), stack(rows_s, 2), stack(rows_s, 3))
```
